```python
import math
import jax, jax.numpy as jnp
from jax import lax
import numpy as np


D_MODEL = 1024
BATCH = 16
SEQ = 2048
DEPTH = 1
DEC_BATCH = 16
DEC_SEQ = 16
PAST_LEN = 2048

CHUNK = 64
A_WIDTH = 512
A_GROUPS = 4
A_GROUP_CH = A_WIDTH // A_GROUPS
A_SPAN = 128
B_HEADS = 8
B_HEAD_DIM = 64
B_KV_HEADS = 2
B_GROUP = B_HEADS // B_KV_HEADS
B_WIDTH = B_HEADS * B_HEAD_DIM
IDX_HEADS = 8
IDX_DIM = 64
TOPK_MAX = 256
Q_BLOCK = 128
D_FF = 4 * D_MODEL
N_BRANCH = 2
DN_ALPHA = (2 * DEPTH) ** 0.25
DN_BETA = (8 * DEPTH) ** -0.25
LN_EPS = 1e-5
COLS_A = 2 * A_WIDTH
COLS_Q = B_WIDTH
COLS_KV = B_KV_HEADS * B_HEAD_DIM
COLS_QI = IDX_HEADS * IDX_DIM
COLS_KI = IDX_DIM
COLS_WI = IDX_HEADS
COLS_GATE = N_BRANCH * D_MODEL
SPLIT_POINTS = tuple(int(c) for c in np.cumsum([COLS_A, COLS_Q, COLS_KV, COLS_KV, COLS_QI, COLS_KI, COLS_WI]))
D_IN = SPLIT_POINTS[-1] + COLS_GATE

kernel_name = "gmlp_dsa_gated_hybrid_stream_step"


def layer_norm(x, g, b):
    xf = x.astype(jnp.float32)
    mu = jnp.mean(xf, axis=-1, keepdims=True)
    var = jnp.mean(jnp.square(xf - mu), axis=-1, keepdims=True)
    return ((xf - mu) * lax.rsqrt(var + LN_EPS) * g + b).astype(x.dtype)


def alibi_slopes():
    return jnp.exp2(-8.0 * jnp.arange(1, B_HEADS + 1, dtype=jnp.float32) / B_HEADS)


def chunk_mask(q_pos, k_pos):
    return (k_pos[None, :] // CHUNK) <= (q_pos[:, None] // CHUNK)


def in_projection(x, w_in, lnv_g, lnv_b, lnk_g, lnk_b):
    lead = x.shape[:-1]
    z = x @ w_in
    za, q, k, v, qi, ki, wi, gl = jnp.split(z, SPLIT_POINTS, axis=-1)
    u, va = jnp.split(jax.nn.gelu(za, approximate=False), 2, axis=-1)
    va = layer_norm(va, lnv_g, lnv_b)
    q = q.reshape(*lead, B_HEADS, B_HEAD_DIM)
    k = k.reshape(*lead, B_KV_HEADS, B_HEAD_DIM)
    v = v.reshape(*lead, B_KV_HEADS, B_HEAD_DIM)
    qi = qi.reshape(*lead, IDX_HEADS, IDX_DIM)
    ki = layer_norm(ki, lnk_g, lnk_b)
    wi = wi * (IDX_HEADS ** -0.5)
    gates = jax.nn.sigmoid(gl).reshape(*lead, N_BRANCH, D_MODEL)
    return u, va, q, k, v, qi, ki, wi, gates


def masked_spatial_weights(w_s):
    pos = jnp.arange(A_SPAN)
    return jnp.where(chunk_mask(pos, pos)[None], w_s, jnp.zeros((), w_s.dtype))


def spatial_gate_prompt(u, va, w_s, b_s):
    n, s, _ = va.shape
    ws = masked_spatial_weights(w_s)
    vb = va.reshape(n, s // A_SPAN, A_SPAN, A_GROUPS, A_GROUP_CH)
    mixed = jnp.einsum('gij,bnjgc->bnigc', ws, vb) + b_s.T[None, None, :, :, None]
    return u * mixed.reshape(n, s, A_WIDTH)


def spatial_gate_sample(u, va, w_s, b_s):
    n, t, _ = va.shape
    ws = masked_spatial_weights(w_s)[:, :t, :t]
    vb = va.reshape(n, t, A_GROUPS, A_GROUP_CH)
    mixed = jnp.einsum('gij,bjgc->bigc', ws, vb) + b_s[:, :t].T[None, :, :, None]
    return u * mixed.reshape(n, t, A_WIDTH)


def dsa_attend(q, qi, wi, q_pos, k, v, ki, k_pos, topk, slopes):
    n, t = q.shape[:2]
    f32 = jnp.float32
    idx_logits = jnp.einsum('nthd,nld->nhtl', qi.astype(f32), ki.astype(f32)) * (IDX_DIM ** -0.5)
    score = jnp.einsum('nhtl,nth->ntl', jax.nn.relu(idx_logits), wi.astype(f32))
    score = jnp.where(chunk_mask(q_pos, k_pos)[None], score, -jnp.inf)
    top_val, top_idx = lax.top_k(score, topk)
    valid = jnp.isfinite(top_val)
    bidx = jnp.arange(n)[:, None, None]
    k_sel = k[bidx, top_idx]
    v_sel = v[bidx, top_idx]
    dist = jnp.abs(q_pos[None, :, None] - k_pos[top_idx]).astype(f32)
    qg = q.reshape(n, t, B_KV_HEADS, B_GROUP, B_HEAD_DIM)
    logits = jnp.einsum('ntkgd,ntjkd->ntkgj', qg, k_sel).astype(f32) * (B_HEAD_DIM ** -0.5)
    logits = logits - slopes.reshape(B_KV_HEADS, B_GROUP)[None, None, :, :, None] * dist[:, :, None, None, :]
    logits = jnp.where(valid[:, :, None, None, :], logits, -jnp.inf)
    p = jax.nn.softmax(logits, axis=-1).astype(v.dtype)
    out = jnp.einsum('ntkgj,ntjkd->ntkgd', p, v_sel)
    return out.reshape(n, t, B_WIDTH)


def dsa_prompt(q, k, v, qi, ki, wi, slopes):
    n, s = q.shape[:2]
    nblk = s // Q_BLOCK
    topk = min(TOPK_MAX, s // 4)
    pos = jnp.arange(s)

    def to_blocks(a):
        return jnp.moveaxis(a.reshape(n, nblk, Q_BLOCK, *a.shape[2:]), 1, 0)

    def one_block(blk):
        qb, qib, wib, qpos = blk
        return dsa_attend(qb, qib, wib, qpos, k, v, ki, pos, topk, slopes)

    out = lax.map(one_block, (to_blocks(q), to_blocks(qi), to_blocks(wi), pos.reshape(nblk, Q_BLOCK)))
    return jnp.moveaxis(out, 0, 1).reshape(n, s, B_WIDTH)


def dsa_sample(q, k_new, v_new, qi, ki_new, wi, cache_k, cache_v, cache_kidx, slopes):
    t = q.shape[1]
    past = cache_k.shape[1]
    k_all = jnp.concatenate([cache_k, k_new], axis=1)
    v_all = jnp.concatenate([cache_v, v_new], axis=1)
    ki_all = jnp.concatenate([cache_kidx, ki_new], axis=1)
    k_pos = jnp.arange(past + t)
    q_pos = past + jnp.arange(t)
    topk = min(TOPK_MAX, (past + t) // 4)
    return dsa_attend(q, qi, wi, q_pos, k_all, v_all, ki_all, k_pos, topk, slopes)


def merge_and_channel_mix(x, a, b, gates, w_branch, w_out, ln1_g, ln1_b, w_ff1, b_ff1, w_ff2, b_ff2, ln2_g, ln2_b):
    m = gates[..., 0, :] * (a @ w_branch[0]) + gates[..., 1, :] * (b @ w_branch[1])
    h = layer_norm(DN_ALPHA * x + m @ w_out, ln1_g, ln1_b)
    f = jnp.square(jax.nn.relu(h @ w_ff1 + b_ff1)) @ w_ff2 + b_ff2
    return layer_norm(DN_ALPHA * h + f, ln2_g, ln2_b)


def setup_inputs(seed: int = 0) -> dict:
    key = jax.random.key(seed)
    ks = jax.random.split(key, 24)
    f32 = jnp.float32

    def nrm(k, shape, scale):
        return jax.random.normal(k, shape, f32) * scale

    return {
        'x_prompt': nrm(ks[0], (BATCH, SEQ, D_MODEL), 1.0),
        'x_sample': nrm(ks[1], (DEC_BATCH, DEC_SEQ, D_MODEL), 1.0),
        'cache_k': nrm(ks[2], (DEPTH, DEC_BATCH, PAST_LEN, B_KV_HEADS, B_HEAD_DIM), 1.0),
        'cache_v': nrm(ks[3], (DEPTH, DEC_BATCH, PAST_LEN, B_KV_HEADS, B_HEAD_DIM), 1.0),
        'cache_kidx': nrm(ks[4], (DEPTH, DEC_BATCH, PAST_LEN, IDX_DIM), 1.0),
        'w_in': nrm(ks[5], (DEPTH, D_MODEL, D_IN), D_MODEL ** -0.5),
        'lnv_g': 1.0 + nrm(ks[6], (DEPTH, A_WIDTH), 0.01),
        'lnv_b': nrm(ks[7], (DEPTH, A_WIDTH), 0.01),
        'w_s': nrm(ks[8], (DEPTH, A_GROUPS, A_SPAN, A_SPAN), A_SPAN ** -0.5),
        'b_s': 1.0 + nrm(ks[9], (DEPTH, A_GROUPS, A_SPAN), 0.01),
        'lnk_g': 1.0 + nrm(ks[10], (DEPTH, IDX_DIM), 0.01),
        'lnk_b': nrm(ks[11], (DEPTH, IDX_DIM), 0.01),
        'w_branch': nrm(ks[12], (DEPTH, N_BRANCH, A_WIDTH, D_MODEL), DN_BETA * A_WIDTH ** -0.5),
        'w_out': nrm(ks[13], (DEPTH, D_MODEL, D_MODEL), DN_BETA * D_MODEL ** -0.5),
        'ln1_g': 1.0 + nrm(ks[14], (DEPTH, D_MODEL), 0.01),
        'ln1_b': nrm(ks[15], (DEPTH, D_MODEL), 0.01),
        'w_ff1': nrm(ks[16], (DEPTH, D_MODEL, D_FF), DN_BETA * D_MODEL ** -0.5),
        'b_ff1': nrm(ks[17], (DEPTH, D_FF), 0.01),
        'w_ff2': nrm(ks[18], (DEPTH, D_FF, D_MODEL), DN_BETA * D_FF ** -0.5),
        'b_ff2': nrm(ks[19], (DEPTH, D_MODEL), 0.01),
        'ln2_g': 1.0 + nrm(ks[20], (DEPTH, D_MODEL), 0.01),
        'ln2_b': nrm(ks[21], (DEPTH, D_MODEL), 0.01),
    }


def reference(x_prompt, x_sample, cache_k, cache_v, cache_kidx, w_in, lnv_g, lnv_b, w_s, b_s, lnk_g, lnk_b,
              w_branch, w_out, ln1_g, ln1_b, w_ff1, b_ff1, w_ff2, b_ff2, ln2_g, ln2_b):
    slopes = alibi_slopes()
    xp, xs = x_prompt, x_sample
    kp_l, vp_l, kip_l, ks_l, vs_l, kis_l, vas_l = [], [], [], [], [], [], []
    for l in range(DEPTH):
        up, vap, qp, kp, vp, qip, kip, wip, gp = in_projection(xp, w_in[l], lnv_g[l], lnv_b[l], lnk_g[l], lnk_b[l])
        us, vas, qs, kss, vss, qis, kis, wis, gs = in_projection(xs, w_in[l], lnv_g[l], lnv_b[l], lnk_g[l], lnk_b[l])
        ap = spatial_gate_prompt(up, vap, w_s[l], b_s[l])
        a_s = spatial_gate_sample(us, vas, w_s[l], b_s[l])
        bp = dsa_prompt(qp, kp, vp, qip, kip, wip, slopes)
        bs = dsa_sample(qs, kss, vss, qis, kis, wis, cache_k[l], cache_v[l], cache_kidx[l], slopes)
        xp = merge_and_channel_mix(xp, ap, bp, gp, w_branch[l], w_out[l], ln1_g[l], ln1_b[l],
                                   w_ff1[l], b_ff1[l], w_ff2[l], b_ff2[l], ln2_g[l], ln2_b[l])
        xs = merge_and_channel_mix(xs, a_s, bs, gs, w_branch[l], w_out[l], ln1_g[l], ln1_b[l],
                                   w_ff1[l], b_ff1[l], w_ff2[l], b_ff2[l], ln2_g[l], ln2_b[l])
        kp_l.append(kp); vp_l.append(vp); kip_l.append(kip)
        ks_l.append(kss); vs_l.append(vss); kis_l.append(kis); vas_l.append(vas)
    return (xp, xs, jnp.stack(kp_l), jnp.stack(vp_l), jnp.stack(kip_l),
            jnp.stack(ks_l), jnp.stack(vs_l), jnp.stack(kis_l), jnp.stack(vas_l))
```

```python
import functools

import numpy as np
import jax
import jax.numpy as jnp
from jax import lax
from jax.experimental import pallas as pl
from jax.experimental.pallas import tpu as pltpu

CHUNK = 64
A_WIDTH = 512
A_GROUPS = 4
A_GROUP_CH = A_WIDTH // A_GROUPS
A_SPAN = 128
B_HEADS = 8
B_HEAD_DIM = 64
B_KV_HEADS = 2
B_GROUP = B_HEADS // B_KV_HEADS
B_WIDTH = B_HEADS * B_HEAD_DIM
IDX_HEADS = 8
IDX_DIM = 64
TOPK_MAX = 256
Q_BLOCK = 128
LN_EPS = 1e-5

LANES = 128
VMEM_LIMIT = 52 * 1024 * 1024
KEY_CHUNK = 512
BISECT_STEPS = 14

F32 = jnp.float32
BF16 = jnp.bfloat16
NT_DIMS = (((1,), (1,)), ((), ()))


def _dot(a, b):
    return jnp.dot(a, b, preferred_element_type=F32)


def _dot_nt(a, b):
    return lax.dot_general(a, b, NT_DIMS, preferred_element_type=F32)


def _layer_norm(x, g, b):
    mu = jnp.mean(x, axis=-1, keepdims=True)
    xc = x - mu
    var = jnp.mean(xc * xc, axis=-1, keepdims=True)
    return xc * lax.rsqrt(var + LN_EPS) * g + b


def _inproj_kernel(x_ref, wa_ref, wq_ref, wkv_ref, wqi_ref, wkw_ref, lnv_ref, lnk_ref, ws_ref, bs_ref,
                   a_ref, q_ref, qi_ref, k_ref, v_ref, kb_ref, vb_ref, ki_ref, kib_ref, wi_ref, *va_refs,
                   span):
    tm = x_ref.shape[0]
    xb = x_ref[...].astype(BF16)

    za = _dot(xb, wa_ref[...])
    ga = 0.5 * za * (1.0 + lax.erf(za * np.float32(np.sqrt(0.5))))
    u = ga[:, :A_WIDTH]
    va = _layer_norm(ga[:, A_WIDTH:], lnv_ref[0:1, :], lnv_ref[1:2, :])
    if va_refs:
        va_refs[0][...] = va
    vab = va.astype(BF16)
    for s in range(tm // span):
        rows = slice(s * span, (s + 1) * span)
        for g in range(A_GROUPS):
            cols = slice(g * A_GROUP_CH, (g + 1) * A_GROUP_CH)
            mixed = _dot(ws_ref[g], vab[rows, cols]) + bs_ref[:, cols]
            a_ref[rows, cols] = (u[rows, cols] * mixed).astype(BF16)

    zq = _dot(xb, wq_ref[...])
    zqi = _dot(xb, wqi_ref[...])
    for h in range(B_HEADS):
        q_ref[h] = zq[:, h * B_HEAD_DIM:(h + 1) * B_HEAD_DIM].astype(BF16)
    for h in range(IDX_HEADS):
        qi_ref[h] = zqi[:, h * IDX_DIM:(h + 1) * IDX_DIM].astype(BF16)

    zkv = _dot(xb, wkv_ref[...])
    kvw = B_KV_HEADS * B_HEAD_DIM
    k_ref[...] = zkv[:, :kvw]
    v_ref[...] = zkv[:, kvw:]
    for g in range(B_KV_HEADS):
        kb_ref[g] = zkv[:, g * B_HEAD_DIM:(g + 1) * B_HEAD_DIM].astype(BF16)
        vb_ref[g] = zkv[:, kvw + g * B_HEAD_DIM:kvw + (g + 1) * B_HEAD_DIM].astype(BF16)

    zkw = _dot(xb, wkw_ref[...])
    ki = _layer_norm(zkw[:, :IDX_DIM], lnk_ref[0:1, :], lnk_ref[1:2, :])
    ki_ref[...] = ki
    kib_ref[...] = ki.astype(BF16)
    wi_ref[...] = zkw[:, IDX_DIM:IDX_DIM + IDX_HEADS] * np.float32(IDX_HEADS ** -0.5)


def _const_spec(shape):
    nd = len(shape)
    return pl.BlockSpec(shape, lambda *_: (0,) * nd)


def _in_projection(x2, wts, span, tm, emit_va):
    t, d = x2.shape
    wa, wq, wkv, wqi, wkw, lnv, lnk, ws, bs = wts
    grid = (t // tm,)
    row = lambda i: (i, 0)
    hm = lambda i: (0, i, 0)
    in_specs = [pl.BlockSpec((tm, d), row)] + [_const_spec(w.shape) for w in wts]
    out_shape = [
        jax.ShapeDtypeStruct((t, A_WIDTH), BF16),
        jax.ShapeDtypeStruct((B_HEADS, t, B_HEAD_DIM), BF16),
        jax.ShapeDtypeStruct((IDX_HEADS, t, IDX_DIM), BF16),
        jax.ShapeDtypeStruct((t, B_KV_HEADS * B_HEAD_DIM), F32),
        jax.ShapeDtypeStruct((t, B_KV_HEADS * B_HEAD_DIM), F32),
        jax.ShapeDtypeStruct((B_KV_HEADS, t, B_HEAD_DIM), BF16),
        jax.ShapeDtypeStruct((B_KV_HEADS, t, B_HEAD_DIM), BF16),
        jax.ShapeDtypeStruct((t, IDX_DIM), F32),
        jax.ShapeDtypeStruct((t, IDX_DIM), BF16),
        jax.ShapeDtypeStruct((t, IDX_HEADS), F32),
    ]
    out_specs = [
        pl.BlockSpec((tm, A_WIDTH), row),
        pl.BlockSpec((B_HEADS, tm, B_HEAD_DIM), hm),
        pl.BlockSpec((IDX_HEADS, tm, IDX_DIM), hm),
        pl.BlockSpec((tm, B_KV_HEADS * B_HEAD_DIM), row),
        pl.BlockSpec((tm, B_KV_HEADS * B_HEAD_DIM), row),
        pl.BlockSpec((B_KV_HEADS, tm, B_HEAD_DIM), hm),
        pl.BlockSpec((B_KV_HEADS, tm, B_HEAD_DIM), hm),
        pl.BlockSpec((tm, IDX_DIM), row),
        pl.BlockSpec((tm, IDX_DIM), row),
        pl.BlockSpec((tm, IDX_HEADS), row),
    ]
    if emit_va:
        out_shape.append(jax.ShapeDtypeStruct((t, A_WIDTH), F32))
        out_specs.append(pl.BlockSpec((tm, A_WIDTH), row))
    return pl.pallas_call(
        functools.partial(_inproj_kernel, span=span),
        grid=grid, in_specs=in_specs, out_specs=out_specs, out_shape=out_shape,
        compiler_params=pltpu.CompilerParams(dimension_semantics=("parallel",), vmem_limit_bytes=VMEM_LIMIT),
        name="in_projection",
    )(x2, *wts)


def _dsa_kernel(q_ref, qi_ref, wi_ref, k_ref, v_ref, ki_ref, o_ref, s_ref, d_ref,
                *, tq, n_keys, n_valid, q_off, topk, slopes):
    qb = pl.program_id(1)
    neg_inf = np.float32(-np.inf)
    pos_inf = np.float32(np.inf)
    kf = np.float32(topk)
    q_pos = q_off + qb * tq + lax.broadcasted_iota(jnp.int32, (tq, 1), 0)

    qi = qi_ref[...].reshape(IDX_HEADS * tq, IDX_DIM)
    wi = wi_ref[...]
    for c0 in range(0, n_keys, KEY_CHUNK):
        kc = min(KEY_CHUNK, n_keys - c0)
        k_pos = c0 + lax.broadcasted_iota(jnp.int32, (1, kc), 1)
        lg = _dot_nt(qi, ki_ref[0, c0:c0 + kc, :])
        sc = wi[:, 0:1] * jnp.maximum(lg[0:tq], 0.0)
        for h in range(1, IDX_HEADS):
            sc = sc + wi[:, h:h + 1] * jnp.maximum(lg[h * tq:(h + 1) * tq], 0.0)
        k_chunk = k_pos // CHUNK
        if c0 + kc > n_valid:
            k_chunk = jnp.where(k_pos < n_valid, k_chunk, np.int32(2 ** 30))
        adm = k_chunk <= (q_pos // CHUNK)
        s_ref[:, c0:c0 + kc] = jnp.where(adm, sc, neg_inf)
        d_ref[:, c0:c0 + kc] = jnp.abs(q_pos - k_pos).astype(F32)

    def count_ge(t):
        return jnp.sum(jnp.where(s_ref[...] >= t, 1.0, 0.0), axis=1, keepdims=True)

    s = s_ref[...]
    finite = s > neg_inf
    n_adm = jnp.sum(jnp.where(finite, 1.0, 0.0), axis=1, keepdims=True)
    row_max = jnp.max(s, axis=1, keepdims=True)
    row_min = jnp.min(jnp.where(finite, s, pos_inf), axis=1, keepdims=True)

    def bisect(_, carry):
        lo, hb, hiv = carry
        mid = 0.5 * lo + 0.5 * hb
        ge = count_ge(mid) >= kf
        return jnp.where(ge, mid, lo), jnp.where(ge, hb, mid), jnp.where(ge, hiv, mid)

    _, _, hiv = lax.fori_loop(0, BISECT_STEPS, bisect, (row_min, row_max, jnp.full((tq, 1), pos_inf, F32)))

    done = jnp.where(n_adm <= kf, 1.0, 0.0)
    thr = jnp.full((tq, 1), np.finfo(np.float32).min, F32)

    def scan_cond(carry):
        _, _, _, n_left, it = carry
        return jnp.logical_and(n_left > 0.0, it < n_keys)

    def scan_body(carry):
        thr, hiv, done, _, it = carry
        sv = s_ref[...]
        cand = jnp.max(jnp.where(sv < hiv, sv, neg_inf), axis=1, keepdims=True)
        found = jnp.where(done > 0.5, 0.0, jnp.where(count_ge(cand) >= kf, 1.0, 0.0))
        thr = jnp.where(found > 0.5, cand, thr)
        done = jnp.maximum(done, found)
        hiv = jnp.where(done > 0.5, hiv, cand)
        return thr, hiv, done, jnp.sum(1.0 - done), it + 1

    thr, _, _, _, _ = lax.while_loop(scan_cond, scan_body, (thr, hiv, done, jnp.sum(1.0 - done), jnp.int32(0)))

    n_ge = count_ge(thr)
    tied = jnp.max(jnp.where(n_adm > kf, n_ge, 0.0)) > kf

    @pl.when(jnp.logical_not(tied))
    def _():
        d_ref[...] = jnp.where(s_ref[...] >= thr, d_ref[...], pos_inf)

    @pl.when(tied)
    def _():
        n_gt = jnp.sum(jnp.where(s_ref[...] > thr, 1.0, 0.0), axis=1, keepdims=True)
        room = kf - n_gt
        tri = (lax.broadcasted_iota(jnp.int32, (LANES, LANES), 0)
               <= lax.broadcasted_iota(jnp.int32, (LANES, LANES), 1))
        tri = jnp.where(tri, 1.0, 0.0).astype(BF16)
        run = jnp.zeros((tq, 1), F32)
        for c0 in range(0, n_keys, LANES):
            blk = s_ref[:, c0:c0 + LANES]
            eq = jnp.where(blk == thr, 1.0, 0.0)
            prefix = _dot(eq.astype(BF16), tri) + run
            keep = jnp.where(blk > thr, 1.0, jnp.where(prefix <= room, eq, 0.0))
            d_ref[:, c0:c0 + LANES] = jnp.where(keep > 0.5, d_ref[:, c0:c0 + LANES], pos_inf)
            run = run + jnp.sum(eq, axis=1, keepdims=True)

    dist = d_ref[...]
    for g in range(B_KV_HEADS):
        qg = q_ref[g * B_GROUP:(g + 1) * B_GROUP].reshape(B_GROUP * tq, B_HEAD_DIM)
        logits = _dot_nt(qg, k_ref[g, 0])
        ps, ls = [], []
        for hh in range(B_GROUP):
            lgt = logits[hh * tq:(hh + 1) * tq] - np.float32(slopes[g * B_GROUP + hh]) * dist
            m = jnp.max(lgt, axis=1, keepdims=True)
            p = jnp.exp(lgt - m)
            ls.append(jnp.sum(p, axis=1, keepdims=True))
            ps.append(p.astype(BF16))
        og = _dot(jnp.concatenate(ps, axis=0), v_ref[g, 0])
        for hh in range(B_GROUP):
            h = g * B_GROUP + hh
            o_ref[:, h * B_HEAD_DIM:(h + 1) * B_HEAD_DIM] = (og[hh * tq:(hh + 1) * tq] / ls[hh]).astype(BF16)


def _dsa_attention(q_hm, qi_hm, wi, kb, vb, kib, *, n, t_q, tq, n_valid, q_off, topk, slopes):
    n_keys = kb.shape[2]
    nqb = t_q // tq
    qmap = lambda b, j: (0, b * nqb + j, 0)
    rmap = lambda b, j: (b * nqb + j, 0)
    in_specs = [
        pl.BlockSpec((B_HEADS, tq, B_HEAD_DIM), qmap),
        pl.BlockSpec((IDX_HEADS, tq, IDX_DIM), qmap),
        pl.BlockSpec((tq, IDX_HEADS), rmap),
        pl.BlockSpec((B_KV_HEADS, 1, n_keys, B_HEAD_DIM), lambda b, j: (0, b, 0, 0)),
        pl.BlockSpec((B_KV_HEADS, 1, n_keys, B_HEAD_DIM), lambda b, j: (0, b, 0, 0)),
        pl.BlockSpec((1, n_keys, IDX_DIM), lambda b, j: (b, 0, 0)),
    ]
    return pl.pallas_call(
        functools.partial(_dsa_kernel, tq=tq, n_keys=n_keys, n_valid=n_valid, q_off=q_off, topk=topk,
                          slopes=slopes),
        grid=(n, nqb), in_specs=in_specs,
        out_specs=pl.BlockSpec((tq, B_WIDTH), rmap),
        out_shape=jax.ShapeDtypeStruct((n * t_q, B_WIDTH), BF16),
        scratch_shapes=[pltpu.VMEM((tq, n_keys), F32), pltpu.VMEM((tq, n_keys), F32)],
        compiler_params=pltpu.CompilerParams(dimension_semantics=("parallel", "arbitrary"),
                                             vmem_limit_bytes=VMEM_LIMIT),
        name="dsa_attention",
    )(q_hm, qi_hm, wi, kb, vb, kib)


def _merge_kernel(x_ref, a_ref, b_ref, wg_ref, wb_ref, wo_ref, ln1_ref, w1_ref, b1_ref, w2_ref, b2_ref, ln2_ref,
                  y_ref, *, alpha):
    d = x_ref.shape[1]
    x = x_ref[...]
    gates = jax.nn.sigmoid(_dot(x.astype(BF16), wg_ref[...]))
    m = gates[:, :d] * _dot(a_ref[...], wb_ref[0]) + gates[:, d:] * _dot(b_ref[...], wb_ref[1])
    h = _layer_norm(alpha * x + _dot(m.astype(BF16), wo_ref[...]), ln1_ref[0:1, :], ln1_ref[1:2, :])
    f = jnp.square(jnp.maximum(_dot(h.astype(BF16), w1_ref[...]) + b1_ref[...], 0.0))
    f = _dot(f.astype(BF16), w2_ref[...]) + b2_ref[...]
    y_ref[...] = _layer_norm(alpha * h + f, ln2_ref[0:1, :], ln2_ref[1:2, :])


def _resident_spec(shape):
    nd = len(shape)
    return pl.BlockSpec(shape, lambda *_: (0,) * nd, pipeline_mode=pl.Buffered(1))


def _merge_ffn(x2, a, b, wts, tm, alpha):
    t, d = x2.shape
    row = lambda i: (i, 0)
    in_specs = [pl.BlockSpec((tm, d), row), pl.BlockSpec((tm, A_WIDTH), row), pl.BlockSpec((tm, B_WIDTH), row)]
    in_specs += [_resident_spec(w.shape) for w in wts]
    return pl.pallas_call(
        functools.partial(_merge_kernel, alpha=np.float32(alpha)),
        grid=(t // tm,), in_specs=in_specs,
        out_specs=pl.BlockSpec((tm, d), row),
        out_shape=jax.ShapeDtypeStruct((t, d), F32),
        compiler_params=pltpu.CompilerParams(dimension_semantics=("parallel",), vmem_limit_bytes=VMEM_LIMIT),
        name="merge_ffn",
    )(x2, a, b, *wts)


def _spatial_weights(w_s, b_s, span):
    pos = jnp.arange(A_SPAN)
    mask = (pos[None, :] // CHUNK) <= (pos[:, None] // CHUNK)
    ws = jnp.where(mask[None], w_s, 0.0)[:, :span, :span].astype(BF16)
    bs = jnp.repeat(b_s[:, :span].T, A_GROUP_CH, axis=1)
    return ws, bs


def kernel(x_prompt, x_sample, cache_k, cache_v, cache_kidx, w_in, lnv_g, lnv_b, w_s, b_s, lnk_g, lnk_b,
           w_branch, w_out, ln1_g, ln1_b, w_ff1, b_ff1, w_ff2, b_ff2, ln2_g, ln2_b):
    depth = w_in.shape[0]
    n_p, s_p, d = x_prompt.shape
    n_s, t_s, _ = x_sample.shape
    past = cache_k.shape[2]
    alpha = (2 * depth) ** 0.25
    slopes = tuple(float(2.0 ** (-8.0 * h / B_HEADS)) for h in range(1, B_HEADS + 1))
    kvw = B_KV_HEADS * B_HEAD_DIM
    c_a = 2 * A_WIDTH
    c_q = c_a + B_WIDTH
    c_k = c_q + kvw
    c_v = c_k + kvw
    c_qi = c_v + IDX_HEADS * IDX_DIM
    c_wi = c_qi + IDX_DIM + IDX_HEADS

    xp = x_prompt.reshape(n_p * s_p, d)
    xs = x_sample.reshape(n_s * t_s, d)
    outs = [[] for _ in range(7)]
    for l in range(depth):
        w = w_in[l]
        pad = jnp.zeros((d, LANES - (c_wi - c_qi)), F32)
        proj_w = (
            w[:, :c_a].astype(BF16),
            (w[:, c_a:c_q] * (B_HEAD_DIM ** -0.5)).astype(BF16),
            w[:, c_q:c_v].astype(BF16),
            (w[:, c_v:c_qi] * (IDX_DIM ** -0.5)).astype(BF16),
            jnp.concatenate([w[:, c_qi:c_wi], pad], axis=1).astype(BF16),
            jnp.stack([lnv_g[l], lnv_b[l]]),
            jnp.stack([lnk_g[l], lnk_b[l]]),
        )
        merge_w = (
            w[:, c_wi:].astype(BF16),
            w_branch[l].astype(BF16),
            w_out[l].astype(BF16),
            jnp.stack([ln1_g[l], ln1_b[l]]),
            w_ff1[l].astype(BF16),
            b_ff1[l][None, :],
            w_ff2[l].astype(BF16),
            b_ff2[l][None, :],
            jnp.stack([ln2_g[l], ln2_b[l]]),
        )

        a_p, q_p, qi_p, k_p, v_p, kb_p, vb_p, ki_p, kib_p, wi_p = _in_projection(
            xp, proj_w + _spatial_weights(w_s[l], b_s[l], A_SPAN), A_SPAN, 512, False)
        b_p = _dsa_attention(
            q_p, qi_p, wi_p,
            kb_p.reshape(B_KV_HEADS, n_p, s_p, B_HEAD_DIM), vb_p.reshape(B_KV_HEADS, n_p, s_p, B_HEAD_DIM),
            kib_p.reshape(n_p, s_p, IDX_DIM),
            n=n_p, t_q=s_p, tq=Q_BLOCK, n_valid=s_p, q_off=0, topk=min(TOPK_MAX, s_p // 4), slopes=slopes)
        xp = _merge_ffn(xp, a_p, b_p, merge_w, 512, alpha)

        a_s, q_s, qi_s, k_s, v_s, kb_s, vb_s, ki_s, kib_s, wi_s, va_s = _in_projection(
            xs, proj_w + _spatial_weights(w_s[l], b_s[l], t_s), t_s, n_s * t_s, True)
        n_all = past + t_s
        n_keys = -(-n_all // LANES) * LANES

        def with_cache(cache, new):
            c = jnp.moveaxis(cache.astype(BF16), 2, 0)
            new = new.reshape(B_KV_HEADS, n_s, t_s, B_HEAD_DIM)
            zero = jnp.zeros((B_KV_HEADS, n_s, n_keys - n_all, B_HEAD_DIM), BF16)
            return jnp.concatenate([c, new, zero], axis=2)

        kib_all = jnp.concatenate([cache_kidx[l].astype(BF16), kib_s.reshape(n_s, t_s, IDX_DIM),
                                   jnp.zeros((n_s, n_keys - n_all, IDX_DIM), BF16)], axis=1)
        b_s_ = _dsa_attention(
            q_s, qi_s, wi_s, with_cache(cache_k[l], kb_s), with_cache(cache_v[l], vb_s), kib_all,
            n=n_s, t_q=t_s, tq=t_s, n_valid=n_all, q_off=past, topk=min(TOPK_MAX, n_all // 4), slopes=slopes)
        xs = _merge_ffn(xs, a_s, b_s_, merge_w, n_s * t_s, alpha)

        outs[0].append(k_p.reshape(n_p, s_p, B_KV_HEADS, B_HEAD_DIM))
        outs[1].append(v_p.reshape(n_p, s_p, B_KV_HEADS, B_HEAD_DIM))
        outs[2].append(ki_p.reshape(n_p, s_p, IDX_DIM))
        outs[3].append(k_s.reshape(n_s, t_s, B_KV_HEADS, B_HEAD_DIM))
        outs[4].append(v_s.reshape(n_s, t_s, B_KV_HEADS, B_HEAD_DIM))
        outs[5].append(ki_s.reshape(n_s, t_s, IDX_DIM))
        outs[6].append(va_s.reshape(n_s, t_s, A_WIDTH))

    return (xp.reshape(n_p, s_p, d), xs.reshape(n_s, t_s, d)) + tuple(jnp.stack(o) for o in outs)
```

```python
import functools

import numpy as np
import jax
import jax.numpy as jnp
from jax import lax
from jax.experimental import pallas as pl
from jax.experimental.pallas import tpu as pltpu

CHUNK = 64
A_WIDTH = 512
A_GROUPS = 4
A_GROUP_CH = A_WIDTH // A_GROUPS
A_SPAN = 128
B_HEADS = 8
B_HEAD_DIM = 64
B_KV_HEADS = 2
B_GROUP = B_HEADS // B_KV_HEADS
B_WIDTH = B_HEADS * B_HEAD_DIM
IDX_HEADS = 8
IDX_DIM = 64
TOPK_MAX = 256
Q_BLOCK = 128
LN_EPS = 1e-5

LANES = 128
VMEM_LIMIT = 52 * 1024 * 1024
KEY_CHUNK = 512
BISECT_STEPS = 14

F32 = jnp.float32
BF16 = jnp.bfloat16
NT_DIMS = (((1,), (1,)), ((), ()))


def _dot(a, b):
    return jnp.dot(a, b, preferred_element_type=F32)


def _dot_nt(a, b):
    return lax.dot_general(a, b, NT_DIMS, preferred_element_type=F32)


def _layer_norm(x, g, b):
    mu = jnp.mean(x, axis=-1, keepdims=True)
    xc = x - mu
    var = jnp.mean(xc * xc, axis=-1, keepdims=True)
    return xc * lax.rsqrt(var + LN_EPS) * g + b


def _inproj_kernel(x_ref, wa_ref, wq_ref, wkv_ref, wqi_ref, wkw_ref, lnv_ref, lnk_ref, ws_ref, bs_ref,
                   a_ref, q_ref, qi_ref, k_ref, v_ref, kb_ref, vb_ref, ki_ref, kib_ref, wi_ref, *va_refs,
                   span):
    tm = x_ref.shape[0]
    xb = x_ref[...].astype(BF16)

    za = _dot(xb, wa_ref[...])
    ga = 0.5 * za * (1.0 + lax.erf(za * np.float32(np.sqrt(0.5))))
    u = ga[:, :A_WIDTH]
    va = _layer_norm(ga[:, A_WIDTH:], lnv_ref[0:1, :], lnv_ref[1:2, :])
    if va_refs:
        va_refs[0][...] = va
    vab = va.astype(BF16)
    for s in range(tm // span):
        rows = slice(s * span, (s + 1) * span)
        for g in range(A_GROUPS):
            cols = slice(g * A_GROUP_CH, (g + 1) * A_GROUP_CH)
            mixed = _dot(ws_ref[g], vab[rows, cols]) + bs_ref[:, cols]
            a_ref[rows, cols] = (u[rows, cols] * mixed).astype(BF16)

    zq = _dot(xb, wq_ref[...])
    zqi = _dot(xb, wqi_ref[...])
    for h in range(B_HEADS):
        q_ref[h] = zq[:, h * B_HEAD_DIM:(h + 1) * B_HEAD_DIM].astype(BF16)
    for h in range(IDX_HEADS):
        qi_ref[h] = zqi[:, h * IDX_DIM:(h + 1) * IDX_DIM].astype(BF16)

    zkv = _dot(xb, wkv_ref[...])
    kvw = B_KV_HEADS * B_HEAD_DIM
    k_ref[...] = zkv[:, :kvw]
    v_ref[...] = zkv[:, kvw:]
    if len(kb_ref.shape) == 4:
        kb_ref, vb_ref, kib_ref = kb_ref.at[:, 0], vb_ref.at[:, 0], kib_ref.at[0]
    for g in range(B_KV_HEADS):
        kb_ref[g] = zkv[:, g * B_HEAD_DIM:(g + 1) * B_HEAD_DIM].astype(BF16)
        vb_ref[g] = zkv[:, kvw + g * B_HEAD_DIM:kvw + (g + 1) * B_HEAD_DIM].astype(BF16)

    zkw = _dot(xb, wkw_ref[...])
    ki = _layer_norm(zkw[:, :IDX_DIM], lnk_ref[0:1, :], lnk_ref[1:2, :])
    ki_ref[...] = ki
    kib_ref[...] = ki.astype(BF16)
    wi_ref[...] = zkw[:, IDX_DIM:IDX_DIM + IDX_HEADS] * np.float32(IDX_HEADS ** -0.5)


def _const_spec(shape):
    nd = len(shape)
    return pl.BlockSpec(shape, lambda *_: (0,) * nd)


def _in_projection(x2, wts, span, tm, emit_va, batch_rows=None):
    t, d = x2.shape
    grid = (t // tm,)
    row = lambda i: (i, 0)
    hm = lambda i: (0, i, 0)
    if batch_rows is None:
        kv_shape, kv_block, kv_map = (B_KV_HEADS, t, B_HEAD_DIM), (B_KV_HEADS, tm, B_HEAD_DIM), hm
        ki_shape, ki_block, ki_map = (t, IDX_DIM), (tm, IDX_DIM), row
    else:
        per = batch_rows // tm
        kv_shape = (B_KV_HEADS, t // batch_rows, batch_rows, B_HEAD_DIM)
        kv_block, kv_map = (B_KV_HEADS, 1, tm, B_HEAD_DIM), lambda i: (0, i // per, i % per, 0)
        ki_shape = (t // batch_rows, batch_rows, IDX_DIM)
        ki_block, ki_map = (1, tm, IDX_DIM), lambda i: (i // per, i % per, 0)
    in_specs = [pl.BlockSpec((tm, d), row)] + [_const_spec(w.shape) for w in wts]
    out_shape = [
        jax.ShapeDtypeStruct((t, A_WIDTH), BF16),
        jax.ShapeDtypeStruct((B_HEADS, t, B_HEAD_DIM), BF16),
        jax.ShapeDtypeStruct((IDX_HEADS, t, IDX_DIM), BF16),
        jax.ShapeDtypeStruct((t, B_KV_HEADS * B_HEAD_DIM), F32),
        jax.ShapeDtypeStruct((t, B_KV_HEADS * B_HEAD_DIM), F32),
        jax.ShapeDtypeStruct(kv_shape, BF16),
        jax.ShapeDtypeStruct(kv_shape, BF16),
        jax.ShapeDtypeStruct((t, IDX_DIM), F32),
        jax.ShapeDtypeStruct(ki_shape, BF16),
        jax.ShapeDtypeStruct((t, IDX_HEADS), F32),
    ]
    out_specs = [
        pl.BlockSpec((tm, A_WIDTH), row),
        pl.BlockSpec((B_HEADS, tm, B_HEAD_DIM), hm),
        pl.BlockSpec((IDX_HEADS, tm, IDX_DIM), hm),
        pl.BlockSpec((tm, B_KV_HEADS * B_HEAD_DIM), row),
        pl.BlockSpec((tm, B_KV_HEADS * B_HEAD_DIM), row),
        pl.BlockSpec(kv_block, kv_map),
        pl.BlockSpec(kv_block, kv_map),
        pl.BlockSpec((tm, IDX_DIM), row),
        pl.BlockSpec(ki_block, ki_map),
        pl.BlockSpec((tm, IDX_HEADS), row),
    ]
    if emit_va:
        out_shape.append(jax.ShapeDtypeStruct((t, A_WIDTH), F32))
        out_specs.append(pl.BlockSpec((tm, A_WIDTH), row))
    return pl.pallas_call(
        functools.partial(_inproj_kernel, span=span),
        grid=grid, in_specs=in_specs, out_specs=out_specs, out_shape=out_shape,
        compiler_params=pltpu.CompilerParams(dimension_semantics=("parallel",), vmem_limit_bytes=VMEM_LIMIT),
        name="in_projection",
    )(x2, *wts)


def _dsa_block(q_ref, qi_ref, wi_ref, k_ref, v_ref, ki_ref, o_ref, s_ref, d_ref,
               *, qb, tq, n_keys, n_valid, q_off, topk, slopes):
    neg_inf = np.float32(-np.inf)
    pos_inf = np.float32(np.inf)
    kf = np.float32(topk)
    q_pos = q_off + qb * tq + lax.broadcasted_iota(jnp.int32, (tq, 1), 0)

    qi = qi_ref[...].reshape(IDX_HEADS * tq, IDX_DIM)
    wi = wi_ref[...]
    for c0 in range(0, n_keys, KEY_CHUNK):
        kc = min(KEY_CHUNK, n_keys - c0)
        k_pos = c0 + lax.broadcasted_iota(jnp.int32, (1, kc), 1)
        lg = _dot_nt(qi, ki_ref[0, c0:c0 + kc, :])
        sc = wi[:, 0:1] * jnp.maximum(lg[0:tq], 0.0)
        for h in range(1, IDX_HEADS):
            sc = sc + wi[:, h:h + 1] * jnp.maximum(lg[h * tq:(h + 1) * tq], 0.0)
        k_chunk = k_pos // CHUNK
        if c0 + kc > n_valid:
            k_chunk = jnp.where(k_pos < n_valid, k_chunk, np.int32(2 ** 30))
        adm = k_chunk <= (q_pos // CHUNK)
        s_ref[:, c0:c0 + kc] = jnp.where(adm, sc, neg_inf)
        d_ref[:, c0:c0 + kc] = jnp.abs(q_pos - k_pos).astype(F32)

    def count_ge(t):
        return jnp.sum(jnp.where(s_ref[...] >= t, 1.0, 0.0), axis=1, keepdims=True)

    s = s_ref[...]
    finite = s > neg_inf
    n_adm = jnp.sum(jnp.where(finite, 1.0, 0.0), axis=1, keepdims=True)
    row_max = jnp.max(s, axis=1, keepdims=True)
    row_min = jnp.min(jnp.where(finite, s, pos_inf), axis=1, keepdims=True)

    def bisect(_, carry):
        lo, hb, hiv = carry
        mid = 0.5 * lo + 0.5 * hb
        ge = count_ge(mid) >= kf
        return jnp.where(ge, mid, lo), jnp.where(ge, hb, mid), jnp.where(ge, hiv, mid)

    _, _, hiv = lax.fori_loop(0, BISECT_STEPS, bisect, (row_min, row_max, jnp.full((tq, 1), pos_inf, F32)))

    done = jnp.where(n_adm <= kf, 1.0, 0.0)
    thr = jnp.full((tq, 1), np.finfo(np.float32).min, F32)

    def scan_cond(carry):
        _, _, _, n_left, it = carry
        return jnp.logical_and(n_left > 0.0, it < n_keys)

    def scan_body(carry):
        thr, hiv, done, _, it = carry
        sv = s_ref[...]
        cand = jnp.max(jnp.where(sv < hiv, sv, neg_inf), axis=1, keepdims=True)
        found = jnp.where(done > 0.5, 0.0, jnp.where(count_ge(cand) >= kf, 1.0, 0.0))
        thr = jnp.where(found > 0.5, cand, thr)
        done = jnp.maximum(done, found)
        hiv = jnp.where(done > 0.5, hiv, cand)
        return thr, hiv, done, jnp.sum(1.0 - done), it + 1

    thr, _, _, _, _ = lax.while_loop(scan_cond, scan_body, (thr, hiv, done, jnp.sum(1.0 - done), jnp.int32(0)))

    n_ge = count_ge(thr)
    tied = jnp.max(jnp.where(n_adm > kf, n_ge, 0.0)) > kf

    @pl.when(jnp.logical_not(tied))
    def _():
        d_ref[...] = jnp.where(s_ref[...] >= thr, d_ref[...], pos_inf)

    @pl.when(tied)
    def _():
        n_gt = jnp.sum(jnp.where(s_ref[...] > thr, 1.0, 0.0), axis=1, keepdims=True)
        room = kf - n_gt
        tri = (lax.broadcasted_iota(jnp.int32, (LANES, LANES), 0)
               <= lax.broadcasted_iota(jnp.int32, (LANES, LANES), 1))
        tri = jnp.where(tri, 1.0, 0.0).astype(BF16)
        run = jnp.zeros((tq, 1), F32)
        for c0 in range(0, n_keys, LANES):
            blk = s_ref[:, c0:c0 + LANES]
            eq = jnp.where(blk == thr, 1.0, 0.0)
            prefix = _dot(eq.astype(BF16), tri) + run
            keep = jnp.where(blk > thr, 1.0, jnp.where(prefix <= room, eq, 0.0))
            d_ref[:, c0:c0 + LANES] = jnp.where(keep > 0.5, d_ref[:, c0:c0 + LANES], pos_inf)
            run = run + jnp.sum(eq, axis=1, keepdims=True)

    dist = d_ref[...]
    for g in range(B_KV_HEADS):
        qg = q_ref[g * B_GROUP:(g + 1) * B_GROUP].reshape(B_GROUP * tq, B_HEAD_DIM)
        logits = _dot_nt(qg, k_ref[g, 0])
        ps, ls = [], []
        for hh in range(B_GROUP):
            lgt = logits[hh * tq:(hh + 1) * tq] - np.float32(slopes[g * B_GROUP + hh]) * dist
            m = jnp.max(lgt, axis=1, keepdims=True)
            p = jnp.exp(lgt - m)
            ls.append(jnp.sum(p, axis=1, keepdims=True))
            ps.append(p.astype(BF16))
        og = _dot(jnp.concatenate(ps, axis=0), v_ref[g, 0])
        for hh in range(B_GROUP):
            h = g * B_GROUP + hh
            o_ref[:, h * B_HEAD_DIM:(h + 1) * B_HEAD_DIM] = (og[hh * tq:(hh + 1) * tq] / ls[hh]).astype(BF16)


def _dsa_kernel(q_ref, qi_ref, wi_ref, k_ref, v_ref, ki_ref, o_ref, s_ref, d_ref,
                *, tq, key_buckets, n_valid, q_off, **static):
    qb = pl.program_id(1)
    last_chunk = (q_off + (qb + 1) * tq - 1) // CHUNK
    need = jnp.minimum((last_chunk + 1) * CHUNK, n_valid)
    lower = 0
    for nk in key_buckets:
        body = functools.partial(
            _dsa_block, q_ref, qi_ref, wi_ref, k_ref.at[:, :, :nk, :], v_ref.at[:, :, :nk, :],
            ki_ref.at[:, :nk, :], o_ref, s_ref.at[:, :nk], d_ref.at[:, :nk],
            qb=qb, tq=tq, n_keys=nk, n_valid=n_valid, q_off=q_off, **static)
        if len(key_buckets) == 1:
            body()
        else:
            pl.when(jnp.logical_and(need > lower, need <= nk))(body)
        lower = nk


def _dsa_attention(q_hm, qi_hm, wi, kb, vb, kib, *, n, t_q, tq, key_buckets, n_valid, q_off, topk, slopes):
    n_keys = kb.shape[2]
    nqb = t_q // tq
    qmap = lambda b, j: (0, b * nqb + j, 0)
    rmap = lambda b, j: (b * nqb + j, 0)
    in_specs = [
        pl.BlockSpec((B_HEADS, tq, B_HEAD_DIM), qmap),
        pl.BlockSpec((IDX_HEADS, tq, IDX_DIM), qmap),
        pl.BlockSpec((tq, IDX_HEADS), rmap),
        pl.BlockSpec((B_KV_HEADS, 1, n_keys, B_HEAD_DIM), lambda b, j: (0, b, 0, 0)),
        pl.BlockSpec((B_KV_HEADS, 1, n_keys, B_HEAD_DIM), lambda b, j: (0, b, 0, 0)),
        pl.BlockSpec((1, n_keys, IDX_DIM), lambda b, j: (b, 0, 0)),
    ]
    return pl.pallas_call(
        functools.partial(_dsa_kernel, tq=tq, key_buckets=key_buckets, n_valid=n_valid, q_off=q_off, topk=topk,
                          slopes=slopes),
        grid=(n, nqb), in_specs=in_specs,
        out_specs=pl.BlockSpec((tq, B_WIDTH), rmap),
        out_shape=jax.ShapeDtypeStruct((n * t_q, B_WIDTH), BF16),
        scratch_shapes=[pltpu.VMEM((tq, n_keys), F32), pltpu.VMEM((tq, n_keys), F32)],
        compiler_params=pltpu.CompilerParams(dimension_semantics=("parallel", "arbitrary"),
                                             vmem_limit_bytes=VMEM_LIMIT),
        name="dsa_attention",
    )(q_hm, qi_hm, wi, kb, vb, kib)


def _merge_kernel(x_ref, a_ref, b_ref, wg_ref, wb_ref, wo_ref, ln1_ref, w1_ref, b1_ref, w2_ref, b2_ref, ln2_ref,
                  y_ref, *, alpha):
    d = x_ref.shape[1]
    x = x_ref[...]
    gates = jax.nn.sigmoid(_dot(x.astype(BF16), wg_ref[...]))
    m = gates[:, :d] * _dot(a_ref[...], wb_ref[0]) + gates[:, d:] * _dot(b_ref[...], wb_ref[1])
    h = _layer_norm(alpha * x + _dot(m.astype(BF16), wo_ref[...]), ln1_ref[0:1, :], ln1_ref[1:2, :])
    f = jnp.square(jnp.maximum(_dot(h.astype(BF16), w1_ref[...]) + b1_ref[...], 0.0))
    f = _dot(f.astype(BF16), w2_ref[...]) + b2_ref[...]
    y_ref[...] = _layer_norm(alpha * h + f, ln2_ref[0:1, :], ln2_ref[1:2, :])


def _resident_spec(shape):
    nd = len(shape)
    return pl.BlockSpec(shape, lambda *_: (0,) * nd, pipeline_mode=pl.Buffered(1))


def _merge_ffn(x2, a, b, wts, tm, alpha):
    t, d = x2.shape
    row = lambda i: (i, 0)
    in_specs = [pl.BlockSpec((tm, d), row), pl.BlockSpec((tm, A_WIDTH), row), pl.BlockSpec((tm, B_WIDTH), row)]
    in_specs += [_resident_spec(w.shape) for w in wts]
    return pl.pallas_call(
        functools.partial(_merge_kernel, alpha=np.float32(alpha)),
        grid=(t // tm,), in_specs=in_specs,
        out_specs=pl.BlockSpec((tm, d), row),
        out_shape=jax.ShapeDtypeStruct((t, d), F32),
        compiler_params=pltpu.CompilerParams(dimension_semantics=("parallel",), vmem_limit_bytes=VMEM_LIMIT),
        name="merge_ffn",
    )(x2, a, b, *wts)


def _key_buckets(n_keys):
    step = 4 * Q_BLOCK
    return tuple(range(step, n_keys, step)) + (n_keys,)


def _spatial_weights(w_s, b_s, span):
    pos = jnp.arange(A_SPAN)
    mask = (pos[None, :] // CHUNK) <= (pos[:, None] // CHUNK)
    ws = jnp.where(mask[None], w_s, 0.0)[:, :span, :span].astype(BF16)
    bs = jnp.repeat(b_s[:, :span].T, A_GROUP_CH, axis=1)
    return ws, bs


def kernel(x_prompt, x_sample, cache_k, cache_v, cache_kidx, w_in, lnv_g, lnv_b, w_s, b_s, lnk_g, lnk_b,
           w_branch, w_out, ln1_g, ln1_b, w_ff1, b_ff1, w_ff2, b_ff2, ln2_g, ln2_b):
    depth = w_in.shape[0]
    n_p, s_p, d = x_prompt.shape
    n_s, t_s, _ = x_sample.shape
    past = cache_k.shape[2]
    alpha = (2 * depth) ** 0.25
    slopes = tuple(float(2.0 ** (-8.0 * h / B_HEADS)) for h in range(1, B_HEADS + 1))
    kvw = B_KV_HEADS * B_HEAD_DIM
    c_a = 2 * A_WIDTH
    c_q = c_a + B_WIDTH
    c_k = c_q + kvw
    c_v = c_k + kvw
    c_qi = c_v + IDX_HEADS * IDX_DIM
    c_wi = c_qi + IDX_DIM + IDX_HEADS

    xp = x_prompt.reshape(n_p * s_p, d)
    xs = x_sample.reshape(n_s * t_s, d)
    outs = [[] for _ in range(7)]
    for l in range(depth):
        w = w_in[l]
        pad = jnp.zeros((d, LANES - (c_wi - c_qi)), F32)
        proj_w = (
            w[:, :c_a].astype(BF16),
            (w[:, c_a:c_q] * (B_HEAD_DIM ** -0.5)).astype(BF16),
            w[:, c_q:c_v].astype(BF16),
            (w[:, c_v:c_qi] * (IDX_DIM ** -0.5)).astype(BF16),
            jnp.concatenate([w[:, c_qi:c_wi], pad], axis=1).astype(BF16),
            jnp.stack([lnv_g[l], lnv_b[l]]),
            jnp.stack([lnk_g[l], lnk_b[l]]),
        )
        merge_w = (
            w[:, c_wi:].astype(BF16),
            w_branch[l].astype(BF16),
            w_out[l].astype(BF16),
            jnp.stack([ln1_g[l], ln1_b[l]]),
            w_ff1[l].astype(BF16),
            b_ff1[l][None, :],
            w_ff2[l].astype(BF16),
            b_ff2[l][None, :],
            jnp.stack([ln2_g[l], ln2_b[l]]),
        )

        a_p, q_p, qi_p, k_p, v_p, kb_p, vb_p, ki_p, kib_p, wi_p = _in_projection(
            xp, proj_w + _spatial_weights(w_s[l], b_s[l], A_SPAN), A_SPAN, 512, False, batch_rows=s_p)
        b_p = _dsa_attention(
            q_p, qi_p, wi_p, kb_p, vb_p, kib_p,
            n=n_p, t_q=s_p, tq=Q_BLOCK, key_buckets=_key_buckets(s_p), n_valid=s_p, q_off=0,
            topk=min(TOPK_MAX, s_p // 4), slopes=slopes)
        xp = _merge_ffn(xp, a_p, b_p, merge_w, 512, alpha)

        a_s, q_s, qi_s, k_s, v_s, kb_s, vb_s, ki_s, kib_s, wi_s, va_s = _in_projection(
            xs, proj_w + _spatial_weights(w_s[l], b_s[l], t_s), t_s, n_s * t_s, True)
        n_all = past + t_s
        n_keys = -(-n_all // LANES) * LANES

        def with_cache(cache, new):
            c = jnp.moveaxis(cache.astype(BF16), 2, 0)
            new = new.reshape(B_KV_HEADS, n_s, t_s, B_HEAD_DIM)
            zero = jnp.zeros((B_KV_HEADS, n_s, n_keys - n_all, B_HEAD_DIM), BF16)
            return jnp.concatenate([c, new, zero], axis=2)

        kib_all = jnp.concatenate([cache_kidx[l].astype(BF16), kib_s.reshape(n_s, t_s, IDX_DIM),
                                   jnp.zeros((n_s, n_keys - n_all, IDX_DIM), BF16)], axis=1)
        b_s_ = _dsa_attention(
            q_s, qi_s, wi_s, with_cache(cache_k[l], kb_s), with_cache(cache_v[l], vb_s), kib_all,
            n=n_s, t_q=t_s, tq=t_s, key_buckets=(n_keys,), n_valid=n_all, q_off=past,
            topk=min(TOPK_MAX, n_all // 4), slopes=slopes)
        xs = _merge_ffn(xs, a_s, b_s_, merge_w, n_s * t_s, alpha)

        outs[0].append(k_p.reshape(n_p, s_p, B_KV_HEADS, B_HEAD_DIM))
        outs[1].append(v_p.reshape(n_p, s_p, B_KV_HEADS, B_HEAD_DIM))
        outs[2].append(ki_p.reshape(n_p, s_p, IDX_DIM))
        outs[3].append(k_s.reshape(n_s, t_s, B_KV_HEADS, B_HEAD_DIM))
        outs[4].append(v_s.reshape(n_s, t_s, B_KV_HEADS, B_HEAD_DIM))
        outs[5].append(ki_s.reshape(n_s, t_s, IDX_DIM))
        outs[6].append(va_s.reshape(n_s, t_s, A_WIDTH))

    return (xp.reshape(n_p, s_p, d), xs.reshape(n_s, t_s, d)) + tuple(jnp.stack(o) for o in outs)
```

```python
import functools

import numpy as np
import jax
import jax.numpy as jnp
from jax import lax
from jax.experimental import pallas as pl
from jax.experimental.pallas import tpu as pltpu

CHUNK = 64
A_WIDTH = 512
A_GROUPS = 4
A_GROUP_CH = A_WIDTH // A_GROUPS
A_SPAN = 128
B_HEADS = 8
B_HEAD_DIM = 64
B_KV_HEADS = 2
B_GROUP = B_HEADS // B_KV_HEADS
B_WIDTH = B_HEADS * B_HEAD_DIM
IDX_HEADS = 8
IDX_DIM = 64
TOPK_MAX = 256
Q_BLOCK = 128
LN_EPS = 1e-5

LANES = 128
VMEM_LIMIT = 52 * 1024 * 1024
TOKEN_TILE = 512
KEY_CHUNK = 512
DSA_ROWS = 4 * Q_BLOCK
BISECT_STEPS = 14

F32 = jnp.float32
BF16 = jnp.bfloat16
NT_DIMS = (((1,), (1,)), ((), ()))


def _dot(a, b):
    return jnp.dot(a, b, preferred_element_type=F32)


def _dot_nt(a, b):
    return lax.dot_general(a, b, NT_DIMS, preferred_element_type=F32)


def _layer_norm(x, g, b):
    mu = jnp.mean(x, axis=-1, keepdims=True)
    xc = x - mu
    var = jnp.mean(xc * xc, axis=-1, keepdims=True)
    return xc * lax.rsqrt(var + LN_EPS) * g + b


def _inproj_kernel(x_ref, wa_ref, wq_ref, wkv_ref, wqi_ref, wkw_ref, lnv_ref, lnk_ref, ws_ref, bs_ref,
                   a_ref, q_ref, qi_ref, k_ref, v_ref, kb_ref, vb_ref, ki_ref, kib_ref, wi_ref, *va_refs,
                   span):
    tm = x_ref.shape[0]
    xb = x_ref[...].astype(BF16)

    za = _dot(xb, wa_ref[...])
    ga = 0.5 * za * (1.0 + lax.erf(za * np.float32(np.sqrt(0.5))))
    u = ga[:, :A_WIDTH]
    va = _layer_norm(ga[:, A_WIDTH:], lnv_ref[0:1, :], lnv_ref[1:2, :])
    if va_refs:
        va_refs[0][...] = va
    vab = va.astype(BF16)
    for s in range(tm // span):
        rows = slice(s * span, (s + 1) * span)
        for g in range(A_GROUPS):
            cols = slice(g * A_GROUP_CH, (g + 1) * A_GROUP_CH)
            mixed = _dot(ws_ref[g], vab[rows, cols]) + bs_ref[:, cols]
            a_ref[rows, cols] = (u[rows, cols] * mixed).astype(BF16)

    zq = _dot(xb, wq_ref[...])
    zqi = _dot(xb, wqi_ref[...])
    for h in range(B_HEADS):
        q_ref[h] = zq[:, h * B_HEAD_DIM:(h + 1) * B_HEAD_DIM].astype(BF16)
    for h in range(IDX_HEADS):
        qi_ref[h] = zqi[:, h * IDX_DIM:(h + 1) * IDX_DIM].astype(BF16)

    zkv = _dot(xb, wkv_ref[...])
    kvw = B_KV_HEADS * B_HEAD_DIM
    k_ref[...] = zkv[:, :kvw]
    v_ref[...] = zkv[:, kvw:]
    if len(kb_ref.shape) == 4:
        kb_ref, vb_ref, kib_ref = kb_ref.at[:, 0], vb_ref.at[:, 0], kib_ref.at[0]
    for g in range(B_KV_HEADS):
        kb_ref[g] = zkv[:, g * B_HEAD_DIM:(g + 1) * B_HEAD_DIM].astype(BF16)
        vb_ref[g] = zkv[:, kvw + g * B_HEAD_DIM:kvw + (g + 1) * B_HEAD_DIM].astype(BF16)

    zkw = _dot(xb, wkw_ref[...])
    ki = _layer_norm(zkw[:, :IDX_DIM], lnk_ref[0:1, :], lnk_ref[1:2, :])
    ki_ref[...] = ki
    kib_ref[...] = ki.astype(BF16)
    wi_ref[...] = zkw[:, IDX_DIM:IDX_DIM + IDX_HEADS] * np.float32(IDX_HEADS ** -0.5)


def _const_spec(shape):
    nd = len(shape)
    return pl.BlockSpec(shape, lambda *_: (0,) * nd)


def _in_projection(x2, wts, span, tm, emit_va, batch_rows=None):
    t, d = x2.shape
    grid = (t // tm,)
    row = lambda i: (i, 0)
    hm = lambda i: (0, i, 0)
    if batch_rows is None:
        kv_shape, kv_block, kv_map = (B_KV_HEADS, t, B_HEAD_DIM), (B_KV_HEADS, tm, B_HEAD_DIM), hm
        ki_shape, ki_block, ki_map = (t, IDX_DIM), (tm, IDX_DIM), row
    else:
        per = batch_rows // tm
        kv_shape = (B_KV_HEADS, t // batch_rows, batch_rows, B_HEAD_DIM)
        kv_block, kv_map = (B_KV_HEADS, 1, tm, B_HEAD_DIM), lambda i: (0, i // per, i % per, 0)
        ki_shape = (t // batch_rows, batch_rows, IDX_DIM)
        ki_block, ki_map = (1, tm, IDX_DIM), lambda i: (i // per, i % per, 0)
    in_specs = [pl.BlockSpec((tm, d), row)] + [_const_spec(w.shape) for w in wts]
    out_shape = [
        jax.ShapeDtypeStruct((t, A_WIDTH), BF16),
        jax.ShapeDtypeStruct((B_HEADS, t, B_HEAD_DIM), BF16),
        jax.ShapeDtypeStruct((IDX_HEADS, t, IDX_DIM), BF16),
        jax.ShapeDtypeStruct((t, B_KV_HEADS * B_HEAD_DIM), F32),
        jax.ShapeDtypeStruct((t, B_KV_HEADS * B_HEAD_DIM), F32),
        jax.ShapeDtypeStruct(kv_shape, BF16),
        jax.ShapeDtypeStruct(kv_shape, BF16),
        jax.ShapeDtypeStruct((t, IDX_DIM), F32),
        jax.ShapeDtypeStruct(ki_shape, BF16),
        jax.ShapeDtypeStruct((t, IDX_HEADS), F32),
    ]
    out_specs = [
        pl.BlockSpec((tm, A_WIDTH), row),
        pl.BlockSpec((B_HEADS, tm, B_HEAD_DIM), hm),
        pl.BlockSpec((IDX_HEADS, tm, IDX_DIM), hm),
        pl.BlockSpec((tm, B_KV_HEADS * B_HEAD_DIM), row),
        pl.BlockSpec((tm, B_KV_HEADS * B_HEAD_DIM), row),
        pl.BlockSpec(kv_block, kv_map),
        pl.BlockSpec(kv_block, kv_map),
        pl.BlockSpec((tm, IDX_DIM), row),
        pl.BlockSpec(ki_block, ki_map),
        pl.BlockSpec((tm, IDX_HEADS), row),
    ]
    if emit_va:
        out_shape.append(jax.ShapeDtypeStruct((t, A_WIDTH), F32))
        out_specs.append(pl.BlockSpec((tm, A_WIDTH), row))
    return pl.pallas_call(
        functools.partial(_inproj_kernel, span=span),
        grid=grid, in_specs=in_specs, out_specs=out_specs, out_shape=out_shape,
        compiler_params=pltpu.CompilerParams(dimension_semantics=("parallel",), vmem_limit_bytes=VMEM_LIMIT),
        name="in_projection",
    )(x2, *wts)


def _for_row_blocks(n_blocks, rows, body):
    if n_blocks == 1:
        body(0)
        return

    def step(r, carry):
        body(pl.multiple_of(r * rows, rows))
        return carry

    lax.fori_loop(0, n_blocks, step, 0)


def _dsa_block(q_ref, qi_ref, wi_ref, k_ref, v_ref, ki_ref, o_ref, s_ref, d_ref, thr_ref,
               *, qb, tq, rb, n_keys, n_valid, q_off, topk, slopes):
    neg_inf = np.float32(-np.inf)
    pos_inf = np.float32(np.inf)
    kf = np.float32(topk)
    n_rb = tq // rb

    def score_rows(r0):
        rows = pl.ds(r0, rb)
        q_pos = q_off + qb * tq + r0 + lax.broadcasted_iota(jnp.int32, (rb, 1), 0)
        qi = qi_ref[:, rows, :].reshape(IDX_HEADS * rb, IDX_DIM)
        wi = wi_ref[rows, :]
        for c0 in range(0, n_keys, KEY_CHUNK):
            kc = min(KEY_CHUNK, n_keys - c0)
            k_pos = c0 + lax.broadcasted_iota(jnp.int32, (1, kc), 1)
            lg = _dot_nt(qi, ki_ref[0, c0:c0 + kc, :])
            sc = wi[:, 0:1] * jnp.maximum(lg[0:rb], 0.0)
            for h in range(1, IDX_HEADS):
                sc = sc + wi[:, h:h + 1] * jnp.maximum(lg[h * rb:(h + 1) * rb], 0.0)
            k_chunk = k_pos // CHUNK
            if c0 + kc > n_valid:
                k_chunk = jnp.where(k_pos < n_valid, k_chunk, np.int32(2 ** 30))
            adm = k_chunk <= (q_pos // CHUNK)
            s_ref[rows, c0:c0 + kc] = jnp.where(adm, sc, neg_inf)
            d_ref[rows, c0:c0 + kc] = jnp.abs(q_pos - k_pos).astype(F32)

    _for_row_blocks(n_rb, rb, score_rows)

    chains = [slice(c * rb, (c + 1) * rb) for c in range(n_rb)]

    def count_ge(rows, t):
        return jnp.sum(jnp.where(s_ref[rows, :] >= t, 1.0, 0.0), axis=1, keepdims=True)

    brackets, states, n_adms = [], [], []
    for rows in chains:
        s = s_ref[rows, :]
        finite = s > neg_inf
        n_adm = jnp.sum(jnp.where(finite, 1.0, 0.0), axis=1, keepdims=True)
        row_max = jnp.max(s, axis=1, keepdims=True)
        row_min = jnp.min(jnp.where(finite, s, pos_inf), axis=1, keepdims=True)
        brackets.append((row_min, row_max, jnp.full((rb, 1), pos_inf, F32)))
        n_adms.append(n_adm)

    def bisect(_, carry):
        out = []
        for rows, (lo, hb, hiv) in zip(chains, carry):
            mid = 0.5 * lo + 0.5 * hb
            ge = count_ge(rows, mid) >= kf
            out.append((jnp.where(ge, mid, lo), jnp.where(ge, hb, mid), jnp.where(ge, hiv, mid)))
        return tuple(out)

    brackets = lax.fori_loop(0, BISECT_STEPS, bisect, tuple(brackets))

    for (_, _, hiv), n_adm in zip(brackets, n_adms):
        done = jnp.where(n_adm <= kf, 1.0, 0.0)
        states.append((jnp.full((rb, 1), np.finfo(np.float32).min, F32), hiv, done))

    def n_open(states):
        return sum(jnp.sum(1.0 - done) for _, _, done in states)

    def scan_cond(carry):
        _, n_left, it = carry
        return jnp.logical_and(n_left > 0.0, it < n_keys)

    def scan_body(carry):
        states, _, it = carry
        out = []
        for rows, (thr, hiv, done) in zip(chains, states):
            sv = s_ref[rows, :]
            cand = jnp.max(jnp.where(sv < hiv, sv, neg_inf), axis=1, keepdims=True)
            found = jnp.where(done > 0.5, 0.0, jnp.where(count_ge(rows, cand) >= kf, 1.0, 0.0))
            thr = jnp.where(found > 0.5, cand, thr)
            done = jnp.maximum(done, found)
            out.append((thr, jnp.where(done > 0.5, hiv, cand), done))
        return tuple(out), n_open(out), it + 1

    states, _, _ = lax.while_loop(scan_cond, scan_body, (tuple(states), n_open(states), jnp.int32(0)))

    over = []
    for rows, (thr, _, _), n_adm in zip(chains, states, n_adms):
        thr_ref[rows, :] = thr
        over.append(jnp.max(jnp.where(n_adm > kf, count_ge(rows, thr), 0.0)))
    tied = functools.reduce(jnp.maximum, over) > kf

    def attend_rows(r0):
        rows = pl.ds(r0, rb)
        thr = thr_ref[rows, :]

        @pl.when(jnp.logical_not(tied))
        def _():
            d_ref[rows, :] = jnp.where(s_ref[rows, :] >= thr, d_ref[rows, :], pos_inf)

        @pl.when(tied)
        def _():
            n_gt = jnp.sum(jnp.where(s_ref[rows, :] > thr, 1.0, 0.0), axis=1, keepdims=True)
            room = kf - n_gt
            tri = (lax.broadcasted_iota(jnp.int32, (LANES, LANES), 0)
                   <= lax.broadcasted_iota(jnp.int32, (LANES, LANES), 1))
            tri = jnp.where(tri, 1.0, 0.0).astype(BF16)
            run = jnp.zeros((rb, 1), F32)
            for c0 in range(0, n_keys, LANES):
                blk = s_ref[rows, c0:c0 + LANES]
                eq = jnp.where(blk == thr, 1.0, 0.0)
                prefix = _dot(eq.astype(BF16), tri) + run
                keep = jnp.where(blk > thr, 1.0, jnp.where(prefix <= room, eq, 0.0))
                d_ref[rows, c0:c0 + LANES] = jnp.where(keep > 0.5, d_ref[rows, c0:c0 + LANES], pos_inf)
                run = run + jnp.sum(eq, axis=1, keepdims=True)

        dist = d_ref[rows, :]
        for g in range(B_KV_HEADS):
            qg = q_ref[g * B_GROUP:(g + 1) * B_GROUP, rows, :].reshape(B_GROUP * rb, B_HEAD_DIM)
            logits = _dot_nt(qg, k_ref[g, 0])
            ps, ls = [], []
            for hh in range(B_GROUP):
                lgt = logits[hh * rb:(hh + 1) * rb] - np.float32(slopes[g * B_GROUP + hh]) * dist
                m = jnp.max(lgt, axis=1, keepdims=True)
                p = jnp.exp(lgt - m)
                ls.append(jnp.sum(p, axis=1, keepdims=True))
                ps.append(p.astype(BF16))
            og = _dot(jnp.concatenate(ps, axis=0), v_ref[g, 0])
            for hh in range(B_GROUP):
                h = g * B_GROUP + hh
                o_ref[rows, h * B_HEAD_DIM:(h + 1) * B_HEAD_DIM] = (og[hh * rb:(hh + 1) * rb] / ls[hh]).astype(BF16)

    _for_row_blocks(n_rb, rb, attend_rows)


def _dsa_kernel(q_ref, qi_ref, wi_ref, k_ref, v_ref, ki_ref, o_ref, s_ref, d_ref, thr_ref,
                *, tq, key_buckets, n_valid, q_off, **static):
    qb = pl.program_id(1)
    last_chunk = (q_off + (qb + 1) * tq - 1) // CHUNK
    need = jnp.minimum((last_chunk + 1) * CHUNK, n_valid)
    lower = 0
    for nk in key_buckets:
        body = functools.partial(
            _dsa_block, q_ref, qi_ref, wi_ref, k_ref.at[:, :, :nk, :], v_ref.at[:, :, :nk, :],
            ki_ref.at[:, :nk, :], o_ref, s_ref.at[:, :nk], d_ref.at[:, :nk], thr_ref,
            qb=qb, tq=tq, n_keys=nk, n_valid=n_valid, q_off=q_off, **static)
        if len(key_buckets) == 1:
            body()
        else:
            pl.when(jnp.logical_and(need > lower, need <= nk))(body)
        lower = nk


def _dsa_attention(q_hm, qi_hm, wi, kb, vb, kib, *, n, t_q, tq, rb, key_buckets, n_valid, q_off, topk, slopes):
    n_keys = kb.shape[2]
    nqb = t_q // tq
    qmap = lambda b, j: (0, b * nqb + j, 0)
    rmap = lambda b, j: (b * nqb + j, 0)
    in_specs = [
        pl.BlockSpec((B_HEADS, tq, B_HEAD_DIM), qmap),
        pl.BlockSpec((IDX_HEADS, tq, IDX_DIM), qmap),
        pl.BlockSpec((tq, IDX_HEADS), rmap),
        pl.BlockSpec((B_KV_HEADS, 1, n_keys, B_HEAD_DIM), lambda b, j: (0, b, 0, 0)),
        pl.BlockSpec((B_KV_HEADS, 1, n_keys, B_HEAD_DIM), lambda b, j: (0, b, 0, 0)),
        pl.BlockSpec((1, n_keys, IDX_DIM), lambda b, j: (b, 0, 0)),
    ]
    return pl.pallas_call(
        functools.partial(_dsa_kernel, tq=tq, rb=rb, key_buckets=key_buckets, n_valid=n_valid, q_off=q_off,
                          topk=topk, slopes=slopes),
        grid=(n, nqb), in_specs=in_specs,
        out_specs=pl.BlockSpec((tq, B_WIDTH), rmap),
        out_shape=jax.ShapeDtypeStruct((n * t_q, B_WIDTH), BF16),
        scratch_shapes=[pltpu.VMEM((tq, n_keys), F32), pltpu.VMEM((tq, n_keys), F32), pltpu.VMEM((tq, 1), F32)],
        compiler_params=pltpu.CompilerParams(dimension_semantics=("parallel", "arbitrary"),
                                             vmem_limit_bytes=VMEM_LIMIT),
        name="dsa_attention",
    )(q_hm, qi_hm, wi, kb, vb, kib)


def _merge_kernel(x_ref, a_ref, b_ref, wg_ref, wb_ref, wo_ref, ln1_ref, w1_ref, b1_ref, w2_ref, b2_ref, ln2_ref,
                  y_ref, *, alpha):
    d = x_ref.shape[1]
    x = x_ref[...]
    gates = jax.nn.sigmoid(_dot(x.astype(BF16), wg_ref[...]))
    m = gates[:, :d] * _dot(a_ref[...], wb_ref[0]) + gates[:, d:] * _dot(b_ref[...], wb_ref[1])
    h = _layer_norm(alpha * x + _dot(m.astype(BF16), wo_ref[...]), ln1_ref[0:1, :], ln1_ref[1:2, :])
    f = jnp.square(jnp.maximum(_dot(h.astype(BF16), w1_ref[...]) + b1_ref[...], 0.0))
    f = _dot(f.astype(BF16), w2_ref[...]) + b2_ref[...]
    y_ref[...] = _layer_norm(alpha * h + f, ln2_ref[0:1, :], ln2_ref[1:2, :])


def _resident_spec(shape):
    nd = len(shape)
    return pl.BlockSpec(shape, lambda *_: (0,) * nd, pipeline_mode=pl.Buffered(1))


def _merge_ffn(x2, a, b, wts, tm, alpha):
    t, d = x2.shape
    row = lambda i: (i, 0)
    in_specs = [pl.BlockSpec((tm, d), row), pl.BlockSpec((tm, A_WIDTH), row), pl.BlockSpec((tm, B_WIDTH), row)]
    in_specs += [_resident_spec(w.shape) for w in wts]
    return pl.pallas_call(
        functools.partial(_merge_kernel, alpha=np.float32(alpha)),
        grid=(t // tm,), in_specs=in_specs,
        out_specs=pl.BlockSpec((tm, d), row),
        out_shape=jax.ShapeDtypeStruct((t, d), F32),
        compiler_params=pltpu.CompilerParams(dimension_semantics=("parallel",), vmem_limit_bytes=VMEM_LIMIT),
        name="merge_ffn",
    )(x2, a, b, *wts)


def _key_buckets(n_keys):
    return tuple(range(DSA_ROWS, n_keys, DSA_ROWS)) + (n_keys,)


def _spatial_weights(w_s, b_s, span):
    pos = jnp.arange(A_SPAN)
    mask = (pos[None, :] // CHUNK) <= (pos[:, None] // CHUNK)
    ws = jnp.where(mask[None], w_s, 0.0)[:, :span, :span].astype(BF16)
    bs = jnp.repeat(b_s[:, :span].T, A_GROUP_CH, axis=1)
    return ws, bs


def kernel(x_prompt, x_sample, cache_k, cache_v, cache_kidx, w_in, lnv_g, lnv_b, w_s, b_s, lnk_g, lnk_b,
           w_branch, w_out, ln1_g, ln1_b, w_ff1, b_ff1, w_ff2, b_ff2, ln2_g, ln2_b):
    depth = w_in.shape[0]
    n_p, s_p, d = x_prompt.shape
    n_s, t_s, _ = x_sample.shape
    past = cache_k.shape[2]
    alpha = (2 * depth) ** 0.25
    slopes = tuple(float(2.0 ** (-8.0 * h / B_HEADS)) for h in range(1, B_HEADS + 1))
    kvw = B_KV_HEADS * B_HEAD_DIM
    c_a = 2 * A_WIDTH
    c_q = c_a + B_WIDTH
    c_k = c_q + kvw
    c_v = c_k + kvw
    c_qi = c_v + IDX_HEADS * IDX_DIM
    c_wi = c_qi + IDX_DIM + IDX_HEADS

    xp = x_prompt.reshape(n_p * s_p, d)
    xs = x_sample.reshape(n_s * t_s, d)
    outs = [[] for _ in range(7)]
    for l in range(depth):
        w = w_in[l]
        pad = jnp.zeros((d, LANES - (c_wi - c_qi)), F32)
        proj_w = (
            w[:, :c_a].astype(BF16),
            (w[:, c_a:c_q] * (B_HEAD_DIM ** -0.5)).astype(BF16),
            w[:, c_q:c_v].astype(BF16),
            (w[:, c_v:c_qi] * (IDX_DIM ** -0.5)).astype(BF16),
            jnp.concatenate([w[:, c_qi:c_wi], pad], axis=1).astype(BF16),
            jnp.stack([lnv_g[l], lnv_b[l]]),
            jnp.stack([lnk_g[l], lnk_b[l]]),
        )
        merge_w = (
            w[:, c_wi:].astype(BF16),
            w_branch[l].astype(BF16),
            w_out[l].astype(BF16),
            jnp.stack([ln1_g[l], ln1_b[l]]),
            w_ff1[l].astype(BF16),
            b_ff1[l][None, :],
            w_ff2[l].astype(BF16),
            b_ff2[l][None, :],
            jnp.stack([ln2_g[l], ln2_b[l]]),
        )

        a_p, q_p, qi_p, k_p, v_p, kb_p, vb_p, ki_p, kib_p, wi_p = _in_projection(
            xp, proj_w + _spatial_weights(w_s[l], b_s[l], A_SPAN), A_SPAN, TOKEN_TILE, False, batch_rows=s_p)
        b_p = _dsa_attention(
            q_p, qi_p, wi_p, kb_p, vb_p, kib_p,
            n=n_p, t_q=s_p, tq=DSA_ROWS, rb=Q_BLOCK, key_buckets=_key_buckets(s_p), n_valid=s_p, q_off=0,
            topk=min(TOPK_MAX, s_p // 4), slopes=slopes)
        xp = _merge_ffn(xp, a_p, b_p, merge_w, TOKEN_TILE, alpha)

        a_s, q_s, qi_s, k_s, v_s, kb_s, vb_s, ki_s, kib_s, wi_s, va_s = _in_projection(
            xs, proj_w + _spatial_weights(w_s[l], b_s[l], t_s), t_s, n_s * t_s, True)
        n_all = past + t_s
        n_keys = -(-n_all // LANES) * LANES

        def with_cache(cache, new):
            c = jnp.moveaxis(cache.astype(BF16), 2, 0)
            new = new.reshape(B_KV_HEADS, n_s, t_s, B_HEAD_DIM)
            zero = jnp.zeros((B_KV_HEADS, n_s, n_keys - n_all, B_HEAD_DIM), BF16)
            return jnp.concatenate([c, new, zero], axis=2)

        kib_all = jnp.concatenate([cache_kidx[l].astype(BF16), kib_s.reshape(n_s, t_s, IDX_DIM),
                                   jnp.zeros((n_s, n_keys - n_all, IDX_DIM), BF16)], axis=1)
        b_s_ = _dsa_attention(
            q_s, qi_s, wi_s, with_cache(cache_k[l], kb_s), with_cache(cache_v[l], vb_s), kib_all,
            n=n_s, t_q=t_s, tq=t_s, rb=t_s, key_buckets=(n_keys,), n_valid=n_all, q_off=past,
            topk=min(TOPK_MAX, n_all // 4), slopes=slopes)
        xs = _merge_ffn(xs, a_s, b_s_, merge_w, n_s * t_s, alpha)

        outs[0].append(k_p.reshape(n_p, s_p, B_KV_HEADS, B_HEAD_DIM))
        outs[1].append(v_p.reshape(n_p, s_p, B_KV_HEADS, B_HEAD_DIM))
        outs[2].append(ki_p.reshape(n_p, s_p, IDX_DIM))
        outs[3].append(k_s.reshape(n_s, t_s, B_KV_HEADS, B_HEAD_DIM))
        outs[4].append(v_s.reshape(n_s, t_s, B_KV_HEADS, B_HEAD_DIM))
        outs[5].append(ki_s.reshape(n_s, t_s, IDX_DIM))
        outs[6].append(va_s.reshape(n_s, t_s, A_WIDTH))

    return (xp.reshape(n_p, s_p, d), xs.reshape(n_s, t_s, d)) + tuple(jnp.stack(o) for o in outs)
```

```python
import functools

import numpy as np
import jax
import jax.numpy as jnp
from jax import lax
from jax.experimental import pallas as pl
from jax.experimental.pallas import tpu as pltpu

CHUNK = 64
A_WIDTH = 512
A_GROUPS = 4
A_GROUP_CH = A_WIDTH // A_GROUPS
A_SPAN = 128
B_HEADS = 8
B_HEAD_DIM = 64
B_KV_HEADS = 2
B_GROUP = B_HEADS // B_KV_HEADS
B_WIDTH = B_HEADS * B_HEAD_DIM
IDX_HEADS = 8
IDX_DIM = 64
TOPK_MAX = 256
Q_BLOCK = 128
LN_EPS = 1e-5
LOG2_E = 1.4426950408889634

LANES = 128
VMEM_LIMIT = 52 * 1024 * 1024
TOKEN_TILE = 512
KEY_CHUNK = 512
DSA_ROWS = 4 * Q_BLOCK
BISECT_STEPS = 14

F32 = jnp.float32
BF16 = jnp.bfloat16
NT_DIMS = (((1,), (1,)), ((), ()))


def _dot(a, b):
    return jnp.dot(a, b, preferred_element_type=F32)


def _dot_nt(a, b):
    return lax.dot_general(a, b, NT_DIMS, preferred_element_type=F32)


def _layer_norm(x, g, b):
    mu = jnp.mean(x, axis=-1, keepdims=True)
    xc = x - mu
    var = jnp.mean(xc * xc, axis=-1, keepdims=True)
    return xc * lax.rsqrt(var + LN_EPS) * g + b


def _inproj_kernel(x_ref, wa_ref, wq_ref, wkv_ref, wva_ref, wqi_ref, wkw_ref, lnv_ref, lnk_ref, ws_ref, bs_ref,
                   a_ref, q_ref, qi_ref, k_ref, v_ref, kb_ref, vb_ref, ki_ref, kib_ref, wi_ref, *va_refs,
                   span):
    tm = x_ref.shape[0]
    xb = x_ref[...].astype(BF16)

    za = _dot(xb, wa_ref[...])
    ga = 0.5 * za * (1.0 + lax.erf(za * np.float32(np.sqrt(0.5))))
    u = ga[:, :A_WIDTH]
    va = _layer_norm(ga[:, A_WIDTH:], lnv_ref[0:1, :], lnv_ref[1:2, :])
    if va_refs:
        va_refs[0][...] = va
    vab = va.astype(BF16)
    for s in range(tm // span):
        rows = slice(s * span, (s + 1) * span)
        for g in range(A_GROUPS):
            cols = slice(g * A_GROUP_CH, (g + 1) * A_GROUP_CH)
            mixed = _dot(ws_ref[g], vab[rows, cols]) + bs_ref[:, cols]
            a_ref[rows, cols] = (u[rows, cols] * mixed).astype(BF16)

    zq = _dot(xb, wq_ref[...])
    zqi = _dot(xb, wqi_ref[...])
    for h in range(B_HEADS):
        q_ref[h] = zq[:, h * B_HEAD_DIM:(h + 1) * B_HEAD_DIM].astype(BF16)
    for h in range(IDX_HEADS):
        qi_ref[h] = zqi[:, h * IDX_DIM:(h + 1) * IDX_DIM].astype(BF16)

    zkv = _dot(xb, wkv_ref[...])
    kvw = B_KV_HEADS * B_HEAD_DIM
    k_ref[...] = zkv[:, :kvw]
    v_ref[...] = zkv[:, kvw:]
    if len(kb_ref.shape) == 4:
        kb_ref, vb_ref, kib_ref = kb_ref.at[:, 0], vb_ref.at[:, 0], kib_ref.at[0]
    zva = _dot(xb, wva_ref[...])
    ones_lane = lax.broadcasted_iota(jnp.int32, (tm, LANES), 1) == B_HEAD_DIM
    for g in range(B_KV_HEADS):
        kb_ref[g] = zkv[:, g * B_HEAD_DIM:(g + 1) * B_HEAD_DIM].astype(BF16)
        vb_ref[g] = jnp.where(ones_lane, 1.0, zva[:, g * LANES:(g + 1) * LANES]).astype(BF16)

    zkw = _dot(xb, wkw_ref[...])
    ki = _layer_norm(zkw[:, :IDX_DIM], lnk_ref[0:1, :], lnk_ref[1:2, :])
    ki_ref[...] = ki
    kib_ref[...] = ki.astype(BF16)
    wi_ref[...] = zkw[:, IDX_DIM:IDX_DIM + IDX_HEADS] * np.float32(IDX_HEADS ** -0.5)


def _const_spec(shape):
    nd = len(shape)
    return pl.BlockSpec(shape, lambda *_: (0,) * nd)


def _in_projection(x2, wts, span, tm, emit_va, batch_rows=None):
    t, d = x2.shape
    grid = (t // tm,)
    row = lambda i: (i, 0)
    hm = lambda i: (0, i, 0)
    if batch_rows is None:
        kv_shape, kv_block, kv_map = (B_KV_HEADS, t), (B_KV_HEADS, tm), hm
        ki_shape, ki_block, ki_map = (t, IDX_DIM), (tm, IDX_DIM), row
    else:
        per = batch_rows // tm
        kv_shape = (B_KV_HEADS, t // batch_rows, batch_rows)
        kv_block, kv_map = (B_KV_HEADS, 1, tm), lambda i: (0, i // per, i % per, 0)
        ki_shape = (t // batch_rows, batch_rows, IDX_DIM)
        ki_block, ki_map = (1, tm, IDX_DIM), lambda i: (i // per, i % per, 0)
    in_specs = [pl.BlockSpec((tm, d), row)] + [_const_spec(w.shape) for w in wts]
    out_shape = [
        jax.ShapeDtypeStruct((t, A_WIDTH), BF16),
        jax.ShapeDtypeStruct((B_HEADS, t, B_HEAD_DIM), BF16),
        jax.ShapeDtypeStruct((IDX_HEADS, t, IDX_DIM), BF16),
        jax.ShapeDtypeStruct((t, B_KV_HEADS * B_HEAD_DIM), F32),
        jax.ShapeDtypeStruct((t, B_KV_HEADS * B_HEAD_DIM), F32),
        jax.ShapeDtypeStruct(kv_shape + (B_HEAD_DIM,), BF16),
        jax.ShapeDtypeStruct(kv_shape + (LANES,), BF16),
        jax.ShapeDtypeStruct((t, IDX_DIM), F32),
        jax.ShapeDtypeStruct(ki_shape, BF16),
        jax.ShapeDtypeStruct((t, IDX_HEADS), F32),
    ]
    out_specs = [
        pl.BlockSpec((tm, A_WIDTH), row),
        pl.BlockSpec((B_HEADS, tm, B_HEAD_DIM), hm),
        pl.BlockSpec((IDX_HEADS, tm, IDX_DIM), hm),
        pl.BlockSpec((tm, B_KV_HEADS * B_HEAD_DIM), row),
        pl.BlockSpec((tm, B_KV_HEADS * B_HEAD_DIM), row),
        pl.BlockSpec(kv_block + (B_HEAD_DIM,), kv_map),
        pl.BlockSpec(kv_block + (LANES,), kv_map),
        pl.BlockSpec((tm, IDX_DIM), row),
        pl.BlockSpec(ki_block, ki_map),
        pl.BlockSpec((tm, IDX_HEADS), row),
    ]
    if emit_va:
        out_shape.append(jax.ShapeDtypeStruct((t, A_WIDTH), F32))
        out_specs.append(pl.BlockSpec((tm, A_WIDTH), row))
    return pl.pallas_call(
        functools.partial(_inproj_kernel, span=span),
        grid=grid, in_specs=in_specs, out_specs=out_specs, out_shape=out_shape,
        compiler_params=pltpu.CompilerParams(dimension_semantics=("parallel",), vmem_limit_bytes=VMEM_LIMIT),
        name="in_projection",
    )(x2, *wts)


def _for_row_blocks(n_blocks, rows, body):
    if n_blocks == 1:
        body(0)
        return

    def step(r, carry):
        body(pl.multiple_of(r * rows, rows))
        return carry

    lax.fori_loop(0, n_blocks, step, 0)


def _dsa_block(q_ref, qi_ref, wi_ref, k_ref, v_ref, ki_ref, o_ref, s_ref, d_ref, thr_ref,
               *, qb, tq, rb, n_keys, n_valid, q_off, topk, slopes):
    neg_inf = np.float32(-np.inf)
    pos_inf = np.float32(np.inf)
    kf = np.float32(topk)
    n_rb = tq // rb

    def score_rows(r0):
        rows = pl.ds(r0, rb)
        q_pos = q_off + qb * tq + r0 + lax.broadcasted_iota(jnp.int32, (rb, 1), 0)
        qi = qi_ref[:, rows, :].reshape(IDX_HEADS * rb, IDX_DIM)
        wi = wi_ref[rows, :]
        for c0 in range(0, n_keys, KEY_CHUNK):
            kc = min(KEY_CHUNK, n_keys - c0)
            k_pos = c0 + lax.broadcasted_iota(jnp.int32, (1, kc), 1)
            lg = _dot_nt(qi, ki_ref[0, c0:c0 + kc, :])
            sc = wi[:, 0:1] * jnp.maximum(lg[0:rb], 0.0)
            for h in range(1, IDX_HEADS):
                sc = sc + wi[:, h:h + 1] * jnp.maximum(lg[h * rb:(h + 1) * rb], 0.0)
            k_chunk = k_pos // CHUNK
            if c0 + kc > n_valid:
                k_chunk = jnp.where(k_pos < n_valid, k_chunk, np.int32(2 ** 30))
            adm = k_chunk <= (q_pos // CHUNK)
            s_ref[rows, c0:c0 + kc] = jnp.where(adm, sc, neg_inf)
            d_ref[rows, c0:c0 + kc] = jnp.abs(q_pos - k_pos).astype(F32)

    _for_row_blocks(n_rb, rb, score_rows)

    chains = [slice(c * rb, (c + 1) * rb) for c in range(n_rb)]

    def count_ge(rows, t):
        return jnp.sum(jnp.where(s_ref[rows, :] >= t, 1.0, 0.0), axis=1, keepdims=True)

    brackets, states, n_adms = [], [], []
    for rows in chains:
        s = s_ref[rows, :]
        finite = s > neg_inf
        n_adm = jnp.sum(jnp.where(finite, 1.0, 0.0), axis=1, keepdims=True)
        row_max = jnp.max(s, axis=1, keepdims=True)
        row_min = jnp.min(jnp.where(finite, s, pos_inf), axis=1, keepdims=True)
        brackets.append((row_min, row_max, jnp.full((rb, 1), pos_inf, F32)))
        n_adms.append(n_adm)

    def bisect(_, carry):
        out = []
        for rows, (lo, hb, hiv) in zip(chains, carry):
            mid = 0.5 * lo + 0.5 * hb
            ge = count_ge(rows, mid) >= kf
            out.append((jnp.where(ge, mid, lo), jnp.where(ge, hb, mid), jnp.where(ge, hiv, mid)))
        return tuple(out)

    brackets = lax.fori_loop(0, BISECT_STEPS, bisect, tuple(brackets))

    for (_, _, hiv), n_adm in zip(brackets, n_adms):
        done = jnp.where(n_adm <= kf, 1.0, 0.0)
        states.append((jnp.full((rb, 1), np.finfo(np.float32).min, F32), hiv, done))

    def n_open(states):
        return sum(jnp.sum(1.0 - done) for _, _, done in states)

    def scan_cond(carry):
        _, n_left, it = carry
        return jnp.logical_and(n_left > 0.0, it < n_keys)

    def scan_body(carry):
        states, _, it = carry
        out = []
        for rows, (thr, hiv, done) in zip(chains, states):
            sv = s_ref[rows, :]
            cand = jnp.max(jnp.where(sv < hiv, sv, neg_inf), axis=1, keepdims=True)
            found = jnp.where(done > 0.5, 0.0, jnp.where(count_ge(rows, cand) >= kf, 1.0, 0.0))
            thr = jnp.where(found > 0.5, cand, thr)
            done = jnp.maximum(done, found)
            out.append((thr, jnp.where(done > 0.5, hiv, cand), done))
        return tuple(out), n_open(out), it + 1

    states, _, _ = lax.while_loop(scan_cond, scan_body, (tuple(states), n_open(states), jnp.int32(0)))

    over = []
    for rows, (thr, _, _), n_adm in zip(chains, states, n_adms):
        thr_ref[rows, :] = thr
        over.append(jnp.max(jnp.where(n_adm > kf, count_ge(rows, thr), 0.0)))
    tied = functools.reduce(jnp.maximum, over) > kf

    def attend_rows(r0):
        rows = pl.ds(r0, rb)
        thr = thr_ref[rows, :]

        @pl.when(jnp.logical_not(tied))
        def _():
            d_ref[rows, :] = jnp.where(s_ref[rows, :] >= thr, d_ref[rows, :], pos_inf)

        @pl.when(tied)
        def _():
            n_gt = jnp.sum(jnp.where(s_ref[rows, :] > thr, 1.0, 0.0), axis=1, keepdims=True)
            room = kf - n_gt
            tri = (lax.broadcasted_iota(jnp.int32, (LANES, LANES), 0)
                   <= lax.broadcasted_iota(jnp.int32, (LANES, LANES), 1))
            tri = jnp.where(tri, 1.0, 0.0).astype(BF16)
            run = jnp.zeros((rb, 1), F32)
            for c0 in range(0, n_keys, LANES):
                blk = s_ref[rows, c0:c0 + LANES]
                eq = jnp.where(blk == thr, 1.0, 0.0)
                prefix = _dot(eq.astype(BF16), tri) + run
                keep = jnp.where(blk > thr, 1.0, jnp.where(prefix <= room, eq, 0.0))
                d_ref[rows, c0:c0 + LANES] = jnp.where(keep > 0.5, d_ref[rows, c0:c0 + LANES], pos_inf)
                run = run + jnp.sum(eq, axis=1, keepdims=True)

        qs = [q_ref[g * B_GROUP:(g + 1) * B_GROUP, rows, :].reshape(B_GROUP * rb, B_HEAD_DIM)
              for g in range(B_KV_HEADS)]
        m_run = [jnp.full((rb, 1), neg_inf, F32) for _ in range(B_HEADS)]
        acc = [jnp.zeros((rb, LANES), F32) for _ in range(B_HEADS)]
        for c0 in range(0, n_keys, KEY_CHUNK):
            kc = min(KEY_CHUNK, n_keys - c0)
            dist = d_ref[rows, c0:c0 + kc]
            for g in range(B_KV_HEADS):
                logits = _dot_nt(qs[g], k_ref[g, 0, c0:c0 + kc, :])
                ps, alphas = [], []
                for hh in range(B_GROUP):
                    h = g * B_GROUP + hh
                    lgt = logits[hh * rb:(hh + 1) * rb] - np.float32(slopes[h] * LOG2_E) * dist
                    m_new = jnp.maximum(m_run[h], jnp.max(lgt, axis=1, keepdims=True))
                    m_ref = jnp.where(m_new == neg_inf, 0.0, m_new)
                    alphas.append(jnp.exp2(m_run[h] - m_ref))
                    ps.append(jnp.exp2(lgt - m_ref).astype(BF16))
                    m_run[h] = m_new
                pv = _dot(jnp.concatenate(ps, axis=0), v_ref[g, 0, c0:c0 + kc, :])
                for hh in range(B_GROUP):
                    h = g * B_GROUP + hh
                    acc[h] = alphas[hh] * acc[h] + pv[hh * rb:(hh + 1) * rb]
        for h in range(B_HEADS):
            out = acc[h][:, :B_HEAD_DIM] / acc[h][:, B_HEAD_DIM:B_HEAD_DIM + 1]
            o_ref[rows, h * B_HEAD_DIM:(h + 1) * B_HEAD_DIM] = out.astype(BF16)

    _for_row_blocks(n_rb, rb, attend_rows)


def _dsa_kernel(q_ref, qi_ref, wi_ref, k_ref, v_ref, ki_ref, o_ref, s_ref, d_ref, thr_ref,
                *, tq, key_buckets, n_valid, q_off, **static):
    qb = pl.program_id(1)
    last_chunk = (q_off + (qb + 1) * tq - 1) // CHUNK
    need = jnp.minimum((last_chunk + 1) * CHUNK, n_valid)
    lower = 0
    for nk in key_buckets:
        body = functools.partial(
            _dsa_block, q_ref, qi_ref, wi_ref, k_ref.at[:, :, :nk, :], v_ref.at[:, :, :nk, :],
            ki_ref.at[:, :nk, :], o_ref, s_ref.at[:, :nk], d_ref.at[:, :nk], thr_ref,
            qb=qb, tq=tq, n_keys=nk, n_valid=n_valid, q_off=q_off, **static)
        if len(key_buckets) == 1:
            body()
        else:
            pl.when(jnp.logical_and(need > lower, need <= nk))(body)
        lower = nk


def _dsa_attention(q_hm, qi_hm, wi, kb, vb, kib, *, n, t_q, tq, rb, key_buckets, n_valid, q_off, topk, slopes):
    n_keys = kb.shape[2]
    nqb = t_q // tq
    qmap = lambda b, j: (0, b * nqb + j, 0)
    rmap = lambda b, j: (b * nqb + j, 0)
    in_specs = [
        pl.BlockSpec((B_HEADS, tq, B_HEAD_DIM), qmap),
        pl.BlockSpec((IDX_HEADS, tq, IDX_DIM), qmap),
        pl.BlockSpec((tq, IDX_HEADS), rmap),
        pl.BlockSpec((B_KV_HEADS, 1, n_keys, B_HEAD_DIM), lambda b, j: (0, b, 0, 0)),
        pl.BlockSpec((B_KV_HEADS, 1, n_keys, LANES), lambda b, j: (0, b, 0, 0)),
        pl.BlockSpec((1, n_keys, IDX_DIM), lambda b, j: (b, 0, 0)),
    ]
    return pl.pallas_call(
        functools.partial(_dsa_kernel, tq=tq, rb=rb, key_buckets=key_buckets, n_valid=n_valid, q_off=q_off,
                          topk=topk, slopes=slopes),
        grid=(n, nqb), in_specs=in_specs,
        out_specs=pl.BlockSpec((tq, B_WIDTH), rmap),
        out_shape=jax.ShapeDtypeStruct((n * t_q, B_WIDTH), BF16),
        scratch_shapes=[pltpu.VMEM((tq, n_keys), F32), pltpu.VMEM((tq, n_keys), F32), pltpu.VMEM((tq, 1), F32)],
        compiler_params=pltpu.CompilerParams(dimension_semantics=("parallel", "arbitrary"),
                                             vmem_limit_bytes=VMEM_LIMIT),
        name="dsa_attention",
    )(q_hm, qi_hm, wi, kb, vb, kib)


def _merge_kernel(x_ref, a_ref, b_ref, wg_ref, wb_ref, wo_ref, ln1_ref, w1_ref, b1_ref, w2_ref, b2_ref, ln2_ref,
                  y_ref, *, alpha):
    d = x_ref.shape[1]
    x = x_ref[...]
    gates = jax.nn.sigmoid(_dot(x.astype(BF16), wg_ref[...]))
    m = gates[:, :d] * _dot(a_ref[...], wb_ref[0]) + gates[:, d:] * _dot(b_ref[...], wb_ref[1])
    h = _layer_norm(alpha * x + _dot(m.astype(BF16), wo_ref[...]), ln1_ref[0:1, :], ln1_ref[1:2, :])
    f = jnp.square(jnp.maximum(_dot(h.astype(BF16), w1_ref[...]) + b1_ref[...], 0.0))
    f = _dot(f.astype(BF16), w2_ref[...]) + b2_ref[...]
    y_ref[...] = _layer_norm(alpha * h + f, ln2_ref[0:1, :], ln2_ref[1:2, :])


def _resident_spec(shape):
    nd = len(shape)
    return pl.BlockSpec(shape, lambda *_: (0,) * nd, pipeline_mode=pl.Buffered(1))


def _merge_ffn(x2, a, b, wts, tm, alpha):
    t, d = x2.shape
    row = lambda i: (i, 0)
    in_specs = [pl.BlockSpec((tm, d), row), pl.BlockSpec((tm, A_WIDTH), row), pl.BlockSpec((tm, B_WIDTH), row)]
    in_specs += [_resident_spec(w.shape) for w in wts]
    return pl.pallas_call(
        functools.partial(_merge_kernel, alpha=np.float32(alpha)),
        grid=(t // tm,), in_specs=in_specs,
        out_specs=pl.BlockSpec((tm, d), row),
        out_shape=jax.ShapeDtypeStruct((t, d), F32),
        compiler_params=pltpu.CompilerParams(dimension_semantics=("parallel",), vmem_limit_bytes=VMEM_LIMIT),
        name="merge_ffn",
    )(x2, a, b, *wts)


def _key_buckets(n_keys):
    return tuple(range(DSA_ROWS, n_keys, DSA_ROWS)) + (n_keys,)


def _spatial_weights(w_s, b_s, span):
    pos = jnp.arange(A_SPAN)
    mask = (pos[None, :] // CHUNK) <= (pos[:, None] // CHUNK)
    ws = jnp.where(mask[None], w_s, 0.0)[:, :span, :span].astype(BF16)
    bs = jnp.repeat(b_s[:, :span].T, A_GROUP_CH, axis=1)
    return ws, bs


def kernel(x_prompt, x_sample, cache_k, cache_v, cache_kidx, w_in, lnv_g, lnv_b, w_s, b_s, lnk_g, lnk_b,
           w_branch, w_out, ln1_g, ln1_b, w_ff1, b_ff1, w_ff2, b_ff2, ln2_g, ln2_b):
    depth = w_in.shape[0]
    n_p, s_p, d = x_prompt.shape
    n_s, t_s, _ = x_sample.shape
    past = cache_k.shape[2]
    alpha = (2 * depth) ** 0.25
    slopes = tuple(float(2.0 ** (-8.0 * h / B_HEADS)) for h in range(1, B_HEADS + 1))
    kvw = B_KV_HEADS * B_HEAD_DIM
    c_a = 2 * A_WIDTH
    c_q = c_a + B_WIDTH
    c_k = c_q + kvw
    c_v = c_k + kvw
    c_qi = c_v + IDX_HEADS * IDX_DIM
    c_wi = c_qi + IDX_DIM + IDX_HEADS

    xp = x_prompt.reshape(n_p * s_p, d)
    xs = x_sample.reshape(n_s * t_s, d)
    outs = [[] for _ in range(7)]
    for l in range(depth):
        w = w_in[l]
        pad = jnp.zeros((d, LANES - (c_wi - c_qi)), F32)
        proj_w = (
            w[:, :c_a].astype(BF16),
            (w[:, c_a:c_q] * (LOG2_E * B_HEAD_DIM ** -0.5)).astype(BF16),
            w[:, c_q:c_v].astype(BF16),
            jnp.pad(w[:, c_k:c_v].reshape(d, B_KV_HEADS, B_HEAD_DIM),
                    ((0, 0), (0, 0), (0, LANES - B_HEAD_DIM))).reshape(d, B_KV_HEADS * LANES).astype(BF16),
            (w[:, c_v:c_qi] * (IDX_DIM ** -0.5)).astype(BF16),
            jnp.concatenate([w[:, c_qi:c_wi], pad], axis=1).astype(BF16),
            jnp.stack([lnv_g[l], lnv_b[l]]),
            jnp.stack([lnk_g[l], lnk_b[l]]),
        )
        merge_w = (
            w[:, c_wi:].astype(BF16),
            w_branch[l].astype(BF16),
            w_out[l].astype(BF16),
            jnp.stack([ln1_g[l], ln1_b[l]]),
            w_ff1[l].astype(BF16),
            b_ff1[l][None, :],
            w_ff2[l].astype(BF16),
            b_ff2[l][None, :],
            jnp.stack([ln2_g[l], ln2_b[l]]),
        )

        a_p, q_p, qi_p, k_p, v_p, kb_p, vb_p, ki_p, kib_p, wi_p = _in_projection(
            xp, proj_w + _spatial_weights(w_s[l], b_s[l], A_SPAN), A_SPAN, TOKEN_TILE, False, batch_rows=s_p)
        b_p = _dsa_attention(
            q_p, qi_p, wi_p, kb_p, vb_p, kib_p,
            n=n_p, t_q=s_p, tq=DSA_ROWS, rb=Q_BLOCK, key_buckets=_key_buckets(s_p), n_valid=s_p, q_off=0,
            topk=min(TOPK_MAX, s_p // 4), slopes=slopes)
        xp = _merge_ffn(xp, a_p, b_p, merge_w, TOKEN_TILE, alpha)

        a_s, q_s, qi_s, k_s, v_s, kb_s, vb_s, ki_s, kib_s, wi_s, va_s = _in_projection(
            xs, proj_w + _spatial_weights(w_s[l], b_s[l], t_s), t_s, n_s * t_s, True)
        n_all = past + t_s
        n_keys = -(-n_all // LANES) * LANES

        def with_cache(cache, new):
            width = new.shape[-1]
            c = jnp.moveaxis(cache.astype(BF16), 2, 0)
            if width > B_HEAD_DIM:
                tail = jnp.zeros(c.shape[:-1] + (width - B_HEAD_DIM,), BF16).at[..., 0].set(1.0)
                c = jnp.concatenate([c, tail], axis=-1)
            new = new.reshape(B_KV_HEADS, n_s, t_s, width)
            zero = jnp.zeros((B_KV_HEADS, n_s, n_keys - n_all, width), BF16)
            return jnp.concatenate([c, new, zero], axis=2)

        kib_all = jnp.concatenate([cache_kidx[l].astype(BF16), kib_s.reshape(n_s, t_s, IDX_DIM),
                                   jnp.zeros((n_s, n_keys - n_all, IDX_DIM), BF16)], axis=1)
        b_s_ = _dsa_attention(
            q_s, qi_s, wi_s, with_cache(cache_k[l], kb_s), with_cache(cache_v[l], vb_s), kib_all,
            n=n_s, t_q=t_s, tq=t_s, rb=t_s, key_buckets=(n_keys,), n_valid=n_all, q_off=past,
            topk=min(TOPK_MAX, n_all // 4), slopes=slopes)
        xs = _merge_ffn(xs, a_s, b_s_, merge_w, n_s * t_s, alpha)

        outs[0].append(k_p.reshape(n_p, s_p, B_KV_HEADS, B_HEAD_DIM))
        outs[1].append(v_p.reshape(n_p, s_p, B_KV_HEADS, B_HEAD_DIM))
        outs[2].append(ki_p.reshape(n_p, s_p, IDX_DIM))
        outs[3].append(k_s.reshape(n_s, t_s, B_KV_HEADS, B_HEAD_DIM))
        outs[4].append(v_s.reshape(n_s, t_s, B_KV_HEADS, B_HEAD_DIM))
        outs[5].append(ki_s.reshape(n_s, t_s, IDX_DIM))
        outs[6].append(va_s.reshape(n_s, t_s, A_WIDTH))

    return (xp.reshape(n_p, s_p, d), xs.reshape(n_s, t_s, d)) + tuple(jnp.stack(o) for o in outs)
```

```python
import functools

import numpy as np
import jax
import jax.numpy as jnp
from jax import lax
from jax.experimental import pallas as pl
from jax.experimental.pallas import tpu as pltpu

CHUNK = 64
A_WIDTH = 512
A_GROUPS = 4
A_GROUP_CH = A_WIDTH // A_GROUPS
A_SPAN = 128
B_HEADS = 8
B_HEAD_DIM = 64
B_KV_HEADS = 2
B_GROUP = B_HEADS // B_KV_HEADS
B_WIDTH = B_HEADS * B_HEAD_DIM
IDX_HEADS = 8
IDX_DIM = 64
TOPK_MAX = 256
Q_BLOCK = 128
LN_EPS = 1e-5
LOG2_E = 1.4426950408889634

LANES = 128
VMEM_LIMIT = 52 * 1024 * 1024
TOKEN_TILE = 512
KEY_CHUNK = 512
DSA_ROWS = 4 * Q_BLOCK
BISECT_STEPS = 14

F32 = jnp.float32
BF16 = jnp.bfloat16
NT_DIMS = (((1,), (1,)), ((), ()))


def _dot(a, b):
    return jnp.dot(a, b, preferred_element_type=F32)


def _dot_nt(a, b):
    return lax.dot_general(a, b, NT_DIMS, preferred_element_type=F32)


def _layer_norm(x, g, b):
    mu = jnp.mean(x, axis=-1, keepdims=True)
    xc = x - mu
    var = jnp.mean(xc * xc, axis=-1, keepdims=True)
    return xc * lax.rsqrt(var + LN_EPS) * g + b


def _inproj_kernel(x_ref, wa_ref, wq_ref, wkv_ref, wva_ref, wqi_ref, wkw_ref, lnv_ref, lnk_ref, ws_ref, bs_ref,
                   a_ref, q_ref, qi_ref, k_ref, v_ref, kb_ref, vb_ref, ki_ref, kib_ref, wi_ref, *va_refs,
                   span):
    tm = x_ref.shape[0]
    xb = x_ref[...].astype(BF16)

    za = _dot(xb, wa_ref[...])
    ga = 0.5 * za * (1.0 + lax.erf(za * np.float32(np.sqrt(0.5))))
    u = ga[:, :A_WIDTH]
    va = _layer_norm(ga[:, A_WIDTH:], lnv_ref[0:1, :], lnv_ref[1:2, :])
    if va_refs:
        va_refs[0][...] = va
    vab = va.astype(BF16)
    for s in range(tm // span):
        rows = slice(s * span, (s + 1) * span)
        for g in range(A_GROUPS):
            cols = slice(g * A_GROUP_CH, (g + 1) * A_GROUP_CH)
            mixed = _dot(ws_ref[g], vab[rows, cols]) + bs_ref[:, cols]
            a_ref[rows, cols] = (u[rows, cols] * mixed).astype(BF16)

    zq = _dot(xb, wq_ref[...])
    zqi = _dot(xb, wqi_ref[...])
    for h in range(B_HEADS):
        q_ref[h] = zq[:, h * B_HEAD_DIM:(h + 1) * B_HEAD_DIM].astype(BF16)
    for h in range(IDX_HEADS):
        qi_ref[h] = zqi[:, h * IDX_DIM:(h + 1) * IDX_DIM].astype(BF16)

    zkv = _dot(xb, wkv_ref[...])
    kvw = B_KV_HEADS * B_HEAD_DIM
    k_ref[...] = zkv[:, :kvw]
    v_ref[...] = zkv[:, kvw:]
    if len(kb_ref.shape) == 4:
        kb_ref, vb_ref, kib_ref = kb_ref.at[:, 0], vb_ref.at[:, 0], kib_ref.at[0]
    zva = _dot(xb, wva_ref[...])
    ones_lane = lax.broadcasted_iota(jnp.int32, (tm, LANES), 1) == B_HEAD_DIM
    for g in range(B_KV_HEADS):
        kb_ref[g] = zkv[:, g * B_HEAD_DIM:(g + 1) * B_HEAD_DIM].astype(BF16)
        vb_ref[g] = jnp.where(ones_lane, 1.0, zva[:, g * LANES:(g + 1) * LANES]).astype(BF16)

    zkw = _dot(xb, wkw_ref[...])
    ki = _layer_norm(zkw[:, :IDX_DIM], lnk_ref[0:1, :], lnk_ref[1:2, :])
    ki_ref[...] = ki
    kib_ref[...] = ki.astype(BF16)
    wi_ref[...] = zkw[:, IDX_DIM:IDX_DIM + IDX_HEADS] * np.float32(IDX_HEADS ** -0.5)


def _const_spec(shape):
    nd = len(shape)
    return pl.BlockSpec(shape, lambda *_: (0,) * nd)


def _in_projection(x2, wts, span, tm, emit_va, batch_rows=None):
    t, d = x2.shape
    grid = (t // tm,)
    row = lambda i: (i, 0)
    hm = lambda i: (0, i, 0)
    if batch_rows is None:
        kv_shape, kv_block, kv_map = (B_KV_HEADS, t), (B_KV_HEADS, tm), hm
        ki_shape, ki_block, ki_map = (t, IDX_DIM), (tm, IDX_DIM), row
    else:
        per = batch_rows // tm
        kv_shape = (B_KV_HEADS, t // batch_rows, batch_rows)
        kv_block, kv_map = (B_KV_HEADS, 1, tm), lambda i: (0, i // per, i % per, 0)
        ki_shape = (t // batch_rows, batch_rows, IDX_DIM)
        ki_block, ki_map = (1, tm, IDX_DIM), lambda i: (i // per, i % per, 0)
    in_specs = [pl.BlockSpec((tm, d), row)] + [_const_spec(w.shape) for w in wts]
    out_shape = [
        jax.ShapeDtypeStruct((t, A_WIDTH), BF16),
        jax.ShapeDtypeStruct((B_HEADS, t, B_HEAD_DIM), BF16),
        jax.ShapeDtypeStruct((IDX_HEADS, t, IDX_DIM), BF16),
        jax.ShapeDtypeStruct((t, B_KV_HEADS * B_HEAD_DIM), F32),
        jax.ShapeDtypeStruct((t, B_KV_HEADS * B_HEAD_DIM), F32),
        jax.ShapeDtypeStruct(kv_shape + (B_HEAD_DIM,), BF16),
        jax.ShapeDtypeStruct(kv_shape + (LANES,), BF16),
        jax.ShapeDtypeStruct((t, IDX_DIM), F32),
        jax.ShapeDtypeStruct(ki_shape, BF16),
        jax.ShapeDtypeStruct((t, IDX_HEADS), F32),
    ]
    out_specs = [
        pl.BlockSpec((tm, A_WIDTH), row),
        pl.BlockSpec((B_HEADS, tm, B_HEAD_DIM), hm),
        pl.BlockSpec((IDX_HEADS, tm, IDX_DIM), hm),
        pl.BlockSpec((tm, B_KV_HEADS * B_HEAD_DIM), row),
        pl.BlockSpec((tm, B_KV_HEADS * B_HEAD_DIM), row),
        pl.BlockSpec(kv_block + (B_HEAD_DIM,), kv_map),
        pl.BlockSpec(kv_block + (LANES,), kv_map),
        pl.BlockSpec((tm, IDX_DIM), row),
        pl.BlockSpec(ki_block, ki_map),
        pl.BlockSpec((tm, IDX_HEADS), row),
    ]
    if emit_va:
        out_shape.append(jax.ShapeDtypeStruct((t, A_WIDTH), F32))
        out_specs.append(pl.BlockSpec((tm, A_WIDTH), row))
    return pl.pallas_call(
        functools.partial(_inproj_kernel, span=span),
        grid=grid, in_specs=in_specs, out_specs=out_specs, out_shape=out_shape,
        compiler_params=pltpu.CompilerParams(dimension_semantics=("parallel",), vmem_limit_bytes=VMEM_LIMIT),
        name="in_projection",
    )(x2, *wts)


def _dsa_block(q_ref, qi_ref, wi_ref, k_ref, v_ref, ki_ref, o_ref, s_ref, d_ref, thr_ref,
               *, qb, tq, rb, row_keys, n_valid, q_off, topk, slopes):
    neg_inf = np.float32(-np.inf)
    pos_inf = np.float32(np.inf)
    kf = np.float32(topk)
    n_rb = tq // rb
    blocks = [(slice(r * rb, (r + 1) * rb), row_keys[r]) for r in range(n_rb)]

    for rows, n_keys in blocks:
        q_pos = q_off + qb * tq + rows.start + lax.broadcasted_iota(jnp.int32, (rb, 1), 0)
        qi = qi_ref[:, rows, :].reshape(IDX_HEADS * rb, IDX_DIM)
        wi = wi_ref[rows, :]
        for c0 in range(0, n_keys, KEY_CHUNK):
            kc = min(KEY_CHUNK, n_keys - c0)
            k_pos = c0 + lax.broadcasted_iota(jnp.int32, (1, kc), 1)
            lg = _dot_nt(qi, ki_ref[0, c0:c0 + kc, :])
            sc = wi[:, 0:1] * jnp.maximum(lg[0:rb], 0.0)
            for h in range(1, IDX_HEADS):
                sc = sc + wi[:, h:h + 1] * jnp.maximum(lg[h * rb:(h + 1) * rb], 0.0)
            k_chunk = k_pos // CHUNK
            if c0 + kc > n_valid:
                k_chunk = jnp.where(k_pos < n_valid, k_chunk, np.int32(2 ** 30))
            adm = k_chunk <= (q_pos // CHUNK)
            s_ref[rows, c0:c0 + kc] = jnp.where(adm, sc, neg_inf)
            d_ref[rows, c0:c0 + kc] = jnp.abs(q_pos - k_pos).astype(F32)

    def scores(blk):
        rows, n_keys = blk
        return s_ref[rows, :n_keys]

    def count_ge(blk, t):
        return jnp.sum(jnp.where(scores(blk) >= t, 1.0, 0.0), axis=1, keepdims=True)

    brackets, states, n_adms = [], [], []
    for blk in blocks:
        s = scores(blk)
        finite = s > neg_inf
        n_adm = jnp.sum(jnp.where(finite, 1.0, 0.0), axis=1, keepdims=True)
        row_max = jnp.max(s, axis=1, keepdims=True)
        row_min = jnp.min(jnp.where(finite, s, pos_inf), axis=1, keepdims=True)
        brackets.append((row_min, row_max, jnp.full((rb, 1), pos_inf, F32)))
        n_adms.append(n_adm)

    def bisect(_, carry):
        out = []
        for blk, (lo, hb, hiv) in zip(blocks, carry):
            mid = 0.5 * lo + 0.5 * hb
            ge = count_ge(blk, mid) >= kf
            out.append((jnp.where(ge, mid, lo), jnp.where(ge, hb, mid), jnp.where(ge, hiv, mid)))
        return tuple(out)

    brackets = lax.fori_loop(0, BISECT_STEPS, bisect, tuple(brackets))

    for (_, _, hiv), n_adm in zip(brackets, n_adms):
        done = jnp.where(n_adm <= kf, 1.0, 0.0)
        states.append((jnp.full((rb, 1), np.finfo(np.float32).min, F32), hiv, done))

    def n_open(states):
        return sum(jnp.sum(1.0 - done) for _, _, done in states)

    def scan_cond(carry):
        _, n_left, it = carry
        return jnp.logical_and(n_left > 0.0, it < max(row_keys))

    def scan_body(carry):
        states, _, it = carry
        out = []
        for blk, (thr, hiv, done) in zip(blocks, states):
            sv = scores(blk)
            cand = jnp.max(jnp.where(sv < hiv, sv, neg_inf), axis=1, keepdims=True)
            found = jnp.where(done > 0.5, 0.0, jnp.where(count_ge(blk, cand) >= kf, 1.0, 0.0))
            thr = jnp.where(found > 0.5, cand, thr)
            done = jnp.maximum(done, found)
            out.append((thr, jnp.where(done > 0.5, hiv, cand), done))
        return tuple(out), n_open(out), it + 1

    states, _, _ = lax.while_loop(scan_cond, scan_body, (tuple(states), n_open(states), jnp.int32(0)))

    over = []
    for blk, (thr, _, _), n_adm in zip(blocks, states, n_adms):
        thr_ref[blk[0], :] = thr
        over.append(jnp.max(jnp.where(n_adm > kf, count_ge(blk, thr), 0.0)))
    tied = functools.reduce(jnp.maximum, over) > kf

    for rows, n_keys in blocks:
        thr = thr_ref[rows, :]

        @pl.when(jnp.logical_not(tied))
        def _():
            d_ref[rows, :n_keys] = jnp.where(s_ref[rows, :n_keys] >= thr, d_ref[rows, :n_keys], pos_inf)

        @pl.when(tied)
        def _():
            n_gt = jnp.sum(jnp.where(s_ref[rows, :n_keys] > thr, 1.0, 0.0), axis=1, keepdims=True)
            room = kf - n_gt
            tri = (lax.broadcasted_iota(jnp.int32, (LANES, LANES), 0)
                   <= lax.broadcasted_iota(jnp.int32, (LANES, LANES), 1))
            tri = jnp.where(tri, 1.0, 0.0).astype(BF16)
            run = jnp.zeros((rb, 1), F32)
            for c0 in range(0, n_keys, LANES):
                blk = s_ref[rows, c0:c0 + LANES]
                eq = jnp.where(blk == thr, 1.0, 0.0)
                prefix = _dot(eq.astype(BF16), tri) + run
                keep = jnp.where(blk > thr, 1.0, jnp.where(prefix <= room, eq, 0.0))
                d_ref[rows, c0:c0 + LANES] = jnp.where(keep > 0.5, d_ref[rows, c0:c0 + LANES], pos_inf)
                run = run + jnp.sum(eq, axis=1, keepdims=True)

        qs = [q_ref[g * B_GROUP:(g + 1) * B_GROUP, rows, :].reshape(B_GROUP * rb, B_HEAD_DIM)
              for g in range(B_KV_HEADS)]
        m_run = [jnp.full((rb, 1), neg_inf, F32) for _ in range(B_HEADS)]
        acc = [jnp.zeros((rb, LANES), F32) for _ in range(B_HEADS)]
        for c0 in range(0, n_keys, KEY_CHUNK):
            kc = min(KEY_CHUNK, n_keys - c0)
            dist = d_ref[rows, c0:c0 + kc]
            for g in range(B_KV_HEADS):
                logits = _dot_nt(qs[g], k_ref[g, 0, c0:c0 + kc, :])
                ps, alphas = [], []
                for hh in range(B_GROUP):
                    h = g * B_GROUP + hh
                    lgt = logits[hh * rb:(hh + 1) * rb] - np.float32(slopes[h] * LOG2_E) * dist
                    m_new = jnp.maximum(m_run[h], jnp.max(lgt, axis=1, keepdims=True))
                    m_ref = jnp.where(m_new == neg_inf, 0.0, m_new)
                    alphas.append(jnp.exp2(m_run[h] - m_ref))
                    ps.append(jnp.exp2(lgt - m_ref).astype(BF16))
                    m_run[h] = m_new
                pv = _dot(jnp.concatenate(ps, axis=0), v_ref[g, 0, c0:c0 + kc, :])
                for hh in range(B_GROUP):
                    h = g * B_GROUP + hh
                    acc[h] = alphas[hh] * acc[h] + pv[hh * rb:(hh + 1) * rb]
        for h in range(B_HEADS):
            out = acc[h][:, :B_HEAD_DIM] / acc[h][:, B_HEAD_DIM:B_HEAD_DIM + 1]
            o_ref[rows, h * B_HEAD_DIM:(h + 1) * B_HEAD_DIM] = out.astype(BF16)


def _dsa_kernel(q_ref, qi_ref, wi_ref, k_ref, v_ref, ki_ref, o_ref, s_ref, d_ref, thr_ref,
                *, tq, rb, key_buckets, n_valid, q_off, **static):
    qb = pl.program_id(1)
    n_rb = tq // rb
    last_chunk = (q_off + (qb + 1) * tq - 1) // CHUNK
    need = jnp.minimum((last_chunk + 1) * CHUNK, n_valid)
    lower = 0
    for nk in key_buckets:
        if len(key_buckets) > 1:
            assert q_off == 0 and rb % CHUNK == 0 and nk - lower == tq
            row_keys = tuple(nk - (n_rb - 1 - r) * rb for r in range(n_rb))
        else:
            row_keys = (nk,) * n_rb
        body = functools.partial(
            _dsa_block, q_ref, qi_ref, wi_ref, k_ref.at[:, :, :nk, :], v_ref.at[:, :, :nk, :],
            ki_ref.at[:, :nk, :], o_ref, s_ref.at[:, :nk], d_ref.at[:, :nk], thr_ref,
            qb=qb, tq=tq, rb=rb, row_keys=row_keys, n_valid=n_valid, q_off=q_off, **static)
        if len(key_buckets) == 1:
            body()
        else:
            pl.when(jnp.logical_and(need > lower, need <= nk))(body)
        lower = nk


def _dsa_attention(q_hm, qi_hm, wi, kb, vb, kib, *, n, t_q, tq, rb, key_buckets, n_valid, q_off, topk, slopes):
    n_keys = kb.shape[2]
    nqb = t_q // tq
    qmap = lambda b, j: (0, b * nqb + j, 0)
    rmap = lambda b, j: (b * nqb + j, 0)
    in_specs = [
        pl.BlockSpec((B_HEADS, tq, B_HEAD_DIM), qmap),
        pl.BlockSpec((IDX_HEADS, tq, IDX_DIM), qmap),
        pl.BlockSpec((tq, IDX_HEADS), rmap),
        pl.BlockSpec((B_KV_HEADS, 1, n_keys, B_HEAD_DIM), lambda b, j: (0, b, 0, 0)),
        pl.BlockSpec((B_KV_HEADS, 1, n_keys, LANES), lambda b, j: (0, b, 0, 0)),
        pl.BlockSpec((1, n_keys, IDX_DIM), lambda b, j: (b, 0, 0)),
    ]
    return pl.pallas_call(
        functools.partial(_dsa_kernel, tq=tq, rb=rb, key_buckets=key_buckets, n_valid=n_valid, q_off=q_off,
                          topk=topk, slopes=slopes),
        grid=(n, nqb), in_specs=in_specs,
        out_specs=pl.BlockSpec((tq, B_WIDTH), rmap),
        out_shape=jax.ShapeDtypeStruct((n * t_q, B_WIDTH), BF16),
        scratch_shapes=[pltpu.VMEM((tq, n_keys), F32), pltpu.VMEM((tq, n_keys), F32), pltpu.VMEM((tq, 1), F32)],
        compiler_params=pltpu.CompilerParams(dimension_semantics=("parallel", "arbitrary"),
                                             vmem_limit_bytes=VMEM_LIMIT),
        name="dsa_attention",
    )(q_hm, qi_hm, wi, kb, vb, kib)


def _merge_kernel(x_ref, a_ref, b_ref, wg_ref, wb_ref, wo_ref, ln1_ref, w1_ref, b1_ref, w2_ref, b2_ref, ln2_ref,
                  y_ref, *, alpha):
    d = x_ref.shape[1]
    x = x_ref[...]
    gates = jax.nn.sigmoid(_dot(x.astype(BF16), wg_ref[...]))
    m = gates[:, :d] * _dot(a_ref[...], wb_ref[0]) + gates[:, d:] * _dot(b_ref[...], wb_ref[1])
    h = _layer_norm(alpha * x + _dot(m.astype(BF16), wo_ref[...]), ln1_ref[0:1, :], ln1_ref[1:2, :])
    f = jnp.square(jnp.maximum(_dot(h.astype(BF16), w1_ref[...]) + b1_ref[...], 0.0))
    f = _dot(f.astype(BF16), w2_ref[...]) + b2_ref[...]
    y_ref[...] = _layer_norm(alpha * h + f, ln2_ref[0:1, :], ln2_ref[1:2, :])


def _resident_spec(shape):
    nd = len(shape)
    return pl.BlockSpec(shape, lambda *_: (0,) * nd, pipeline_mode=pl.Buffered(1))


def _merge_ffn(x2, a, b, wts, tm, alpha):
    t, d = x2.shape
    row = lambda i: (i, 0)
    in_specs = [pl.BlockSpec((tm, d), row), pl.BlockSpec((tm, A_WIDTH), row), pl.BlockSpec((tm, B_WIDTH), row)]
    in_specs += [_resident_spec(w.shape) for w in wts]
    return pl.pallas_call(
        functools.partial(_merge_kernel, alpha=np.float32(alpha)),
        grid=(t // tm,), in_specs=in_specs,
        out_specs=pl.BlockSpec((tm, d), row),
        out_shape=jax.ShapeDtypeStruct((t, d), F32),
        compiler_params=pltpu.CompilerParams(dimension_semantics=("parallel",), vmem_limit_bytes=VMEM_LIMIT),
        name="merge_ffn",
    )(x2, a, b, *wts)


def _key_buckets(n_keys):
    return tuple(range(DSA_ROWS, n_keys, DSA_ROWS)) + (n_keys,)


def _spatial_weights(w_s, b_s, span):
    pos = jnp.arange(A_SPAN)
    mask = (pos[None, :] // CHUNK) <= (pos[:, None] // CHUNK)
    ws = jnp.where(mask[None], w_s, 0.0)[:, :span, :span].astype(BF16)
    bs = jnp.repeat(b_s[:, :span].T, A_GROUP_CH, axis=1)
    return ws, bs


def kernel(x_prompt, x_sample, cache_k, cache_v, cache_kidx, w_in, lnv_g, lnv_b, w_s, b_s, lnk_g, lnk_b,
           w_branch, w_out, ln1_g, ln1_b, w_ff1, b_ff1, w_ff2, b_ff2, ln2_g, ln2_b):
    depth = w_in.shape[0]
    n_p, s_p, d = x_prompt.shape
    n_s, t_s, _ = x_sample.shape
    past = cache_k.shape[2]
    alpha = (2 * depth) ** 0.25
    slopes = tuple(float(2.0 ** (-8.0 * h / B_HEADS)) for h in range(1, B_HEADS + 1))
    kvw = B_KV_HEADS * B_HEAD_DIM
    c_a = 2 * A_WIDTH
    c_q = c_a + B_WIDTH
    c_k = c_q + kvw
    c_v = c_k + kvw
    c_qi = c_v + IDX_HEADS * IDX_DIM
    c_wi = c_qi + IDX_DIM + IDX_HEADS

    xp = x_prompt.reshape(n_p * s_p, d)
    xs = x_sample.reshape(n_s * t_s, d)
    outs = [[] for _ in range(7)]
    for l in range(depth):
        w = w_in[l]
        pad = jnp.zeros((d, LANES - (c_wi - c_qi)), F32)
        proj_w = (
            w[:, :c_a].astype(BF16),
            (w[:, c_a:c_q] * (LOG2_E * B_HEAD_DIM ** -0.5)).astype(BF16),
            w[:, c_q:c_v].astype(BF16),
            jnp.pad(w[:, c_k:c_v].reshape(d, B_KV_HEADS, B_HEAD_DIM),
                    ((0, 0), (0, 0), (0, LANES - B_HEAD_DIM))).reshape(d, B_KV_HEADS * LANES).astype(BF16),
            (w[:, c_v:c_qi] * (IDX_DIM ** -0.5)).astype(BF16),
            jnp.concatenate([w[:, c_qi:c_wi], pad], axis=1).astype(BF16),
            jnp.stack([lnv_g[l], lnv_b[l]]),
            jnp.stack([lnk_g[l], lnk_b[l]]),
        )
        merge_w = (
            w[:, c_wi:].astype(BF16),
            w_branch[l].astype(BF16),
            w_out[l].astype(BF16),
            jnp.stack([ln1_g[l], ln1_b[l]]),
            w_ff1[l].astype(BF16),
            b_ff1[l][None, :],
            w_ff2[l].astype(BF16),
            b_ff2[l][None, :],
            jnp.stack([ln2_g[l], ln2_b[l]]),
        )

        a_p, q_p, qi_p, k_p, v_p, kb_p, vb_p, ki_p, kib_p, wi_p = _in_projection(
            xp, proj_w + _spatial_weights(w_s[l], b_s[l], A_SPAN), A_SPAN, TOKEN_TILE, False, batch_rows=s_p)
        b_p = _dsa_attention(
            q_p, qi_p, wi_p, kb_p, vb_p, kib_p,
            n=n_p, t_q=s_p, tq=DSA_ROWS, rb=Q_BLOCK, key_buckets=_key_buckets(s_p), n_valid=s_p, q_off=0,
            topk=min(TOPK_MAX, s_p // 4), slopes=slopes)
        xp = _merge_ffn(xp, a_p, b_p, merge_w, TOKEN_TILE, alpha)

        a_s, q_s, qi_s, k_s, v_s, kb_s, vb_s, ki_s, kib_s, wi_s, va_s = _in_projection(
            xs, proj_w + _spatial_weights(w_s[l], b_s[l], t_s), t_s, n_s * t_s, True)
        n_all = past + t_s
        n_keys = -(-n_all // LANES) * LANES

        def with_cache(cache, new):
            width = new.shape[-1]
            c = jnp.moveaxis(cache.astype(BF16), 2, 0)
            if width > B_HEAD_DIM:
                tail = jnp.zeros(c.shape[:-1] + (width - B_HEAD_DIM,), BF16).at[..., 0].set(1.0)
                c = jnp.concatenate([c, tail], axis=-1)
            new = new.reshape(B_KV_HEADS, n_s, t_s, width)
            zero = jnp.zeros((B_KV_HEADS, n_s, n_keys - n_all, width), BF16)
            return jnp.concatenate([c, new, zero], axis=2)

        kib_all = jnp.concatenate([cache_kidx[l].astype(BF16), kib_s.reshape(n_s, t_s, IDX_DIM),
                                   jnp.zeros((n_s, n_keys - n_all, IDX_DIM), BF16)], axis=1)
        b_s_ = _dsa_attention(
            q_s, qi_s, wi_s, with_cache(cache_k[l], kb_s), with_cache(cache_v[l], vb_s), kib_all,
            n=n_s, t_q=t_s, tq=t_s, rb=t_s, key_buckets=(n_keys,), n_valid=n_all, q_off=past,
            topk=min(TOPK_MAX, n_all // 4), slopes=slopes)
        xs = _merge_ffn(xs, a_s, b_s_, merge_w, n_s * t_s, alpha)

        outs[0].append(k_p.reshape(n_p, s_p, B_KV_HEADS, B_HEAD_DIM))
        outs[1].append(v_p.reshape(n_p, s_p, B_KV_HEADS, B_HEAD_DIM))
        outs[2].append(ki_p.reshape(n_p, s_p, IDX_DIM))
        outs[3].append(k_s.reshape(n_s, t_s, B_KV_HEADS, B_HEAD_DIM))
        outs[4].append(v_s.reshape(n_s, t_s, B_KV_HEADS, B_HEAD_DIM))
        outs[5].append(ki_s.reshape(n_s, t_s, IDX_DIM))
        outs[6].append(va_s.reshape(n_s, t_s, A_WIDTH))

    return (xp.reshape(n_p, s_p, d), xs.reshape(n_s, t_s, d)) + tuple(jnp.stack(o) for o in outs)
```

```python
import functools

import numpy as np
import jax
import jax.numpy as jnp
from jax import lax
from jax.experimental import pallas as pl
from jax.experimental.pallas import tpu as pltpu

CHUNK = 64
A_WIDTH = 512
A_GROUPS = 4
A_GROUP_CH = A_WIDTH // A_GROUPS
A_SPAN = 128
B_HEADS = 8
B_HEAD_DIM = 64
B_KV_HEADS = 2
B_GROUP = B_HEADS // B_KV_HEADS
B_WIDTH = B_HEADS * B_HEAD_DIM
IDX_HEADS = 8
IDX_DIM = 64
TOPK_MAX = 256
Q_BLOCK = 128
LN_EPS = 1e-5
LOG2_E = 1.4426950408889634

LANES = 128
VMEM_LIMIT = 52 * 1024 * 1024
TOKEN_TILE = 512
KEY_CHUNK = 512
DSA_ROWS = 4 * Q_BLOCK
BISECT_STEPS = 14

F32 = jnp.float32
BF16 = jnp.bfloat16
NT_DIMS = (((1,), (1,)), ((), ()))


def _dot(a, b):
    return jnp.dot(a, b, preferred_element_type=F32)


def _dot_nt(a, b):
    return lax.dot_general(a, b, NT_DIMS, preferred_element_type=F32)


def _layer_norm(x, g, b):
    mu = jnp.mean(x, axis=-1, keepdims=True)
    xc = x - mu
    var = jnp.mean(xc * xc, axis=-1, keepdims=True)
    return xc * lax.rsqrt(var + LN_EPS) * g + b


def _inproj_kernel(x_ref, wa_ref, wq_ref, wkv_ref, wva_ref, wqi_ref, wkw_ref, lnv_ref, lnk_ref, ws_ref, bs_ref,
                   a_ref, q_ref, qi_ref, k_ref, v_ref, kb_ref, vb_ref, ki_ref, kib_ref, wi_ref, *va_refs,
                   span):
    tm = x_ref.shape[0]
    xb = x_ref[...].astype(BF16)

    za = _dot(xb, wa_ref[...])
    ga = 0.5 * za * (1.0 + lax.erf(za * np.float32(np.sqrt(0.5))))
    u = ga[:, :A_WIDTH]
    va = _layer_norm(ga[:, A_WIDTH:], lnv_ref[0:1, :], lnv_ref[1:2, :])
    if va_refs:
        va_refs[0][...] = va
    vab = va.astype(BF16)
    for s in range(tm // span):
        rows = slice(s * span, (s + 1) * span)
        for g in range(A_GROUPS):
            cols = slice(g * A_GROUP_CH, (g + 1) * A_GROUP_CH)
            mixed = _dot(ws_ref[g], vab[rows, cols]) + bs_ref[:, cols]
            a_ref[rows, cols] = (u[rows, cols] * mixed).astype(BF16)

    zq = _dot(xb, wq_ref[...])
    zqi = _dot(xb, wqi_ref[...])
    for h in range(B_HEADS):
        q_ref[h] = zq[:, h * B_HEAD_DIM:(h + 1) * B_HEAD_DIM].astype(BF16)
    for h in range(IDX_HEADS):
        qi_ref[h] = zqi[:, h * IDX_DIM:(h + 1) * IDX_DIM].astype(BF16)

    zkv = _dot(xb, wkv_ref[...])
    kvw = B_KV_HEADS * B_HEAD_DIM
    k_ref[...] = zkv[:, :kvw]
    v_ref[...] = zkv[:, kvw:]
    if len(kb_ref.shape) == 4:
        kb_ref, vb_ref, kib_ref = kb_ref.at[:, 0], vb_ref.at[:, 0], kib_ref.at[0]
    zva = _dot(xb, wva_ref[...])
    ones_lane = lax.broadcasted_iota(jnp.int32, (tm, LANES), 1) == B_HEAD_DIM
    for g in range(B_KV_HEADS):
        kb_ref[g] = zkv[:, g * B_HEAD_DIM:(g + 1) * B_HEAD_DIM].astype(BF16)
        vb_ref[g] = jnp.where(ones_lane, 1.0, zva[:, g * LANES:(g + 1) * LANES]).astype(BF16)

    zkw = _dot(xb, wkw_ref[...])
    ki = _layer_norm(zkw[:, :IDX_DIM], lnk_ref[0:1, :], lnk_ref[1:2, :])
    ki_ref[...] = ki
    kib_ref[...] = ki.astype(BF16)
    wi_ref[...] = zkw[:, IDX_DIM:IDX_DIM + IDX_HEADS] * np.float32(IDX_HEADS ** -0.5)


def _const_spec(shape):
    nd = len(shape)
    return pl.BlockSpec(shape, lambda *_: (0,) * nd)


def _in_projection(x2, wts, span, tm, emit_va, batch_rows=None):
    t, d = x2.shape
    grid = (t // tm,)
    row = lambda i: (i, 0)
    hm = lambda i: (0, i, 0)
    if batch_rows is None:
        kv_shape, kv_block, kv_map = (B_KV_HEADS, t), (B_KV_HEADS, tm), hm
        ki_shape, ki_block, ki_map = (t, IDX_DIM), (tm, IDX_DIM), row
    else:
        per = batch_rows // tm
        kv_shape = (B_KV_HEADS, t // batch_rows, batch_rows)
        kv_block, kv_map = (B_KV_HEADS, 1, tm), lambda i: (0, i // per, i % per, 0)
        ki_shape = (t // batch_rows, batch_rows, IDX_DIM)
        ki_block, ki_map = (1, tm, IDX_DIM), lambda i: (i // per, i % per, 0)
    in_specs = [pl.BlockSpec((tm, d), row)] + [_const_spec(w.shape) for w in wts]
    out_shape = [
        jax.ShapeDtypeStruct((t, A_WIDTH), BF16),
        jax.ShapeDtypeStruct((B_HEADS, t, B_HEAD_DIM), BF16),
        jax.ShapeDtypeStruct((IDX_HEADS, t, IDX_DIM), BF16),
        jax.ShapeDtypeStruct((t, B_KV_HEADS * B_HEAD_DIM), F32),
        jax.ShapeDtypeStruct((t, B_KV_HEADS * B_HEAD_DIM), F32),
        jax.ShapeDtypeStruct(kv_shape + (B_HEAD_DIM,), BF16),
        jax.ShapeDtypeStruct(kv_shape + (LANES,), BF16),
        jax.ShapeDtypeStruct((t, IDX_DIM), F32),
        jax.ShapeDtypeStruct(ki_shape, BF16),
        jax.ShapeDtypeStruct((t, IDX_HEADS), F32),
    ]
    out_specs = [
        pl.BlockSpec((tm, A_WIDTH), row),
        pl.BlockSpec((B_HEADS, tm, B_HEAD_DIM), hm),
        pl.BlockSpec((IDX_HEADS, tm, IDX_DIM), hm),
        pl.BlockSpec((tm, B_KV_HEADS * B_HEAD_DIM), row),
        pl.BlockSpec((tm, B_KV_HEADS * B_HEAD_DIM), row),
        pl.BlockSpec(kv_block + (B_HEAD_DIM,), kv_map),
        pl.BlockSpec(kv_block + (LANES,), kv_map),
        pl.BlockSpec((tm, IDX_DIM), row),
        pl.BlockSpec(ki_block, ki_map),
        pl.BlockSpec((tm, IDX_HEADS), row),
    ]
    if emit_va:
        out_shape.append(jax.ShapeDtypeStruct((t, A_WIDTH), F32))
        out_specs.append(pl.BlockSpec((tm, A_WIDTH), row))
    return pl.pallas_call(
        functools.partial(_inproj_kernel, span=span),
        grid=grid, in_specs=in_specs, out_specs=out_specs, out_shape=out_shape,
        compiler_params=pltpu.CompilerParams(dimension_semantics=("parallel",), vmem_limit_bytes=VMEM_LIMIT),
        name="in_projection",
    )(x2, *wts)


def _dsa_block(q_ref, qi_ref, wi_ref, k_ref, v_ref, ki_ref, o_ref, s_ref, d_ref, thr_ref,
               *, qb, tq, rb, row_keys, n_valid, q_off, topk, slopes):
    neg_inf = np.float32(-np.inf)
    pos_inf = np.float32(np.inf)
    kf = np.float32(topk)
    n_rb = tq // rb
    blocks = [(slice(r * rb, (r + 1) * rb), row_keys[r]) for r in range(n_rb)]

    for rows, n_keys in blocks:
        q_pos = q_off + qb * tq + rows.start + lax.broadcasted_iota(jnp.int32, (rb, 1), 0)
        qi = qi_ref[:, rows, :].reshape(IDX_HEADS * rb, IDX_DIM)
        wi = wi_ref[rows, :]
        for c0 in range(0, n_keys, KEY_CHUNK):
            kc = min(KEY_CHUNK, n_keys - c0)
            k_pos = c0 + lax.broadcasted_iota(jnp.int32, (1, kc), 1)
            lg = _dot_nt(qi, ki_ref[0, c0:c0 + kc, :])
            sc = wi[:, 0:1] * jnp.maximum(lg[0:rb], 0.0)
            for h in range(1, IDX_HEADS):
                sc = sc + wi[:, h:h + 1] * jnp.maximum(lg[h * rb:(h + 1) * rb], 0.0)
            k_chunk = k_pos // CHUNK
            if c0 + kc > n_valid:
                k_chunk = jnp.where(k_pos < n_valid, k_chunk, np.int32(2 ** 30))
            adm = k_chunk <= (q_pos // CHUNK)
            s_ref[rows, c0:c0 + kc] = jnp.where(adm, sc, neg_inf)
            d_ref[rows, c0:c0 + kc] = jnp.abs(q_pos - k_pos).astype(F32)

    def scores(blk):
        rows, n_keys = blk
        return s_ref[rows, :n_keys]

    def count_ge(blk, t):
        return jnp.sum(jnp.where(scores(blk) >= t, 1.0, 0.0), axis=1, keepdims=True)

    brackets, states, n_adms = [], [], []
    for blk in blocks:
        s = scores(blk)
        finite = s > neg_inf
        n_adm = jnp.sum(jnp.where(finite, 1.0, 0.0), axis=1, keepdims=True)
        row_max = jnp.max(s, axis=1, keepdims=True)
        row_min = jnp.min(jnp.where(finite, s, pos_inf), axis=1, keepdims=True)
        brackets.append((row_min, row_max, jnp.full((rb, 1), pos_inf, F32)))
        n_adms.append(n_adm)

    def bisect(_, carry):
        out = []
        for blk, (lo, hb, hiv) in zip(blocks, carry):
            mid = 0.5 * lo + 0.5 * hb
            ge = count_ge(blk, mid) >= kf
            out.append((jnp.where(ge, mid, lo), jnp.where(ge, hb, mid), jnp.where(ge, hiv, mid)))
        return tuple(out)

    brackets = lax.fori_loop(0, BISECT_STEPS, bisect, tuple(brackets))

    for (_, _, hiv), n_adm in zip(brackets, n_adms):
        done = jnp.where(n_adm <= kf, 1.0, 0.0)
        states.append((jnp.full((rb, 1), np.finfo(np.float32).min, F32), hiv, done))

    def n_open(states):
        return sum(jnp.sum(1.0 - done) for _, _, done in states)

    def scan_cond(carry):
        _, n_left, it = carry
        return jnp.logical_and(n_left > 0.0, it < max(row_keys))

    def scan_body(carry):
        states, _, it = carry
        out = []
        for blk, (thr, hiv, done) in zip(blocks, states):
            sv = scores(blk)
            cand = jnp.max(jnp.where(sv < hiv, sv, neg_inf), axis=1, keepdims=True)
            found = jnp.where(done > 0.5, 0.0, jnp.where(count_ge(blk, cand) >= kf, 1.0, 0.0))
            thr = jnp.where(found > 0.5, cand, thr)
            done = jnp.maximum(done, found)
            out.append((thr, jnp.where(done > 0.5, hiv, cand), done))
        return tuple(out), n_open(out), it + 1

    states, _, _ = lax.while_loop(scan_cond, scan_body, (tuple(states), n_open(states), jnp.int32(0)))

    over = []
    for blk, (thr, _, _), n_adm in zip(blocks, states, n_adms):
        thr_ref[blk[0], :] = thr
        over.append(jnp.max(jnp.where(n_adm > kf, count_ge(blk, thr), 0.0)))
    tied = functools.reduce(jnp.maximum, over) > kf

    for rows, n_keys in blocks:
        thr = thr_ref[rows, :]

        @pl.when(jnp.logical_not(tied))
        def _():
            d_ref[rows, :n_keys] = jnp.where(s_ref[rows, :n_keys] >= thr, d_ref[rows, :n_keys], pos_inf)

        @pl.when(tied)
        def _():
            n_gt = jnp.sum(jnp.where(s_ref[rows, :n_keys] > thr, 1.0, 0.0), axis=1, keepdims=True)
            room = kf - n_gt
            tri = (lax.broadcasted_iota(jnp.int32, (LANES, LANES), 0)
                   <= lax.broadcasted_iota(jnp.int32, (LANES, LANES), 1))
            tri = jnp.where(tri, 1.0, 0.0).astype(BF16)
            run = jnp.zeros((rb, 1), F32)
            for c0 in range(0, n_keys, LANES):
                blk = s_ref[rows, c0:c0 + LANES]
                eq = jnp.where(blk == thr, 1.0, 0.0)
                prefix = _dot(eq.astype(BF16), tri) + run
                keep = jnp.where(blk > thr, 1.0, jnp.where(prefix <= room, eq, 0.0))
                d_ref[rows, c0:c0 + LANES] = jnp.where(keep > 0.5, d_ref[rows, c0:c0 + LANES], pos_inf)
                run = run + jnp.sum(eq, axis=1, keepdims=True)

        qs = [q_ref[g * B_GROUP:(g + 1) * B_GROUP, rows, :].reshape(B_GROUP * rb, B_HEAD_DIM)
              for g in range(B_KV_HEADS)]
        m_run = [jnp.full((rb, 1), neg_inf, F32) for _ in range(B_HEADS)]
        acc = [jnp.zeros((rb, LANES), F32) for _ in range(B_HEADS)]
        for c0 in range(0, n_keys, KEY_CHUNK):
            kc = min(KEY_CHUNK, n_keys - c0)
            dist = d_ref[rows, c0:c0 + kc]
            for g in range(B_KV_HEADS):
                logits = _dot_nt(qs[g], k_ref[g, 0, c0:c0 + kc, :])
                ps, alphas = [], []
                for hh in range(B_GROUP):
                    h = g * B_GROUP + hh
                    lgt = logits[hh * rb:(hh + 1) * rb] - np.float32(slopes[h] * LOG2_E) * dist
                    m_new = jnp.maximum(m_run[h], jnp.max(lgt, axis=1, keepdims=True))
                    m_ref = jnp.where(m_new == neg_inf, 0.0, m_new)
                    alphas.append(jnp.exp2(m_run[h] - m_ref))
                    ps.append(jnp.exp2(lgt - m_ref).astype(BF16))
                    m_run[h] = m_new
                pv = _dot(jnp.concatenate(ps, axis=0), v_ref[g, 0, c0:c0 + kc, :])
                for hh in range(B_GROUP):
                    h = g * B_GROUP + hh
                    acc[h] = alphas[hh] * acc[h] + pv[hh * rb:(hh + 1) * rb]
        for h in range(B_HEADS):
            out = acc[h][:, :B_HEAD_DIM] / acc[h][:, B_HEAD_DIM:B_HEAD_DIM + 1]
            o_ref[rows, h * B_HEAD_DIM:(h + 1) * B_HEAD_DIM] = out.astype(BF16)


def _dsa_kernel(q_ref, qi_ref, wi_ref, k_ref, v_ref, ki_ref, o_ref, s_ref, d_ref, thr_ref,
                *, tq, rb, key_buckets, n_valid, q_off, **static):
    qb = pl.program_id(0)
    n_rb = tq // rb
    last_chunk = (q_off + (qb + 1) * tq - 1) // CHUNK
    need = jnp.minimum((last_chunk + 1) * CHUNK, n_valid)
    lower = 0
    for nk in key_buckets:
        if len(key_buckets) > 1:
            assert q_off == 0 and rb % CHUNK == 0 and nk - lower == tq
            row_keys = tuple(nk - (n_rb - 1 - r) * rb for r in range(n_rb))
        else:
            row_keys = (nk,) * n_rb
        body = functools.partial(
            _dsa_block, q_ref, qi_ref, wi_ref, k_ref.at[:, :, :nk, :], v_ref.at[:, :, :nk, :],
            ki_ref.at[:, :nk, :], o_ref, s_ref.at[:, :nk], d_ref.at[:, :nk], thr_ref,
            qb=qb, tq=tq, rb=rb, row_keys=row_keys, n_valid=n_valid, q_off=q_off, **static)
        if len(key_buckets) == 1:
            body()
        else:
            pl.when(jnp.logical_and(need > lower, need <= nk))(body)
        lower = nk


def _dsa_attention(q_hm, qi_hm, wi, kb, vb, kib, *, n, t_q, tq, rb, key_buckets, n_valid, q_off, topk, slopes):
    n_keys = kb.shape[2]
    nqb = t_q // tq
    qmap = lambda j, b: (0, b * nqb + j, 0)
    rmap = lambda j, b: (b * nqb + j, 0)
    in_specs = [
        pl.BlockSpec((B_HEADS, tq, B_HEAD_DIM), qmap),
        pl.BlockSpec((IDX_HEADS, tq, IDX_DIM), qmap),
        pl.BlockSpec((tq, IDX_HEADS), rmap),
        pl.BlockSpec((B_KV_HEADS, 1, n_keys, B_HEAD_DIM), lambda j, b: (0, b, 0, 0)),
        pl.BlockSpec((B_KV_HEADS, 1, n_keys, LANES), lambda j, b: (0, b, 0, 0)),
        pl.BlockSpec((1, n_keys, IDX_DIM), lambda j, b: (b, 0, 0)),
    ]
    return pl.pallas_call(
        functools.partial(_dsa_kernel, tq=tq, rb=rb, key_buckets=key_buckets, n_valid=n_valid, q_off=q_off,
                          topk=topk, slopes=slopes),
        grid=(nqb, n), in_specs=in_specs,
        out_specs=pl.BlockSpec((tq, B_WIDTH), rmap),
        out_shape=jax.ShapeDtypeStruct((n * t_q, B_WIDTH), BF16),
        scratch_shapes=[pltpu.VMEM((tq, n_keys), F32), pltpu.VMEM((tq, n_keys), F32), pltpu.VMEM((tq, 1), F32)],
        compiler_params=pltpu.CompilerParams(dimension_semantics=("parallel", "arbitrary"),
                                             vmem_limit_bytes=VMEM_LIMIT),
        name="dsa_attention",
    )(q_hm, qi_hm, wi, kb, vb, kib)


def _merge_kernel(x_ref, a_ref, b_ref, wg_ref, wb_ref, wo_ref, ln1_ref, w1_ref, b1_ref, w2_ref, b2_ref, ln2_ref,
                  y_ref, *, alpha):
    d = x_ref.shape[1]
    x = x_ref[...]
    gates = jax.nn.sigmoid(_dot(x.astype(BF16), wg_ref[...]))
    m = gates[:, :d] * _dot(a_ref[...], wb_ref[0]) + gates[:, d:] * _dot(b_ref[...], wb_ref[1])
    h = _layer_norm(alpha * x + _dot(m.astype(BF16), wo_ref[...]), ln1_ref[0:1, :], ln1_ref[1:2, :])
    f = jnp.square(jnp.maximum(_dot(h.astype(BF16), w1_ref[...]) + b1_ref[...], 0.0))
    f = _dot(f.astype(BF16), w2_ref[...]) + b2_ref[...]
    y_ref[...] = _layer_norm(alpha * h + f, ln2_ref[0:1, :], ln2_ref[1:2, :])


def _resident_spec(shape):
    nd = len(shape)
    return pl.BlockSpec(shape, lambda *_: (0,) * nd, pipeline_mode=pl.Buffered(1))


def _merge_ffn(x2, a, b, wts, tm, alpha):
    t, d = x2.shape
    row = lambda i: (i, 0)
    in_specs = [pl.BlockSpec((tm, d), row), pl.BlockSpec((tm, A_WIDTH), row), pl.BlockSpec((tm, B_WIDTH), row)]
    in_specs += [_resident_spec(w.shape) for w in wts]
    return pl.pallas_call(
        functools.partial(_merge_kernel, alpha=np.float32(alpha)),
        grid=(t // tm,), in_specs=in_specs,
        out_specs=pl.BlockSpec((tm, d), row),
        out_shape=jax.ShapeDtypeStruct((t, d), F32),
        compiler_params=pltpu.CompilerParams(dimension_semantics=("parallel",), vmem_limit_bytes=VMEM_LIMIT),
        name="merge_ffn",
    )(x2, a, b, *wts)


def _key_buckets(n_keys):
    return tuple(range(DSA_ROWS, n_keys, DSA_ROWS)) + (n_keys,)


def _spatial_weights(w_s, b_s, span):
    pos = jnp.arange(A_SPAN)
    mask = (pos[None, :] // CHUNK) <= (pos[:, None] // CHUNK)
    ws = jnp.where(mask[None], w_s, 0.0)[:, :span, :span].astype(BF16)
    bs = jnp.repeat(b_s[:, :span].T, A_GROUP_CH, axis=1)
    return ws, bs


def kernel(x_prompt, x_sample, cache_k, cache_v, cache_kidx, w_in, lnv_g, lnv_b, w_s, b_s, lnk_g, lnk_b,
           w_branch, w_out, ln1_g, ln1_b, w_ff1, b_ff1, w_ff2, b_ff2, ln2_g, ln2_b):
    depth = w_in.shape[0]
    n_p, s_p, d = x_prompt.shape
    n_s, t_s, _ = x_sample.shape
    past = cache_k.shape[2]
    alpha = (2 * depth) ** 0.25
    slopes = tuple(float(2.0 ** (-8.0 * h / B_HEADS)) for h in range(1, B_HEADS + 1))
    kvw = B_KV_HEADS * B_HEAD_DIM
    c_a = 2 * A_WIDTH
    c_q = c_a + B_WIDTH
    c_k = c_q + kvw
    c_v = c_k + kvw
    c_qi = c_v + IDX_HEADS * IDX_DIM
    c_wi = c_qi + IDX_DIM + IDX_HEADS

    xp = x_prompt.reshape(n_p * s_p, d)
    xs = x_sample.reshape(n_s * t_s, d)
    outs = [[] for _ in range(7)]
    for l in range(depth):
        w = w_in[l]
        pad = jnp.zeros((d, LANES - (c_wi - c_qi)), F32)
        proj_w = (
            w[:, :c_a].astype(BF16),
            (w[:, c_a:c_q] * (LOG2_E * B_HEAD_DIM ** -0.5)).astype(BF16),
            w[:, c_q:c_v].astype(BF16),
            jnp.pad(w[:, c_k:c_v].reshape(d, B_KV_HEADS, B_HEAD_DIM),
                    ((0, 0), (0, 0), (0, LANES - B_HEAD_DIM))).reshape(d, B_KV_HEADS * LANES).astype(BF16),
            (w[:, c_v:c_qi] * (IDX_DIM ** -0.5)).astype(BF16),
            jnp.concatenate([w[:, c_qi:c_wi], pad], axis=1).astype(BF16),
            jnp.stack([lnv_g[l], lnv_b[l]]),
            jnp.stack([lnk_g[l], lnk_b[l]]),
        )
        merge_w = (
            w[:, c_wi:].astype(BF16),
            w_branch[l].astype(BF16),
            w_out[l].astype(BF16),
            jnp.stack([ln1_g[l], ln1_b[l]]),
            w_ff1[l].astype(BF16),
            b_ff1[l][None, :],
            w_ff2[l].astype(BF16),
            b_ff2[l][None, :],
            jnp.stack([ln2_g[l], ln2_b[l]]),
        )

        a_p, q_p, qi_p, k_p, v_p, kb_p, vb_p, ki_p, kib_p, wi_p = _in_projection(
            xp, proj_w + _spatial_weights(w_s[l], b_s[l], A_SPAN), A_SPAN, TOKEN_TILE, False, batch_rows=s_p)
        b_p = _dsa_attention(
            q_p, qi_p, wi_p, kb_p, vb_p, kib_p,
            n=n_p, t_q=s_p, tq=DSA_ROWS, rb=Q_BLOCK, key_buckets=_key_buckets(s_p), n_valid=s_p, q_off=0,
            topk=min(TOPK_MAX, s_p // 4), slopes=slopes)
        xp = _merge_ffn(xp, a_p, b_p, merge_w, TOKEN_TILE, alpha)

        a_s, q_s, qi_s, k_s, v_s, kb_s, vb_s, ki_s, kib_s, wi_s, va_s = _in_projection(
            xs, proj_w + _spatial_weights(w_s[l], b_s[l], t_s), t_s, n_s * t_s, True)
        n_all = past + t_s
        n_keys = -(-n_all // LANES) * LANES

        def with_cache(cache, new):
            width = new.shape[-1]
            c = jnp.moveaxis(cache.astype(BF16), 2, 0)
            if width > B_HEAD_DIM:
                tail = jnp.zeros(c.shape[:-1] + (width - B_HEAD_DIM,), BF16).at[..., 0].set(1.0)
                c = jnp.concatenate([c, tail], axis=-1)
            new = new.reshape(B_KV_HEADS, n_s, t_s, width)
            zero = jnp.zeros((B_KV_HEADS, n_s, n_keys - n_all, width), BF16)
            return jnp.concatenate([c, new, zero], axis=2)

        kib_all = jnp.concatenate([cache_kidx[l].astype(BF16), kib_s.reshape(n_s, t_s, IDX_DIM),
                                   jnp.zeros((n_s, n_keys - n_all, IDX_DIM), BF16)], axis=1)
        b_s_ = _dsa_attention(
            q_s, qi_s, wi_s, with_cache(cache_k[l], kb_s), with_cache(cache_v[l], vb_s), kib_all,
            n=n_s, t_q=t_s, tq=t_s, rb=t_s, key_buckets=(n_keys,), n_valid=n_all, q_off=past,
            topk=min(TOPK_MAX, n_all // 4), slopes=slopes)
        xs = _merge_ffn(xs, a_s, b_s_, merge_w, n_s * t_s, alpha)

        outs[0].append(k_p.reshape(n_p, s_p, B_KV_HEADS, B_HEAD_DIM))
        outs[1].append(v_p.reshape(n_p, s_p, B_KV_HEADS, B_HEAD_DIM))
        outs[2].append(ki_p.reshape(n_p, s_p, IDX_DIM))
        outs[3].append(k_s.reshape(n_s, t_s, B_KV_HEADS, B_HEAD_DIM))
        outs[4].append(v_s.reshape(n_s, t_s, B_KV_HEADS, B_HEAD_DIM))
        outs[5].append(ki_s.reshape(n_s, t_s, IDX_DIM))
        outs[6].append(va_s.reshape(n_s, t_s, A_WIDTH))

    return (xp.reshape(n_p, s_p, d), xs.reshape(n_s, t_s, d)) + tuple(jnp.stack(o) for o in outs)
```

```python
import functools

import numpy as np
import jax
import jax.numpy as jnp
from jax import lax
from jax.experimental import pallas as pl
from jax.experimental.pallas import tpu as pltpu

CHUNK = 64
A_WIDTH = 512
A_GROUPS = 4
A_GROUP_CH = A_WIDTH // A_GROUPS
A_SPAN = 128
B_HEADS = 8
B_HEAD_DIM = 64
B_KV_HEADS = 2
B_GROUP = B_HEADS // B_KV_HEADS
B_WIDTH = B_HEADS * B_HEAD_DIM
IDX_HEADS = 8
IDX_DIM = 64
TOPK_MAX = 256
Q_BLOCK = 128
LN_EPS = 1e-5
LOG2_E = 1.4426950408889634

LANES = 128
VMEM_LIMIT = 52 * 1024 * 1024
TOKEN_TILE = 512
KEY_CHUNK = 512
DSA_ROWS = 4 * Q_BLOCK
BISECT_STEPS = 14

F32 = jnp.float32
BF16 = jnp.bfloat16
NT_DIMS = (((1,), (1,)), ((), ()))


def _dot(a, b):
    return jnp.dot(a, b, preferred_element_type=F32)


def _dot_nt(a, b):
    return lax.dot_general(a, b, NT_DIMS, preferred_element_type=F32)


def _layer_norm(x, g, b):
    mu = jnp.mean(x, axis=-1, keepdims=True)
    xc = x - mu
    var = jnp.mean(xc * xc, axis=-1, keepdims=True)
    return xc * lax.rsqrt(var + LN_EPS) * g + b


def _inproj_kernel(x_ref, wa_ref, wq_ref, wkv_ref, wva_ref, wqi_ref, wkw_ref, lnv_ref, lnk_ref, ws_ref, bs_ref,
                   a_ref, q_ref, qi_ref, k_ref, v_ref, kb_ref, vb_ref, ki_ref, kib_ref, wi_ref, *va_refs,
                   span):
    tm = x_ref.shape[0]
    xb = x_ref[...].astype(BF16)

    za = _dot(xb, wa_ref[...])
    ga = 0.5 * za * (1.0 + lax.erf(za * np.float32(np.sqrt(0.5))))
    u = ga[:, :A_WIDTH]
    va = _layer_norm(ga[:, A_WIDTH:], lnv_ref[0:1, :], lnv_ref[1:2, :])
    if va_refs:
        va_refs[0][...] = va
    vab = va.astype(BF16)
    for s in range(tm // span):
        rows = slice(s * span, (s + 1) * span)
        for g in range(A_GROUPS):
            cols = slice(g * A_GROUP_CH, (g + 1) * A_GROUP_CH)
            mixed = _dot(ws_ref[g], vab[rows, cols]) + bs_ref[:, cols]
            a_ref[rows, cols] = (u[rows, cols] * mixed).astype(BF16)

    zq = _dot(xb, wq_ref[...])
    zqi = _dot(xb, wqi_ref[...])
    for h in range(B_HEADS):
        q_ref[h] = zq[:, h * B_HEAD_DIM:(h + 1) * B_HEAD_DIM].astype(BF16)
    for h in range(IDX_HEADS):
        qi_ref[h] = zqi[:, h * IDX_DIM:(h + 1) * IDX_DIM].astype(BF16)

    zkv = _dot(xb, wkv_ref[...])
    kvw = B_KV_HEADS * B_HEAD_DIM
    k_ref[...] = zkv[:, :kvw]
    v_ref[...] = zkv[:, kvw:]
    if len(kb_ref.shape) == 4:
        kb_ref, vb_ref, kib_ref = kb_ref.at[:, 0], vb_ref.at[:, 0], kib_ref.at[0]
    zva = _dot(xb, wva_ref[...])
    ones_lane = lax.broadcasted_iota(jnp.int32, (tm, LANES), 1) == B_HEAD_DIM
    for g in range(B_KV_HEADS):
        kb_ref[g] = zkv[:, g * B_HEAD_DIM:(g + 1) * B_HEAD_DIM].astype(BF16)
        vb_ref[g] = jnp.where(ones_lane, 1.0, zva[:, g * LANES:(g + 1) * LANES]).astype(BF16)

    zkw = _dot(xb, wkw_ref[...])
    ki = _layer_norm(zkw[:, :IDX_DIM], lnk_ref[0:1, :], lnk_ref[1:2, :])
    ki_ref[...] = ki
    kib_ref[...] = ki.astype(BF16)
    wi_ref[...] = zkw[:, IDX_DIM:IDX_DIM + IDX_HEADS] * np.float32(IDX_HEADS ** -0.5)


def _const_spec(shape):
    nd = len(shape)
    return pl.BlockSpec(shape, lambda *_: (0,) * nd)


def _in_projection(x2, wts, span, tm, emit_va, batch_rows=None):
    t, d = x2.shape
    grid = (t // tm,)
    row = lambda i: (i, 0)
    hm = lambda i: (0, i, 0)
    if batch_rows is None:
        kv_shape, kv_block, kv_map = (B_KV_HEADS, t), (B_KV_HEADS, tm), hm
        ki_shape, ki_block, ki_map = (t, IDX_DIM), (tm, IDX_DIM), row
    else:
        per = batch_rows // tm
        kv_shape = (B_KV_HEADS, t // batch_rows, batch_rows)
        kv_block, kv_map = (B_KV_HEADS, 1, tm), lambda i: (0, i // per, i % per, 0)
        ki_shape = (t // batch_rows, batch_rows, IDX_DIM)
        ki_block, ki_map = (1, tm, IDX_DIM), lambda i: (i // per, i % per, 0)
    in_specs = [pl.BlockSpec((tm, d), row)] + [_const_spec(w.shape) for w in wts]
    out_shape = [
        jax.ShapeDtypeStruct((t, A_WIDTH), BF16),
        jax.ShapeDtypeStruct((B_HEADS, t, B_HEAD_DIM), BF16),
        jax.ShapeDtypeStruct((IDX_HEADS, t, IDX_DIM), BF16),
        jax.ShapeDtypeStruct((t, B_KV_HEADS * B_HEAD_DIM), F32),
        jax.ShapeDtypeStruct((t, B_KV_HEADS * B_HEAD_DIM), F32),
        jax.ShapeDtypeStruct(kv_shape + (B_HEAD_DIM,), BF16),
        jax.ShapeDtypeStruct(kv_shape + (LANES,), BF16),
        jax.ShapeDtypeStruct((t, IDX_DIM), F32),
        jax.ShapeDtypeStruct(ki_shape, BF16),
        jax.ShapeDtypeStruct((t, IDX_HEADS), F32),
    ]
    out_specs = [
        pl.BlockSpec((tm, A_WIDTH), row),
        pl.BlockSpec((B_HEADS, tm, B_HEAD_DIM), hm),
        pl.BlockSpec((IDX_HEADS, tm, IDX_DIM), hm),
        pl.BlockSpec((tm, B_KV_HEADS * B_HEAD_DIM), row),
        pl.BlockSpec((tm, B_KV_HEADS * B_HEAD_DIM), row),
        pl.BlockSpec(kv_block + (B_HEAD_DIM,), kv_map),
        pl.BlockSpec(kv_block + (LANES,), kv_map),
        pl.BlockSpec((tm, IDX_DIM), row),
        pl.BlockSpec(ki_block, ki_map),
        pl.BlockSpec((tm, IDX_HEADS), row),
    ]
    if emit_va:
        out_shape.append(jax.ShapeDtypeStruct((t, A_WIDTH), F32))
        out_specs.append(pl.BlockSpec((tm, A_WIDTH), row))
    return pl.pallas_call(
        functools.partial(_inproj_kernel, span=span),
        grid=grid, in_specs=in_specs, out_specs=out_specs, out_shape=out_shape,
        compiler_params=pltpu.CompilerParams(dimension_semantics=("parallel",), vmem_limit_bytes=VMEM_LIMIT),
        name="in_projection",
    )(x2, *wts)


def _dsa_block(q_ref, qi_ref, wi_ref, k_ref, v_ref, ki_ref, o_ref, s_ref, d_ref, thr_ref,
               *, qb, tq, rb, row_keys, n_valid, q_off, topk, slopes):
    neg_inf = np.float32(-np.inf)
    pos_inf = np.float32(np.inf)
    kf = np.float32(topk)
    n_rb = tq // rb
    blocks = [(slice(r * rb, (r + 1) * rb), row_keys[r]) for r in range(n_rb)]

    for rows, n_keys in blocks:
        q_pos = q_off + qb * tq + rows.start + lax.broadcasted_iota(jnp.int32, (rb, 1), 0)
        qi = qi_ref[:, rows, :].reshape(IDX_HEADS * rb, IDX_DIM)
        wi = wi_ref[rows, :]
        for c0 in range(0, n_keys, KEY_CHUNK):
            kc = min(KEY_CHUNK, n_keys - c0)
            k_pos = c0 + lax.broadcasted_iota(jnp.int32, (1, kc), 1)
            lg = _dot_nt(qi, ki_ref[0, c0:c0 + kc, :])
            sc = wi[:, 0:1] * jnp.maximum(lg[0:rb], 0.0)
            for h in range(1, IDX_HEADS):
                sc = sc + wi[:, h:h + 1] * jnp.maximum(lg[h * rb:(h + 1) * rb], 0.0)
            k_chunk = k_pos // CHUNK
            if c0 + kc > n_valid:
                k_chunk = jnp.where(k_pos < n_valid, k_chunk, np.int32(2 ** 30))
            adm = k_chunk <= (q_pos // CHUNK)
            s_ref[rows, c0:c0 + kc] = jnp.where(adm, sc, neg_inf)
            d_ref[rows, c0:c0 + kc] = jnp.abs(q_pos - k_pos).astype(F32)

    def scores(blk):
        rows, n_keys = blk
        return s_ref[rows, :n_keys]

    def count_ge(blk, t):
        return jnp.sum(jnp.where(scores(blk) >= t, 1.0, 0.0), axis=1, keepdims=True)

    brackets, states, n_adms = [], [], []
    for blk in blocks:
        s = scores(blk)
        finite = s > neg_inf
        n_adm = jnp.sum(jnp.where(finite, 1.0, 0.0), axis=1, keepdims=True)
        row_max = jnp.max(s, axis=1, keepdims=True)
        row_min = jnp.min(jnp.where(finite, s, pos_inf), axis=1, keepdims=True)
        brackets.append((row_min, row_max, jnp.full((rb, 1), pos_inf, F32)))
        n_adms.append(n_adm)

    def bisect(_, carry):
        out = []
        for blk, (lo, hb, hiv) in zip(blocks, carry):
            mid = 0.5 * lo + 0.5 * hb
            ge = count_ge(blk, mid) >= kf
            out.append((jnp.where(ge, mid, lo), jnp.where(ge, hb, mid), jnp.where(ge, hiv, mid)))
        return tuple(out)

    brackets = lax.fori_loop(0, BISECT_STEPS, bisect, tuple(brackets))

    for (_, _, hiv), n_adm in zip(brackets, n_adms):
        done = jnp.where(n_adm <= kf, 1.0, 0.0)
        states.append((jnp.full((rb, 1), np.finfo(np.float32).min, F32), hiv, done))

    def n_open(states):
        return sum(jnp.sum(1.0 - done) for _, _, done in states)

    def scan_cond(carry):
        _, n_left, it = carry
        return jnp.logical_and(n_left > 0.0, it < max(row_keys))

    def scan_body(carry):
        states, _, it = carry
        out = []
        for blk, (thr, hiv, done) in zip(blocks, states):
            sv = scores(blk)
            cand = jnp.max(jnp.where(sv < hiv, sv, neg_inf), axis=1, keepdims=True)
            found = jnp.where(done > 0.5, 0.0, jnp.where(count_ge(blk, cand) >= kf, 1.0, 0.0))
            thr = jnp.where(found > 0.5, cand, thr)
            done = jnp.maximum(done, found)
            out.append((thr, jnp.where(done > 0.5, hiv, cand), done))
        return tuple(out), n_open(out), it + 1

    states, _, _ = lax.while_loop(scan_cond, scan_body, (tuple(states), n_open(states), jnp.int32(0)))

    over = []
    for blk, (thr, _, _), n_adm in zip(blocks, states, n_adms):
        thr_ref[blk[0], :] = thr
        over.append(jnp.max(jnp.where(n_adm > kf, count_ge(blk, thr), 0.0)))
    tied = functools.reduce(jnp.maximum, over) > kf

    for rows, n_keys in blocks:
        thr = thr_ref[rows, :]

        @pl.when(jnp.logical_not(tied))
        def _():
            d_ref[rows, :n_keys] = jnp.where(s_ref[rows, :n_keys] >= thr, d_ref[rows, :n_keys], pos_inf)

        @pl.when(tied)
        def _():
            n_gt = jnp.sum(jnp.where(s_ref[rows, :n_keys] > thr, 1.0, 0.0), axis=1, keepdims=True)
            room = kf - n_gt
            tri = (lax.broadcasted_iota(jnp.int32, (LANES, LANES), 0)
                   <= lax.broadcasted_iota(jnp.int32, (LANES, LANES), 1))
            tri = jnp.where(tri, 1.0, 0.0).astype(BF16)
            run = jnp.zeros((rb, 1), F32)
            for c0 in range(0, n_keys, LANES):
                blk = s_ref[rows, c0:c0 + LANES]
                eq = jnp.where(blk == thr, 1.0, 0.0)
                prefix = _dot(eq.astype(BF16), tri) + run
                keep = jnp.where(blk > thr, 1.0, jnp.where(prefix <= room, eq, 0.0))
                d_ref[rows, c0:c0 + LANES] = jnp.where(keep > 0.5, d_ref[rows, c0:c0 + LANES], pos_inf)
                run = run + jnp.sum(eq, axis=1, keepdims=True)

        qs = [q_ref[g * B_GROUP:(g + 1) * B_GROUP, rows, :].reshape(B_GROUP * rb, B_HEAD_DIM)
              for g in range(B_KV_HEADS)]
        m_run = [jnp.full((rb, 1), neg_inf, F32) for _ in range(B_HEADS)]
        acc = [jnp.zeros((rb, LANES), F32) for _ in range(B_HEADS)]
        for c0 in range(0, n_keys, KEY_CHUNK):
            kc = min(KEY_CHUNK, n_keys - c0)
            dist = d_ref[rows, c0:c0 + kc]
            for g in range(B_KV_HEADS):
                logits = _dot_nt(qs[g], k_ref[g, 0, c0:c0 + kc, :])
                ps, alphas = [], []
                for hh in range(B_GROUP):
                    h = g * B_GROUP + hh
                    lgt = logits[hh * rb:(hh + 1) * rb] - np.float32(slopes[h] * LOG2_E) * dist
                    m_new = jnp.maximum(m_run[h], jnp.max(lgt, axis=1, keepdims=True))
                    m_ref = jnp.where(m_new == neg_inf, 0.0, m_new)
                    alphas.append(jnp.exp2(m_run[h] - m_ref))
                    ps.append(jnp.exp2(lgt - m_ref).astype(BF16))
                    m_run[h] = m_new
                pv = _dot(jnp.concatenate(ps, axis=0), v_ref[g, 0, c0:c0 + kc, :])
                for hh in range(B_GROUP):
                    h = g * B_GROUP + hh
                    acc[h] = alphas[hh] * acc[h] + pv[hh * rb:(hh + 1) * rb]
        for h in range(B_HEADS):
            out = acc[h][:, :B_HEAD_DIM] / acc[h][:, B_HEAD_DIM:B_HEAD_DIM + 1]
            o_ref[rows, h * B_HEAD_DIM:(h + 1) * B_HEAD_DIM] = out.astype(BF16)


def _dsa_attention(q_hm, qi_hm, wi, kb, vb, kib, *, n, t_q, tq, rb, qb, n_valid, q_off, topk, slopes):
    nqb = t_q // tq
    n_rb = tq // rb
    last_chunk = (q_off + (qb + 1) * tq - 1) // CHUNK
    need = min((last_chunk + 1) * CHUNK, n_valid)
    n_keys = -(-need // LANES) * LANES
    if rb % CHUNK == 0 and need == q_off + (qb + 1) * tq:
        row_keys = tuple(n_keys - (n_rb - 1 - r) * rb for r in range(n_rb))
    else:
        row_keys = (n_keys,) * n_rb
    qmap = lambda b: (0, b * nqb + qb, 0)
    in_specs = [
        pl.BlockSpec((B_HEADS, tq, B_HEAD_DIM), qmap),
        pl.BlockSpec((IDX_HEADS, tq, IDX_DIM), qmap),
        pl.BlockSpec((tq, IDX_HEADS), lambda b: (b * nqb + qb, 0)),
        pl.BlockSpec((B_KV_HEADS, 1, n_keys, B_HEAD_DIM), lambda b: (0, b, 0, 0)),
        pl.BlockSpec((B_KV_HEADS, 1, n_keys, LANES), lambda b: (0, b, 0, 0)),
        pl.BlockSpec((1, n_keys, IDX_DIM), lambda b: (b, 0, 0)),
    ]
    return pl.pallas_call(
        functools.partial(_dsa_block, qb=qb, tq=tq, rb=rb, row_keys=row_keys, n_valid=n_valid, q_off=q_off,
                          topk=topk, slopes=slopes),
        grid=(n,), in_specs=in_specs,
        out_specs=pl.BlockSpec((tq, B_WIDTH), lambda b: (b, 0)),
        out_shape=jax.ShapeDtypeStruct((n * tq, B_WIDTH), BF16),
        scratch_shapes=[pltpu.VMEM((tq, n_keys), F32), pltpu.VMEM((tq, n_keys), F32), pltpu.VMEM((tq, 1), F32)],
        compiler_params=pltpu.CompilerParams(dimension_semantics=("parallel",), vmem_limit_bytes=VMEM_LIMIT),
        name="dsa_attention",
    )(q_hm, qi_hm, wi, kb, vb, kib)


def _merge_kernel(x_ref, a_ref, b_ref, wg_ref, wb_ref, wo_ref, ln1_ref, w1_ref, b1_ref, w2_ref, b2_ref, ln2_ref,
                  y_ref, *, alpha):
    d = x_ref.shape[1]
    x = x_ref[...]
    gates = jax.nn.sigmoid(_dot(x.astype(BF16), wg_ref[...]))
    m = gates[:, :d] * _dot(a_ref[...], wb_ref[0]) + gates[:, d:] * _dot(b_ref[...], wb_ref[1])
    h = _layer_norm(alpha * x + _dot(m.astype(BF16), wo_ref[...]), ln1_ref[0:1, :], ln1_ref[1:2, :])
    f = jnp.square(jnp.maximum(_dot(h.astype(BF16), w1_ref[...]) + b1_ref[...], 0.0))
    f = _dot(f.astype(BF16), w2_ref[...]) + b2_ref[...]
    y_ref[...] = _layer_norm(alpha * h + f, ln2_ref[0:1, :], ln2_ref[1:2, :])


def _resident_spec(shape):
    nd = len(shape)
    return pl.BlockSpec(shape, lambda *_: (0,) * nd, pipeline_mode=pl.Buffered(1))


def _merge_ffn(x2, a, b, wts, tm, alpha):
    t, d = x2.shape
    row = lambda i: (i, 0)
    in_specs = [pl.BlockSpec((tm, d), row), pl.BlockSpec((tm, A_WIDTH), row), pl.BlockSpec((tm, B_WIDTH), row)]
    in_specs += [_resident_spec(w.shape) for w in wts]
    return pl.pallas_call(
        functools.partial(_merge_kernel, alpha=np.float32(alpha)),
        grid=(t // tm,), in_specs=in_specs,
        out_specs=pl.BlockSpec((tm, d), row),
        out_shape=jax.ShapeDtypeStruct((t, d), F32),
        compiler_params=pltpu.CompilerParams(dimension_semantics=("parallel",), vmem_limit_bytes=VMEM_LIMIT),
        name="merge_ffn",
    )(x2, a, b, *wts)


def _spatial_weights(w_s, b_s, span):
    pos = jnp.arange(A_SPAN)
    mask = (pos[None, :] // CHUNK) <= (pos[:, None] // CHUNK)
    ws = jnp.where(mask[None], w_s, 0.0)[:, :span, :span].astype(BF16)
    bs = jnp.repeat(b_s[:, :span].T, A_GROUP_CH, axis=1)
    return ws, bs


def kernel(x_prompt, x_sample, cache_k, cache_v, cache_kidx, w_in, lnv_g, lnv_b, w_s, b_s, lnk_g, lnk_b,
           w_branch, w_out, ln1_g, ln1_b, w_ff1, b_ff1, w_ff2, b_ff2, ln2_g, ln2_b):
    depth = w_in.shape[0]
    n_p, s_p, d = x_prompt.shape
    n_s, t_s, _ = x_sample.shape
    past = cache_k.shape[2]
    alpha = (2 * depth) ** 0.25
    slopes = tuple(float(2.0 ** (-8.0 * h / B_HEADS)) for h in range(1, B_HEADS + 1))
    kvw = B_KV_HEADS * B_HEAD_DIM
    c_a = 2 * A_WIDTH
    c_q = c_a + B_WIDTH
    c_k = c_q + kvw
    c_v = c_k + kvw
    c_qi = c_v + IDX_HEADS * IDX_DIM
    c_wi = c_qi + IDX_DIM + IDX_HEADS

    xp = x_prompt.reshape(n_p * s_p, d)
    xs = x_sample.reshape(n_s * t_s, d)
    outs = [[] for _ in range(7)]
    for l in range(depth):
        w = w_in[l]
        pad = jnp.zeros((d, LANES - (c_wi - c_qi)), F32)
        proj_w = (
            w[:, :c_a].astype(BF16),
            (w[:, c_a:c_q] * (LOG2_E * B_HEAD_DIM ** -0.5)).astype(BF16),
            w[:, c_q:c_v].astype(BF16),
            jnp.pad(w[:, c_k:c_v].reshape(d, B_KV_HEADS, B_HEAD_DIM),
                    ((0, 0), (0, 0), (0, LANES - B_HEAD_DIM))).reshape(d, B_KV_HEADS * LANES).astype(BF16),
            (w[:, c_v:c_qi] * (IDX_DIM ** -0.5)).astype(BF16),
            jnp.concatenate([w[:, c_qi:c_wi], pad], axis=1).astype(BF16),
            jnp.stack([lnv_g[l], lnv_b[l]]),
            jnp.stack([lnk_g[l], lnk_b[l]]),
        )
        merge_w = (
            w[:, c_wi:].astype(BF16),
            w_branch[l].astype(BF16),
            w_out[l].astype(BF16),
            jnp.stack([ln1_g[l], ln1_b[l]]),
            w_ff1[l].astype(BF16),
            b_ff1[l][None, :],
            w_ff2[l].astype(BF16),
            b_ff2[l][None, :],
            jnp.stack([ln2_g[l], ln2_b[l]]),
        )

        a_p, q_p, qi_p, k_p, v_p, kb_p, vb_p, ki_p, kib_p, wi_p = _in_projection(
            xp, proj_w + _spatial_weights(w_s[l], b_s[l], A_SPAN), A_SPAN, TOKEN_TILE, False, batch_rows=s_p)
        b_p = [_dsa_attention(
            q_p, qi_p, wi_p, kb_p, vb_p, kib_p,
            n=n_p, t_q=s_p, tq=DSA_ROWS, rb=Q_BLOCK, qb=j, n_valid=s_p, q_off=0,
            topk=min(TOPK_MAX, s_p // 4), slopes=slopes).reshape(n_p, DSA_ROWS, B_WIDTH)
            for j in range(s_p // DSA_ROWS)]
        b_p = jnp.stack(b_p, axis=1).reshape(n_p * s_p, B_WIDTH)
        xp = _merge_ffn(xp, a_p, b_p, merge_w, TOKEN_TILE, alpha)

        a_s, q_s, qi_s, k_s, v_s, kb_s, vb_s, ki_s, kib_s, wi_s, va_s = _in_projection(
            xs, proj_w + _spatial_weights(w_s[l], b_s[l], t_s), t_s, n_s * t_s, True)
        n_all = past + t_s
        n_keys = -(-n_all // LANES) * LANES

        def with_cache(cache, new):
            width = new.shape[-1]
            c = jnp.moveaxis(cache.astype(BF16), 2, 0)
            if width > B_HEAD_DIM:
                tail = jnp.zeros(c.shape[:-1] + (width - B_HEAD_DIM,), BF16).at[..., 0].set(1.0)
                c = jnp.concatenate([c, tail], axis=-1)
            new = new.reshape(B_KV_HEADS, n_s, t_s, width)
            zero = jnp.zeros((B_KV_HEADS, n_s, n_keys - n_all, width), BF16)
            return jnp.concatenate([c, new, zero], axis=2)

        kib_all = jnp.concatenate([cache_kidx[l].astype(BF16), kib_s.reshape(n_s, t_s, IDX_DIM),
                                   jnp.zeros((n_s, n_keys - n_all, IDX_DIM), BF16)], axis=1)
        b_s_ = _dsa_attention(
            q_s, qi_s, wi_s, with_cache(cache_k[l], kb_s), with_cache(cache_v[l], vb_s), kib_all,
            n=n_s, t_q=t_s, tq=t_s, rb=t_s, qb=0, n_valid=n_all, q_off=past,
            topk=min(TOPK_MAX, n_all // 4), slopes=slopes)
        xs = _merge_ffn(xs, a_s, b_s_, merge_w, n_s * t_s, alpha)

        outs[0].append(k_p.reshape(n_p, s_p, B_KV_HEADS, B_HEAD_DIM))
        outs[1].append(v_p.reshape(n_p, s_p, B_KV_HEADS, B_HEAD_DIM))
        outs[2].append(ki_p.reshape(n_p, s_p, IDX_DIM))
        outs[3].append(k_s.reshape(n_s, t_s, B_KV_HEADS, B_HEAD_DIM))
        outs[4].append(v_s.reshape(n_s, t_s, B_KV_HEADS, B_HEAD_DIM))
        outs[5].append(ki_s.reshape(n_s, t_s, IDX_DIM))
        outs[6].append(va_s.reshape(n_s, t_s, A_WIDTH))

    return (xp.reshape(n_p, s_p, d), xs.reshape(n_s, t_s, d)) + tuple(jnp.stack(o) for o in outs)
```

```python
import functools

import numpy as np
import jax
import jax.numpy as jnp
from jax import lax
from jax.experimental import pallas as pl
from jax.experimental.pallas import tpu as pltpu

CHUNK = 64
A_WIDTH = 512
A_GROUPS = 4
A_GROUP_CH = A_WIDTH // A_GROUPS
A_SPAN = 128
B_HEADS = 8
B_HEAD_DIM = 64
B_KV_HEADS = 2
B_GROUP = B_HEADS // B_KV_HEADS
B_WIDTH = B_HEADS * B_HEAD_DIM
IDX_HEADS = 8
IDX_DIM = 64
TOPK_MAX = 256
Q_BLOCK = 128
LN_EPS = 1e-5
LOG2_E = 1.4426950408889634

LANES = 128
VMEM_LIMIT = 52 * 1024 * 1024
TOKEN_TILE = 512
KEY_CHUNK = 512
DSA_ROWS = 4 * Q_BLOCK
BISECT_STEPS = 14

F32 = jnp.float32
BF16 = jnp.bfloat16
NT_DIMS = (((1,), (1,)), ((), ()))


def _dot(a, b):
    return jnp.dot(a, b, preferred_element_type=F32)


def _dot_nt(a, b):
    return lax.dot_general(a, b, NT_DIMS, preferred_element_type=F32)


def _layer_norm(x, g, b):
    mu = jnp.mean(x, axis=-1, keepdims=True)
    xc = x - mu
    var = jnp.mean(xc * xc, axis=-1, keepdims=True)
    return xc * lax.rsqrt(var + LN_EPS) * g + b


def _inproj_kernel(x_ref, wa_ref, wq_ref, wkv_ref, wva_ref, wqi_ref, wkw_ref, lnv_ref, lnk_ref, ws_ref, bs_ref,
                   a_ref, q_ref, qi_ref, k_ref, v_ref, kb_ref, vb_ref, ki_ref, kib_ref, wi_ref, *va_refs,
                   span):
    tm = x_ref.shape[0]
    xb = x_ref[...].astype(BF16)

    za = _dot(xb, wa_ref[...])
    ga = 0.5 * za * (1.0 + lax.erf(za * np.float32(np.sqrt(0.5))))
    u = ga[:, :A_WIDTH]
    va = _layer_norm(ga[:, A_WIDTH:], lnv_ref[0:1, :], lnv_ref[1:2, :])
    if va_refs:
        va_refs[0][...] = va
    vab = va.astype(BF16)

    zq = _dot(xb, wq_ref[...])
    zqi = _dot(xb, wqi_ref[...])
    for h in range(B_HEADS):
        q_ref[h] = zq[:, h * B_HEAD_DIM:(h + 1) * B_HEAD_DIM].astype(BF16)
    for h in range(IDX_HEADS):
        qi_ref[h] = zqi[:, h * IDX_DIM:(h + 1) * IDX_DIM].astype(BF16)

    zkv = _dot(xb, wkv_ref[...])
    kvw = B_KV_HEADS * B_HEAD_DIM
    for g in range(B_KV_HEADS):
        k_ref[pl.ds(g, tm, stride=B_KV_HEADS), :] = zkv[:, g * B_HEAD_DIM:(g + 1) * B_HEAD_DIM]
        v_ref[pl.ds(g, tm, stride=B_KV_HEADS), :] = zkv[:, kvw + g * B_HEAD_DIM:kvw + (g + 1) * B_HEAD_DIM]
    if len(kb_ref.shape) == 4:
        kb_ref, vb_ref, kib_ref = kb_ref.at[:, 0], vb_ref.at[:, 0], kib_ref.at[0]
    zva = _dot(xb, wva_ref[...])
    ones_lane = lax.broadcasted_iota(jnp.int32, (tm, LANES), 1) == B_HEAD_DIM
    for g in range(B_KV_HEADS):
        kb_ref[g] = zkv[:, g * B_HEAD_DIM:(g + 1) * B_HEAD_DIM].astype(BF16)
        vb_ref[g] = jnp.where(ones_lane, 1.0, zva[:, g * LANES:(g + 1) * LANES]).astype(BF16)

    zkw = _dot(xb, wkw_ref[...])
    ki = _layer_norm(zkw[:, :IDX_DIM], lnk_ref[0:1, :], lnk_ref[1:2, :])
    ki_ref[...] = ki
    kib_ref[...] = ki.astype(BF16)
    wi_ref[...] = zkw[:, IDX_DIM:IDX_DIM + IDX_HEADS] * np.float32(IDX_HEADS ** -0.5)

    for s in range(tm // span):
        rows = slice(s * span, (s + 1) * span)
        for g in range(A_GROUPS):
            cols = slice(g * A_GROUP_CH, (g + 1) * A_GROUP_CH)
            mixed = _dot(ws_ref[g], vab[rows, cols]) + bs_ref[:, cols]
            a_ref[rows, cols] = (u[rows, cols] * mixed).astype(BF16)


def _const_spec(shape):
    nd = len(shape)
    return pl.BlockSpec(shape, lambda *_: (0,) * nd)


def _in_projection(x2, wts, span, tm, emit_va, batch_rows=None):
    t, d = x2.shape
    grid = (t // tm,)
    row = lambda i: (i, 0)
    hm = lambda i: (0, i, 0)
    if batch_rows is None:
        kv_shape, kv_block, kv_map = (B_KV_HEADS, t), (B_KV_HEADS, tm), hm
        ki_shape, ki_block, ki_map = (t, IDX_DIM), (tm, IDX_DIM), row
    else:
        per = batch_rows // tm
        kv_shape = (B_KV_HEADS, t // batch_rows, batch_rows)
        kv_block, kv_map = (B_KV_HEADS, 1, tm), lambda i: (0, i // per, i % per, 0)
        ki_shape = (t // batch_rows, batch_rows, IDX_DIM)
        ki_block, ki_map = (1, tm, IDX_DIM), lambda i: (i // per, i % per, 0)
    in_specs = [pl.BlockSpec((tm, d), row)] + [_const_spec(w.shape) for w in wts]
    out_shape = [
        jax.ShapeDtypeStruct((t, A_WIDTH), BF16),
        jax.ShapeDtypeStruct((B_HEADS, t, B_HEAD_DIM), BF16),
        jax.ShapeDtypeStruct((IDX_HEADS, t, IDX_DIM), BF16),
        jax.ShapeDtypeStruct((t * B_KV_HEADS, B_HEAD_DIM), F32),
        jax.ShapeDtypeStruct((t * B_KV_HEADS, B_HEAD_DIM), F32),
        jax.ShapeDtypeStruct(kv_shape + (B_HEAD_DIM,), BF16),
        jax.ShapeDtypeStruct(kv_shape + (LANES,), BF16),
        jax.ShapeDtypeStruct((t, IDX_DIM), F32),
        jax.ShapeDtypeStruct(ki_shape, BF16),
        jax.ShapeDtypeStruct((t, IDX_HEADS), F32),
    ]
    out_specs = [
        pl.BlockSpec((tm, A_WIDTH), row),
        pl.BlockSpec((B_HEADS, tm, B_HEAD_DIM), hm),
        pl.BlockSpec((IDX_HEADS, tm, IDX_DIM), hm),
        pl.BlockSpec((tm * B_KV_HEADS, B_HEAD_DIM), row),
        pl.BlockSpec((tm * B_KV_HEADS, B_HEAD_DIM), row),
        pl.BlockSpec(kv_block + (B_HEAD_DIM,), kv_map),
        pl.BlockSpec(kv_block + (LANES,), kv_map),
        pl.BlockSpec((tm, IDX_DIM), row),
        pl.BlockSpec(ki_block, ki_map),
        pl.BlockSpec((tm, IDX_HEADS), row),
    ]
    if emit_va:
        out_shape.append(jax.ShapeDtypeStruct((t, A_WIDTH), F32))
        out_specs.append(pl.BlockSpec((tm, A_WIDTH), row))
    return pl.pallas_call(
        functools.partial(_inproj_kernel, span=span),
        grid=grid, in_specs=in_specs, out_specs=out_specs, out_shape=out_shape,
        compiler_params=pltpu.CompilerParams(dimension_semantics=("parallel",), vmem_limit_bytes=VMEM_LIMIT),
        name="in_projection",
    )(x2, *wts)


def _dsa_block(q_ref, qi_ref, wi_ref, k_ref, v_ref, ki_ref, o_ref, s_ref, d_ref, thr_ref,
               *, qb, tq, rb, row_keys, n_valid, q_off, topk, slopes):
    neg_inf = np.float32(-np.inf)
    pos_inf = np.float32(np.inf)
    kf = np.float32(topk)
    n_rb = tq // rb
    blocks = [(slice(r * rb, (r + 1) * rb), row_keys[r]) for r in range(n_rb)]

    for rows, n_keys in blocks:
        q_pos = q_off + qb * tq + rows.start + lax.broadcasted_iota(jnp.int32, (rb, 1), 0)
        qi = qi_ref[:, rows, :].reshape(IDX_HEADS * rb, IDX_DIM)
        wi = wi_ref[rows, :]
        for c0 in range(0, n_keys, KEY_CHUNK):
            kc = min(KEY_CHUNK, n_keys - c0)
            k_pos = c0 + lax.broadcasted_iota(jnp.int32, (1, kc), 1)
            lg = _dot_nt(qi, ki_ref[0, c0:c0 + kc, :])
            sc = wi[:, 0:1] * jnp.maximum(lg[0:rb], 0.0)
            for h in range(1, IDX_HEADS):
                sc = sc + wi[:, h:h + 1] * jnp.maximum(lg[h * rb:(h + 1) * rb], 0.0)
            k_chunk = k_pos // CHUNK
            if c0 + kc > n_valid:
                k_chunk = jnp.where(k_pos < n_valid, k_chunk, np.int32(2 ** 30))
            adm = k_chunk <= (q_pos // CHUNK)
            s_ref[rows, c0:c0 + kc] = jnp.where(adm, sc, neg_inf)
            d_ref[rows, c0:c0 + kc] = jnp.abs(q_pos - k_pos).astype(F32)

    def scores(blk):
        rows, n_keys = blk
        return s_ref[rows, :n_keys]

    def count_ge(blk, t):
        return jnp.sum(jnp.where(scores(blk) >= t, 1.0, 0.0), axis=1, keepdims=True)

    brackets, states, n_adms = [], [], []
    for blk in blocks:
        s = scores(blk)
        finite = s > neg_inf
        n_adm = jnp.sum(jnp.where(finite, 1.0, 0.0), axis=1, keepdims=True)
        row_max = jnp.max(s, axis=1, keepdims=True)
        row_min = jnp.min(jnp.where(finite, s, pos_inf), axis=1, keepdims=True)
        brackets.append((row_min, row_max, jnp.full((rb, 1), pos_inf, F32)))
        n_adms.append(n_adm)

    def bisect(_, carry):
        out = []
        for blk, (lo, hb, hiv) in zip(blocks, carry):
            mid = 0.5 * lo + 0.5 * hb
            ge = count_ge(blk, mid) >= kf
            out.append((jnp.where(ge, mid, lo), jnp.where(ge, hb, mid), jnp.where(ge, hiv, mid)))
        return tuple(out)

    brackets = lax.fori_loop(0, BISECT_STEPS, bisect, tuple(brackets))

    for (_, _, hiv), n_adm in zip(brackets, n_adms):
        done = jnp.where(n_adm <= kf, 1.0, 0.0)
        states.append((jnp.full((rb, 1), np.finfo(np.float32).min, F32), hiv, done))

    def n_open(states):
        return sum(jnp.sum(1.0 - done) for _, _, done in states)

    def scan_cond(carry):
        _, n_left, it = carry
        return jnp.logical_and(n_left > 0.0, it < max(row_keys))

    def scan_body(carry):
        states, _, it = carry
        out = []
        for blk, (thr, hiv, done) in zip(blocks, states):
            sv = scores(blk)
            cand = jnp.max(jnp.where(sv < hiv, sv, neg_inf), axis=1, keepdims=True)
            found = jnp.where(done > 0.5, 0.0, jnp.where(count_ge(blk, cand) >= kf, 1.0, 0.0))
            thr = jnp.where(found > 0.5, cand, thr)
            done = jnp.maximum(done, found)
            out.append((thr, jnp.where(done > 0.5, hiv, cand), done))
        return tuple(out), n_open(out), it + 1

    states, _, _ = lax.while_loop(scan_cond, scan_body, (tuple(states), n_open(states), jnp.int32(0)))

    over = []
    for blk, (thr, _, _), n_adm in zip(blocks, states, n_adms):
        thr_ref[blk[0], :] = thr
        over.append(jnp.max(jnp.where(n_adm > kf, count_ge(blk, thr), 0.0)))
    tied = functools.reduce(jnp.maximum, over) > kf

    for rows, n_keys in blocks:
        thr = thr_ref[rows, :]

        @pl.when(jnp.logical_not(tied))
        def _():
            d_ref[rows, :n_keys] = jnp.where(s_ref[rows, :n_keys] >= thr, d_ref[rows, :n_keys], pos_inf)

        @pl.when(tied)
        def _():
            n_gt = jnp.sum(jnp.where(s_ref[rows, :n_keys] > thr, 1.0, 0.0), axis=1, keepdims=True)
            room = kf - n_gt
            tri = (lax.broadcasted_iota(jnp.int32, (LANES, LANES), 0)
                   <= lax.broadcasted_iota(jnp.int32, (LANES, LANES), 1))
            tri = jnp.where(tri, 1.0, 0.0).astype(BF16)
            run = jnp.zeros((rb, 1), F32)
            for c0 in range(0, n_keys, LANES):
                blk = s_ref[rows, c0:c0 + LANES]
                eq = jnp.where(blk == thr, 1.0, 0.0)
                prefix = _dot(eq.astype(BF16), tri) + run
                keep = jnp.where(blk > thr, 1.0, jnp.where(prefix <= room, eq, 0.0))
                d_ref[rows, c0:c0 + LANES] = jnp.where(keep > 0.5, d_ref[rows, c0:c0 + LANES], pos_inf)
                run = run + jnp.sum(eq, axis=1, keepdims=True)

        qs = [q_ref[g * B_GROUP:(g + 1) * B_GROUP, rows, :].reshape(B_GROUP * rb, B_HEAD_DIM)
              for g in range(B_KV_HEADS)]
        m_run = [jnp.full((rb, 1), neg_inf, F32) for _ in range(B_HEADS)]
        acc = [jnp.zeros((rb, LANES), F32) for _ in range(B_HEADS)]
        for c0 in range(0, n_keys, KEY_CHUNK):
            kc = min(KEY_CHUNK, n_keys - c0)
            dist = d_ref[rows, c0:c0 + kc]
            for g in range(B_KV_HEADS):
                logits = _dot_nt(qs[g], k_ref[g, 0, c0:c0 + kc, :])
                ps, alphas = [], []
                for hh in range(B_GROUP):
                    h = g * B_GROUP + hh
                    lgt = logits[hh * rb:(hh + 1) * rb] - np.float32(slopes[h] * LOG2_E) * dist
                    m_new = jnp.maximum(m_run[h], jnp.max(lgt, axis=1, keepdims=True))
                    m_ref = jnp.where(m_new == neg_inf, 0.0, m_new)
                    alphas.append(jnp.exp2(m_run[h] - m_ref))
                    ps.append(jnp.exp2(lgt - m_ref).astype(BF16))
                    m_run[h] = m_new
                pv = _dot(jnp.concatenate(ps, axis=0), v_ref[g, 0, c0:c0 + kc, :])
                for hh in range(B_GROUP):
                    h = g * B_GROUP + hh
                    acc[h] = alphas[hh] * acc[h] + pv[hh * rb:(hh + 1) * rb]
        for h in range(B_HEADS):
            out = acc[h][:, :B_HEAD_DIM] / acc[h][:, B_HEAD_DIM:B_HEAD_DIM + 1]
            o_ref[rows, h * B_HEAD_DIM:(h + 1) * B_HEAD_DIM] = out.astype(BF16)


def _dsa_attention(q_hm, qi_hm, wi, kb, vb, kib, *, n, t_q, tq, rb, qb, n_valid, q_off, topk, slopes):
    nqb = t_q // tq
    n_rb = tq // rb
    last_chunk = (q_off + (qb + 1) * tq - 1) // CHUNK
    need = min((last_chunk + 1) * CHUNK, n_valid)
    n_keys = -(-need // LANES) * LANES
    if rb % CHUNK == 0 and need == q_off + (qb + 1) * tq:
        row_keys = tuple(n_keys - (n_rb - 1 - r) * rb for r in range(n_rb))
    else:
        row_keys = (n_keys,) * n_rb
    qmap = lambda b: (0, b * nqb + qb, 0)
    in_specs = [
        pl.BlockSpec((B_HEADS, tq, B_HEAD_DIM), qmap),
        pl.BlockSpec((IDX_HEADS, tq, IDX_DIM), qmap),
        pl.BlockSpec((tq, IDX_HEADS), lambda b: (b * nqb + qb, 0)),
        pl.BlockSpec((B_KV_HEADS, 1, n_keys, B_HEAD_DIM), lambda b: (0, b, 0, 0)),
        pl.BlockSpec((B_KV_HEADS, 1, n_keys, LANES), lambda b: (0, b, 0, 0)),
        pl.BlockSpec((1, n_keys, IDX_DIM), lambda b: (b, 0, 0)),
    ]
    return pl.pallas_call(
        functools.partial(_dsa_block, qb=qb, tq=tq, rb=rb, row_keys=row_keys, n_valid=n_valid, q_off=q_off,
                          topk=topk, slopes=slopes),
        grid=(n,), in_specs=in_specs,
        out_specs=pl.BlockSpec((tq, B_WIDTH), lambda b: (b, 0)),
        out_shape=jax.ShapeDtypeStruct((n * tq, B_WIDTH), BF16),
        scratch_shapes=[pltpu.VMEM((tq, n_keys), F32), pltpu.VMEM((tq, n_keys), F32), pltpu.VMEM((tq, 1), F32)],
        compiler_params=pltpu.CompilerParams(dimension_semantics=("parallel",), vmem_limit_bytes=VMEM_LIMIT),
        name="dsa_attention",
    )(q_hm, qi_hm, wi, kb, vb, kib)


def _merge_kernel(x_ref, a_ref, b_ref, wg_ref, wb_ref, wo_ref, ln1_ref, w1_ref, b1_ref, w2_ref, b2_ref, ln2_ref,
                  y_ref, *, alpha):
    d = x_ref.shape[1]
    x = x_ref[...]
    gates = jax.nn.sigmoid(_dot(x.astype(BF16), wg_ref[...]))
    m = gates[:, :d] * _dot(a_ref[...], wb_ref[0]) + gates[:, d:] * _dot(b_ref[...], wb_ref[1])
    h = _layer_norm(alpha * x + _dot(m.astype(BF16), wo_ref[...]), ln1_ref[0:1, :], ln1_ref[1:2, :])
    f = jnp.square(jnp.maximum(_dot(h.astype(BF16), w1_ref[...]) + b1_ref[...], 0.0))
    f = _dot(f.astype(BF16), w2_ref[...]) + b2_ref[...]
    y_ref[...] = _layer_norm(alpha * h + f, ln2_ref[0:1, :], ln2_ref[1:2, :])


def _resident_spec(shape):
    nd = len(shape)
    return pl.BlockSpec(shape, lambda *_: (0,) * nd, pipeline_mode=pl.Buffered(1))


def _merge_ffn(x2, a, b, wts, tm, alpha):
    t, d = x2.shape
    row = lambda i: (i, 0)
    in_specs = [pl.BlockSpec((tm, d), row), pl.BlockSpec((tm, A_WIDTH), row), pl.BlockSpec((tm, B_WIDTH), row)]
    in_specs += [_resident_spec(w.shape) for w in wts]
    return pl.pallas_call(
        functools.partial(_merge_kernel, alpha=np.float32(alpha)),
        grid=(t // tm,), in_specs=in_specs,
        out_specs=pl.BlockSpec((tm, d), row),
        out_shape=jax.ShapeDtypeStruct((t, d), F32),
        compiler_params=pltpu.CompilerParams(dimension_semantics=("parallel",), vmem_limit_bytes=VMEM_LIMIT),
        name="merge_ffn",
    )(x2, a, b, *wts)


def _spatial_weights(w_s, b_s, span):
    pos = jnp.arange(A_SPAN)
    mask = (pos[None, :] // CHUNK) <= (pos[:, None] // CHUNK)
    ws = jnp.where(mask[None], w_s, 0.0)[:, :span, :span].astype(BF16)
    bs = jnp.repeat(b_s[:, :span].T, A_GROUP_CH, axis=1)
    return ws, bs


def kernel(x_prompt, x_sample, cache_k, cache_v, cache_kidx, w_in, lnv_g, lnv_b, w_s, b_s, lnk_g, lnk_b,
           w_branch, w_out, ln1_g, ln1_b, w_ff1, b_ff1, w_ff2, b_ff2, ln2_g, ln2_b):
    depth = w_in.shape[0]
    n_p, s_p, d = x_prompt.shape
    n_s, t_s, _ = x_sample.shape
    past = cache_k.shape[2]
    alpha = (2 * depth) ** 0.25
    slopes = tuple(float(2.0 ** (-8.0 * h / B_HEADS)) for h in range(1, B_HEADS + 1))
    kvw = B_KV_HEADS * B_HEAD_DIM
    c_a = 2 * A_WIDTH
    c_q = c_a + B_WIDTH
    c_k = c_q + kvw
    c_v = c_k + kvw
    c_qi = c_v + IDX_HEADS * IDX_DIM
    c_wi = c_qi + IDX_DIM + IDX_HEADS

    xp = x_prompt.reshape(n_p * s_p, d)
    xs = x_sample.reshape(n_s * t_s, d)
    outs = [[] for _ in range(7)]
    for l in range(depth):
        w = w_in[l]
        pad = jnp.zeros((d, LANES - (c_wi - c_qi)), F32)
        proj_w = (
            w[:, :c_a].astype(BF16),
            (w[:, c_a:c_q] * (LOG2_E * B_HEAD_DIM ** -0.5)).astype(BF16),
            w[:, c_q:c_v].astype(BF16),
            jnp.pad(w[:, c_k:c_v].reshape(d, B_KV_HEADS, B_HEAD_DIM),
                    ((0, 0), (0, 0), (0, LANES - B_HEAD_DIM))).reshape(d, B_KV_HEADS * LANES).astype(BF16),
            (w[:, c_v:c_qi] * (IDX_DIM ** -0.5)).astype(BF16),
            jnp.concatenate([w[:, c_qi:c_wi], pad], axis=1).astype(BF16),
            jnp.stack([lnv_g[l], lnv_b[l]]),
            jnp.stack([lnk_g[l], lnk_b[l]]),
        )
        merge_w = (
            w[:, c_wi:].astype(BF16),
            w_branch[l].astype(BF16),
            w_out[l].astype(BF16),
            jnp.stack([ln1_g[l], ln1_b[l]]),
            w_ff1[l].astype(BF16),
            b_ff1[l][None, :],
            w_ff2[l].astype(BF16),
            b_ff2[l][None, :],
            jnp.stack([ln2_g[l], ln2_b[l]]),
        )

        a_p, q_p, qi_p, k_p, v_p, kb_p, vb_p, ki_p, kib_p, wi_p = _in_projection(
            xp, proj_w + _spatial_weights(w_s[l], b_s[l], A_SPAN), A_SPAN, TOKEN_TILE, False, batch_rows=s_p)
        b_p = [_dsa_attention(
            q_p, qi_p, wi_p, kb_p, vb_p, kib_p,
            n=n_p, t_q=s_p, tq=DSA_ROWS, rb=Q_BLOCK, qb=j, n_valid=s_p, q_off=0,
            topk=min(TOPK_MAX, s_p // 4), slopes=slopes).reshape(n_p, DSA_ROWS, B_WIDTH)
            for j in range(s_p // DSA_ROWS)]
        b_p = jnp.stack(b_p, axis=1).reshape(n_p * s_p, B_WIDTH)
        xp = _merge_ffn(xp, a_p, b_p, merge_w, TOKEN_TILE, alpha)

        a_s, q_s, qi_s, k_s, v_s, kb_s, vb_s, ki_s, kib_s, wi_s, va_s = _in_projection(
            xs, proj_w + _spatial_weights(w_s[l], b_s[l], t_s), t_s, n_s * t_s, True)
        n_all = past + t_s
        n_keys = -(-n_all // LANES) * LANES

        def with_cache(cache, new):
            width = new.shape[-1]
            c = jnp.moveaxis(cache.astype(BF16), 2, 0)
            if width > B_HEAD_DIM:
                tail = jnp.zeros(c.shape[:-1] + (width - B_HEAD_DIM,), BF16).at[..., 0].set(1.0)
                c = jnp.concatenate([c, tail], axis=-1)
            new = new.reshape(B_KV_HEADS, n_s, t_s, width)
            zero = jnp.zeros((B_KV_HEADS, n_s, n_keys - n_all, width), BF16)
            return jnp.concatenate([c, new, zero], axis=2)

        kib_all = jnp.concatenate([cache_kidx[l].astype(BF16), kib_s.reshape(n_s, t_s, IDX_DIM),
                                   jnp.zeros((n_s, n_keys - n_all, IDX_DIM), BF16)], axis=1)
        b_s_ = _dsa_attention(
            q_s, qi_s, wi_s, with_cache(cache_k[l], kb_s), with_cache(cache_v[l], vb_s), kib_all,
            n=n_s, t_q=t_s, tq=t_s, rb=t_s, qb=0, n_valid=n_all, q_off=past,
            topk=min(TOPK_MAX, n_all // 4), slopes=slopes)
        xs = _merge_ffn(xs, a_s, b_s_, merge_w, n_s * t_s, alpha)

        outs[0].append(k_p.reshape(n_p, s_p, B_KV_HEADS, B_HEAD_DIM))
        outs[1].append(v_p.reshape(n_p, s_p, B_KV_HEADS, B_HEAD_DIM))
        outs[2].append(ki_p.reshape(n_p, s_p, IDX_DIM))
        outs[3].append(k_s.reshape(n_s, t_s, B_KV_HEADS, B_HEAD_DIM))
        outs[4].append(v_s.reshape(n_s, t_s, B_KV_HEADS, B_HEAD_DIM))
        outs[5].append(ki_s.reshape(n_s, t_s, IDX_DIM))
        outs[6].append(va_s.reshape(n_s, t_s, A_WIDTH))

    return (xp.reshape(n_p, s_p, d), xs.reshape(n_s, t_s, d)) + tuple(jnp.stack(o) for o in outs)
```

```python
import functools

import numpy as np
import jax
import jax.numpy as jnp
from jax import lax
from jax.experimental import pallas as pl
from jax.experimental.pallas import tpu as pltpu

CHUNK = 64
A_WIDTH = 512
A_GROUPS = 4
A_GROUP_CH = A_WIDTH // A_GROUPS
A_SPAN = 128
B_HEADS = 8
B_HEAD_DIM = 64
B_KV_HEADS = 2
B_GROUP = B_HEADS // B_KV_HEADS
B_WIDTH = B_HEADS * B_HEAD_DIM
IDX_HEADS = 8
IDX_DIM = 64
TOPK_MAX = 256
Q_BLOCK = 128
LN_EPS = 1e-5
LOG2_E = 1.4426950408889634

LANES = 128
VMEM_LIMIT = 52 * 1024 * 1024
TOKEN_TILE = 512
KEY_CHUNK = 512
DSA_ROWS = 4 * Q_BLOCK
BISECT_STEPS = 14

F32 = jnp.float32
BF16 = jnp.bfloat16
NT_DIMS = (((1,), (1,)), ((), ()))


def _dot(a, b):
    return jnp.dot(a, b, preferred_element_type=F32)


def _dot_nt(a, b):
    return lax.dot_general(a, b, NT_DIMS, preferred_element_type=F32)


def _layer_norm(x, g, b):
    mu = jnp.mean(x, axis=-1, keepdims=True)
    xc = x - mu
    var = jnp.mean(xc * xc, axis=-1, keepdims=True)
    return xc * lax.rsqrt(var + LN_EPS) * g + b


def _inproj_kernel(x_ref, wa_ref, wq_ref, wkv_ref, wva_ref, wqi_ref, wkw_ref, lnv_ref, lnk_ref, ws_ref, bs_ref,
                   a_ref, q_ref, qi_ref, k_ref, v_ref, kt_ref, vb_ref, ki_ref, kit_ref, wi_ref, *va_refs,
                   span):
    tm = x_ref.shape[0]
    xb = x_ref[...].astype(BF16)

    za = _dot(xb, wa_ref[...])
    ga = 0.5 * za * (1.0 + lax.erf(za * np.float32(np.sqrt(0.5))))
    u = ga[:, :A_WIDTH]
    va = _layer_norm(ga[:, A_WIDTH:], lnv_ref[0:1, :], lnv_ref[1:2, :])
    if va_refs:
        va_refs[0][...] = va
    vab = va.astype(BF16)

    zq = _dot(xb, wq_ref[...])
    zqi = _dot(xb, wqi_ref[...])
    for h in range(B_HEADS):
        q_ref[h] = zq[:, h * B_HEAD_DIM:(h + 1) * B_HEAD_DIM].astype(BF16)
    for h in range(IDX_HEADS):
        qi_ref[h] = zqi[:, h * IDX_DIM:(h + 1) * IDX_DIM].astype(BF16)

    zkv = _dot(xb, wkv_ref[...])
    kvw = B_KV_HEADS * B_HEAD_DIM
    if len(kt_ref.shape) == 4:
        kt_ref, vb_ref, kit_ref = kt_ref.at[0], vb_ref.at[0], kit_ref.at[0]
    zva = _dot(xb, wva_ref[...])
    ones_lane = lax.broadcasted_iota(jnp.int32, (tm, LANES), 1) == B_HEAD_DIM
    for g in range(B_KV_HEADS):
        kg = zkv[:, g * B_HEAD_DIM:(g + 1) * B_HEAD_DIM]
        k_ref[pl.ds(g, tm, stride=B_KV_HEADS), :] = kg
        v_ref[pl.ds(g, tm, stride=B_KV_HEADS), :] = zkv[:, kvw + g * B_HEAD_DIM:kvw + (g + 1) * B_HEAD_DIM]
        kt_ref[g] = kg.T.astype(BF16)
        vb_ref[g] = jnp.where(ones_lane, 1.0, zva[:, g * LANES:(g + 1) * LANES]).astype(BF16)

    zkw = _dot(xb, wkw_ref[...])
    ki = _layer_norm(zkw[:, :IDX_DIM], lnk_ref[0:1, :], lnk_ref[1:2, :])
    ki_ref[...] = ki
    kit_ref[...] = ki.T.astype(BF16)
    wi_ref[...] = zkw[:, IDX_DIM:IDX_DIM + IDX_HEADS] * np.float32(IDX_HEADS ** -0.5)

    for s in range(tm // span):
        rows = slice(s * span, (s + 1) * span)
        for g in range(A_GROUPS):
            cols = slice(g * A_GROUP_CH, (g + 1) * A_GROUP_CH)
            mixed = _dot(ws_ref[g], vab[rows, cols]) + bs_ref[:, cols]
            a_ref[rows, cols] = (u[rows, cols] * mixed).astype(BF16)


def _const_spec(shape):
    nd = len(shape)
    return pl.BlockSpec(shape, lambda *_: (0,) * nd)


def _in_projection(x2, wts, span, tm, emit_va, batch_rows=None):
    t, d = x2.shape
    grid = (t // tm,)
    row = lambda i: (i, 0)
    hm = lambda i: (0, i, 0)
    if batch_rows is None:
        kt_shape, kt_block, kt_map = (B_KV_HEADS, B_HEAD_DIM, t), (B_KV_HEADS, B_HEAD_DIM, tm), lambda i: (0, 0, i)
        vb_shape, vb_block, vb_map = (B_KV_HEADS, t, LANES), (B_KV_HEADS, tm, LANES), hm
        ki_shape, ki_block, ki_map = (IDX_DIM, t), (IDX_DIM, tm), lambda i: (0, i)
    else:
        per, n = batch_rows // tm, t // batch_rows
        kt_shape, kt_block = (n, B_KV_HEADS, B_HEAD_DIM, batch_rows), (1, B_KV_HEADS, B_HEAD_DIM, tm)
        kt_map = lambda i: (i // per, 0, 0, i % per)
        vb_shape, vb_block = (n, B_KV_HEADS, batch_rows, LANES), (1, B_KV_HEADS, tm, LANES)
        vb_map = lambda i: (i // per, 0, i % per, 0)
        ki_shape, ki_block, ki_map = (n, IDX_DIM, batch_rows), (1, IDX_DIM, tm), lambda i: (i // per, 0, i % per)
    in_specs = [pl.BlockSpec((tm, d), row)] + [_const_spec(w.shape) for w in wts]
    out_shape = [
        jax.ShapeDtypeStruct((t, A_WIDTH), BF16),
        jax.ShapeDtypeStruct((B_HEADS, t, B_HEAD_DIM), BF16),
        jax.ShapeDtypeStruct((IDX_HEADS, t, IDX_DIM), BF16),
        jax.ShapeDtypeStruct((t * B_KV_HEADS, B_HEAD_DIM), F32),
        jax.ShapeDtypeStruct((t * B_KV_HEADS, B_HEAD_DIM), F32),
        jax.ShapeDtypeStruct(kt_shape, BF16),
        jax.ShapeDtypeStruct(vb_shape, BF16),
        jax.ShapeDtypeStruct((t, IDX_DIM), F32),
        jax.ShapeDtypeStruct(ki_shape, BF16),
        jax.ShapeDtypeStruct((t, IDX_HEADS), F32),
    ]
    out_specs = [
        pl.BlockSpec((tm, A_WIDTH), row),
        pl.BlockSpec((B_HEADS, tm, B_HEAD_DIM), hm),
        pl.BlockSpec((IDX_HEADS, tm, IDX_DIM), hm),
        pl.BlockSpec((tm * B_KV_HEADS, B_HEAD_DIM), row),
        pl.BlockSpec((tm * B_KV_HEADS, B_HEAD_DIM), row),
        pl.BlockSpec(kt_block, kt_map),
        pl.BlockSpec(vb_block, vb_map),
        pl.BlockSpec((tm, IDX_DIM), row),
        pl.BlockSpec(ki_block, ki_map),
        pl.BlockSpec((tm, IDX_HEADS), row),
    ]
    if emit_va:
        out_shape.append(jax.ShapeDtypeStruct((t, A_WIDTH), F32))
        out_specs.append(pl.BlockSpec((tm, A_WIDTH), row))
    return pl.pallas_call(
        functools.partial(_inproj_kernel, span=span),
        grid=grid, in_specs=in_specs, out_specs=out_specs, out_shape=out_shape,
        compiler_params=pltpu.CompilerParams(dimension_semantics=("parallel",), vmem_limit_bytes=VMEM_LIMIT),
        name="in_projection",
    )(x2, *wts)


def _key_pieces(seg_lens, n_keys):
    pieces, col = [], 0
    for si, seg_len in enumerate(seg_lens):
        off = 0
        while off < seg_len and col < n_keys:
            width = min(KEY_CHUNK, seg_len - off, n_keys - col)
            pieces.append((col, width, si, off))
            off += width
            col += width
    return pieces


def _dsa_block(q_ref, qi_ref, wi_ref, *refs, segments, qb, tq, rb, row_keys, n_valid, q_off, topk, slopes):
    n_seg = len(segments)
    seg_refs = [refs[3 * i:3 * i + 3] for i in range(n_seg)]
    o_ref, s_ref, d_ref, thr_ref = refs[3 * n_seg:]
    seg_lens = [length for length, _ in segments]
    neg_inf = np.float32(-np.inf)
    pos_inf = np.float32(np.inf)
    kf = np.float32(topk)
    n_rb = tq // rb
    blocks = [(slice(r * rb, (r + 1) * rb), row_keys[r]) for r in range(n_rb)]

    for rows, n_keys in blocks:
        q_pos = q_off + qb * tq + rows.start + lax.broadcasted_iota(jnp.int32, (rb, 1), 0)
        qi = qi_ref[:, rows, :].reshape(IDX_HEADS * rb, IDX_DIM)
        wi = wi_ref[rows, :]
        for c0, kc, si, off in _key_pieces(seg_lens, n_keys):
            k_pos = c0 + lax.broadcasted_iota(jnp.int32, (1, kc), 1)
            lg = _dot(qi, seg_refs[si][2][0, :, off:off + kc].astype(BF16))
            sc = wi[:, 0:1] * jnp.maximum(lg[0:rb], 0.0)
            for h in range(1, IDX_HEADS):
                sc = sc + wi[:, h:h + 1] * jnp.maximum(lg[h * rb:(h + 1) * rb], 0.0)
            k_chunk = k_pos // CHUNK
            if c0 + kc > n_valid:
                k_chunk = jnp.where(k_pos < n_valid, k_chunk, np.int32(2 ** 30))
            adm = k_chunk <= (q_pos // CHUNK)
            s_ref[rows, c0:c0 + kc] = jnp.where(adm, sc, neg_inf)
            d_ref[rows, c0:c0 + kc] = jnp.abs(q_pos - k_pos).astype(F32)

    def scores(blk):
        rows, n_keys = blk
        return s_ref[rows, :n_keys]

    def count_ge(blk, t):
        return jnp.sum(jnp.where(scores(blk) >= t, 1.0, 0.0), axis=1, keepdims=True)

    brackets, states, n_adms = [], [], []
    for blk in blocks:
        s = scores(blk)
        finite = s > neg_inf
        n_adm = jnp.sum(jnp.where(finite, 1.0, 0.0), axis=1, keepdims=True)
        row_max = jnp.max(s, axis=1, keepdims=True)
        row_min = jnp.min(jnp.where(finite, s, pos_inf), axis=1, keepdims=True)
        brackets.append((row_min, row_max, jnp.full((rb, 1), pos_inf, F32)))
        n_adms.append(n_adm)

    def bisect(_, carry):
        out = []
        for blk, (lo, hb, hiv) in zip(blocks, carry):
            mid = 0.5 * lo + 0.5 * hb
            ge = count_ge(blk, mid) >= kf
            out.append((jnp.where(ge, mid, lo), jnp.where(ge, hb, mid), jnp.where(ge, hiv, mid)))
        return tuple(out)

    brackets = lax.fori_loop(0, BISECT_STEPS, bisect, tuple(brackets))

    for (_, _, hiv), n_adm in zip(brackets, n_adms):
        done = jnp.where(n_adm <= kf, 1.0, 0.0)
        states.append((jnp.full((rb, 1), np.finfo(np.float32).min, F32), hiv, done))

    def n_open(states):
        return sum(jnp.sum(1.0 - done) for _, _, done in states)

    def scan_cond(carry):
        _, n_left, it = carry
        return jnp.logical_and(n_left > 0.0, it < max(row_keys))

    def scan_body(carry):
        states, _, it = carry
        out = []
        for blk, (thr, hiv, done) in zip(blocks, states):
            sv = scores(blk)
            cand = jnp.max(jnp.where(sv < hiv, sv, neg_inf), axis=1, keepdims=True)
            found = jnp.where(done > 0.5, 0.0, jnp.where(count_ge(blk, cand) >= kf, 1.0, 0.0))
            thr = jnp.where(found > 0.5, cand, thr)
            done = jnp.maximum(done, found)
            out.append((thr, jnp.where(done > 0.5, hiv, cand), done))
        return tuple(out), n_open(out), it + 1

    states, _, _ = lax.while_loop(scan_cond, scan_body, (tuple(states), n_open(states), jnp.int32(0)))

    over = []
    for blk, (thr, _, _), n_adm in zip(blocks, states, n_adms):
        thr_ref[blk[0], :] = thr
        over.append(jnp.max(jnp.where(n_adm > kf, count_ge(blk, thr), 0.0)))
    tied = functools.reduce(jnp.maximum, over) > kf

    ones_row = jnp.where(lax.broadcasted_iota(jnp.int32, (LANES - B_HEAD_DIM, KEY_CHUNK), 0) == 0, 1.0, 0.0)
    ones_row = ones_row.astype(BF16)

    for rows, n_keys in blocks:
        thr = thr_ref[rows, :]

        @pl.when(jnp.logical_not(tied))
        def _():
            d_ref[rows, :n_keys] = jnp.where(s_ref[rows, :n_keys] >= thr, d_ref[rows, :n_keys], pos_inf)

        @pl.when(tied)
        def _():
            n_gt = jnp.sum(jnp.where(s_ref[rows, :n_keys] > thr, 1.0, 0.0), axis=1, keepdims=True)
            room = kf - n_gt
            tri = (lax.broadcasted_iota(jnp.int32, (LANES, LANES), 0)
                   <= lax.broadcasted_iota(jnp.int32, (LANES, LANES), 1))
            tri = jnp.where(tri, 1.0, 0.0).astype(BF16)
            run = jnp.zeros((rb, 1), F32)
            for c0 in range(0, n_keys, LANES):
                blk = s_ref[rows, c0:c0 + LANES]
                eq = jnp.where(blk == thr, 1.0, 0.0)
                prefix = _dot(eq.astype(BF16), tri) + run
                keep = jnp.where(blk > thr, 1.0, jnp.where(prefix <= room, eq, 0.0))
                d_ref[rows, c0:c0 + LANES] = jnp.where(keep > 0.5, d_ref[rows, c0:c0 + LANES], pos_inf)
                run = run + jnp.sum(eq, axis=1, keepdims=True)

        qs = [q_ref[g * B_GROUP:(g + 1) * B_GROUP, rows, :].reshape(B_GROUP * rb, B_HEAD_DIM)
              for g in range(B_KV_HEADS)]
        m_run = [jnp.full((rb, 1), neg_inf, F32) for _ in range(B_HEADS)]
        acc = [jnp.zeros((rb, LANES), F32) for _ in range(B_HEADS)]
        for c0, kc, si, off in _key_pieces(seg_lens, n_keys):
            kt_ref, v_ref, _ = seg_refs[si]
            dist = d_ref[rows, c0:c0 + kc]
            for g in range(B_KV_HEADS):
                logits = _dot(qs[g], kt_ref[0, g, :, off:off + kc].astype(BF16))
                ps, alphas = [], []
                for hh in range(B_GROUP):
                    h = g * B_GROUP + hh
                    lgt = logits[hh * rb:(hh + 1) * rb] - np.float32(slopes[h] * LOG2_E) * dist
                    m_new = jnp.maximum(m_run[h], jnp.max(lgt, axis=1, keepdims=True))
                    m_ref = jnp.where(m_new == neg_inf, 0.0, m_new)
                    alphas.append(jnp.exp2(m_run[h] - m_ref))
                    ps.append(jnp.exp2(lgt - m_ref).astype(BF16))
                    m_run[h] = m_new
                p = jnp.concatenate(ps, axis=0)
                if segments[si][1]:
                    vt = jnp.concatenate([v_ref[0, g, :, off:off + kc].astype(BF16), ones_row[:, :kc]], axis=0)
                    pv = _dot_nt(p, vt)
                else:
                    pv = _dot(p, v_ref[0, g, off:off + kc, :])
                for hh in range(B_GROUP):
                    h = g * B_GROUP + hh
                    acc[h] = alphas[hh] * acc[h] + pv[hh * rb:(hh + 1) * rb]
        for h in range(B_HEADS):
            out = acc[h][:, :B_HEAD_DIM] / acc[h][:, B_HEAD_DIM:B_HEAD_DIM + 1]
            o_ref[rows, h * B_HEAD_DIM:(h + 1) * B_HEAD_DIM] = out.astype(BF16)


def _dsa_attention(q_hm, qi_hm, wi, key_segments, *, n, t_q, tq, rb, qb, n_valid, q_off, topk, slopes):
    nqb = t_q // tq
    n_rb = tq // rb
    last_chunk = (q_off + (qb + 1) * tq - 1) // CHUNK
    need = min((last_chunk + 1) * CHUNK, n_valid)
    n_keys = -(-need // LANES) * LANES
    if rb % CHUNK == 0 and need == q_off + (qb + 1) * tq:
        row_keys = tuple(n_keys - (n_rb - 1 - r) * rb for r in range(n_rb))
    else:
        row_keys = (n_keys,) * n_rb
    qmap = lambda b: (0, b * nqb + qb, 0)
    in_specs = [
        pl.BlockSpec((B_HEADS, tq, B_HEAD_DIM), qmap),
        pl.BlockSpec((IDX_HEADS, tq, IDX_DIM), qmap),
        pl.BlockSpec((tq, IDX_HEADS), lambda b: (b * nqb + qb, 0)),
    ]
    operands, segments, col = [], [], 0
    for kt, v, kit, v_transposed in key_segments:
        length = min(kt.shape[-1], n_keys - col)
        if length <= 0:
            break
        v_block = (1, B_KV_HEADS, B_HEAD_DIM, length) if v_transposed else (1, B_KV_HEADS, length, LANES)
        in_specs += [
            pl.BlockSpec((1, B_KV_HEADS, B_HEAD_DIM, length), lambda b: (b, 0, 0, 0)),
            pl.BlockSpec(v_block, lambda b: (b, 0, 0, 0)),
            pl.BlockSpec((1, IDX_DIM, length), lambda b: (b, 0, 0)),
        ]
        operands += [kt, v, kit]
        segments.append((length, v_transposed))
        col += length
    assert col == n_keys, (col, n_keys)
    return pl.pallas_call(
        functools.partial(_dsa_block, segments=tuple(segments), qb=qb, tq=tq, rb=rb, row_keys=row_keys,
                          n_valid=n_valid, q_off=q_off, topk=topk, slopes=slopes),
        grid=(n,), in_specs=in_specs,
        out_specs=pl.BlockSpec((tq, B_WIDTH), lambda b: (b, 0)),
        out_shape=jax.ShapeDtypeStruct((n * tq, B_WIDTH), BF16),
        scratch_shapes=[pltpu.VMEM((tq, n_keys), F32), pltpu.VMEM((tq, n_keys), F32), pltpu.VMEM((tq, 1), F32)],
        compiler_params=pltpu.CompilerParams(dimension_semantics=("parallel",), vmem_limit_bytes=VMEM_LIMIT),
        name="dsa_attention",
    )(q_hm, qi_hm, wi, *operands)


def _merge_kernel(x_ref, a_ref, b_ref, wg_ref, wb_ref, wo_ref, ln1_ref, w1_ref, b1_ref, w2_ref, b2_ref, ln2_ref,
                  y_ref, *, alpha):
    d = x_ref.shape[1]
    x = x_ref[...]
    gates = jax.nn.sigmoid(_dot(x.astype(BF16), wg_ref[...]))
    m = gates[:, :d] * _dot(a_ref[...], wb_ref[0]) + gates[:, d:] * _dot(b_ref[...], wb_ref[1])
    h = _layer_norm(alpha * x + _dot(m.astype(BF16), wo_ref[...]), ln1_ref[0:1, :], ln1_ref[1:2, :])
    f = jnp.square(jnp.maximum(_dot(h.astype(BF16), w1_ref[...]) + b1_ref[...], 0.0))
    f = _dot(f.astype(BF16), w2_ref[...]) + b2_ref[...]
    y_ref[...] = _layer_norm(alpha * h + f, ln2_ref[0:1, :], ln2_ref[1:2, :])


def _resident_spec(shape):
    nd = len(shape)
    return pl.BlockSpec(shape, lambda *_: (0,) * nd, pipeline_mode=pl.Buffered(1))


def _merge_ffn(x2, a, b, wts, tm, alpha):
    t, d = x2.shape
    row = lambda i: (i, 0)
    in_specs = [pl.BlockSpec((tm, d), row), pl.BlockSpec((tm, A_WIDTH), row), pl.BlockSpec((tm, B_WIDTH), row)]
    in_specs += [_resident_spec(w.shape) for w in wts]
    return pl.pallas_call(
        functools.partial(_merge_kernel, alpha=np.float32(alpha)),
        grid=(t // tm,), in_specs=in_specs,
        out_specs=pl.BlockSpec((tm, d), row),
        out_shape=jax.ShapeDtypeStruct((t, d), F32),
        compiler_params=pltpu.CompilerParams(dimension_semantics=("parallel",), vmem_limit_bytes=VMEM_LIMIT),
        name="merge_ffn",
    )(x2, a, b, *wts)


def _spatial_weights(w_s, b_s, span):
    pos = jnp.arange(A_SPAN)
    mask = (pos[None, :] // CHUNK) <= (pos[:, None] // CHUNK)
    ws = jnp.where(mask[None], w_s, 0.0)[:, :span, :span].astype(BF16)
    bs = jnp.repeat(b_s[:, :span].T, A_GROUP_CH, axis=1)
    return ws, bs


def _pad_axis(x, axis, size):
    pads = [(0, 0)] * x.ndim
    pads[axis] = (0, size - x.shape[axis])
    return jnp.pad(x, pads)


def kernel(x_prompt, x_sample, cache_k, cache_v, cache_kidx, w_in, lnv_g, lnv_b, w_s, b_s, lnk_g, lnk_b,
           w_branch, w_out, ln1_g, ln1_b, w_ff1, b_ff1, w_ff2, b_ff2, ln2_g, ln2_b):
    depth = w_in.shape[0]
    n_p, s_p, d = x_prompt.shape
    n_s, t_s, _ = x_sample.shape
    past = cache_k.shape[2]
    alpha = (2 * depth) ** 0.25
    slopes = tuple(float(2.0 ** (-8.0 * h / B_HEADS)) for h in range(1, B_HEADS + 1))
    kvw = B_KV_HEADS * B_HEAD_DIM
    c_a = 2 * A_WIDTH
    c_q = c_a + B_WIDTH
    c_k = c_q + kvw
    c_v = c_k + kvw
    c_qi = c_v + IDX_HEADS * IDX_DIM
    c_wi = c_qi + IDX_DIM + IDX_HEADS

    xp = x_prompt.reshape(n_p * s_p, d)
    xs = x_sample.reshape(n_s * t_s, d)
    outs = [[] for _ in range(7)]
    for l in range(depth):
        w = w_in[l]
        pad = jnp.zeros((d, LANES - (c_wi - c_qi)), F32)
        proj_w = (
            w[:, :c_a].astype(BF16),
            (w[:, c_a:c_q] * (LOG2_E * B_HEAD_DIM ** -0.5)).astype(BF16),
            w[:, c_q:c_v].astype(BF16),
            jnp.pad(w[:, c_k:c_v].reshape(d, B_KV_HEADS, B_HEAD_DIM),
                    ((0, 0), (0, 0), (0, LANES - B_HEAD_DIM))).reshape(d, B_KV_HEADS * LANES).astype(BF16),
            (w[:, c_v:c_qi] * (IDX_DIM ** -0.5)).astype(BF16),
            jnp.concatenate([w[:, c_qi:c_wi], pad], axis=1).astype(BF16),
            jnp.stack([lnv_g[l], lnv_b[l]]),
            jnp.stack([lnk_g[l], lnk_b[l]]),
        )
        merge_w = (
            w[:, c_wi:].astype(BF16),
            w_branch[l].astype(BF16),
            w_out[l].astype(BF16),
            jnp.stack([ln1_g[l], ln1_b[l]]),
            w_ff1[l].astype(BF16),
            b_ff1[l][None, :],
            w_ff2[l].astype(BF16),
            b_ff2[l][None, :],
            jnp.stack([ln2_g[l], ln2_b[l]]),
        )

        a_p, q_p, qi_p, k_p, v_p, kt_p, vb_p, ki_p, kit_p, wi_p = _in_projection(
            xp, proj_w + _spatial_weights(w_s[l], b_s[l], A_SPAN), A_SPAN, TOKEN_TILE, False, batch_rows=s_p)
        b_p = [_dsa_attention(
            q_p, qi_p, wi_p, [(kt_p, vb_p, kit_p, False)],
            n=n_p, t_q=s_p, tq=DSA_ROWS, rb=Q_BLOCK, qb=j, n_valid=s_p, q_off=0,
            topk=min(TOPK_MAX, s_p // 4), slopes=slopes).reshape(n_p, DSA_ROWS, B_WIDTH)
            for j in range(s_p // DSA_ROWS)]
        b_p = jnp.stack(b_p, axis=1).reshape(n_p * s_p, B_WIDTH)
        xp = _merge_ffn(xp, a_p, b_p, merge_w, TOKEN_TILE, alpha)

        a_s, q_s, qi_s, k_s, v_s, kt_s, vb_s, ki_s, kit_s, wi_s, va_s = _in_projection(
            xs, proj_w + _spatial_weights(w_s[l], b_s[l], t_s), t_s, n_s * t_s, True)
        n_all = past + t_s
        new_len = -(-t_s // LANES) * LANES
        cache_seg = (jnp.transpose(cache_k[l], (0, 2, 3, 1)), jnp.transpose(cache_v[l], (0, 2, 3, 1)),
                     jnp.transpose(cache_kidx[l], (0, 2, 1)), True)
        kt_new = jnp.transpose(kt_s.reshape(B_KV_HEADS, B_HEAD_DIM, n_s, t_s), (2, 0, 1, 3))
        vb_new = jnp.transpose(vb_s.reshape(B_KV_HEADS, n_s, t_s, LANES), (1, 0, 2, 3))
        kit_new = jnp.transpose(kit_s.reshape(IDX_DIM, n_s, t_s), (1, 0, 2))
        new_seg = (_pad_axis(kt_new, 3, new_len), _pad_axis(vb_new, 2, new_len), _pad_axis(kit_new, 2, new_len), False)
        b_s_ = _dsa_attention(
            q_s, qi_s, wi_s, [cache_seg, new_seg],
            n=n_s, t_q=t_s, tq=t_s, rb=t_s, qb=0, n_valid=n_all, q_off=past,
            topk=min(TOPK_MAX, n_all // 4), slopes=slopes)
        xs = _merge_ffn(xs, a_s, b_s_, merge_w, n_s * t_s, alpha)

        outs[0].append(k_p.reshape(n_p, s_p, B_KV_HEADS, B_HEAD_DIM))
        outs[1].append(v_p.reshape(n_p, s_p, B_KV_HEADS, B_HEAD_DIM))
        outs[2].append(ki_p.reshape(n_p, s_p, IDX_DIM))
        outs[3].append(k_s.reshape(n_s, t_s, B_KV_HEADS, B_HEAD_DIM))
        outs[4].append(v_s.reshape(n_s, t_s, B_KV_HEADS, B_HEAD_DIM))
        outs[5].append(ki_s.reshape(n_s, t_s, IDX_DIM))
        outs[6].append(va_s.reshape(n_s, t_s, A_WIDTH))

    return (xp.reshape(n_p, s_p, d), xs.reshape(n_s, t_s, d)) + tuple(jnp.stack(o) for o in outs)
```

```python
import functools

import numpy as np
import jax
import jax.numpy as jnp
from jax import lax
from jax.experimental import pallas as pl
from jax.experimental.pallas import tpu as pltpu

CHUNK = 64
A_WIDTH = 512
A_GROUPS = 4
A_GROUP_CH = A_WIDTH // A_GROUPS
A_SPAN = 128
B_HEADS = 8
B_HEAD_DIM = 64
B_KV_HEADS = 2
B_GROUP = B_HEADS // B_KV_HEADS
B_WIDTH = B_HEADS * B_HEAD_DIM
IDX_HEADS = 8
IDX_DIM = 64
TOPK_MAX = 256
Q_BLOCK = 128
LN_EPS = 1e-5
LOG2_E = 1.4426950408889634

LANES = 128
VMEM_LIMIT = 52 * 1024 * 1024
TOKEN_TILE = 512
KEY_CHUNK = 512
ATT_CHUNK = 256
DSA_ROWS = 4 * Q_BLOCK
BISECT_STEPS = 14

F32 = jnp.float32
BF16 = jnp.bfloat16
NT_DIMS = (((1,), (1,)), ((), ()))


def _dot(a, b):
    return jnp.dot(a, b, preferred_element_type=F32)


def _dot_nt(a, b):
    return lax.dot_general(a, b, NT_DIMS, preferred_element_type=F32)


def _layer_norm(x, g, b):
    mu = jnp.mean(x, axis=-1, keepdims=True)
    xc = x - mu
    var = jnp.mean(xc * xc, axis=-1, keepdims=True)
    return xc * lax.rsqrt(var + LN_EPS) * g + b


def _inproj_kernel(x_ref, wa_ref, wq_ref, wkv_ref, wva_ref, wqi_ref, wkw_ref, lnv_ref, lnk_ref, ws_ref, bs_ref,
                   a_ref, q_ref, qi_ref, k_ref, v_ref, kt_ref, vb_ref, ki_ref, kit_ref, wi_ref, *va_refs,
                   span):
    tm = x_ref.shape[0]
    xb = x_ref[...].astype(BF16)

    za = _dot(xb, wa_ref[...])
    ga = 0.5 * za * (1.0 + lax.erf(za * np.float32(np.sqrt(0.5))))
    u = ga[:, :A_WIDTH]
    va = _layer_norm(ga[:, A_WIDTH:], lnv_ref[0:1, :], lnv_ref[1:2, :])
    if va_refs:
        va_refs[0][...] = va
    vab = va.astype(BF16)

    zq = _dot(xb, wq_ref[...])
    zqi = _dot(xb, wqi_ref[...])
    for h in range(B_HEADS):
        q_ref[h] = zq[:, h * B_HEAD_DIM:(h + 1) * B_HEAD_DIM].astype(BF16)
    for h in range(IDX_HEADS):
        qi_ref[h] = zqi[:, h * IDX_DIM:(h + 1) * IDX_DIM].astype(BF16)

    zkv = _dot(xb, wkv_ref[...])
    kvw = B_KV_HEADS * B_HEAD_DIM
    if len(kt_ref.shape) == 4:
        kt_ref, vb_ref, kit_ref = kt_ref.at[0], vb_ref.at[0], kit_ref.at[0]
    zva = _dot(xb, wva_ref[...])
    ones_lane = lax.broadcasted_iota(jnp.int32, (tm, LANES), 1) == B_HEAD_DIM
    for g in range(B_KV_HEADS):
        kg = zkv[:, g * B_HEAD_DIM:(g + 1) * B_HEAD_DIM]
        k_ref[pl.ds(g, tm, stride=B_KV_HEADS), :] = kg
        v_ref[pl.ds(g, tm, stride=B_KV_HEADS), :] = zkv[:, kvw + g * B_HEAD_DIM:kvw + (g + 1) * B_HEAD_DIM]
        kt_ref[g] = kg.T.astype(BF16)
        vb_ref[g] = jnp.where(ones_lane, 1.0, zva[:, g * LANES:(g + 1) * LANES]).astype(BF16)

    zkw = _dot(xb, wkw_ref[...])
    ki = _layer_norm(zkw[:, :IDX_DIM], lnk_ref[0:1, :], lnk_ref[1:2, :])
    ki_ref[...] = ki
    kit_ref[...] = ki.T.astype(BF16)
    wi_ref[...] = zkw[:, IDX_DIM:IDX_DIM + IDX_HEADS] * np.float32(IDX_HEADS ** -0.5)

    for s in range(tm // span):
        rows = slice(s * span, (s + 1) * span)
        for g in range(A_GROUPS):
            cols = slice(g * A_GROUP_CH, (g + 1) * A_GROUP_CH)
            mixed = _dot(ws_ref[g], vab[rows, cols]) + bs_ref[:, cols]
            a_ref[rows, cols] = (u[rows, cols] * mixed).astype(BF16)


def _const_spec(shape):
    nd = len(shape)
    return pl.BlockSpec(shape, lambda *_: (0,) * nd)


def _in_projection(x2, wts, span, tm, emit_va, batch_rows=None):
    t, d = x2.shape
    grid = (t // tm,)
    row = lambda i: (i, 0)
    hm = lambda i: (0, i, 0)
    if batch_rows is None:
        kt_shape, kt_block, kt_map = (B_KV_HEADS, B_HEAD_DIM, t), (B_KV_HEADS, B_HEAD_DIM, tm), lambda i: (0, 0, i)
        vb_shape, vb_block, vb_map = (B_KV_HEADS, t, LANES), (B_KV_HEADS, tm, LANES), hm
        ki_shape, ki_block, ki_map = (IDX_DIM, t), (IDX_DIM, tm), lambda i: (0, i)
    else:
        per, n = batch_rows // tm, t // batch_rows
        kt_shape, kt_block = (n, B_KV_HEADS, B_HEAD_DIM, batch_rows), (1, B_KV_HEADS, B_HEAD_DIM, tm)
        kt_map = lambda i: (i // per, 0, 0, i % per)
        vb_shape, vb_block = (n, B_KV_HEADS, batch_rows, LANES), (1, B_KV_HEADS, tm, LANES)
        vb_map = lambda i: (i // per, 0, i % per, 0)
        ki_shape, ki_block, ki_map = (n, IDX_DIM, batch_rows), (1, IDX_DIM, tm), lambda i: (i // per, 0, i % per)
    in_specs = [pl.BlockSpec((tm, d), row)] + [_const_spec(w.shape) for w in wts]
    out_shape = [
        jax.ShapeDtypeStruct((t, A_WIDTH), BF16),
        jax.ShapeDtypeStruct((B_HEADS, t, B_HEAD_DIM), BF16),
        jax.ShapeDtypeStruct((IDX_HEADS, t, IDX_DIM), BF16),
        jax.ShapeDtypeStruct((t * B_KV_HEADS, B_HEAD_DIM), F32),
        jax.ShapeDtypeStruct((t * B_KV_HEADS, B_HEAD_DIM), F32),
        jax.ShapeDtypeStruct(kt_shape, BF16),
        jax.ShapeDtypeStruct(vb_shape, BF16),
        jax.ShapeDtypeStruct((t, IDX_DIM), F32),
        jax.ShapeDtypeStruct(ki_shape, BF16),
        jax.ShapeDtypeStruct((t, IDX_HEADS), F32),
    ]
    out_specs = [
        pl.BlockSpec((tm, A_WIDTH), row),
        pl.BlockSpec((B_HEADS, tm, B_HEAD_DIM), hm),
        pl.BlockSpec((IDX_HEADS, tm, IDX_DIM), hm),
        pl.BlockSpec((tm * B_KV_HEADS, B_HEAD_DIM), row),
        pl.BlockSpec((tm * B_KV_HEADS, B_HEAD_DIM), row),
        pl.BlockSpec(kt_block, kt_map),
        pl.BlockSpec(vb_block, vb_map),
        pl.BlockSpec((tm, IDX_DIM), row),
        pl.BlockSpec(ki_block, ki_map),
        pl.BlockSpec((tm, IDX_HEADS), row),
    ]
    if emit_va:
        out_shape.append(jax.ShapeDtypeStruct((t, A_WIDTH), F32))
        out_specs.append(pl.BlockSpec((tm, A_WIDTH), row))
    return pl.pallas_call(
        functools.partial(_inproj_kernel, span=span),
        grid=grid, in_specs=in_specs, out_specs=out_specs, out_shape=out_shape,
        compiler_params=pltpu.CompilerParams(dimension_semantics=("parallel",), vmem_limit_bytes=VMEM_LIMIT),
        name="in_projection",
    )(x2, *wts)


def _key_pieces(seg_lens, n_keys, chunk):
    pieces, col = [], 0
    for si, seg_len in enumerate(seg_lens):
        off = 0
        while off < seg_len and col < n_keys:
            width = min(chunk, seg_len - off, n_keys - col)
            pieces.append((col, width, si, off))
            off += width
            col += width
    return pieces


def _dsa_block(q_ref, qi_ref, wi_ref, *refs, segments, qb, tq, rb, row_keys, n_valid, q_off, topk, slopes):
    n_seg = len(segments)
    seg_refs = [refs[3 * i:3 * i + 3] for i in range(n_seg)]
    o_ref, s_ref, d_ref, thr_ref = refs[3 * n_seg:]
    seg_lens = [length for length, _ in segments]
    neg_inf = np.float32(-np.inf)
    pos_inf = np.float32(np.inf)
    kf = np.float32(topk)
    n_rb = tq // rb
    blocks = [(slice(r * rb, (r + 1) * rb), row_keys[r]) for r in range(n_rb)]

    for rows, n_keys in blocks:
        q_pos = q_off + qb * tq + rows.start + lax.broadcasted_iota(jnp.int32, (rb, 1), 0)
        qi = qi_ref[:, rows, :].reshape(IDX_HEADS * rb, IDX_DIM)
        wi = wi_ref[rows, :]
        for c0, kc, si, off in _key_pieces(seg_lens, n_keys, KEY_CHUNK):
            k_pos = c0 + lax.broadcasted_iota(jnp.int32, (1, kc), 1)
            lg = _dot(qi, seg_refs[si][2][0, :, off:off + kc].astype(BF16))
            sc = wi[:, 0:1] * jnp.maximum(lg[0:rb], 0.0)
            for h in range(1, IDX_HEADS):
                sc = sc + wi[:, h:h + 1] * jnp.maximum(lg[h * rb:(h + 1) * rb], 0.0)
            k_chunk = k_pos // CHUNK
            if c0 + kc > n_valid:
                k_chunk = jnp.where(k_pos < n_valid, k_chunk, np.int32(2 ** 30))
            adm = k_chunk <= (q_pos // CHUNK)
            s_ref[rows, c0:c0 + kc] = jnp.where(adm, sc, neg_inf)
            d_ref[rows, c0:c0 + kc] = jnp.abs(q_pos - k_pos).astype(F32)

    def scores(blk):
        rows, n_keys = blk
        return s_ref[rows, :n_keys]

    def count_ge(blk, t):
        return jnp.sum(jnp.where(scores(blk) >= t, 1.0, 0.0), axis=1, keepdims=True)

    brackets, states, n_adms = [], [], []
    for blk in blocks:
        s = scores(blk)
        finite = s > neg_inf
        n_adm = jnp.sum(jnp.where(finite, 1.0, 0.0), axis=1, keepdims=True)
        row_max = jnp.max(s, axis=1, keepdims=True)
        row_min = jnp.min(jnp.where(finite, s, pos_inf), axis=1, keepdims=True)
        brackets.append((row_min, row_max, jnp.full((rb, 1), pos_inf, F32)))
        n_adms.append(n_adm)

    def bisect(_, carry):
        out = []
        for blk, (lo, hb, hiv) in zip(blocks, carry):
            mid = 0.5 * lo + 0.5 * hb
            ge = count_ge(blk, mid) >= kf
            out.append((jnp.where(ge, mid, lo), jnp.where(ge, hb, mid), jnp.where(ge, hiv, mid)))
        return tuple(out)

    brackets = lax.fori_loop(0, BISECT_STEPS, bisect, tuple(brackets))

    for (_, _, hiv), n_adm in zip(brackets, n_adms):
        done = jnp.where(n_adm <= kf, 1.0, 0.0)
        states.append((jnp.full((rb, 1), np.finfo(np.float32).min, F32), hiv, done))

    def n_open(states):
        return sum(jnp.sum(1.0 - done) for _, _, done in states)

    def scan_cond(carry):
        _, n_left, it = carry
        return jnp.logical_and(n_left > 0.0, it < max(row_keys))

    def scan_body(carry):
        states, _, it = carry
        out = []
        for blk, (thr, hiv, done) in zip(blocks, states):
            sv = scores(blk)
            cand = jnp.max(jnp.where(sv < hiv, sv, neg_inf), axis=1, keepdims=True)
            found = jnp.where(done > 0.5, 0.0, jnp.where(count_ge(blk, cand) >= kf, 1.0, 0.0))
            thr = jnp.where(found > 0.5, cand, thr)
            done = jnp.maximum(done, found)
            out.append((thr, jnp.where(done > 0.5, hiv, cand), done))
        return tuple(out), n_open(out), it + 1

    states, _, _ = lax.while_loop(scan_cond, scan_body, (tuple(states), n_open(states), jnp.int32(0)))

    over = []
    for blk, (thr, _, _), n_adm in zip(blocks, states, n_adms):
        thr_ref[blk[0], :] = thr
        over.append(jnp.max(jnp.where(n_adm > kf, count_ge(blk, thr), 0.0)))
    tied = functools.reduce(jnp.maximum, over) > kf

    ones_row = jnp.where(lax.broadcasted_iota(jnp.int32, (LANES - B_HEAD_DIM, ATT_CHUNK), 0) == 0, 1.0, 0.0)
    ones_row = ones_row.astype(BF16)

    for rows, n_keys in blocks:
        thr = thr_ref[rows, :]

        @pl.when(jnp.logical_not(tied))
        def _():
            d_ref[rows, :n_keys] = jnp.where(s_ref[rows, :n_keys] >= thr, d_ref[rows, :n_keys], pos_inf)

        @pl.when(tied)
        def _():
            n_gt = jnp.sum(jnp.where(s_ref[rows, :n_keys] > thr, 1.0, 0.0), axis=1, keepdims=True)
            room = kf - n_gt
            tri = (lax.broadcasted_iota(jnp.int32, (LANES, LANES), 0)
                   <= lax.broadcasted_iota(jnp.int32, (LANES, LANES), 1))
            tri = jnp.where(tri, 1.0, 0.0).astype(BF16)
            run = jnp.zeros((rb, 1), F32)
            for c0 in range(0, n_keys, LANES):
                blk = s_ref[rows, c0:c0 + LANES]
                eq = jnp.where(blk == thr, 1.0, 0.0)
                prefix = _dot(eq.astype(BF16), tri) + run
                keep = jnp.where(blk > thr, 1.0, jnp.where(prefix <= room, eq, 0.0))
                d_ref[rows, c0:c0 + LANES] = jnp.where(keep > 0.5, d_ref[rows, c0:c0 + LANES], pos_inf)
                run = run + jnp.sum(eq, axis=1, keepdims=True)

        qs = [q_ref[g * B_GROUP:(g + 1) * B_GROUP, rows, :].reshape(B_GROUP * rb, B_HEAD_DIM)
              for g in range(B_KV_HEADS)]
        m_run = [jnp.full((rb, 1), neg_inf, F32) for _ in range(B_HEADS)]
        acc = [jnp.zeros((rb, LANES), F32) for _ in range(B_HEADS)]
        for c0, kc, si, off in _key_pieces(seg_lens, n_keys, ATT_CHUNK):
            kt_ref, v_ref, _ = seg_refs[si]
            dist = d_ref[rows, c0:c0 + kc]
            for g in range(B_KV_HEADS):
                logits = _dot(qs[g], kt_ref[0, g, :, off:off + kc].astype(BF16))
                ps, alphas = [], []
                for hh in range(B_GROUP):
                    h = g * B_GROUP + hh
                    lgt = logits[hh * rb:(hh + 1) * rb] - np.float32(slopes[h] * LOG2_E) * dist
                    m_new = jnp.maximum(m_run[h], jnp.max(lgt, axis=1, keepdims=True))
                    m_ref = jnp.where(m_new == neg_inf, 0.0, m_new)
                    alphas.append(jnp.exp2(m_run[h] - m_ref))
                    ps.append(jnp.exp2(lgt - m_ref).astype(BF16))
                    m_run[h] = m_new
                p = jnp.concatenate(ps, axis=0)
                if segments[si][1]:
                    vt = jnp.concatenate([v_ref[0, g, :, off:off + kc].astype(BF16), ones_row[:, :kc]], axis=0)
                    pv = _dot_nt(p, vt)
                else:
                    pv = _dot(p, v_ref[0, g, off:off + kc, :])
                for hh in range(B_GROUP):
                    h = g * B_GROUP + hh
                    acc[h] = alphas[hh] * acc[h] + pv[hh * rb:(hh + 1) * rb]
        for h in range(B_HEADS):
            out = acc[h][:, :B_HEAD_DIM] / acc[h][:, B_HEAD_DIM:B_HEAD_DIM + 1]
            o_ref[rows, h * B_HEAD_DIM:(h + 1) * B_HEAD_DIM] = out.astype(BF16)


def _dsa_attention(q_hm, qi_hm, wi, key_segments, *, n, t_q, tq, rb, qb, n_valid, q_off, topk, slopes):
    nqb = t_q // tq
    n_rb = tq // rb
    last_chunk = (q_off + (qb + 1) * tq - 1) // CHUNK
    need = min((last_chunk + 1) * CHUNK, n_valid)
    n_keys = -(-need // LANES) * LANES
    if rb % CHUNK == 0 and need == q_off + (qb + 1) * tq:
        row_keys = tuple(n_keys - (n_rb - 1 - r) * rb for r in range(n_rb))
    else:
        row_keys = (n_keys,) * n_rb
    qmap = lambda b: (0, b * nqb + qb, 0)
    in_specs = [
        pl.BlockSpec((B_HEADS, tq, B_HEAD_DIM), qmap),
        pl.BlockSpec((IDX_HEADS, tq, IDX_DIM), qmap),
        pl.BlockSpec((tq, IDX_HEADS), lambda b: (b * nqb + qb, 0)),
    ]
    operands, segments, col = [], [], 0
    for kt, v, kit, v_transposed in key_segments:
        length = min(kt.shape[-1], n_keys - col)
        if length <= 0:
            break
        v_block = (1, B_KV_HEADS, B_HEAD_DIM, length) if v_transposed else (1, B_KV_HEADS, length, LANES)
        in_specs += [
            pl.BlockSpec((1, B_KV_HEADS, B_HEAD_DIM, length), lambda b: (b, 0, 0, 0)),
            pl.BlockSpec(v_block, lambda b: (b, 0, 0, 0)),
            pl.BlockSpec((1, IDX_DIM, length), lambda b: (b, 0, 0)),
        ]
        operands += [kt, v, kit]
        segments.append((length, v_transposed))
        col += length
    assert col == n_keys, (col, n_keys)
    return pl.pallas_call(
        functools.partial(_dsa_block, segments=tuple(segments), qb=qb, tq=tq, rb=rb, row_keys=row_keys,
                          n_valid=n_valid, q_off=q_off, topk=topk, slopes=slopes),
        grid=(n,), in_specs=in_specs,
        out_specs=pl.BlockSpec((tq, B_WIDTH), lambda b: (b, 0)),
        out_shape=jax.ShapeDtypeStruct((n * tq, B_WIDTH), BF16),
        scratch_shapes=[pltpu.VMEM((tq, n_keys), F32), pltpu.VMEM((tq, n_keys), F32), pltpu.VMEM((tq, 1), F32)],
        compiler_params=pltpu.CompilerParams(dimension_semantics=("parallel",), vmem_limit_bytes=VMEM_LIMIT),
        name="dsa_attention",
    )(q_hm, qi_hm, wi, *operands)


def _merge_kernel(x_ref, a_ref, b_ref, wg_ref, wb_ref, wo_ref, ln1_ref, w1_ref, b1_ref, w2_ref, b2_ref, ln2_ref,
                  y_ref, *, alpha):
    d = x_ref.shape[1]
    x = x_ref[...]
    gates = jax.nn.sigmoid(_dot(x.astype(BF16), wg_ref[...]))
    m = gates[:, :d] * _dot(a_ref[...], wb_ref[0]) + gates[:, d:] * _dot(b_ref[...], wb_ref[1])
    h = _layer_norm(alpha * x + _dot(m.astype(BF16), wo_ref[...]), ln1_ref[0:1, :], ln1_ref[1:2, :])
    f = jnp.square(jnp.maximum(_dot(h.astype(BF16), w1_ref[...]) + b1_ref[...], 0.0))
    f = _dot(f.astype(BF16), w2_ref[...]) + b2_ref[...]
    y_ref[...] = _layer_norm(alpha * h + f, ln2_ref[0:1, :], ln2_ref[1:2, :])


def _resident_spec(shape):
    nd = len(shape)
    return pl.BlockSpec(shape, lambda *_: (0,) * nd, pipeline_mode=pl.Buffered(1))


def _merge_ffn(x2, a, b, wts, tm, alpha):
    t, d = x2.shape
    row = lambda i: (i, 0)
    in_specs = [pl.BlockSpec((tm, d), row), pl.BlockSpec((tm, A_WIDTH), row), pl.BlockSpec((tm, B_WIDTH), row)]
    in_specs += [_resident_spec(w.shape) for w in wts]
    return pl.pallas_call(
        functools.partial(_merge_kernel, alpha=np.float32(alpha)),
        grid=(t // tm,), in_specs=in_specs,
        out_specs=pl.BlockSpec((tm, d), row),
        out_shape=jax.ShapeDtypeStruct((t, d), F32),
        compiler_params=pltpu.CompilerParams(dimension_semantics=("parallel",), vmem_limit_bytes=VMEM_LIMIT),
        name="merge_ffn",
    )(x2, a, b, *wts)


def _spatial_weights(w_s, b_s, span):
    pos = jnp.arange(A_SPAN)
    mask = (pos[None, :] // CHUNK) <= (pos[:, None] // CHUNK)
    ws = jnp.where(mask[None], w_s, 0.0)[:, :span, :span].astype(BF16)
    bs = jnp.repeat(b_s[:, :span].T, A_GROUP_CH, axis=1)
    return ws, bs


def _pad_axis(x, axis, size):
    pads = [(0, 0)] * x.ndim
    pads[axis] = (0, size - x.shape[axis])
    return jnp.pad(x, pads)


def kernel(x_prompt, x_sample, cache_k, cache_v, cache_kidx, w_in, lnv_g, lnv_b, w_s, b_s, lnk_g, lnk_b,
           w_branch, w_out, ln1_g, ln1_b, w_ff1, b_ff1, w_ff2, b_ff2, ln2_g, ln2_b):
    depth = w_in.shape[0]
    n_p, s_p, d = x_prompt.shape
    n_s, t_s, _ = x_sample.shape
    past = cache_k.shape[2]
    alpha = (2 * depth) ** 0.25
    slopes = tuple(float(2.0 ** (-8.0 * h / B_HEADS)) for h in range(1, B_HEADS + 1))
    kvw = B_KV_HEADS * B_HEAD_DIM
    c_a = 2 * A_WIDTH
    c_q = c_a + B_WIDTH
    c_k = c_q + kvw
    c_v = c_k + kvw
    c_qi = c_v + IDX_HEADS * IDX_DIM
    c_wi = c_qi + IDX_DIM + IDX_HEADS

    xp = x_prompt.reshape(n_p * s_p, d)
    xs = x_sample.reshape(n_s * t_s, d)
    outs = [[] for _ in range(7)]
    for l in range(depth):
        w = w_in[l]
        pad = jnp.zeros((d, LANES - (c_wi - c_qi)), F32)
        proj_w = (
            w[:, :c_a].astype(BF16),
            (w[:, c_a:c_q] * (LOG2_E * B_HEAD_DIM ** -0.5)).astype(BF16),
            w[:, c_q:c_v].astype(BF16),
            jnp.pad(w[:, c_k:c_v].reshape(d, B_KV_HEADS, B_HEAD_DIM),
                    ((0, 0), (0, 0), (0, LANES - B_HEAD_DIM))).reshape(d, B_KV_HEADS * LANES).astype(BF16),
            (w[:, c_v:c_qi] * (IDX_DIM ** -0.5)).astype(BF16),
            jnp.concatenate([w[:, c_qi:c_wi], pad], axis=1).astype(BF16),
            jnp.stack([lnv_g[l], lnv_b[l]]),
            jnp.stack([lnk_g[l], lnk_b[l]]),
        )
        merge_w = (
            w[:, c_wi:].astype(BF16),
            w_branch[l].astype(BF16),
            w_out[l].astype(BF16),
            jnp.stack([ln1_g[l], ln1_b[l]]),
            w_ff1[l].astype(BF16),
            b_ff1[l][None, :],
            w_ff2[l].astype(BF16),
            b_ff2[l][None, :],
            jnp.stack([ln2_g[l], ln2_b[l]]),
        )

        a_p, q_p, qi_p, k_p, v_p, kt_p, vb_p, ki_p, kit_p, wi_p = _in_projection(
            xp, proj_w + _spatial_weights(w_s[l], b_s[l], A_SPAN), A_SPAN, TOKEN_TILE, False, batch_rows=s_p)
        b_p = [_dsa_attention(
            q_p, qi_p, wi_p, [(kt_p, vb_p, kit_p, False)],
            n=n_p, t_q=s_p, tq=DSA_ROWS, rb=Q_BLOCK, qb=j, n_valid=s_p, q_off=0,
            topk=min(TOPK_MAX, s_p // 4), slopes=slopes).reshape(n_p, DSA_ROWS, B_WIDTH)
            for j in range(s_p // DSA_ROWS)]
        b_p = jnp.stack(b_p, axis=1).reshape(n_p * s_p, B_WIDTH)
        xp = _merge_ffn(xp, a_p, b_p, merge_w, TOKEN_TILE, alpha)

        a_s, q_s, qi_s, k_s, v_s, kt_s, vb_s, ki_s, kit_s, wi_s, va_s = _in_projection(
            xs, proj_w + _spatial_weights(w_s[l], b_s[l], t_s), t_s, n_s * t_s, True)
        n_all = past + t_s
        new_len = -(-t_s // LANES) * LANES
        cache_seg = (jnp.transpose(cache_k[l], (0, 2, 3, 1)), jnp.transpose(cache_v[l], (0, 2, 3, 1)),
                     jnp.transpose(cache_kidx[l], (0, 2, 1)), True)
        kt_new = jnp.transpose(kt_s.reshape(B_KV_HEADS, B_HEAD_DIM, n_s, t_s), (2, 0, 1, 3))
        vb_new = jnp.transpose(vb_s.reshape(B_KV_HEADS, n_s, t_s, LANES), (1, 0, 2, 3))
        kit_new = jnp.transpose(kit_s.reshape(IDX_DIM, n_s, t_s), (1, 0, 2))
        new_seg = (_pad_axis(kt_new, 3, new_len), _pad_axis(vb_new, 2, new_len), _pad_axis(kit_new, 2, new_len), False)
        b_s_ = _dsa_attention(
            q_s, qi_s, wi_s, [cache_seg, new_seg],
            n=n_s, t_q=t_s, tq=t_s, rb=t_s, qb=0, n_valid=n_all, q_off=past,
            topk=min(TOPK_MAX, n_all // 4), slopes=slopes)
        xs = _merge_ffn(xs, a_s, b_s_, merge_w, n_s * t_s, alpha)

        outs[0].append(k_p.reshape(n_p, s_p, B_KV_HEADS, B_HEAD_DIM))
        outs[1].append(v_p.reshape(n_p, s_p, B_KV_HEADS, B_HEAD_DIM))
        outs[2].append(ki_p.reshape(n_p, s_p, IDX_DIM))
        outs[3].append(k_s.reshape(n_s, t_s, B_KV_HEADS, B_HEAD_DIM))
        outs[4].append(v_s.reshape(n_s, t_s, B_KV_HEADS, B_HEAD_DIM))
        outs[5].append(ki_s.reshape(n_s, t_s, IDX_DIM))
        outs[6].append(va_s.reshape(n_s, t_s, A_WIDTH))

    return (xp.reshape(n_p, s_p, d), xs.reshape(n_s, t_s, d)) + tuple(jnp.stack(o) for o in outs)
```

```python
import functools

import numpy as np
import jax
import jax.numpy as jnp
from jax import lax
from jax.experimental import pallas as pl
from jax.experimental.pallas import tpu as pltpu

CHUNK = 64
A_WIDTH = 512
A_GROUPS = 4
A_GROUP_CH = A_WIDTH // A_GROUPS
A_SPAN = 128
B_HEADS = 8
B_HEAD_DIM = 64
B_KV_HEADS = 2
B_GROUP = B_HEADS // B_KV_HEADS
B_WIDTH = B_HEADS * B_HEAD_DIM
IDX_HEADS = 8
IDX_DIM = 64
TOPK_MAX = 256
Q_BLOCK = 128
LN_EPS = 1e-5
LOG2_E = 1.4426950408889634

LANES = 128
VMEM_LIMIT = 52 * 1024 * 1024
TOKEN_TILE = 512
KEY_CHUNK = 512
ATT_CHUNK = 256
DSA_ROWS = 4 * Q_BLOCK
BISECT16_STEPS = 9
BISECT_STEPS = 6

F32 = jnp.float32
BF16 = jnp.bfloat16
NT_DIMS = (((1,), (1,)), ((), ()))


def _dot(a, b):
    return jnp.dot(a, b, preferred_element_type=F32)


def _dot_nt(a, b):
    return lax.dot_general(a, b, NT_DIMS, preferred_element_type=F32)


def _layer_norm(x, g, b):
    mu = jnp.mean(x, axis=-1, keepdims=True)
    xc = x - mu
    var = jnp.mean(xc * xc, axis=-1, keepdims=True)
    return xc * lax.rsqrt(var + LN_EPS) * g + b


def _inproj_kernel(x_ref, wa_ref, wq_ref, wkv_ref, wva_ref, wqi_ref, wkw_ref, lnv_ref, lnk_ref, ws_ref, bs_ref,
                   a_ref, q_ref, qi_ref, k_ref, v_ref, kt_ref, vb_ref, ki_ref, kit_ref, wi_ref, *va_refs,
                   span):
    tm = x_ref.shape[0]
    xb = x_ref[...].astype(BF16)

    za = _dot(xb, wa_ref[...])
    ga = 0.5 * za * (1.0 + lax.erf(za * np.float32(np.sqrt(0.5))))
    u = ga[:, :A_WIDTH]
    va = _layer_norm(ga[:, A_WIDTH:], lnv_ref[0:1, :], lnv_ref[1:2, :])
    if va_refs:
        va_refs[0][...] = va
    vab = va.astype(BF16)

    zq = _dot(xb, wq_ref[...])
    zqi = _dot(xb, wqi_ref[...])
    for h in range(B_HEADS):
        q_ref[h] = zq[:, h * B_HEAD_DIM:(h + 1) * B_HEAD_DIM].astype(BF16)
    for h in range(IDX_HEADS):
        qi_ref[h] = zqi[:, h * IDX_DIM:(h + 1) * IDX_DIM].astype(BF16)

    zkv = _dot(xb, wkv_ref[...])
    kvw = B_KV_HEADS * B_HEAD_DIM
    if len(kt_ref.shape) == 4:
        kt_ref, vb_ref, kit_ref = kt_ref.at[0], vb_ref.at[0], kit_ref.at[0]
    zva = _dot(xb, wva_ref[...])
    ones_lane = lax.broadcasted_iota(jnp.int32, (tm, LANES), 1) == B_HEAD_DIM
    for g in range(B_KV_HEADS):
        kg = zkv[:, g * B_HEAD_DIM:(g + 1) * B_HEAD_DIM]
        k_ref[pl.ds(g, tm, stride=B_KV_HEADS), :] = kg
        v_ref[pl.ds(g, tm, stride=B_KV_HEADS), :] = zkv[:, kvw + g * B_HEAD_DIM:kvw + (g + 1) * B_HEAD_DIM]
        kt_ref[g] = kg.T.astype(BF16)
        vb_ref[g] = jnp.where(ones_lane, 1.0, zva[:, g * LANES:(g + 1) * LANES]).astype(BF16)

    zkw = _dot(xb, wkw_ref[...])
    ki = _layer_norm(zkw[:, :IDX_DIM], lnk_ref[0:1, :], lnk_ref[1:2, :])
    ki_ref[...] = ki
    kit_ref[...] = ki.T.astype(BF16)
    wi_ref[...] = zkw[:, IDX_DIM:IDX_DIM + IDX_HEADS] * np.float32(IDX_HEADS ** -0.5)

    for s in range(tm // span):
        rows = slice(s * span, (s + 1) * span)
        for g in range(A_GROUPS):
            cols = slice(g * A_GROUP_CH, (g + 1) * A_GROUP_CH)
            mixed = _dot(ws_ref[g], vab[rows, cols]) + bs_ref[:, cols]
            a_ref[rows, cols] = (u[rows, cols] * mixed).astype(BF16)


def _const_spec(shape):
    nd = len(shape)
    return pl.BlockSpec(shape, lambda *_: (0,) * nd)


def _in_projection(x2, wts, span, tm, emit_va, batch_rows=None):
    t, d = x2.shape
    grid = (t // tm,)
    row = lambda i: (i, 0)
    hm = lambda i: (0, i, 0)
    if batch_rows is None:
        kt_shape, kt_block, kt_map = (B_KV_HEADS, B_HEAD_DIM, t), (B_KV_HEADS, B_HEAD_DIM, tm), lambda i: (0, 0, i)
        vb_shape, vb_block, vb_map = (B_KV_HEADS, t, LANES), (B_KV_HEADS, tm, LANES), hm
        ki_shape, ki_block, ki_map = (IDX_DIM, t), (IDX_DIM, tm), lambda i: (0, i)
    else:
        per, n = batch_rows // tm, t // batch_rows
        kt_shape, kt_block = (n, B_KV_HEADS, B_HEAD_DIM, batch_rows), (1, B_KV_HEADS, B_HEAD_DIM, tm)
        kt_map = lambda i: (i // per, 0, 0, i % per)
        vb_shape, vb_block = (n, B_KV_HEADS, batch_rows, LANES), (1, B_KV_HEADS, tm, LANES)
        vb_map = lambda i: (i // per, 0, i % per, 0)
        ki_shape, ki_block, ki_map = (n, IDX_DIM, batch_rows), (1, IDX_DIM, tm), lambda i: (i // per, 0, i % per)
    in_specs = [pl.BlockSpec((tm, d), row)] + [_const_spec(w.shape) for w in wts]
    out_shape = [
        jax.ShapeDtypeStruct((t, A_WIDTH), BF16),
        jax.ShapeDtypeStruct((B_HEADS, t, B_HEAD_DIM), BF16),
        jax.ShapeDtypeStruct((IDX_HEADS, t, IDX_DIM), BF16),
        jax.ShapeDtypeStruct((t * B_KV_HEADS, B_HEAD_DIM), F32),
        jax.ShapeDtypeStruct((t * B_KV_HEADS, B_HEAD_DIM), F32),
        jax.ShapeDtypeStruct(kt_shape, BF16),
        jax.ShapeDtypeStruct(vb_shape, BF16),
        jax.ShapeDtypeStruct((t, IDX_DIM), F32),
        jax.ShapeDtypeStruct(ki_shape, BF16),
        jax.ShapeDtypeStruct((t, IDX_HEADS), F32),
    ]
    out_specs = [
        pl.BlockSpec((tm, A_WIDTH), row),
        pl.BlockSpec((B_HEADS, tm, B_HEAD_DIM), hm),
        pl.BlockSpec((IDX_HEADS, tm, IDX_DIM), hm),
        pl.BlockSpec((tm * B_KV_HEADS, B_HEAD_DIM), row),
        pl.BlockSpec((tm * B_KV_HEADS, B_HEAD_DIM), row),
        pl.BlockSpec(kt_block, kt_map),
        pl.BlockSpec(vb_block, vb_map),
        pl.BlockSpec((tm, IDX_DIM), row),
        pl.BlockSpec(ki_block, ki_map),
        pl.BlockSpec((tm, IDX_HEADS), row),
    ]
    if emit_va:
        out_shape.append(jax.ShapeDtypeStruct((t, A_WIDTH), F32))
        out_specs.append(pl.BlockSpec((tm, A_WIDTH), row))
    return pl.pallas_call(
        functools.partial(_inproj_kernel, span=span),
        grid=grid, in_specs=in_specs, out_specs=out_specs, out_shape=out_shape,
        compiler_params=pltpu.CompilerParams(dimension_semantics=("parallel",), vmem_limit_bytes=VMEM_LIMIT),
        name="in_projection",
    )(x2, *wts)


def _key_pieces(seg_lens, n_keys, chunk):
    pieces, col = [], 0
    for si, seg_len in enumerate(seg_lens):
        off = 0
        while off < seg_len and col < n_keys:
            width = min(chunk, seg_len - off, n_keys - col)
            pieces.append((col, width, si, off))
            off += width
            col += width
    return pieces


def _dsa_block(q_ref, qi_ref, wi_ref, *refs, segments, qb, tq, rb, row_keys, n_valid, q_off, topk, slopes):
    n_seg = len(segments)
    seg_refs = [refs[3 * i:3 * i + 3] for i in range(n_seg)]
    o_ref, s_ref, d_ref, thr_ref, s16_ref = refs[3 * n_seg:]
    seg_lens = [length for length, _ in segments]
    neg_inf = np.float32(-np.inf)
    pos_inf = np.float32(np.inf)
    kf = np.float32(topk)
    n_rb = tq // rb
    blocks = [(slice(r * rb, (r + 1) * rb), row_keys[r]) for r in range(n_rb)]

    for rows, n_keys in blocks:
        q_pos = q_off + qb * tq + rows.start + lax.broadcasted_iota(jnp.int32, (rb, 1), 0)
        qi = qi_ref[:, rows, :].reshape(IDX_HEADS * rb, IDX_DIM)
        wi = wi_ref[rows, :]
        for c0, kc, si, off in _key_pieces(seg_lens, n_keys, KEY_CHUNK):
            k_pos = c0 + lax.broadcasted_iota(jnp.int32, (1, kc), 1)
            lg = _dot(qi, seg_refs[si][2][0, :, off:off + kc].astype(BF16))
            sc = wi[:, 0:1] * jnp.maximum(lg[0:rb], 0.0)
            for h in range(1, IDX_HEADS):
                sc = sc + wi[:, h:h + 1] * jnp.maximum(lg[h * rb:(h + 1) * rb], 0.0)
            k_chunk = k_pos // CHUNK
            if c0 + kc > n_valid:
                k_chunk = jnp.where(k_pos < n_valid, k_chunk, np.int32(2 ** 30))
            adm = k_chunk <= (q_pos // CHUNK)
            sc = jnp.where(adm, sc, neg_inf)
            s_ref[rows, c0:c0 + kc] = sc
            s16_ref[rows, c0:c0 + kc] = sc.astype(BF16)
            d_ref[rows, c0:c0 + kc] = jnp.abs(q_pos - k_pos).astype(F32)

    def scores(blk):
        rows, n_keys = blk
        return s_ref[rows, :n_keys]

    def count_ge(blk, t):
        return jnp.sum(jnp.where(scores(blk) >= t, 1.0, 0.0), axis=1, keepdims=True)

    brackets, states, n_adms = [], [], []
    for blk in blocks:
        s = scores(blk)
        finite = s > neg_inf
        n_adm = jnp.sum(jnp.where(finite, 1.0, 0.0), axis=1, keepdims=True)
        row_max = jnp.max(s, axis=1, keepdims=True)
        row_min = jnp.min(jnp.where(finite, s, pos_inf), axis=1, keepdims=True)
        brackets.append((row_min, row_max, jnp.full((rb, 1), pos_inf, F32)))
        n_adms.append(n_adm)

    def count16_ge(blk, t):
        rows, n_keys = blk
        tb = jnp.broadcast_to(t, (rb, LANES)).astype(BF16)
        one, zero = jnp.ones((rb, LANES), BF16), jnp.zeros((rb, LANES), BF16)
        part = zero
        for c0 in range(0, n_keys, LANES):
            part = part + jnp.where(s16_ref[rows, c0:c0 + LANES] >= tb, one, zero)
        return jnp.sum(part.astype(F32), axis=1, keepdims=True)

    def bisect16(_, carry):
        out = []
        for blk, (lo, hb, hiv) in zip(blocks, carry):
            t = (0.5 * lo + 0.5 * hb).astype(BF16).astype(F32)
            ge = count16_ge(blk, t) >= kf
            below = t - (jnp.abs(t) * np.float32(2.0 ** -7) + np.float32(1e-30))
            out.append((jnp.where(ge, jnp.maximum(lo, below), lo), jnp.where(ge, hb, jnp.minimum(hb, t)),
                        jnp.where(ge, hiv, t)))
        return tuple(out)

    def bisect(_, carry):
        out = []
        for blk, (lo, hb, hiv) in zip(blocks, carry):
            mid = 0.5 * lo + 0.5 * hb
            ge = count_ge(blk, mid) >= kf
            out.append((jnp.where(ge, mid, lo), jnp.where(ge, hb, mid), jnp.where(ge, hiv, mid)))
        return tuple(out)

    brackets = lax.fori_loop(0, BISECT16_STEPS, bisect16, tuple(brackets))
    brackets = lax.fori_loop(0, BISECT_STEPS, bisect, brackets)

    for (_, _, hiv), n_adm in zip(brackets, n_adms):
        done = jnp.where(n_adm <= kf, 1.0, 0.0)
        states.append((jnp.full((rb, 1), np.finfo(np.float32).min, F32), hiv, done))

    def n_open(states):
        return sum(jnp.sum(1.0 - done) for _, _, done in states)

    def scan_cond(carry):
        _, n_left, it = carry
        return jnp.logical_and(n_left > 0.0, it < max(row_keys))

    def scan_body(carry):
        states, _, it = carry
        out = []
        for blk, (thr, hiv, done) in zip(blocks, states):
            sv = scores(blk)
            cand = jnp.max(jnp.where(sv < hiv, sv, neg_inf), axis=1, keepdims=True)
            found = jnp.where(done > 0.5, 0.0, jnp.where(count_ge(blk, cand) >= kf, 1.0, 0.0))
            thr = jnp.where(found > 0.5, cand, thr)
            done = jnp.maximum(done, found)
            out.append((thr, jnp.where(done > 0.5, hiv, cand), done))
        return tuple(out), n_open(out), it + 1

    states, _, _ = lax.while_loop(scan_cond, scan_body, (tuple(states), n_open(states), jnp.int32(0)))

    over = []
    for blk, (thr, _, _), n_adm in zip(blocks, states, n_adms):
        thr_ref[blk[0], :] = thr
        over.append(jnp.max(jnp.where(n_adm > kf, count_ge(blk, thr), 0.0)))
    tied = functools.reduce(jnp.maximum, over) > kf

    ones_row = jnp.where(lax.broadcasted_iota(jnp.int32, (LANES - B_HEAD_DIM, ATT_CHUNK), 0) == 0, 1.0, 0.0)
    ones_row = ones_row.astype(BF16)

    for rows, n_keys in blocks:
        thr = thr_ref[rows, :]

        @pl.when(jnp.logical_not(tied))
        def _():
            d_ref[rows, :n_keys] = jnp.where(s_ref[rows, :n_keys] >= thr, d_ref[rows, :n_keys], pos_inf)

        @pl.when(tied)
        def _():
            n_gt = jnp.sum(jnp.where(s_ref[rows, :n_keys] > thr, 1.0, 0.0), axis=1, keepdims=True)
            room = kf - n_gt
            tri = (lax.broadcasted_iota(jnp.int32, (LANES, LANES), 0)
                   <= lax.broadcasted_iota(jnp.int32, (LANES, LANES), 1))
            tri = jnp.where(tri, 1.0, 0.0).astype(BF16)
            run = jnp.zeros((rb, 1), F32)
            for c0 in range(0, n_keys, LANES):
                blk = s_ref[rows, c0:c0 + LANES]
                eq = jnp.where(blk == thr, 1.0, 0.0)
                prefix = _dot(eq.astype(BF16), tri) + run
                keep = jnp.where(blk > thr, 1.0, jnp.where(prefix <= room, eq, 0.0))
                d_ref[rows, c0:c0 + LANES] = jnp.where(keep > 0.5, d_ref[rows, c0:c0 + LANES], pos_inf)
                run = run + jnp.sum(eq, axis=1, keepdims=True)

        qs = [q_ref[g * B_GROUP:(g + 1) * B_GROUP, rows, :].reshape(B_GROUP * rb, B_HEAD_DIM)
              for g in range(B_KV_HEADS)]
        m_run = [jnp.full((rb, 1), neg_inf, F32) for _ in range(B_HEADS)]
        acc = [jnp.zeros((rb, LANES), F32) for _ in range(B_HEADS)]
        for c0, kc, si, off in _key_pieces(seg_lens, n_keys, ATT_CHUNK):
            kt_ref, v_ref, _ = seg_refs[si]
            dist = d_ref[rows, c0:c0 + kc]
            for g in range(B_KV_HEADS):
                logits = _dot(qs[g], kt_ref[0, g, :, off:off + kc].astype(BF16))
                ps, alphas = [], []
                for hh in range(B_GROUP):
                    h = g * B_GROUP + hh
                    lgt = logits[hh * rb:(hh + 1) * rb] - np.float32(slopes[h] * LOG2_E) * dist
                    m_new = jnp.maximum(m_run[h], jnp.max(lgt, axis=1, keepdims=True))
                    m_ref = jnp.where(m_new == neg_inf, 0.0, m_new)
                    alphas.append(jnp.exp2(m_run[h] - m_ref))
                    ps.append(jnp.exp2(lgt - m_ref).astype(BF16))
                    m_run[h] = m_new
                p = jnp.concatenate(ps, axis=0)
                if segments[si][1]:
                    vt = jnp.concatenate([v_ref[0, g, :, off:off + kc].astype(BF16), ones_row[:, :kc]], axis=0)
                    pv = _dot_nt(p, vt)
                else:
                    pv = _dot(p, v_ref[0, g, off:off + kc, :])
                for hh in range(B_GROUP):
                    h = g * B_GROUP + hh
                    acc[h] = alphas[hh] * acc[h] + pv[hh * rb:(hh + 1) * rb]
        for h in range(B_HEADS):
            out = acc[h][:, :B_HEAD_DIM] / acc[h][:, B_HEAD_DIM:B_HEAD_DIM + 1]
            o_ref[rows, h * B_HEAD_DIM:(h + 1) * B_HEAD_DIM] = out.astype(BF16)


def _dsa_attention(q_hm, qi_hm, wi, key_segments, *, n, t_q, tq, rb, qb, n_valid, q_off, topk, slopes):
    nqb = t_q // tq
    n_rb = tq // rb
    last_chunk = (q_off + (qb + 1) * tq - 1) // CHUNK
    need = min((last_chunk + 1) * CHUNK, n_valid)
    n_keys = -(-need // LANES) * LANES
    if rb % CHUNK == 0 and need == q_off + (qb + 1) * tq:
        row_keys = tuple(n_keys - (n_rb - 1 - r) * rb for r in range(n_rb))
    else:
        row_keys = (n_keys,) * n_rb
    qmap = lambda b: (0, b * nqb + qb, 0)
    in_specs = [
        pl.BlockSpec((B_HEADS, tq, B_HEAD_DIM), qmap),
        pl.BlockSpec((IDX_HEADS, tq, IDX_DIM), qmap),
        pl.BlockSpec((tq, IDX_HEADS), lambda b: (b * nqb + qb, 0)),
    ]
    operands, segments, col = [], [], 0
    for kt, v, kit, v_transposed in key_segments:
        length = min(kt.shape[-1], n_keys - col)
        if length <= 0:
            break
        v_block = (1, B_KV_HEADS, B_HEAD_DIM, length) if v_transposed else (1, B_KV_HEADS, length, LANES)
        in_specs += [
            pl.BlockSpec((1, B_KV_HEADS, B_HEAD_DIM, length), lambda b: (b, 0, 0, 0)),
            pl.BlockSpec(v_block, lambda b: (b, 0, 0, 0)),
            pl.BlockSpec((1, IDX_DIM, length), lambda b: (b, 0, 0)),
        ]
        operands += [kt, v, kit]
        segments.append((length, v_transposed))
        col += length
    assert col == n_keys, (col, n_keys)
    return pl.pallas_call(
        functools.partial(_dsa_block, segments=tuple(segments), qb=qb, tq=tq, rb=rb, row_keys=row_keys,
                          n_valid=n_valid, q_off=q_off, topk=topk, slopes=slopes),
        grid=(n,), in_specs=in_specs,
        out_specs=pl.BlockSpec((tq, B_WIDTH), lambda b: (b, 0)),
        out_shape=jax.ShapeDtypeStruct((n * tq, B_WIDTH), BF16),
        scratch_shapes=[pltpu.VMEM((tq, n_keys), F32), pltpu.VMEM((tq, n_keys), F32), pltpu.VMEM((tq, 1), F32),
                        pltpu.VMEM((tq, n_keys), BF16)],
        compiler_params=pltpu.CompilerParams(dimension_semantics=("parallel",), vmem_limit_bytes=VMEM_LIMIT),
        name="dsa_attention",
    )(q_hm, qi_hm, wi, *operands)


def _merge_kernel(x_ref, a_ref, b_ref, wg_ref, wb_ref, wo_ref, ln1_ref, w1_ref, b1_ref, w2_ref, b2_ref, ln2_ref,
                  y_ref, *, alpha):
    d = x_ref.shape[1]
    x = x_ref[...]
    gates = jax.nn.sigmoid(_dot(x.astype(BF16), wg_ref[...]))
    m = gates[:, :d] * _dot(a_ref[...], wb_ref[0]) + gates[:, d:] * _dot(b_ref[...], wb_ref[1])
    h = _layer_norm(alpha * x + _dot(m.astype(BF16), wo_ref[...]), ln1_ref[0:1, :], ln1_ref[1:2, :])
    f = jnp.square(jnp.maximum(_dot(h.astype(BF16), w1_ref[...]) + b1_ref[...], 0.0))
    f = _dot(f.astype(BF16), w2_ref[...]) + b2_ref[...]
    y_ref[...] = _layer_norm(alpha * h + f, ln2_ref[0:1, :], ln2_ref[1:2, :])


def _resident_spec(shape):
    nd = len(shape)
    return pl.BlockSpec(shape, lambda *_: (0,) * nd, pipeline_mode=pl.Buffered(1))


def _merge_ffn(x2, a, b, wts, tm, alpha):
    t, d = x2.shape
    row = lambda i: (i, 0)
    in_specs = [pl.BlockSpec((tm, d), row), pl.BlockSpec((tm, A_WIDTH), row), pl.BlockSpec((tm, B_WIDTH), row)]
    in_specs += [_resident_spec(w.shape) for w in wts]
    return pl.pallas_call(
        functools.partial(_merge_kernel, alpha=np.float32(alpha)),
        grid=(t // tm,), in_specs=in_specs,
        out_specs=pl.BlockSpec((tm, d), row),
        out_shape=jax.ShapeDtypeStruct((t, d), F32),
        compiler_params=pltpu.CompilerParams(dimension_semantics=("parallel",), vmem_limit_bytes=VMEM_LIMIT),
        name="merge_ffn",
    )(x2, a, b, *wts)


def _spatial_weights(w_s, b_s, span):
    pos = jnp.arange(A_SPAN)
    mask = (pos[None, :] // CHUNK) <= (pos[:, None] // CHUNK)
    ws = jnp.where(mask[None], w_s, 0.0)[:, :span, :span].astype(BF16)
    bs = jnp.repeat(b_s[:, :span].T, A_GROUP_CH, axis=1)
    return ws, bs


def _pad_axis(x, axis, size):
    pads = [(0, 0)] * x.ndim
    pads[axis] = (0, size - x.shape[axis])
    return jnp.pad(x, pads)


def kernel(x_prompt, x_sample, cache_k, cache_v, cache_kidx, w_in, lnv_g, lnv_b, w_s, b_s, lnk_g, lnk_b,
           w_branch, w_out, ln1_g, ln1_b, w_ff1, b_ff1, w_ff2, b_ff2, ln2_g, ln2_b):
    depth = w_in.shape[0]
    n_p, s_p, d = x_prompt.shape
    n_s, t_s, _ = x_sample.shape
    past = cache_k.shape[2]
    alpha = (2 * depth) ** 0.25
    slopes = tuple(float(2.0 ** (-8.0 * h / B_HEADS)) for h in range(1, B_HEADS + 1))
    kvw = B_KV_HEADS * B_HEAD_DIM
    c_a = 2 * A_WIDTH
    c_q = c_a + B_WIDTH
    c_k = c_q + kvw
    c_v = c_k + kvw
    c_qi = c_v + IDX_HEADS * IDX_DIM
    c_wi = c_qi + IDX_DIM + IDX_HEADS

    xp = x_prompt.reshape(n_p * s_p, d)
    xs = x_sample.reshape(n_s * t_s, d)
    outs = [[] for _ in range(7)]
    for l in range(depth):
        w = w_in[l]
        pad = jnp.zeros((d, LANES - (c_wi - c_qi)), F32)
        proj_w = (
            w[:, :c_a].astype(BF16),
            (w[:, c_a:c_q] * (LOG2_E * B_HEAD_DIM ** -0.5)).astype(BF16),
            w[:, c_q:c_v].astype(BF16),
            jnp.pad(w[:, c_k:c_v].reshape(d, B_KV_HEADS, B_HEAD_DIM),
                    ((0, 0), (0, 0), (0, LANES - B_HEAD_DIM))).reshape(d, B_KV_HEADS * LANES).astype(BF16),
            (w[:, c_v:c_qi] * (IDX_DIM ** -0.5)).astype(BF16),
            jnp.concatenate([w[:, c_qi:c_wi], pad], axis=1).astype(BF16),
            jnp.stack([lnv_g[l], lnv_b[l]]),
            jnp.stack([lnk_g[l], lnk_b[l]]),
        )
        merge_w = (
            w[:, c_wi:].astype(BF16),
            w_branch[l].astype(BF16),
            w_out[l].astype(BF16),
            jnp.stack([ln1_g[l], ln1_b[l]]),
            w_ff1[l].astype(BF16),
            b_ff1[l][None, :],
            w_ff2[l].astype(BF16),
            b_ff2[l][None, :],
            jnp.stack([ln2_g[l], ln2_b[l]]),
        )

        a_p, q_p, qi_p, k_p, v_p, kt_p, vb_p, ki_p, kit_p, wi_p = _in_projection(
            xp, proj_w + _spatial_weights(w_s[l], b_s[l], A_SPAN), A_SPAN, TOKEN_TILE, False, batch_rows=s_p)
        b_p = [_dsa_attention(
            q_p, qi_p, wi_p, [(kt_p, vb_p, kit_p, False)],
            n=n_p, t_q=s_p, tq=DSA_ROWS, rb=Q_BLOCK, qb=j, n_valid=s_p, q_off=0,
            topk=min(TOPK_MAX, s_p // 4), slopes=slopes).reshape(n_p, DSA_ROWS, B_WIDTH)
            for j in range(s_p // DSA_ROWS)]
        b_p = jnp.stack(b_p, axis=1).reshape(n_p * s_p, B_WIDTH)
        xp = _merge_ffn(xp, a_p, b_p, merge_w, TOKEN_TILE, alpha)

        a_s, q_s, qi_s, k_s, v_s, kt_s, vb_s, ki_s, kit_s, wi_s, va_s = _in_projection(
            xs, proj_w + _spatial_weights(w_s[l], b_s[l], t_s), t_s, n_s * t_s, True)
        n_all = past + t_s
        new_len = -(-t_s // LANES) * LANES
        cache_seg = (jnp.transpose(cache_k[l], (0, 2, 3, 1)), jnp.transpose(cache_v[l], (0, 2, 3, 1)),
                     jnp.transpose(cache_kidx[l], (0, 2, 1)), True)
        kt_new = jnp.transpose(kt_s.reshape(B_KV_HEADS, B_HEAD_DIM, n_s, t_s), (2, 0, 1, 3))
        vb_new = jnp.transpose(vb_s.reshape(B_KV_HEADS, n_s, t_s, LANES), (1, 0, 2, 3))
        kit_new = jnp.transpose(kit_s.reshape(IDX_DIM, n_s, t_s), (1, 0, 2))
        new_seg = (_pad_axis(kt_new, 3, new_len), _pad_axis(vb_new, 2, new_len), _pad_axis(kit_new, 2, new_len), False)
        b_s_ = _dsa_attention(
            q_s, qi_s, wi_s, [cache_seg, new_seg],
            n=n_s, t_q=t_s, tq=t_s, rb=t_s, qb=0, n_valid=n_all, q_off=past,
            topk=min(TOPK_MAX, n_all // 4), slopes=slopes)
        xs = _merge_ffn(xs, a_s, b_s_, merge_w, n_s * t_s, alpha)

        outs[0].append(k_p.reshape(n_p, s_p, B_KV_HEADS, B_HEAD_DIM))
        outs[1].append(v_p.reshape(n_p, s_p, B_KV_HEADS, B_HEAD_DIM))
        outs[2].append(ki_p.reshape(n_p, s_p, IDX_DIM))
        outs[3].append(k_s.reshape(n_s, t_s, B_KV_HEADS, B_HEAD_DIM))
        outs[4].append(v_s.reshape(n_s, t_s, B_KV_HEADS, B_HEAD_DIM))
        outs[5].append(ki_s.reshape(n_s, t_s, IDX_DIM))
        outs[6].append(va_s.reshape(n_s, t_s, A_WIDTH))

    return (xp.reshape(n_p, s_p, d), xs.reshape(n_s, t_s, d)) + tuple(jnp.stack(o) for o in outs)
```

```python
import functools

import numpy as np
import jax
import jax.numpy as jnp
from jax import lax
from jax.experimental import pallas as pl
from jax.experimental.pallas import tpu as pltpu

CHUNK = 64
A_WIDTH = 512
A_GROUPS = 4
A_GROUP_CH = A_WIDTH // A_GROUPS
A_SPAN = 128
B_HEADS = 8
B_HEAD_DIM = 64
B_KV_HEADS = 2
B_GROUP = B_HEADS // B_KV_HEADS
B_WIDTH = B_HEADS * B_HEAD_DIM
IDX_HEADS = 8
IDX_DIM = 64
TOPK_MAX = 256
Q_BLOCK = 128
LN_EPS = 1e-5
LOG2_E = 1.4426950408889634

LANES = 128
VMEM_LIMIT = 52 * 1024 * 1024
TOKEN_TILE = 512
KEY_CHUNK = 512
ATT_CHUNK = 256
DSA_ROWS = 4 * Q_BLOCK
BISECT_STEPS = 14
KEY_ACC_ROWS = 32

F32 = jnp.float32
BF16 = jnp.bfloat16
NT_DIMS = (((1,), (1,)), ((), ()))


def _dot(a, b):
    return jnp.dot(a, b, preferred_element_type=F32)


def _dot_nt(a, b):
    return lax.dot_general(a, b, NT_DIMS, preferred_element_type=F32)


def _reduce_keys(x, reduce):
    part = reduce(x.reshape(-1, KEY_ACC_ROWS, x.shape[-1]), axis=0)
    return reduce(part, axis=0, keepdims=True)


def _layer_norm(x, g, b):
    mu = jnp.mean(x, axis=-1, keepdims=True)
    xc = x - mu
    var = jnp.mean(xc * xc, axis=-1, keepdims=True)
    return xc * lax.rsqrt(var + LN_EPS) * g + b


def _inproj_kernel(x_ref, wa_ref, wq_ref, wkv_ref, wva_ref, wqi_ref, wkw_ref, lnv_ref, lnk_ref, ws_ref, bs_ref,
                   a_ref, q_ref, qi_ref, k_ref, v_ref, kt_ref, vb_ref, ki_ref, kit_ref, wi_ref, *va_refs,
                   span):
    tm = x_ref.shape[0]
    xb = x_ref[...].astype(BF16)

    za = _dot(xb, wa_ref[...])
    ga = 0.5 * za * (1.0 + lax.erf(za * np.float32(np.sqrt(0.5))))
    u = ga[:, :A_WIDTH]
    va = _layer_norm(ga[:, A_WIDTH:], lnv_ref[0:1, :], lnv_ref[1:2, :])
    if va_refs:
        va_refs[0][...] = va
    vab = va.astype(BF16)

    zq = _dot(xb, wq_ref[...])
    zqi = _dot(xb, wqi_ref[...])
    for h in range(B_HEADS):
        q_ref[h] = zq[:, h * B_HEAD_DIM:(h + 1) * B_HEAD_DIM].astype(BF16)
    for h in range(IDX_HEADS):
        qi_ref[h] = zqi[:, h * IDX_DIM:(h + 1) * IDX_DIM].astype(BF16)

    zkv = _dot(xb, wkv_ref[...])
    kvw = B_KV_HEADS * B_HEAD_DIM
    if len(kt_ref.shape) == 4:
        kt_ref, vb_ref, kit_ref = kt_ref.at[0], vb_ref.at[0], kit_ref.at[0]
    zva = _dot(xb, wva_ref[...])
    ones_lane = lax.broadcasted_iota(jnp.int32, (tm, LANES), 1) == B_HEAD_DIM
    for g in range(B_KV_HEADS):
        kg = zkv[:, g * B_HEAD_DIM:(g + 1) * B_HEAD_DIM]
        k_ref[pl.ds(g, tm, stride=B_KV_HEADS), :] = kg
        v_ref[pl.ds(g, tm, stride=B_KV_HEADS), :] = zkv[:, kvw + g * B_HEAD_DIM:kvw + (g + 1) * B_HEAD_DIM]
        kt_ref[g] = kg.T.astype(BF16)
        vb_ref[g] = jnp.where(ones_lane, 1.0, zva[:, g * LANES:(g + 1) * LANES]).astype(BF16)

    zkw = _dot(xb, wkw_ref[...])
    ki = _layer_norm(zkw[:, :IDX_DIM], lnk_ref[0:1, :], lnk_ref[1:2, :])
    ki_ref[...] = ki
    kit_ref[...] = ki.T.astype(BF16)
    wi_ref[...] = zkw[:, IDX_DIM:IDX_DIM + IDX_HEADS] * np.float32(IDX_HEADS ** -0.5)

    for s in range(tm // span):
        rows = slice(s * span, (s + 1) * span)
        for g in range(A_GROUPS):
            cols = slice(g * A_GROUP_CH, (g + 1) * A_GROUP_CH)
            mixed = _dot(ws_ref[g], vab[rows, cols]) + bs_ref[:, cols]
            a_ref[rows, cols] = (u[rows, cols] * mixed).astype(BF16)


def _const_spec(shape):
    nd = len(shape)
    return pl.BlockSpec(shape, lambda *_: (0,) * nd)


def _in_projection(x2, wts, span, tm, emit_va, batch_rows=None):
    t, d = x2.shape
    grid = (t // tm,)
    row = lambda i: (i, 0)
    hm = lambda i: (0, i, 0)
    if batch_rows is None:
        kt_shape, kt_block, kt_map = (B_KV_HEADS, B_HEAD_DIM, t), (B_KV_HEADS, B_HEAD_DIM, tm), lambda i: (0, 0, i)
        vb_shape, vb_block, vb_map = (B_KV_HEADS, t, LANES), (B_KV_HEADS, tm, LANES), hm
        ki_shape, ki_block, ki_map = (IDX_DIM, t), (IDX_DIM, tm), lambda i: (0, i)
    else:
        per, n = batch_rows // tm, t // batch_rows
        kt_shape, kt_block = (n, B_KV_HEADS, B_HEAD_DIM, batch_rows), (1, B_KV_HEADS, B_HEAD_DIM, tm)
        kt_map = lambda i: (i // per, 0, 0, i % per)
        vb_shape, vb_block = (n, B_KV_HEADS, batch_rows, LANES), (1, B_KV_HEADS, tm, LANES)
        vb_map = lambda i: (i // per, 0, i % per, 0)
        ki_shape, ki_block, ki_map = (n, IDX_DIM, batch_rows), (1, IDX_DIM, tm), lambda i: (i // per, 0, i % per)
    in_specs = [pl.BlockSpec((tm, d), row)] + [_const_spec(w.shape) for w in wts]
    out_shape = [
        jax.ShapeDtypeStruct((t, A_WIDTH), BF16),
        jax.ShapeDtypeStruct((B_HEADS, t, B_HEAD_DIM), BF16),
        jax.ShapeDtypeStruct((IDX_HEADS, t, IDX_DIM), BF16),
        jax.ShapeDtypeStruct((t * B_KV_HEADS, B_HEAD_DIM), F32),
        jax.ShapeDtypeStruct((t * B_KV_HEADS, B_HEAD_DIM), F32),
        jax.ShapeDtypeStruct(kt_shape, BF16),
        jax.ShapeDtypeStruct(vb_shape, BF16),
        jax.ShapeDtypeStruct((t, IDX_DIM), F32),
        jax.ShapeDtypeStruct(ki_shape, BF16),
        jax.ShapeDtypeStruct((t, IDX_HEADS), F32),
    ]
    out_specs = [
        pl.BlockSpec((tm, A_WIDTH), row),
        pl.BlockSpec((B_HEADS, tm, B_HEAD_DIM), hm),
        pl.BlockSpec((IDX_HEADS, tm, IDX_DIM), hm),
        pl.BlockSpec((tm * B_KV_HEADS, B_HEAD_DIM), row),
        pl.BlockSpec((tm * B_KV_HEADS, B_HEAD_DIM), row),
        pl.BlockSpec(kt_block, kt_map),
        pl.BlockSpec(vb_block, vb_map),
        pl.BlockSpec((tm, IDX_DIM), row),
        pl.BlockSpec(ki_block, ki_map),
        pl.BlockSpec((tm, IDX_HEADS), row),
    ]
    if emit_va:
        out_shape.append(jax.ShapeDtypeStruct((t, A_WIDTH), F32))
        out_specs.append(pl.BlockSpec((tm, A_WIDTH), row))
    return pl.pallas_call(
        functools.partial(_inproj_kernel, span=span),
        grid=grid, in_specs=in_specs, out_specs=out_specs, out_shape=out_shape,
        compiler_params=pltpu.CompilerParams(dimension_semantics=("parallel",), vmem_limit_bytes=VMEM_LIMIT),
        name="in_projection",
    )(x2, *wts)


def _key_pieces(seg_lens, n_keys, chunk):
    pieces, col = [], 0
    for si, seg_len in enumerate(seg_lens):
        off = 0
        while off < seg_len and col < n_keys:
            width = min(chunk, seg_len - off, n_keys - col)
            pieces.append((col, width, si, off))
            off += width
            col += width
    return pieces


def _dsa_block(q_ref, qi_ref, wi_ref, *refs, segments, qb, tq, rb, row_keys, n_valid, q_off, topk, slopes):
    n_seg = len(segments)
    seg_refs = [refs[3 * i:3 * i + 3] for i in range(n_seg)]
    o_ref, s_ref, d_ref, thr_ref, st_ref = refs[3 * n_seg:]
    seg_lens = [length for length, _ in segments]
    neg_inf = np.float32(-np.inf)
    pos_inf = np.float32(np.inf)
    kf = np.float32(topk)
    n_rb = tq // rb
    assert rb <= LANES and rb % 8 == 0
    blocks = [(slice(r * rb, (r + 1) * rb), row_keys[r]) for r in range(n_rb)]

    for r, (rows, n_keys) in enumerate(blocks):
        q_pos = q_off + qb * tq + rows.start + lax.broadcasted_iota(jnp.int32, (rb, 1), 0)
        qi = qi_ref[:, rows, :].reshape(IDX_HEADS * rb, IDX_DIM)
        wi = wi_ref[rows, :]
        for c0, kc, si, off in _key_pieces(seg_lens, n_keys, KEY_CHUNK):
            k_pos = c0 + lax.broadcasted_iota(jnp.int32, (1, kc), 1)
            lg = _dot(qi, seg_refs[si][2][0, :, off:off + kc].astype(BF16))
            sc = wi[:, 0:1] * jnp.maximum(lg[0:rb], 0.0)
            for h in range(1, IDX_HEADS):
                sc = sc + wi[:, h:h + 1] * jnp.maximum(lg[h * rb:(h + 1) * rb], 0.0)
            k_chunk = k_pos // CHUNK
            if c0 + kc > n_valid:
                k_chunk = jnp.where(k_pos < n_valid, k_chunk, np.int32(2 ** 30))
            adm = k_chunk <= (q_pos // CHUNK)
            sc = jnp.where(adm, sc, neg_inf)
            s_ref[rows, c0:c0 + kc] = sc
            if rb < LANES:
                sc = jnp.concatenate([sc, jnp.full((LANES - rb, kc), neg_inf, F32)], axis=0)
            st_ref[r, c0:c0 + kc, :] = sc.T
            d_ref[rows, c0:c0 + kc] = jnp.abs(q_pos - k_pos).astype(F32)

    def scores_t(r):
        return st_ref[r, :row_keys[r], :]

    def count_ge(r, t):
        return _reduce_keys(jnp.where(scores_t(r) >= t, 1.0, 0.0), jnp.sum)

    brackets, states, n_adms = [], [], []
    for r in range(n_rb):
        s = scores_t(r)
        finite = s > neg_inf
        n_adm = _reduce_keys(jnp.where(finite, 1.0, 0.0), jnp.sum)
        row_max = _reduce_keys(s, jnp.max)
        row_min = _reduce_keys(jnp.where(finite, s, pos_inf), jnp.min)
        brackets.append((row_min, row_max, jnp.full((1, LANES), pos_inf, F32)))
        n_adms.append(n_adm)

    def bisect(_, carry):
        out = []
        for r, (lo, hb, hiv) in enumerate(carry):
            mid = 0.5 * lo + 0.5 * hb
            ge = count_ge(r, mid) >= kf
            out.append((jnp.where(ge, mid, lo), jnp.where(ge, hb, mid), jnp.where(ge, hiv, mid)))
        return tuple(out)

    brackets = lax.fori_loop(0, BISECT_STEPS, bisect, tuple(brackets))

    for (_, _, hiv), n_adm in zip(brackets, n_adms):
        done = jnp.where(n_adm <= kf, 1.0, 0.0)
        states.append((jnp.full((1, LANES), np.finfo(np.float32).min, F32), hiv, done))

    def n_open(states):
        return sum(jnp.sum(1.0 - done) for _, _, done in states)

    def scan_cond(carry):
        _, n_left, it = carry
        return jnp.logical_and(n_left > 0.0, it < max(row_keys))

    def scan_body(carry):
        states, _, it = carry
        out = []
        for r, (thr, hiv, done) in enumerate(states):
            sv = scores_t(r)
            cand = _reduce_keys(jnp.where(sv < hiv, sv, neg_inf), jnp.max)
            found = jnp.where(done > 0.5, 0.0, jnp.where(count_ge(r, cand) >= kf, 1.0, 0.0))
            thr = jnp.where(found > 0.5, cand, thr)
            done = jnp.maximum(done, found)
            out.append((thr, jnp.where(done > 0.5, hiv, cand), done))
        return tuple(out), n_open(out), it + 1

    states, _, _ = lax.while_loop(scan_cond, scan_body, (tuple(states), n_open(states), jnp.int32(0)))

    over = []
    for r, ((thr, _, _), n_adm) in enumerate(zip(states, n_adms)):
        thr_ref[blocks[r][0], :] = jnp.broadcast_to(thr, (LANES, LANES)).T[:rb, 0:1]
        over.append(jnp.max(jnp.where(n_adm > kf, count_ge(r, thr), 0.0)))
    tied = functools.reduce(jnp.maximum, over) > kf

    ones_row = jnp.where(lax.broadcasted_iota(jnp.int32, (LANES - B_HEAD_DIM, ATT_CHUNK), 0) == 0, 1.0, 0.0)
    ones_row = ones_row.astype(BF16)

    for rows, n_keys in blocks:
        thr = thr_ref[rows, :]

        @pl.when(jnp.logical_not(tied))
        def _():
            d_ref[rows, :n_keys] = jnp.where(s_ref[rows, :n_keys] >= thr, d_ref[rows, :n_keys], pos_inf)

        @pl.when(tied)
        def _():
            n_gt = jnp.sum(jnp.where(s_ref[rows, :n_keys] > thr, 1.0, 0.0), axis=1, keepdims=True)
            room = kf - n_gt
            tri = (lax.broadcasted_iota(jnp.int32, (LANES, LANES), 0)
                   <= lax.broadcasted_iota(jnp.int32, (LANES, LANES), 1))
            tri = jnp.where(tri, 1.0, 0.0).astype(BF16)
            run = jnp.zeros((rb, 1), F32)
            for c0 in range(0, n_keys, LANES):
                blk = s_ref[rows, c0:c0 + LANES]
                eq = jnp.where(blk == thr, 1.0, 0.0)
                prefix = _dot(eq.astype(BF16), tri) + run
                keep = jnp.where(blk > thr, 1.0, jnp.where(prefix <= room, eq, 0.0))
                d_ref[rows, c0:c0 + LANES] = jnp.where(keep > 0.5, d_ref[rows, c0:c0 + LANES], pos_inf)
                run = run + jnp.sum(eq, axis=1, keepdims=True)

        qs = [q_ref[g * B_GROUP:(g + 1) * B_GROUP, rows, :].reshape(B_GROUP * rb, B_HEAD_DIM)
              for g in range(B_KV_HEADS)]
        m_run = [jnp.full((rb, 1), neg_inf, F32) for _ in range(B_HEADS)]
        acc = [jnp.zeros((rb, LANES), F32) for _ in range(B_HEADS)]
        for c0, kc, si, off in _key_pieces(seg_lens, n_keys, ATT_CHUNK):
            kt_ref, v_ref, _ = seg_refs[si]
            dist = d_ref[rows, c0:c0 + kc]
            for g in range(B_KV_HEADS):
                logits = _dot(qs[g], kt_ref[0, g, :, off:off + kc].astype(BF16))
                ps, alphas = [], []
                for hh in range(B_GROUP):
                    h = g * B_GROUP + hh
                    lgt = logits[hh * rb:(hh + 1) * rb] - np.float32(slopes[h] * LOG2_E) * dist
                    m_new = jnp.maximum(m_run[h], jnp.max(lgt, axis=1, keepdims=True))
                    m_ref = jnp.where(m_new == neg_inf, 0.0, m_new)
                    alphas.append(jnp.exp2(m_run[h] - m_ref))
                    ps.append(jnp.exp2(lgt - m_ref).astype(BF16))
                    m_run[h] = m_new
                p = jnp.concatenate(ps, axis=0)
                if segments[si][1]:
                    vt = jnp.concatenate([v_ref[0, g, :, off:off + kc].astype(BF16), ones_row[:, :kc]], axis=0)
                    pv = _dot_nt(p, vt)
                else:
                    pv = _dot(p, v_ref[0, g, off:off + kc, :])
                for hh in range(B_GROUP):
                    h = g * B_GROUP + hh
                    acc[h] = alphas[hh] * acc[h] + pv[hh * rb:(hh + 1) * rb]
        for h in range(B_HEADS):
            out = acc[h][:, :B_HEAD_DIM] / acc[h][:, B_HEAD_DIM:B_HEAD_DIM + 1]
            o_ref[rows, h * B_HEAD_DIM:(h + 1) * B_HEAD_DIM] = out.astype(BF16)


def _dsa_attention(q_hm, qi_hm, wi, key_segments, *, n, t_q, tq, rb, qb, n_valid, q_off, topk, slopes):
    nqb = t_q // tq
    n_rb = tq // rb
    last_chunk = (q_off + (qb + 1) * tq - 1) // CHUNK
    need = min((last_chunk + 1) * CHUNK, n_valid)
    n_keys = -(-need // LANES) * LANES
    if rb % CHUNK == 0 and need == q_off + (qb + 1) * tq:
        row_keys = tuple(n_keys - (n_rb - 1 - r) * rb for r in range(n_rb))
    else:
        row_keys = (n_keys,) * n_rb
    qmap = lambda b: (0, b * nqb + qb, 0)
    in_specs = [
        pl.BlockSpec((B_HEADS, tq, B_HEAD_DIM), qmap),
        pl.BlockSpec((IDX_HEADS, tq, IDX_DIM), qmap),
        pl.BlockSpec((tq, IDX_HEADS), lambda b: (b * nqb + qb, 0)),
    ]
    operands, segments, col = [], [], 0
    for kt, v, kit, v_transposed in key_segments:
        length = min(kt.shape[-1], n_keys - col)
        if length <= 0:
            break
        v_block = (1, B_KV_HEADS, B_HEAD_DIM, length) if v_transposed else (1, B_KV_HEADS, length, LANES)
        in_specs += [
            pl.BlockSpec((1, B_KV_HEADS, B_HEAD_DIM, length), lambda b: (b, 0, 0, 0)),
            pl.BlockSpec(v_block, lambda b: (b, 0, 0, 0)),
            pl.BlockSpec((1, IDX_DIM, length), lambda b: (b, 0, 0)),
        ]
        operands += [kt, v, kit]
        segments.append((length, v_transposed))
        col += length
    assert col == n_keys, (col, n_keys)
    return pl.pallas_call(
        functools.partial(_dsa_block, segments=tuple(segments), qb=qb, tq=tq, rb=rb, row_keys=row_keys,
                          n_valid=n_valid, q_off=q_off, topk=topk, slopes=slopes),
        grid=(n,), in_specs=in_specs,
        out_specs=pl.BlockSpec((tq, B_WIDTH), lambda b: (b, 0)),
        out_shape=jax.ShapeDtypeStruct((n * tq, B_WIDTH), BF16),
        scratch_shapes=[pltpu.VMEM((tq, n_keys), F32), pltpu.VMEM((tq, n_keys), F32), pltpu.VMEM((tq, 1), F32),
                        pltpu.VMEM((n_rb, n_keys, LANES), F32)],
        compiler_params=pltpu.CompilerParams(dimension_semantics=("parallel",), vmem_limit_bytes=VMEM_LIMIT),
        name="dsa_attention",
    )(q_hm, qi_hm, wi, *operands)


def _merge_kernel(x_ref, a_ref, b_ref, wg_ref, wb_ref, wo_ref, ln1_ref, w1_ref, b1_ref, w2_ref, b2_ref, ln2_ref,
                  y_ref, *, alpha):
    d = x_ref.shape[1]
    x = x_ref[...]
    gates = jax.nn.sigmoid(_dot(x.astype(BF16), wg_ref[...]))
    m = gates[:, :d] * _dot(a_ref[...], wb_ref[0]) + gates[:, d:] * _dot(b_ref[...], wb_ref[1])
    h = _layer_norm(alpha * x + _dot(m.astype(BF16), wo_ref[...]), ln1_ref[0:1, :], ln1_ref[1:2, :])
    f = jnp.square(jnp.maximum(_dot(h.astype(BF16), w1_ref[...]) + b1_ref[...], 0.0))
    f = _dot(f.astype(BF16), w2_ref[...]) + b2_ref[...]
    y_ref[...] = _layer_norm(alpha * h + f, ln2_ref[0:1, :], ln2_ref[1:2, :])


def _resident_spec(shape):
    nd = len(shape)
    return pl.BlockSpec(shape, lambda *_: (0,) * nd, pipeline_mode=pl.Buffered(1))


def _merge_ffn(x2, a, b, wts, tm, alpha):
    t, d = x2.shape
    row = lambda i: (i, 0)
    in_specs = [pl.BlockSpec((tm, d), row), pl.BlockSpec((tm, A_WIDTH), row), pl.BlockSpec((tm, B_WIDTH), row)]
    in_specs += [_resident_spec(w.shape) for w in wts]
    return pl.pallas_call(
        functools.partial(_merge_kernel, alpha=np.float32(alpha)),
        grid=(t // tm,), in_specs=in_specs,
        out_specs=pl.BlockSpec((tm, d), row),
        out_shape=jax.ShapeDtypeStruct((t, d), F32),
        compiler_params=pltpu.CompilerParams(dimension_semantics=("parallel",), vmem_limit_bytes=VMEM_LIMIT),
        name="merge_ffn",
    )(x2, a, b, *wts)


def _spatial_weights(w_s, b_s, span):
    pos = jnp.arange(A_SPAN)
    mask = (pos[None, :] // CHUNK) <= (pos[:, None] // CHUNK)
    ws = jnp.where(mask[None], w_s, 0.0)[:, :span, :span].astype(BF16)
    bs = jnp.repeat(b_s[:, :span].T, A_GROUP_CH, axis=1)
    return ws, bs


def _pad_axis(x, axis, size):
    pads = [(0, 0)] * x.ndim
    pads[axis] = (0, size - x.shape[axis])
    return jnp.pad(x, pads)


def kernel(x_prompt, x_sample, cache_k, cache_v, cache_kidx, w_in, lnv_g, lnv_b, w_s, b_s, lnk_g, lnk_b,
           w_branch, w_out, ln1_g, ln1_b, w_ff1, b_ff1, w_ff2, b_ff2, ln2_g, ln2_b):
    depth = w_in.shape[0]
    n_p, s_p, d = x_prompt.shape
    n_s, t_s, _ = x_sample.shape
    past = cache_k.shape[2]
    alpha = (2 * depth) ** 0.25
    slopes = tuple(float(2.0 ** (-8.0 * h / B_HEADS)) for h in range(1, B_HEADS + 1))
    kvw = B_KV_HEADS * B_HEAD_DIM
    c_a = 2 * A_WIDTH
    c_q = c_a + B_WIDTH
    c_k = c_q + kvw
    c_v = c_k + kvw
    c_qi = c_v + IDX_HEADS * IDX_DIM
    c_wi = c_qi + IDX_DIM + IDX_HEADS

    xp = x_prompt.reshape(n_p * s_p, d)
    xs = x_sample.reshape(n_s * t_s, d)
    outs = [[] for _ in range(7)]
    for l in range(depth):
        w = w_in[l]
        pad = jnp.zeros((d, LANES - (c_wi - c_qi)), F32)
        proj_w = (
            w[:, :c_a].astype(BF16),
            (w[:, c_a:c_q] * (LOG2_E * B_HEAD_DIM ** -0.5)).astype(BF16),
            w[:, c_q:c_v].astype(BF16),
            jnp.pad(w[:, c_k:c_v].reshape(d, B_KV_HEADS, B_HEAD_DIM),
                    ((0, 0), (0, 0), (0, LANES - B_HEAD_DIM))).reshape(d, B_KV_HEADS * LANES).astype(BF16),
            (w[:, c_v:c_qi] * (IDX_DIM ** -0.5)).astype(BF16),
            jnp.concatenate([w[:, c_qi:c_wi], pad], axis=1).astype(BF16),
            jnp.stack([lnv_g[l], lnv_b[l]]),
            jnp.stack([lnk_g[l], lnk_b[l]]),
        )
        merge_w = (
            w[:, c_wi:].astype(BF16),
            w_branch[l].astype(BF16),
            w_out[l].astype(BF16),
            jnp.stack([ln1_g[l], ln1_b[l]]),
            w_ff1[l].astype(BF16),
            b_ff1[l][None, :],
            w_ff2[l].astype(BF16),
            b_ff2[l][None, :],
            jnp.stack([ln2_g[l], ln2_b[l]]),
        )

        a_p, q_p, qi_p, k_p, v_p, kt_p, vb_p, ki_p, kit_p, wi_p = _in_projection(
            xp, proj_w + _spatial_weights(w_s[l], b_s[l], A_SPAN), A_SPAN, TOKEN_TILE, False, batch_rows=s_p)
        b_p = [_dsa_attention(
            q_p, qi_p, wi_p, [(kt_p, vb_p, kit_p, False)],
            n=n_p, t_q=s_p, tq=DSA_ROWS, rb=Q_BLOCK, qb=j, n_valid=s_p, q_off=0,
            topk=min(TOPK_MAX, s_p // 4), slopes=slopes).reshape(n_p, DSA_ROWS, B_WIDTH)
            for j in range(s_p // DSA_ROWS)]
        b_p = jnp.stack(b_p, axis=1).reshape(n_p * s_p, B_WIDTH)
        xp = _merge_ffn(xp, a_p, b_p, merge_w, TOKEN_TILE, alpha)

        a_s, q_s, qi_s, k_s, v_s, kt_s, vb_s, ki_s, kit_s, wi_s, va_s = _in_projection(
            xs, proj_w + _spatial_weights(w_s[l], b_s[l], t_s), t_s, n_s * t_s, True)
        n_all = past + t_s
        new_len = -(-t_s // LANES) * LANES
        cache_seg = (jnp.transpose(cache_k[l], (0, 2, 3, 1)), jnp.transpose(cache_v[l], (0, 2, 3, 1)),
                     jnp.transpose(cache_kidx[l], (0, 2, 1)), True)
        kt_new = jnp.transpose(kt_s.reshape(B_KV_HEADS, B_HEAD_DIM, n_s, t_s), (2, 0, 1, 3))
        vb_new = jnp.transpose(vb_s.reshape(B_KV_HEADS, n_s, t_s, LANES), (1, 0, 2, 3))
        kit_new = jnp.transpose(kit_s.reshape(IDX_DIM, n_s, t_s), (1, 0, 2))
        new_seg = (_pad_axis(kt_new, 3, new_len), _pad_axis(vb_new, 2, new_len), _pad_axis(kit_new, 2, new_len), False)
        b_s_ = _dsa_attention(
            q_s, qi_s, wi_s, [cache_seg, new_seg],
            n=n_s, t_q=t_s, tq=t_s, rb=t_s, qb=0, n_valid=n_all, q_off=past,
            topk=min(TOPK_MAX, n_all // 4), slopes=slopes)
        xs = _merge_ffn(xs, a_s, b_s_, merge_w, n_s * t_s, alpha)

        outs[0].append(k_p.reshape(n_p, s_p, B_KV_HEADS, B_HEAD_DIM))
        outs[1].append(v_p.reshape(n_p, s_p, B_KV_HEADS, B_HEAD_DIM))
        outs[2].append(ki_p.reshape(n_p, s_p, IDX_DIM))
        outs[3].append(k_s.reshape(n_s, t_s, B_KV_HEADS, B_HEAD_DIM))
        outs[4].append(v_s.reshape(n_s, t_s, B_KV_HEADS, B_HEAD_DIM))
        outs[5].append(ki_s.reshape(n_s, t_s, IDX_DIM))
        outs[6].append(va_s.reshape(n_s, t_s, A_WIDTH))

    return (xp.reshape(n_p, s_p, d), xs.reshape(n_s, t_s, d)) + tuple(jnp.stack(o) for o in outs)
```

```python
import functools

import numpy as np
import jax
import jax.numpy as jnp
from jax import lax
from jax.experimental import pallas as pl
from jax.experimental.pallas import tpu as pltpu

CHUNK = 64
A_WIDTH = 512
A_GROUPS = 4
A_GROUP_CH = A_WIDTH // A_GROUPS
A_SPAN = 128
B_HEADS = 8
B_HEAD_DIM = 64
B_KV_HEADS = 2
B_GROUP = B_HEADS // B_KV_HEADS
B_WIDTH = B_HEADS * B_HEAD_DIM
IDX_HEADS = 8
IDX_DIM = 64
TOPK_MAX = 256
Q_BLOCK = 128
LN_EPS = 1e-5
LOG2_E = 1.4426950408889634

LANES = 128
VMEM_LIMIT = 52 * 1024 * 1024
TOKEN_TILE = 512
KEY_CHUNK = 512
ATT_CHUNK = 256
DSA_ROWS = 4 * Q_BLOCK
BISECT16_STEPS = 9
BISECT_STEPS = 6
KEY_ACC_ROWS = 32
KEY_ACC16_ROWS = 64

F32 = jnp.float32
BF16 = jnp.bfloat16
NT_DIMS = (((1,), (1,)), ((), ()))


def _dot(a, b):
    return jnp.dot(a, b, preferred_element_type=F32)


def _dot_nt(a, b):
    return lax.dot_general(a, b, NT_DIMS, preferred_element_type=F32)


def _reduce_keys(x, reduce):
    part = reduce(x.reshape(-1, KEY_ACC_ROWS, x.shape[-1]), axis=0)
    return reduce(part, axis=0, keepdims=True)


def _layer_norm(x, g, b):
    mu = jnp.mean(x, axis=-1, keepdims=True)
    xc = x - mu
    var = jnp.mean(xc * xc, axis=-1, keepdims=True)
    return xc * lax.rsqrt(var + LN_EPS) * g + b


def _inproj_kernel(x_ref, wa_ref, wq_ref, wkv_ref, wva_ref, wqi_ref, wkw_ref, lnv_ref, lnk_ref, ws_ref, bs_ref,
                   a_ref, q_ref, qi_ref, k_ref, v_ref, kt_ref, vb_ref, ki_ref, kit_ref, wi_ref, *va_refs,
                   span):
    tm = x_ref.shape[0]
    xb = x_ref[...].astype(BF16)

    za = _dot(xb, wa_ref[...])
    ga = 0.5 * za * (1.0 + lax.erf(za * np.float32(np.sqrt(0.5))))
    u = ga[:, :A_WIDTH]
    va = _layer_norm(ga[:, A_WIDTH:], lnv_ref[0:1, :], lnv_ref[1:2, :])
    if va_refs:
        va_refs[0][...] = va
    vab = va.astype(BF16)

    zq = _dot(xb, wq_ref[...])
    zqi = _dot(xb, wqi_ref[...])
    for h in range(B_HEADS):
        q_ref[h] = zq[:, h * B_HEAD_DIM:(h + 1) * B_HEAD_DIM].astype(BF16)
    for h in range(IDX_HEADS):
        qi_ref[h] = zqi[:, h * IDX_DIM:(h + 1) * IDX_DIM].astype(BF16)

    zkv = _dot(xb, wkv_ref[...])
    kvw = B_KV_HEADS * B_HEAD_DIM
    if len(kt_ref.shape) == 4:
        kt_ref, vb_ref, kit_ref = kt_ref.at[0], vb_ref.at[0], kit_ref.at[0]
    zva = _dot(xb, wva_ref[...])
    ones_lane = lax.broadcasted_iota(jnp.int32, (tm, LANES), 1) == B_HEAD_DIM
    for g in range(B_KV_HEADS):
        kg = zkv[:, g * B_HEAD_DIM:(g + 1) * B_HEAD_DIM]
        k_ref[pl.ds(g, tm, stride=B_KV_HEADS), :] = kg
        v_ref[pl.ds(g, tm, stride=B_KV_HEADS), :] = zkv[:, kvw + g * B_HEAD_DIM:kvw + (g + 1) * B_HEAD_DIM]
        kt_ref[g] = kg.T.astype(BF16)
        vb_ref[g] = jnp.where(ones_lane, 1.0, zva[:, g * LANES:(g + 1) * LANES]).astype(BF16)

    zkw = _dot(xb, wkw_ref[...])
    ki = _layer_norm(zkw[:, :IDX_DIM], lnk_ref[0:1, :], lnk_ref[1:2, :])
    ki_ref[...] = ki
    kit_ref[...] = ki.T.astype(BF16)
    wi_ref[...] = zkw[:, IDX_DIM:IDX_DIM + IDX_HEADS] * np.float32(IDX_HEADS ** -0.5)

    for s in range(tm // span):
        rows = slice(s * span, (s + 1) * span)
        for g in range(A_GROUPS):
            cols = slice(g * A_GROUP_CH, (g + 1) * A_GROUP_CH)
            mixed = _dot(ws_ref[g], vab[rows, cols]) + bs_ref[:, cols]
            a_ref[rows, cols] = (u[rows, cols] * mixed).astype(BF16)


def _const_spec(shape):
    nd = len(shape)
    return pl.BlockSpec(shape, lambda *_: (0,) * nd)


def _in_projection(x2, wts, span, tm, emit_va, batch_rows=None):
    t, d = x2.shape
    grid = (t // tm,)
    row = lambda i: (i, 0)
    hm = lambda i: (0, i, 0)
    if batch_rows is None:
        kt_shape, kt_block, kt_map = (B_KV_HEADS, B_HEAD_DIM, t), (B_KV_HEADS, B_HEAD_DIM, tm), lambda i: (0, 0, i)
        vb_shape, vb_block, vb_map = (B_KV_HEADS, t, LANES), (B_KV_HEADS, tm, LANES), hm
        ki_shape, ki_block, ki_map = (IDX_DIM, t), (IDX_DIM, tm), lambda i: (0, i)
    else:
        per, n = batch_rows // tm, t // batch_rows
        kt_shape, kt_block = (n, B_KV_HEADS, B_HEAD_DIM, batch_rows), (1, B_KV_HEADS, B_HEAD_DIM, tm)
        kt_map = lambda i: (i // per, 0, 0, i % per)
        vb_shape, vb_block = (n, B_KV_HEADS, batch_rows, LANES), (1, B_KV_HEADS, tm, LANES)
        vb_map = lambda i: (i // per, 0, i % per, 0)
        ki_shape, ki_block, ki_map = (n, IDX_DIM, batch_rows), (1, IDX_DIM, tm), lambda i: (i // per, 0, i % per)
    in_specs = [pl.BlockSpec((tm, d), row)] + [_const_spec(w.shape) for w in wts]
    out_shape = [
        jax.ShapeDtypeStruct((t, A_WIDTH), BF16),
        jax.ShapeDtypeStruct((B_HEADS, t, B_HEAD_DIM), BF16),
        jax.ShapeDtypeStruct((IDX_HEADS, t, IDX_DIM), BF16),
        jax.ShapeDtypeStruct((t * B_KV_HEADS, B_HEAD_DIM), F32),
        jax.ShapeDtypeStruct((t * B_KV_HEADS, B_HEAD_DIM), F32),
        jax.ShapeDtypeStruct(kt_shape, BF16),
        jax.ShapeDtypeStruct(vb_shape, BF16),
        jax.ShapeDtypeStruct((t, IDX_DIM), F32),
        jax.ShapeDtypeStruct(ki_shape, BF16),
        jax.ShapeDtypeStruct((t, IDX_HEADS), F32),
    ]
    out_specs = [
        pl.BlockSpec((tm, A_WIDTH), row),
        pl.BlockSpec((B_HEADS, tm, B_HEAD_DIM), hm),
        pl.BlockSpec((IDX_HEADS, tm, IDX_DIM), hm),
        pl.BlockSpec((tm * B_KV_HEADS, B_HEAD_DIM), row),
        pl.BlockSpec((tm * B_KV_HEADS, B_HEAD_DIM), row),
        pl.BlockSpec(kt_block, kt_map),
        pl.BlockSpec(vb_block, vb_map),
        pl.BlockSpec((tm, IDX_DIM), row),
        pl.BlockSpec(ki_block, ki_map),
        pl.BlockSpec((tm, IDX_HEADS), row),
    ]
    if emit_va:
        out_shape.append(jax.ShapeDtypeStruct((t, A_WIDTH), F32))
        out_specs.append(pl.BlockSpec((tm, A_WIDTH), row))
    return pl.pallas_call(
        functools.partial(_inproj_kernel, span=span),
        grid=grid, in_specs=in_specs, out_specs=out_specs, out_shape=out_shape,
        compiler_params=pltpu.CompilerParams(dimension_semantics=("parallel",), vmem_limit_bytes=VMEM_LIMIT),
        name="in_projection",
    )(x2, *wts)


def _key_pieces(seg_lens, n_keys, chunk):
    pieces, col = [], 0
    for si, seg_len in enumerate(seg_lens):
        off = 0
        while off < seg_len and col < n_keys:
            width = min(chunk, seg_len - off, n_keys - col)
            pieces.append((col, width, si, off))
            off += width
            col += width
    return pieces


def _dsa_block(q_ref, qi_ref, wi_ref, *refs, segments, qb, tq, rb, row_keys, n_valid, q_off, topk, slopes):
    n_seg = len(segments)
    seg_refs = [refs[3 * i:3 * i + 3] for i in range(n_seg)]
    o_ref, s_ref, d_ref, thr_ref, st_ref, st16_ref = refs[3 * n_seg:]
    seg_lens = [length for length, _ in segments]
    neg_inf = np.float32(-np.inf)
    pos_inf = np.float32(np.inf)
    kf = np.float32(topk)
    n_rb = tq // rb
    assert rb <= LANES and rb % 8 == 0
    blocks = [(slice(r * rb, (r + 1) * rb), row_keys[r]) for r in range(n_rb)]

    for r, (rows, n_keys) in enumerate(blocks):
        q_pos = q_off + qb * tq + rows.start + lax.broadcasted_iota(jnp.int32, (rb, 1), 0)
        qi = qi_ref[:, rows, :].reshape(IDX_HEADS * rb, IDX_DIM)
        wi = wi_ref[rows, :]
        for c0, kc, si, off in _key_pieces(seg_lens, n_keys, KEY_CHUNK):
            k_pos = c0 + lax.broadcasted_iota(jnp.int32, (1, kc), 1)
            lg = _dot(qi, seg_refs[si][2][0, :, off:off + kc].astype(BF16))
            sc = wi[:, 0:1] * jnp.maximum(lg[0:rb], 0.0)
            for h in range(1, IDX_HEADS):
                sc = sc + wi[:, h:h + 1] * jnp.maximum(lg[h * rb:(h + 1) * rb], 0.0)
            k_chunk = k_pos // CHUNK
            if c0 + kc > n_valid:
                k_chunk = jnp.where(k_pos < n_valid, k_chunk, np.int32(2 ** 30))
            adm = k_chunk <= (q_pos // CHUNK)
            sc = jnp.where(adm, sc, neg_inf)
            s_ref[rows, c0:c0 + kc] = sc
            if rb < LANES:
                sc = jnp.concatenate([sc, jnp.full((LANES - rb, kc), neg_inf, F32)], axis=0)
            st_ref[r, c0:c0 + kc, :] = sc.T
            st16_ref[r, c0:c0 + kc, :] = sc.T.astype(BF16)
            d_ref[rows, c0:c0 + kc] = jnp.abs(q_pos - k_pos).astype(F32)

    def scores_t(r):
        return st_ref[r, :row_keys[r], :]

    def count_ge(r, t):
        return _reduce_keys(jnp.where(scores_t(r) >= t, 1.0, 0.0), jnp.sum)

    brackets, states, n_adms = [], [], []
    for r in range(n_rb):
        s = scores_t(r)
        finite = s > neg_inf
        n_adm = _reduce_keys(jnp.where(finite, 1.0, 0.0), jnp.sum)
        row_max = _reduce_keys(s, jnp.max)
        row_min = _reduce_keys(jnp.where(finite, s, pos_inf), jnp.min)
        brackets.append((row_min, row_max, jnp.full((1, LANES), pos_inf, F32)))
        n_adms.append(n_adm)

    def count16_ge(r, t):
        tb = jnp.broadcast_to(t, (KEY_ACC16_ROWS, LANES)).astype(BF16)
        one, zero = jnp.ones_like(tb), jnp.zeros_like(tb)
        part = zero
        for c0 in range(0, row_keys[r], KEY_ACC16_ROWS):
            part = part + jnp.where(st16_ref[r, c0:c0 + KEY_ACC16_ROWS, :] >= tb, one, zero)
        return jnp.sum(part.astype(F32), axis=0, keepdims=True)

    def bisect16(_, carry):
        out = []
        for r, (lo, hb, hiv) in enumerate(carry):
            t = (0.5 * lo + 0.5 * hb).astype(BF16).astype(F32)
            ge = count16_ge(r, t) >= kf
            below = t - (jnp.abs(t) * np.float32(2.0 ** -7) + np.float32(1e-30))
            out.append((jnp.where(ge, jnp.maximum(lo, below), lo), jnp.where(ge, hb, jnp.minimum(hb, t)),
                        jnp.where(ge, hiv, t)))
        return tuple(out)

    def bisect(_, carry):
        out = []
        for r, (lo, hb, hiv) in enumerate(carry):
            mid = 0.5 * lo + 0.5 * hb
            ge = count_ge(r, mid) >= kf
            out.append((jnp.where(ge, mid, lo), jnp.where(ge, hb, mid), jnp.where(ge, hiv, mid)))
        return tuple(out)

    brackets = lax.fori_loop(0, BISECT16_STEPS, bisect16, tuple(brackets))
    brackets = lax.fori_loop(0, BISECT_STEPS, bisect, brackets)

    for (_, _, hiv), n_adm in zip(brackets, n_adms):
        done = jnp.where(n_adm <= kf, 1.0, 0.0)
        states.append((jnp.full((1, LANES), np.finfo(np.float32).min, F32), hiv, done))

    def n_open(states):
        return sum(jnp.sum(1.0 - done) for _, _, done in states)

    def scan_cond(carry):
        _, n_left, it = carry
        return jnp.logical_and(n_left > 0.0, it < max(row_keys))

    def scan_body(carry):
        states, _, it = carry
        out = []
        for r, (thr, hiv, done) in enumerate(states):
            sv = scores_t(r)
            cand = _reduce_keys(jnp.where(sv < hiv, sv, neg_inf), jnp.max)
            found = jnp.where(done > 0.5, 0.0, jnp.where(count_ge(r, cand) >= kf, 1.0, 0.0))
            thr = jnp.where(found > 0.5, cand, thr)
            done = jnp.maximum(done, found)
            out.append((thr, jnp.where(done > 0.5, hiv, cand), done))
        return tuple(out), n_open(out), it + 1

    states, _, _ = lax.while_loop(scan_cond, scan_body, (tuple(states), n_open(states), jnp.int32(0)))

    over = []
    for r, ((thr, _, _), n_adm) in enumerate(zip(states, n_adms)):
        thr_ref[blocks[r][0], :] = jnp.broadcast_to(thr, (LANES, LANES)).T[:rb, 0:1]
        over.append(jnp.max(jnp.where(n_adm > kf, count_ge(r, thr), 0.0)))
    tied = functools.reduce(jnp.maximum, over) > kf

    ones_row = jnp.where(lax.broadcasted_iota(jnp.int32, (LANES - B_HEAD_DIM, ATT_CHUNK), 0) == 0, 1.0, 0.0)
    ones_row = ones_row.astype(BF16)

    for rows, n_keys in blocks:
        thr = thr_ref[rows, :]

        @pl.when(jnp.logical_not(tied))
        def _():
            d_ref[rows, :n_keys] = jnp.where(s_ref[rows, :n_keys] >= thr, d_ref[rows, :n_keys], pos_inf)

        @pl.when(tied)
        def _():
            n_gt = jnp.sum(jnp.where(s_ref[rows, :n_keys] > thr, 1.0, 0.0), axis=1, keepdims=True)
            room = kf - n_gt
            tri = (lax.broadcasted_iota(jnp.int32, (LANES, LANES), 0)
                   <= lax.broadcasted_iota(jnp.int32, (LANES, LANES), 1))
            tri = jnp.where(tri, 1.0, 0.0).astype(BF16)
            run = jnp.zeros((rb, 1), F32)
            for c0 in range(0, n_keys, LANES):
                blk = s_ref[rows, c0:c0 + LANES]
                eq = jnp.where(blk == thr, 1.0, 0.0)
                prefix = _dot(eq.astype(BF16), tri) + run
                keep = jnp.where(blk > thr, 1.0, jnp.where(prefix <= room, eq, 0.0))
                d_ref[rows, c0:c0 + LANES] = jnp.where(keep > 0.5, d_ref[rows, c0:c0 + LANES], pos_inf)
                run = run + jnp.sum(eq, axis=1, keepdims=True)

        qs = [q_ref[g * B_GROUP:(g + 1) * B_GROUP, rows, :].reshape(B_GROUP * rb, B_HEAD_DIM)
              for g in range(B_KV_HEADS)]
        m_run = [jnp.full((rb, 1), neg_inf, F32) for _ in range(B_HEADS)]
        acc = [jnp.zeros((rb, LANES), F32) for _ in range(B_HEADS)]
        for c0, kc, si, off in _key_pieces(seg_lens, n_keys, ATT_CHUNK):
            kt_ref, v_ref, _ = seg_refs[si]
            dist = d_ref[rows, c0:c0 + kc]
            for g in range(B_KV_HEADS):
                logits = _dot(qs[g], kt_ref[0, g, :, off:off + kc].astype(BF16))
                ps, alphas = [], []
                for hh in range(B_GROUP):
                    h = g * B_GROUP + hh
                    lgt = logits[hh * rb:(hh + 1) * rb] - np.float32(slopes[h] * LOG2_E) * dist
                    m_new = jnp.maximum(m_run[h], jnp.max(lgt, axis=1, keepdims=True))
                    m_ref = jnp.where(m_new == neg_inf, 0.0, m_new)
                    alphas.append(jnp.exp2(m_run[h] - m_ref))
                    ps.append(jnp.exp2(lgt - m_ref).astype(BF16))
                    m_run[h] = m_new
                p = jnp.concatenate(ps, axis=0)
                if segments[si][1]:
                    vt = jnp.concatenate([v_ref[0, g, :, off:off + kc].astype(BF16), ones_row[:, :kc]], axis=0)
                    pv = _dot_nt(p, vt)
                else:
                    pv = _dot(p, v_ref[0, g, off:off + kc, :])
                for hh in range(B_GROUP):
                    h = g * B_GROUP + hh
                    acc[h] = alphas[hh] * acc[h] + pv[hh * rb:(hh + 1) * rb]
        for h in range(B_HEADS):
            out = acc[h][:, :B_HEAD_DIM] / acc[h][:, B_HEAD_DIM:B_HEAD_DIM + 1]
            o_ref[rows, h * B_HEAD_DIM:(h + 1) * B_HEAD_DIM] = out.astype(BF16)


def _dsa_attention(q_hm, qi_hm, wi, key_segments, *, n, t_q, tq, rb, qb, n_valid, q_off, topk, slopes):
    nqb = t_q // tq
    n_rb = tq // rb
    last_chunk = (q_off + (qb + 1) * tq - 1) // CHUNK
    need = min((last_chunk + 1) * CHUNK, n_valid)
    n_keys = -(-need // LANES) * LANES
    if rb % CHUNK == 0 and need == q_off + (qb + 1) * tq:
        row_keys = tuple(n_keys - (n_rb - 1 - r) * rb for r in range(n_rb))
    else:
        row_keys = (n_keys,) * n_rb
    qmap = lambda b: (0, b * nqb + qb, 0)
    in_specs = [
        pl.BlockSpec((B_HEADS, tq, B_HEAD_DIM), qmap),
        pl.BlockSpec((IDX_HEADS, tq, IDX_DIM), qmap),
        pl.BlockSpec((tq, IDX_HEADS), lambda b: (b * nqb + qb, 0)),
    ]
    operands, segments, col = [], [], 0
    for kt, v, kit, v_transposed in key_segments:
        length = min(kt.shape[-1], n_keys - col)
        if length <= 0:
            break
        v_block = (1, B_KV_HEADS, B_HEAD_DIM, length) if v_transposed else (1, B_KV_HEADS, length, LANES)
        in_specs += [
            pl.BlockSpec((1, B_KV_HEADS, B_HEAD_DIM, length), lambda b: (b, 0, 0, 0)),
            pl.BlockSpec(v_block, lambda b: (b, 0, 0, 0)),
            pl.BlockSpec((1, IDX_DIM, length), lambda b: (b, 0, 0)),
        ]
        operands += [kt, v, kit]
        segments.append((length, v_transposed))
        col += length
    assert col == n_keys, (col, n_keys)
    return pl.pallas_call(
        functools.partial(_dsa_block, segments=tuple(segments), qb=qb, tq=tq, rb=rb, row_keys=row_keys,
                          n_valid=n_valid, q_off=q_off, topk=topk, slopes=slopes),
        grid=(n,), in_specs=in_specs,
        out_specs=pl.BlockSpec((tq, B_WIDTH), lambda b: (b, 0)),
        out_shape=jax.ShapeDtypeStruct((n * tq, B_WIDTH), BF16),
        scratch_shapes=[pltpu.VMEM((tq, n_keys), F32), pltpu.VMEM((tq, n_keys), F32), pltpu.VMEM((tq, 1), F32),
                        pltpu.VMEM((n_rb, n_keys, LANES), F32), pltpu.VMEM((n_rb, n_keys, LANES), BF16)],
        compiler_params=pltpu.CompilerParams(dimension_semantics=("parallel",), vmem_limit_bytes=VMEM_LIMIT),
        name="dsa_attention",
    )(q_hm, qi_hm, wi, *operands)


def _merge_kernel(x_ref, a_ref, b_ref, wg_ref, wb_ref, wo_ref, ln1_ref, w1_ref, b1_ref, w2_ref, b2_ref, ln2_ref,
                  y_ref, *, alpha):
    d = x_ref.shape[1]
    x = x_ref[...]
    gates = jax.nn.sigmoid(_dot(x.astype(BF16), wg_ref[...]))
    m = gates[:, :d] * _dot(a_ref[...], wb_ref[0]) + gates[:, d:] * _dot(b_ref[...], wb_ref[1])
    h = _layer_norm(alpha * x + _dot(m.astype(BF16), wo_ref[...]), ln1_ref[0:1, :], ln1_ref[1:2, :])
    f = jnp.square(jnp.maximum(_dot(h.astype(BF16), w1_ref[...]) + b1_ref[...], 0.0))
    f = _dot(f.astype(BF16), w2_ref[...]) + b2_ref[...]
    y_ref[...] = _layer_norm(alpha * h + f, ln2_ref[0:1, :], ln2_ref[1:2, :])


def _resident_spec(shape):
    nd = len(shape)
    return pl.BlockSpec(shape, lambda *_: (0,) * nd, pipeline_mode=pl.Buffered(1))


def _merge_ffn(x2, a, b, wts, tm, alpha):
    t, d = x2.shape
    row = lambda i: (i, 0)
    in_specs = [pl.BlockSpec((tm, d), row), pl.BlockSpec((tm, A_WIDTH), row), pl.BlockSpec((tm, B_WIDTH), row)]
    in_specs += [_resident_spec(w.shape) for w in wts]
    return pl.pallas_call(
        functools.partial(_merge_kernel, alpha=np.float32(alpha)),
        grid=(t // tm,), in_specs=in_specs,
        out_specs=pl.BlockSpec((tm, d), row),
        out_shape=jax.ShapeDtypeStruct((t, d), F32),
        compiler_params=pltpu.CompilerParams(dimension_semantics=("parallel",), vmem_limit_bytes=VMEM_LIMIT),
        name="merge_ffn",
    )(x2, a, b, *wts)


def _spatial_weights(w_s, b_s, span):
    pos = jnp.arange(A_SPAN)
    mask = (pos[None, :] // CHUNK) <= (pos[:, None] // CHUNK)
    ws = jnp.where(mask[None], w_s, 0.0)[:, :span, :span].astype(BF16)
    bs = jnp.repeat(b_s[:, :span].T, A_GROUP_CH, axis=1)
    return ws, bs


def _pad_axis(x, axis, size):
    pads = [(0, 0)] * x.ndim
    pads[axis] = (0, size - x.shape[axis])
    return jnp.pad(x, pads)


def kernel(x_prompt, x_sample, cache_k, cache_v, cache_kidx, w_in, lnv_g, lnv_b, w_s, b_s, lnk_g, lnk_b,
           w_branch, w_out, ln1_g, ln1_b, w_ff1, b_ff1, w_ff2, b_ff2, ln2_g, ln2_b):
    depth = w_in.shape[0]
    n_p, s_p, d = x_prompt.shape
    n_s, t_s, _ = x_sample.shape
    past = cache_k.shape[2]
    alpha = (2 * depth) ** 0.25
    slopes = tuple(float(2.0 ** (-8.0 * h / B_HEADS)) for h in range(1, B_HEADS + 1))
    kvw = B_KV_HEADS * B_HEAD_DIM
    c_a = 2 * A_WIDTH
    c_q = c_a + B_WIDTH
    c_k = c_q + kvw
    c_v = c_k + kvw
    c_qi = c_v + IDX_HEADS * IDX_DIM
    c_wi = c_qi + IDX_DIM + IDX_HEADS

    xp = x_prompt.reshape(n_p * s_p, d)
    xs = x_sample.reshape(n_s * t_s, d)
    outs = [[] for _ in range(7)]
    for l in range(depth):
        w = w_in[l]
        pad = jnp.zeros((d, LANES - (c_wi - c_qi)), F32)
        proj_w = (
            w[:, :c_a].astype(BF16),
            (w[:, c_a:c_q] * (LOG2_E * B_HEAD_DIM ** -0.5)).astype(BF16),
            w[:, c_q:c_v].astype(BF16),
            jnp.pad(w[:, c_k:c_v].reshape(d, B_KV_HEADS, B_HEAD_DIM),
                    ((0, 0), (0, 0), (0, LANES - B_HEAD_DIM))).reshape(d, B_KV_HEADS * LANES).astype(BF16),
            (w[:, c_v:c_qi] * (IDX_DIM ** -0.5)).astype(BF16),
            jnp.concatenate([w[:, c_qi:c_wi], pad], axis=1).astype(BF16),
            jnp.stack([lnv_g[l], lnv_b[l]]),
            jnp.stack([lnk_g[l], lnk_b[l]]),
        )
        merge_w = (
            w[:, c_wi:].astype(BF16),
            w_branch[l].astype(BF16),
            w_out[l].astype(BF16),
            jnp.stack([ln1_g[l], ln1_b[l]]),
            w_ff1[l].astype(BF16),
            b_ff1[l][None, :],
            w_ff2[l].astype(BF16),
            b_ff2[l][None, :],
            jnp.stack([ln2_g[l], ln2_b[l]]),
        )

        a_p, q_p, qi_p, k_p, v_p, kt_p, vb_p, ki_p, kit_p, wi_p = _in_projection(
            xp, proj_w + _spatial_weights(w_s[l], b_s[l], A_SPAN), A_SPAN, TOKEN_TILE, False, batch_rows=s_p)
        b_p = [_dsa_attention(
            q_p, qi_p, wi_p, [(kt_p, vb_p, kit_p, False)],
            n=n_p, t_q=s_p, tq=DSA_ROWS, rb=Q_BLOCK, qb=j, n_valid=s_p, q_off=0,
            topk=min(TOPK_MAX, s_p // 4), slopes=slopes).reshape(n_p, DSA_ROWS, B_WIDTH)
            for j in range(s_p // DSA_ROWS)]
        b_p = jnp.stack(b_p, axis=1).reshape(n_p * s_p, B_WIDTH)
        xp = _merge_ffn(xp, a_p, b_p, merge_w, TOKEN_TILE, alpha)

        a_s, q_s, qi_s, k_s, v_s, kt_s, vb_s, ki_s, kit_s, wi_s, va_s = _in_projection(
            xs, proj_w + _spatial_weights(w_s[l], b_s[l], t_s), t_s, n_s * t_s, True)
        n_all = past + t_s
        new_len = -(-t_s // LANES) * LANES
        cache_seg = (jnp.transpose(cache_k[l], (0, 2, 3, 1)), jnp.transpose(cache_v[l], (0, 2, 3, 1)),
                     jnp.transpose(cache_kidx[l], (0, 2, 1)), True)
        kt_new = jnp.transpose(kt_s.reshape(B_KV_HEADS, B_HEAD_DIM, n_s, t_s), (2, 0, 1, 3))
        vb_new = jnp.transpose(vb_s.reshape(B_KV_HEADS, n_s, t_s, LANES), (1, 0, 2, 3))
        kit_new = jnp.transpose(kit_s.reshape(IDX_DIM, n_s, t_s), (1, 0, 2))
        new_seg = (_pad_axis(kt_new, 3, new_len), _pad_axis(vb_new, 2, new_len), _pad_axis(kit_new, 2, new_len), False)
        b_s_ = _dsa_attention(
            q_s, qi_s, wi_s, [cache_seg, new_seg],
            n=n_s, t_q=t_s, tq=t_s, rb=t_s, qb=0, n_valid=n_all, q_off=past,
            topk=min(TOPK_MAX, n_all // 4), slopes=slopes)
        xs = _merge_ffn(xs, a_s, b_s_, merge_w, n_s * t_s, alpha)

        outs[0].append(k_p.reshape(n_p, s_p, B_KV_HEADS, B_HEAD_DIM))
        outs[1].append(v_p.reshape(n_p, s_p, B_KV_HEADS, B_HEAD_DIM))
        outs[2].append(ki_p.reshape(n_p, s_p, IDX_DIM))
        outs[3].append(k_s.reshape(n_s, t_s, B_KV_HEADS, B_HEAD_DIM))
        outs[4].append(v_s.reshape(n_s, t_s, B_KV_HEADS, B_HEAD_DIM))
        outs[5].append(ki_s.reshape(n_s, t_s, IDX_DIM))
        outs[6].append(va_s.reshape(n_s, t_s, A_WIDTH))

    return (xp.reshape(n_p, s_p, d), xs.reshape(n_s, t_s, d)) + tuple(jnp.stack(o) for o in outs)
```

```python
import functools

import numpy as np
import jax
import jax.numpy as jnp
from jax import lax
from jax.experimental import pallas as pl
from jax.experimental.pallas import tpu as pltpu

CHUNK = 64
A_WIDTH = 512
A_GROUPS = 4
A_GROUP_CH = A_WIDTH // A_GROUPS
A_SPAN = 128
B_HEADS = 8
B_HEAD_DIM = 64
B_KV_HEADS = 2
B_GROUP = B_HEADS // B_KV_HEADS
B_WIDTH = B_HEADS * B_HEAD_DIM
IDX_HEADS = 8
IDX_DIM = 64
TOPK_MAX = 256
Q_BLOCK = 128
LN_EPS = 1e-5
LOG2_E = 1.4426950408889634

LANES = 128
VMEM_LIMIT = 52 * 1024 * 1024
TOKEN_TILE = 512
KEY_CHUNK = 512
ATT_CHUNK = 256
DSA_ROWS = 4 * Q_BLOCK
BISECT_STEPS = 14
REFINE_STEPS = 4
KEY_ACC_ROWS = 32

F32 = jnp.float32
BF16 = jnp.bfloat16
NT_DIMS = (((1,), (1,)), ((), ()))


def _dot(a, b):
    return jnp.dot(a, b, preferred_element_type=F32)


def _dot_nt(a, b):
    return lax.dot_general(a, b, NT_DIMS, preferred_element_type=F32)


def _reduce_keys(x, reduce):
    part = reduce(x.reshape(-1, KEY_ACC_ROWS, x.shape[-1]), axis=0)
    return reduce(part, axis=0, keepdims=True)


def _layer_norm(x, g, b):
    mu = jnp.mean(x, axis=-1, keepdims=True)
    xc = x - mu
    var = jnp.mean(xc * xc, axis=-1, keepdims=True)
    return xc * lax.rsqrt(var + LN_EPS) * g + b


def _inproj_kernel(x_ref, wa_ref, wq_ref, wkv_ref, wva_ref, wqi_ref, wkw_ref, lnv_ref, lnk_ref, ws_ref, bs_ref,
                   a_ref, q_ref, qi_ref, k_ref, v_ref, kt_ref, vb_ref, ki_ref, kit_ref, wi_ref, *va_refs,
                   span):
    tm = x_ref.shape[0]
    xb = x_ref[...].astype(BF16)

    za = _dot(xb, wa_ref[...])
    ga = 0.5 * za * (1.0 + lax.erf(za * np.float32(np.sqrt(0.5))))
    u = ga[:, :A_WIDTH]
    va = _layer_norm(ga[:, A_WIDTH:], lnv_ref[0:1, :], lnv_ref[1:2, :])
    if va_refs:
        va_refs[0][...] = va
    vab = va.astype(BF16)

    zq = _dot(xb, wq_ref[...])
    zqi = _dot(xb, wqi_ref[...])
    for h in range(B_HEADS):
        q_ref[h] = zq[:, h * B_HEAD_DIM:(h + 1) * B_HEAD_DIM].astype(BF16)
    for h in range(IDX_HEADS):
        qi_ref[h] = zqi[:, h * IDX_DIM:(h + 1) * IDX_DIM].astype(BF16)

    zkv = _dot(xb, wkv_ref[...])
    kvw = B_KV_HEADS * B_HEAD_DIM
    if len(kt_ref.shape) == 4:
        kt_ref, vb_ref, kit_ref = kt_ref.at[0], vb_ref.at[0], kit_ref.at[0]
    zva = _dot(xb, wva_ref[...])
    ones_lane = lax.broadcasted_iota(jnp.int32, (tm, LANES), 1) == B_HEAD_DIM
    for g in range(B_KV_HEADS):
        kg = zkv[:, g * B_HEAD_DIM:(g + 1) * B_HEAD_DIM]
        k_ref[pl.ds(g, tm, stride=B_KV_HEADS), :] = kg
        v_ref[pl.ds(g, tm, stride=B_KV_HEADS), :] = zkv[:, kvw + g * B_HEAD_DIM:kvw + (g + 1) * B_HEAD_DIM]
        kt_ref[g] = kg.T.astype(BF16)
        vb_ref[g] = jnp.where(ones_lane, 1.0, zva[:, g * LANES:(g + 1) * LANES]).astype(BF16)

    zkw = _dot(xb, wkw_ref[...])
    ki = _layer_norm(zkw[:, :IDX_DIM], lnk_ref[0:1, :], lnk_ref[1:2, :])
    ki_ref[...] = ki
    kit_ref[...] = ki.T.astype(BF16)
    wi_ref[...] = zkw[:, IDX_DIM:IDX_DIM + IDX_HEADS] * np.float32(IDX_HEADS ** -0.5)

    for s in range(tm // span):
        rows = slice(s * span, (s + 1) * span)
        for g in range(A_GROUPS):
            cols = slice(g * A_GROUP_CH, (g + 1) * A_GROUP_CH)
            mixed = _dot(ws_ref[g], vab[rows, cols]) + bs_ref[:, cols]
            a_ref[rows, cols] = (u[rows, cols] * mixed).astype(BF16)


def _const_spec(shape):
    nd = len(shape)
    return pl.BlockSpec(shape, lambda *_: (0,) * nd)


def _in_projection(x2, wts, span, tm, emit_va, batch_rows=None):
    t, d = x2.shape
    grid = (t // tm,)
    row = lambda i: (i, 0)
    hm = lambda i: (0, i, 0)
    if batch_rows is None:
        kt_shape, kt_block, kt_map = (B_KV_HEADS, B_HEAD_DIM, t), (B_KV_HEADS, B_HEAD_DIM, tm), lambda i: (0, 0, i)
        vb_shape, vb_block, vb_map = (B_KV_HEADS, t, LANES), (B_KV_HEADS, tm, LANES), hm
        ki_shape, ki_block, ki_map = (IDX_DIM, t), (IDX_DIM, tm), lambda i: (0, i)
    else:
        per, n = batch_rows // tm, t // batch_rows
        kt_shape, kt_block = (n, B_KV_HEADS, B_HEAD_DIM, batch_rows), (1, B_KV_HEADS, B_HEAD_DIM, tm)
        kt_map = lambda i: (i // per, 0, 0, i % per)
        vb_shape, vb_block = (n, B_KV_HEADS, batch_rows, LANES), (1, B_KV_HEADS, tm, LANES)
        vb_map = lambda i: (i // per, 0, i % per, 0)
        ki_shape, ki_block, ki_map = (n, IDX_DIM, batch_rows), (1, IDX_DIM, tm), lambda i: (i // per, 0, i % per)
    in_specs = [pl.BlockSpec((tm, d), row)] + [_const_spec(w.shape) for w in wts]
    out_shape = [
        jax.ShapeDtypeStruct((t, A_WIDTH), BF16),
        jax.ShapeDtypeStruct((B_HEADS, t, B_HEAD_DIM), BF16),
        jax.ShapeDtypeStruct((IDX_HEADS, t, IDX_DIM), BF16),
        jax.ShapeDtypeStruct((t * B_KV_HEADS, B_HEAD_DIM), F32),
        jax.ShapeDtypeStruct((t * B_KV_HEADS, B_HEAD_DIM), F32),
        jax.ShapeDtypeStruct(kt_shape, BF16),
        jax.ShapeDtypeStruct(vb_shape, BF16),
        jax.ShapeDtypeStruct((t, IDX_DIM), F32),
        jax.ShapeDtypeStruct(ki_shape, BF16),
        jax.ShapeDtypeStruct((t, IDX_HEADS), F32),
    ]
    out_specs = [
        pl.BlockSpec((tm, A_WIDTH), row),
        pl.BlockSpec((B_HEADS, tm, B_HEAD_DIM), hm),
        pl.BlockSpec((IDX_HEADS, tm, IDX_DIM), hm),
        pl.BlockSpec((tm * B_KV_HEADS, B_HEAD_DIM), row),
        pl.BlockSpec((tm * B_KV_HEADS, B_HEAD_DIM), row),
        pl.BlockSpec(kt_block, kt_map),
        pl.BlockSpec(vb_block, vb_map),
        pl.BlockSpec((tm, IDX_DIM), row),
        pl.BlockSpec(ki_block, ki_map),
        pl.BlockSpec((tm, IDX_HEADS), row),
    ]
    if emit_va:
        out_shape.append(jax.ShapeDtypeStruct((t, A_WIDTH), F32))
        out_specs.append(pl.BlockSpec((tm, A_WIDTH), row))
    return pl.pallas_call(
        functools.partial(_inproj_kernel, span=span),
        grid=grid, in_specs=in_specs, out_specs=out_specs, out_shape=out_shape,
        compiler_params=pltpu.CompilerParams(dimension_semantics=("parallel",), vmem_limit_bytes=VMEM_LIMIT),
        name="in_projection",
    )(x2, *wts)


def _key_pieces(seg_lens, n_keys, chunk):
    pieces, col = [], 0
    for si, seg_len in enumerate(seg_lens):
        off = 0
        while off < seg_len and col < n_keys:
            width = min(chunk, seg_len - off, n_keys - col)
            pieces.append((col, width, si, off))
            off += width
            col += width
    return pieces


def _dsa_block(q_ref, qi_ref, wi_ref, *refs, segments, qb, tq, rb, row_keys, n_valid, q_off, topk, slopes):
    n_seg = len(segments)
    seg_refs = [refs[3 * i:3 * i + 3] for i in range(n_seg)]
    o_ref, s_ref, d_ref, thr_ref, st_ref = refs[3 * n_seg:]
    seg_lens = [length for length, _ in segments]
    neg_inf = np.float32(-np.inf)
    pos_inf = np.float32(np.inf)
    kf = np.float32(topk)
    n_rb = tq // rb
    assert rb <= LANES and rb % 8 == 0
    blocks = [(slice(r * rb, (r + 1) * rb), row_keys[r]) for r in range(n_rb)]

    for r, (rows, n_keys) in enumerate(blocks):
        q_pos = q_off + qb * tq + rows.start + lax.broadcasted_iota(jnp.int32, (rb, 1), 0)
        qi = qi_ref[:, rows, :].reshape(IDX_HEADS * rb, IDX_DIM)
        wi = wi_ref[rows, :]
        for c0, kc, si, off in _key_pieces(seg_lens, n_keys, KEY_CHUNK):
            k_pos = c0 + lax.broadcasted_iota(jnp.int32, (1, kc), 1)
            lg = _dot(qi, seg_refs[si][2][0, :, off:off + kc].astype(BF16))
            sc = wi[:, 0:1] * jnp.maximum(lg[0:rb], 0.0)
            for h in range(1, IDX_HEADS):
                sc = sc + wi[:, h:h + 1] * jnp.maximum(lg[h * rb:(h + 1) * rb], 0.0)
            k_chunk = k_pos // CHUNK
            if c0 + kc > n_valid:
                k_chunk = jnp.where(k_pos < n_valid, k_chunk, np.int32(2 ** 30))
            adm = k_chunk <= (q_pos // CHUNK)
            sc = jnp.where(adm, sc, neg_inf)
            s_ref[rows, c0:c0 + kc] = sc
            if rb < LANES:
                sc = jnp.concatenate([sc, jnp.full((LANES - rb, kc), neg_inf, F32)], axis=0)
            st_ref[r, c0:c0 + kc, :] = sc.T
            d_ref[rows, c0:c0 + kc] = jnp.abs(q_pos - k_pos).astype(F32)

    def scores_t(r):
        return st_ref[r, :row_keys[r], :]

    def count_ge(r, t):
        return _reduce_keys(jnp.where(scores_t(r) >= t, 1.0, 0.0), jnp.sum)

    lane = lax.broadcasted_iota(jnp.int32, (1, LANES), 1)
    brackets, n_adms = [], []
    for r in range(n_rb):
        s = scores_t(r)
        q_pos = q_off + qb * tq + r * rb + lane
        n_adm = jnp.minimum((q_pos // CHUNK + 1) * CHUNK, n_valid)
        n_adm = jnp.where(lane < rb, n_adm, 0).astype(F32)
        row_max = _reduce_keys(s, jnp.max)
        row_min = _reduce_keys(jnp.where(s > neg_inf, s, pos_inf), jnp.min)
        brackets.append((row_min, row_max, jnp.full((1, LANES), pos_inf, F32), n_adm, jnp.zeros((1, LANES), F32)))
        n_adms.append(n_adm)

    def bisect(carry):
        out = []
        for r, (lo, hb, hiv, c_lo, c_hi) in enumerate(carry):
            mid = 0.5 * lo + 0.5 * hb
            cnt = count_ge(r, mid)
            ge = cnt >= kf
            out.append((jnp.where(ge, mid, lo), jnp.where(ge, hb, mid), jnp.where(ge, hiv, mid),
                        jnp.where(ge, cnt, c_lo), jnp.where(ge, c_hi, cnt)))
        return tuple(out)

    def n_crowded(brackets):
        return sum(jnp.sum(jnp.where(jnp.logical_and(c_lo - c_hi > 1.0, n_adm > kf), 1.0, 0.0))
                   for (_, _, _, c_lo, c_hi), n_adm in zip(brackets, n_adms))

    brackets = lax.fori_loop(0, BISECT_STEPS, lambda _, c: bisect(c), tuple(brackets))
    brackets, _, _ = lax.while_loop(
        lambda c: jnp.logical_and(c[1] > 0.0, c[2] < REFINE_STEPS),
        lambda c: (lambda b: (b, n_crowded(b), c[2] + 1))(bisect(c[0])),
        (brackets, n_crowded(brackets), jnp.int32(0)))

    states = []
    for r, ((_, _, hiv, c_lo, c_hi), n_adm) in enumerate(zip(brackets, n_adms)):
        sv = scores_t(r)
        cand = _reduce_keys(jnp.where(sv < hiv, sv, neg_inf), jnp.max)
        select_all = n_adm <= kf
        single = jnp.logical_and(c_lo - c_hi <= 1.0, jnp.logical_not(select_all))
        thr = jnp.where(single, cand, np.finfo(np.float32).min)
        done = jnp.where(jnp.logical_or(single, select_all), 1.0, 0.0)
        states.append((thr, hiv, done, jnp.where(single, c_lo, 0.0)))

    def n_open(states):
        return sum(jnp.sum(1.0 - done) for _, _, done, _ in states)

    def scan_cond(carry):
        _, n_left, it = carry
        return jnp.logical_and(n_left > 0.0, it < max(row_keys))

    def scan_body(carry):
        states, _, it = carry
        out = []
        for r, (thr, hiv, done, n_ge) in enumerate(states):
            sv = scores_t(r)
            cand = _reduce_keys(jnp.where(sv < hiv, sv, neg_inf), jnp.max)
            cnt = count_ge(r, cand)
            found = jnp.logical_and(done < 0.5, cnt >= kf)
            thr = jnp.where(found, cand, thr)
            n_ge = jnp.where(found, cnt, n_ge)
            done = jnp.where(found, 1.0, done)
            out.append((thr, jnp.where(done > 0.5, hiv, cand), done, n_ge))
        return tuple(out), n_open(out), it + 1

    states, _, _ = lax.while_loop(scan_cond, scan_body, (tuple(states), n_open(states), jnp.int32(0)))

    over = []
    for r, (thr, _, _, n_ge) in enumerate(states):
        thr_ref[blocks[r][0], :] = jnp.broadcast_to(thr, (LANES, LANES)).T[:rb, 0:1]
        over.append(jnp.max(n_ge))
    tied = functools.reduce(jnp.maximum, over) > kf

    ones_row = jnp.where(lax.broadcasted_iota(jnp.int32, (LANES - B_HEAD_DIM, ATT_CHUNK), 0) == 0, 1.0, 0.0)
    ones_row = ones_row.astype(BF16)

    for rows, n_keys in blocks:
        thr = thr_ref[rows, :]

        @pl.when(jnp.logical_not(tied))
        def _():
            d_ref[rows, :n_keys] = jnp.where(s_ref[rows, :n_keys] >= thr, d_ref[rows, :n_keys], pos_inf)

        @pl.when(tied)
        def _():
            n_gt = jnp.sum(jnp.where(s_ref[rows, :n_keys] > thr, 1.0, 0.0), axis=1, keepdims=True)
            room = kf - n_gt
            tri = (lax.broadcasted_iota(jnp.int32, (LANES, LANES), 0)
                   <= lax.broadcasted_iota(jnp.int32, (LANES, LANES), 1))
            tri = jnp.where(tri, 1.0, 0.0).astype(BF16)
            run = jnp.zeros((rb, 1), F32)
            for c0 in range(0, n_keys, LANES):
                blk = s_ref[rows, c0:c0 + LANES]
                eq = jnp.where(blk == thr, 1.0, 0.0)
                prefix = _dot(eq.astype(BF16), tri) + run
                keep = jnp.where(blk > thr, 1.0, jnp.where(prefix <= room, eq, 0.0))
                d_ref[rows, c0:c0 + LANES] = jnp.where(keep > 0.5, d_ref[rows, c0:c0 + LANES], pos_inf)
                run = run + jnp.sum(eq, axis=1, keepdims=True)

        qs = [q_ref[g * B_GROUP:(g + 1) * B_GROUP, rows, :].reshape(B_GROUP * rb, B_HEAD_DIM)
              for g in range(B_KV_HEADS)]
        m_run = [jnp.full((rb, 1), neg_inf, F32) for _ in range(B_HEADS)]
        acc = [jnp.zeros((rb, LANES), F32) for _ in range(B_HEADS)]
        for c0, kc, si, off in _key_pieces(seg_lens, n_keys, ATT_CHUNK):
            kt_ref, v_ref, _ = seg_refs[si]
            dist = d_ref[rows, c0:c0 + kc]
            for g in range(B_KV_HEADS):
                logits = _dot(qs[g], kt_ref[0, g, :, off:off + kc].astype(BF16))
                ps, alphas = [], []
                for hh in range(B_GROUP):
                    h = g * B_GROUP + hh
                    lgt = logits[hh * rb:(hh + 1) * rb] - np.float32(slopes[h] * LOG2_E) * dist
                    m_new = jnp.maximum(m_run[h], jnp.max(lgt, axis=1, keepdims=True))
                    m_ref = jnp.where(m_new == neg_inf, 0.0, m_new)
                    alphas.append(jnp.exp2(m_run[h] - m_ref))
                    ps.append(jnp.exp2(lgt - m_ref).astype(BF16))
                    m_run[h] = m_new
                p = jnp.concatenate(ps, axis=0)
                if segments[si][1]:
                    vt = jnp.concatenate([v_ref[0, g, :, off:off + kc].astype(BF16), ones_row[:, :kc]], axis=0)
                    pv = _dot_nt(p, vt)
                else:
                    pv = _dot(p, v_ref[0, g, off:off + kc, :])
                for hh in range(B_GROUP):
                    h = g * B_GROUP + hh
                    acc[h] = alphas[hh] * acc[h] + pv[hh * rb:(hh + 1) * rb]
        for h in range(B_HEADS):
            out = acc[h][:, :B_HEAD_DIM] / acc[h][:, B_HEAD_DIM:B_HEAD_DIM + 1]
            o_ref[rows, h * B_HEAD_DIM:(h + 1) * B_HEAD_DIM] = out.astype(BF16)


def _dsa_attention(q_hm, qi_hm, wi, key_segments, *, n, t_q, tq, rb, qb, n_valid, q_off, topk, slopes):
    nqb = t_q // tq
    n_rb = tq // rb
    last_chunk = (q_off + (qb + 1) * tq - 1) // CHUNK
    need = min((last_chunk + 1) * CHUNK, n_valid)
    n_keys = -(-need // LANES) * LANES
    if rb % CHUNK == 0 and need == q_off + (qb + 1) * tq:
        row_keys = tuple(n_keys - (n_rb - 1 - r) * rb for r in range(n_rb))
    else:
        row_keys = (n_keys,) * n_rb
    qmap = lambda b: (0, b * nqb + qb, 0)
    in_specs = [
        pl.BlockSpec((B_HEADS, tq, B_HEAD_DIM), qmap),
        pl.BlockSpec((IDX_HEADS, tq, IDX_DIM), qmap),
        pl.BlockSpec((tq, IDX_HEADS), lambda b: (b * nqb + qb, 0)),
    ]
    operands, segments, col = [], [], 0
    for kt, v, kit, v_transposed in key_segments:
        length = min(kt.shape[-1], n_keys - col)
        if length <= 0:
            break
        v_block = (1, B_KV_HEADS, B_HEAD_DIM, length) if v_transposed else (1, B_KV_HEADS, length, LANES)
        in_specs += [
            pl.BlockSpec((1, B_KV_HEADS, B_HEAD_DIM, length), lambda b: (b, 0, 0, 0)),
            pl.BlockSpec(v_block, lambda b: (b, 0, 0, 0)),
            pl.BlockSpec((1, IDX_DIM, length), lambda b: (b, 0, 0)),
        ]
        operands += [kt, v, kit]
        segments.append((length, v_transposed))
        col += length
    assert col == n_keys, (col, n_keys)
    return pl.pallas_call(
        functools.partial(_dsa_block, segments=tuple(segments), qb=qb, tq=tq, rb=rb, row_keys=row_keys,
                          n_valid=n_valid, q_off=q_off, topk=topk, slopes=slopes),
        grid=(n,), in_specs=in_specs,
        out_specs=pl.BlockSpec((tq, B_WIDTH), lambda b: (b, 0)),
        out_shape=jax.ShapeDtypeStruct((n * tq, B_WIDTH), BF16),
        scratch_shapes=[pltpu.VMEM((tq, n_keys), F32), pltpu.VMEM((tq, n_keys), F32), pltpu.VMEM((tq, 1), F32),
                        pltpu.VMEM((n_rb, n_keys, LANES), F32)],
        compiler_params=pltpu.CompilerParams(dimension_semantics=("parallel",), vmem_limit_bytes=VMEM_LIMIT),
        name="dsa_attention",
    )(q_hm, qi_hm, wi, *operands)


def _merge_kernel(x_ref, a_ref, b_ref, wg_ref, wb_ref, wo_ref, ln1_ref, w1_ref, b1_ref, w2_ref, b2_ref, ln2_ref,
                  y_ref, *, alpha):
    d = x_ref.shape[1]
    x = x_ref[...]
    gates = jax.nn.sigmoid(_dot(x.astype(BF16), wg_ref[...]))
    m = gates[:, :d] * _dot(a_ref[...], wb_ref[0]) + gates[:, d:] * _dot(b_ref[...], wb_ref[1])
    h = _layer_norm(alpha * x + _dot(m.astype(BF16), wo_ref[...]), ln1_ref[0:1, :], ln1_ref[1:2, :])
    f = jnp.square(jnp.maximum(_dot(h.astype(BF16), w1_ref[...]) + b1_ref[...], 0.0))
    f = _dot(f.astype(BF16), w2_ref[...]) + b2_ref[...]
    y_ref[...] = _layer_norm(alpha * h + f, ln2_ref[0:1, :], ln2_ref[1:2, :])


def _resident_spec(shape):
    nd = len(shape)
    return pl.BlockSpec(shape, lambda *_: (0,) * nd, pipeline_mode=pl.Buffered(1))


def _merge_ffn(x2, a, b, wts, tm, alpha):
    t, d = x2.shape
    row = lambda i: (i, 0)
    in_specs = [pl.BlockSpec((tm, d), row), pl.BlockSpec((tm, A_WIDTH), row), pl.BlockSpec((tm, B_WIDTH), row)]
    in_specs += [_resident_spec(w.shape) for w in wts]
    return pl.pallas_call(
        functools.partial(_merge_kernel, alpha=np.float32(alpha)),
        grid=(t // tm,), in_specs=in_specs,
        out_specs=pl.BlockSpec((tm, d), row),
        out_shape=jax.ShapeDtypeStruct((t, d), F32),
        compiler_params=pltpu.CompilerParams(dimension_semantics=("parallel",), vmem_limit_bytes=VMEM_LIMIT),
        name="merge_ffn",
    )(x2, a, b, *wts)


def _spatial_weights(w_s, b_s, span):
    pos = jnp.arange(A_SPAN)
    mask = (pos[None, :] // CHUNK) <= (pos[:, None] // CHUNK)
    ws = jnp.where(mask[None], w_s, 0.0)[:, :span, :span].astype(BF16)
    bs = jnp.repeat(b_s[:, :span].T, A_GROUP_CH, axis=1)
    return ws, bs


def _pad_axis(x, axis, size):
    pads = [(0, 0)] * x.ndim
    pads[axis] = (0, size - x.shape[axis])
    return jnp.pad(x, pads)


def kernel(x_prompt, x_sample, cache_k, cache_v, cache_kidx, w_in, lnv_g, lnv_b, w_s, b_s, lnk_g, lnk_b,
           w_branch, w_out, ln1_g, ln1_b, w_ff1, b_ff1, w_ff2, b_ff2, ln2_g, ln2_b):
    depth = w_in.shape[0]
    n_p, s_p, d = x_prompt.shape
    n_s, t_s, _ = x_sample.shape
    past = cache_k.shape[2]
    alpha = (2 * depth) ** 0.25
    slopes = tuple(float(2.0 ** (-8.0 * h / B_HEADS)) for h in range(1, B_HEADS + 1))
    kvw = B_KV_HEADS * B_HEAD_DIM
    c_a = 2 * A_WIDTH
    c_q = c_a + B_WIDTH
    c_k = c_q + kvw
    c_v = c_k + kvw
    c_qi = c_v + IDX_HEADS * IDX_DIM
    c_wi = c_qi + IDX_DIM + IDX_HEADS

    xp = x_prompt.reshape(n_p * s_p, d)
    xs = x_sample.reshape(n_s * t_s, d)
    outs = [[] for _ in range(7)]
    for l in range(depth):
        w = w_in[l]
        pad = jnp.zeros((d, LANES - (c_wi - c_qi)), F32)
        proj_w = (
            w[:, :c_a].astype(BF16),
            (w[:, c_a:c_q] * (LOG2_E * B_HEAD_DIM ** -0.5)).astype(BF16),
            w[:, c_q:c_v].astype(BF16),
            jnp.pad(w[:, c_k:c_v].reshape(d, B_KV_HEADS, B_HEAD_DIM),
                    ((0, 0), (0, 0), (0, LANES - B_HEAD_DIM))).reshape(d, B_KV_HEADS * LANES).astype(BF16),
            (w[:, c_v:c_qi] * (IDX_DIM ** -0.5)).astype(BF16),
            jnp.concatenate([w[:, c_qi:c_wi], pad], axis=1).astype(BF16),
            jnp.stack([lnv_g[l], lnv_b[l]]),
            jnp.stack([lnk_g[l], lnk_b[l]]),
        )
        merge_w = (
            w[:, c_wi:].astype(BF16),
            w_branch[l].astype(BF16),
            w_out[l].astype(BF16),
            jnp.stack([ln1_g[l], ln1_b[l]]),
            w_ff1[l].astype(BF16),
            b_ff1[l][None, :],
            w_ff2[l].astype(BF16),
            b_ff2[l][None, :],
            jnp.stack([ln2_g[l], ln2_b[l]]),
        )

        a_p, q_p, qi_p, k_p, v_p, kt_p, vb_p, ki_p, kit_p, wi_p = _in_projection(
            xp, proj_w + _spatial_weights(w_s[l], b_s[l], A_SPAN), A_SPAN, TOKEN_TILE, False, batch_rows=s_p)
        b_p = [_dsa_attention(
            q_p, qi_p, wi_p, [(kt_p, vb_p, kit_p, False)],
            n=n_p, t_q=s_p, tq=DSA_ROWS, rb=Q_BLOCK, qb=j, n_valid=s_p, q_off=0,
            topk=min(TOPK_MAX, s_p // 4), slopes=slopes).reshape(n_p, DSA_ROWS, B_WIDTH)
            for j in range(s_p // DSA_ROWS)]
        b_p = jnp.stack(b_p, axis=1).reshape(n_p * s_p, B_WIDTH)
        xp = _merge_ffn(xp, a_p, b_p, merge_w, TOKEN_TILE, alpha)

        a_s, q_s, qi_s, k_s, v_s, kt_s, vb_s, ki_s, kit_s, wi_s, va_s = _in_projection(
            xs, proj_w + _spatial_weights(w_s[l], b_s[l], t_s), t_s, n_s * t_s, True)
        n_all = past + t_s
        new_len = -(-t_s // LANES) * LANES
        cache_seg = (jnp.transpose(cache_k[l], (0, 2, 3, 1)), jnp.transpose(cache_v[l], (0, 2, 3, 1)),
                     jnp.transpose(cache_kidx[l], (0, 2, 1)), True)
        kt_new = jnp.transpose(kt_s.reshape(B_KV_HEADS, B_HEAD_DIM, n_s, t_s), (2, 0, 1, 3))
        vb_new = jnp.transpose(vb_s.reshape(B_KV_HEADS, n_s, t_s, LANES), (1, 0, 2, 3))
        kit_new = jnp.transpose(kit_s.reshape(IDX_DIM, n_s, t_s), (1, 0, 2))
        new_seg = (_pad_axis(kt_new, 3, new_len), _pad_axis(vb_new, 2, new_len), _pad_axis(kit_new, 2, new_len), False)
        b_s_ = _dsa_attention(
            q_s, qi_s, wi_s, [cache_seg, new_seg],
            n=n_s, t_q=t_s, tq=t_s, rb=t_s, qb=0, n_valid=n_all, q_off=past,
            topk=min(TOPK_MAX, n_all // 4), slopes=slopes)
        xs = _merge_ffn(xs, a_s, b_s_, merge_w, n_s * t_s, alpha)

        outs[0].append(k_p.reshape(n_p, s_p, B_KV_HEADS, B_HEAD_DIM))
        outs[1].append(v_p.reshape(n_p, s_p, B_KV_HEADS, B_HEAD_DIM))
        outs[2].append(ki_p.reshape(n_p, s_p, IDX_DIM))
        outs[3].append(k_s.reshape(n_s, t_s, B_KV_HEADS, B_HEAD_DIM))
        outs[4].append(v_s.reshape(n_s, t_s, B_KV_HEADS, B_HEAD_DIM))
        outs[5].append(ki_s.reshape(n_s, t_s, IDX_DIM))
        outs[6].append(va_s.reshape(n_s, t_s, A_WIDTH))

    return (xp.reshape(n_p, s_p, d), xs.reshape(n_s, t_s, d)) + tuple(jnp.stack(o) for o in outs)
```

```python
import functools

import numpy as np
import jax
import jax.numpy as jnp
from jax import lax
from jax.experimental import pallas as pl
from jax.experimental.pallas import tpu as pltpu

CHUNK = 64
A_WIDTH = 512
A_GROUPS = 4
A_GROUP_CH = A_WIDTH // A_GROUPS
A_SPAN = 128
B_HEADS = 8
B_HEAD_DIM = 64
B_KV_HEADS = 2
B_GROUP = B_HEADS // B_KV_HEADS
B_WIDTH = B_HEADS * B_HEAD_DIM
IDX_HEADS = 8
IDX_DIM = 64
TOPK_MAX = 256
Q_BLOCK = 128
LN_EPS = 1e-5
LOG2_E = 1.4426950408889634

LANES = 128
VMEM_LIMIT = 52 * 1024 * 1024
TOKEN_TILE = 512
KEY_CHUNK = 512
ATT_CHUNK = 256
DSA_ROWS = 2 * Q_BLOCK
BISECT_STEPS = 14
KEY_ACC_ROWS = 32

F32 = jnp.float32
BF16 = jnp.bfloat16
NT_DIMS = (((1,), (1,)), ((), ()))


def _dot(a, b):
    return jnp.dot(a, b, preferred_element_type=F32)


def _dot_nt(a, b):
    return lax.dot_general(a, b, NT_DIMS, preferred_element_type=F32)


def _reduce_keys(x, reduce):
    part = reduce(x.reshape(-1, KEY_ACC_ROWS, x.shape[-1]), axis=0)
    return reduce(part, axis=0, keepdims=True)


def _layer_norm(x, g, b):
    mu = jnp.mean(x, axis=-1, keepdims=True)
    xc = x - mu
    var = jnp.mean(xc * xc, axis=-1, keepdims=True)
    return xc * lax.rsqrt(var + LN_EPS) * g + b


def _inproj_kernel(x_ref, wa_ref, wq_ref, wkv_ref, wva_ref, wqi_ref, wkw_ref, lnv_ref, lnk_ref, ws_ref, bs_ref,
                   a_ref, q_ref, qi_ref, k_ref, v_ref, kt_ref, vb_ref, ki_ref, kit_ref, wi_ref, *va_refs,
                   span):
    tm = x_ref.shape[0]
    xb = x_ref[...].astype(BF16)

    za = _dot(xb, wa_ref[...])
    ga = 0.5 * za * (1.0 + lax.erf(za * np.float32(np.sqrt(0.5))))
    u = ga[:, :A_WIDTH]
    va = _layer_norm(ga[:, A_WIDTH:], lnv_ref[0:1, :], lnv_ref[1:2, :])
    if va_refs:
        va_refs[0][...] = va
    vab = va.astype(BF16)

    zq = _dot(xb, wq_ref[...])
    zqi = _dot(xb, wqi_ref[...])
    for h in range(B_HEADS):
        q_ref[h] = zq[:, h * B_HEAD_DIM:(h + 1) * B_HEAD_DIM].astype(BF16)
    for h in range(IDX_HEADS):
        qi_ref[h] = zqi[:, h * IDX_DIM:(h + 1) * IDX_DIM].astype(BF16)

    zkv = _dot(xb, wkv_ref[...])
    kvw = B_KV_HEADS * B_HEAD_DIM
    if len(kt_ref.shape) == 4:
        kt_ref, vb_ref, kit_ref = kt_ref.at[0], vb_ref.at[0], kit_ref.at[0]
    zva = _dot(xb, wva_ref[...])
    ones_lane = lax.broadcasted_iota(jnp.int32, (tm, LANES), 1) == B_HEAD_DIM
    for g in range(B_KV_HEADS):
        kg = zkv[:, g * B_HEAD_DIM:(g + 1) * B_HEAD_DIM]
        k_ref[pl.ds(g, tm, stride=B_KV_HEADS), :] = kg
        v_ref[pl.ds(g, tm, stride=B_KV_HEADS), :] = zkv[:, kvw + g * B_HEAD_DIM:kvw + (g + 1) * B_HEAD_DIM]
        kt_ref[g] = kg.T.astype(BF16)
        vb_ref[g] = jnp.where(ones_lane, 1.0, zva[:, g * LANES:(g + 1) * LANES]).astype(BF16)

    zkw = _dot(xb, wkw_ref[...])
    ki = _layer_norm(zkw[:, :IDX_DIM], lnk_ref[0:1, :], lnk_ref[1:2, :])
    ki_ref[...] = ki
    kit_ref[...] = ki.T.astype(BF16)
    wi_ref[...] = zkw[:, IDX_DIM:IDX_DIM + IDX_HEADS] * np.float32(IDX_HEADS ** -0.5)

    for s in range(tm // span):
        rows = slice(s * span, (s + 1) * span)
        for g in range(A_GROUPS):
            cols = slice(g * A_GROUP_CH, (g + 1) * A_GROUP_CH)
            mixed = _dot(ws_ref[g], vab[rows, cols]) + bs_ref[:, cols]
            a_ref[rows, cols] = (u[rows, cols] * mixed).astype(BF16)


def _const_spec(shape):
    nd = len(shape)
    return pl.BlockSpec(shape, lambda *_: (0,) * nd)


def _in_projection(x2, wts, span, tm, emit_va, batch_rows=None):
    t, d = x2.shape
    grid = (t // tm,)
    row = lambda i: (i, 0)
    hm = lambda i: (0, i, 0)
    if batch_rows is None:
        kt_shape, kt_block, kt_map = (B_KV_HEADS, B_HEAD_DIM, t), (B_KV_HEADS, B_HEAD_DIM, tm), lambda i: (0, 0, i)
        vb_shape, vb_block, vb_map = (B_KV_HEADS, t, LANES), (B_KV_HEADS, tm, LANES), hm
        ki_shape, ki_block, ki_map = (IDX_DIM, t), (IDX_DIM, tm), lambda i: (0, i)
    else:
        per, n = batch_rows // tm, t // batch_rows
        kt_shape, kt_block = (n, B_KV_HEADS, B_HEAD_DIM, batch_rows), (1, B_KV_HEADS, B_HEAD_DIM, tm)
        kt_map = lambda i: (i // per, 0, 0, i % per)
        vb_shape, vb_block = (n, B_KV_HEADS, batch_rows, LANES), (1, B_KV_HEADS, tm, LANES)
        vb_map = lambda i: (i // per, 0, i % per, 0)
        ki_shape, ki_block, ki_map = (n, IDX_DIM, batch_rows), (1, IDX_DIM, tm), lambda i: (i // per, 0, i % per)
    in_specs = [pl.BlockSpec((tm, d), row)] + [_const_spec(w.shape) for w in wts]
    out_shape = [
        jax.ShapeDtypeStruct((t, A_WIDTH), BF16),
        jax.ShapeDtypeStruct((B_HEADS, t, B_HEAD_DIM), BF16),
        jax.ShapeDtypeStruct((IDX_HEADS, t, IDX_DIM), BF16),
        jax.ShapeDtypeStruct((t * B_KV_HEADS, B_HEAD_DIM), F32),
        jax.ShapeDtypeStruct((t * B_KV_HEADS, B_HEAD_DIM), F32),
        jax.ShapeDtypeStruct(kt_shape, BF16),
        jax.ShapeDtypeStruct(vb_shape, BF16),
        jax.ShapeDtypeStruct((t, IDX_DIM), F32),
        jax.ShapeDtypeStruct(ki_shape, BF16),
        jax.ShapeDtypeStruct((t, IDX_HEADS), F32),
    ]
    out_specs = [
        pl.BlockSpec((tm, A_WIDTH), row),
        pl.BlockSpec((B_HEADS, tm, B_HEAD_DIM), hm),
        pl.BlockSpec((IDX_HEADS, tm, IDX_DIM), hm),
        pl.BlockSpec((tm * B_KV_HEADS, B_HEAD_DIM), row),
        pl.BlockSpec((tm * B_KV_HEADS, B_HEAD_DIM), row),
        pl.BlockSpec(kt_block, kt_map),
        pl.BlockSpec(vb_block, vb_map),
        pl.BlockSpec((tm, IDX_DIM), row),
        pl.BlockSpec(ki_block, ki_map),
        pl.BlockSpec((tm, IDX_HEADS), row),
    ]
    if emit_va:
        out_shape.append(jax.ShapeDtypeStruct((t, A_WIDTH), F32))
        out_specs.append(pl.BlockSpec((tm, A_WIDTH), row))
    return pl.pallas_call(
        functools.partial(_inproj_kernel, span=span),
        grid=grid, in_specs=in_specs, out_specs=out_specs, out_shape=out_shape,
        compiler_params=pltpu.CompilerParams(dimension_semantics=("parallel",), vmem_limit_bytes=VMEM_LIMIT),
        name="in_projection",
    )(x2, *wts)


def _key_pieces(seg_lens, n_keys, chunk):
    pieces, col = [], 0
    for si, seg_len in enumerate(seg_lens):
        off = 0
        while off < seg_len and col < n_keys:
            width = min(chunk, seg_len - off, n_keys - col)
            pieces.append((col, width, si, off))
            off += width
            col += width
    return pieces


def _dsa_block(q_ref, qi_ref, wi_ref, *refs, segments, qb, tq, rb, row_keys, n_valid, q_off, topk, slopes):
    n_seg = len(segments)
    seg_refs = [refs[3 * i:3 * i + 3] for i in range(n_seg)]
    o_ref, s_ref, d_ref, thr_ref, st_ref = refs[3 * n_seg:]
    seg_lens = [length for length, _ in segments]
    neg_inf = np.float32(-np.inf)
    pos_inf = np.float32(np.inf)
    kf = np.float32(topk)
    n_rb = tq // rb
    assert rb <= LANES and rb % 8 == 0
    blocks = [(slice(r * rb, (r + 1) * rb), row_keys[r]) for r in range(n_rb)]

    for r, (rows, n_keys) in enumerate(blocks):
        q_pos = q_off + qb * tq + rows.start + lax.broadcasted_iota(jnp.int32, (rb, 1), 0)
        qi = qi_ref[:, rows, :].reshape(IDX_HEADS * rb, IDX_DIM)
        wi = wi_ref[rows, :]
        for c0, kc, si, off in _key_pieces(seg_lens, n_keys, KEY_CHUNK):
            k_pos = c0 + lax.broadcasted_iota(jnp.int32, (1, kc), 1)
            lg = _dot(qi, seg_refs[si][2][0, :, off:off + kc].astype(BF16))
            sc = wi[:, 0:1] * jnp.maximum(lg[0:rb], 0.0)
            for h in range(1, IDX_HEADS):
                sc = sc + wi[:, h:h + 1] * jnp.maximum(lg[h * rb:(h + 1) * rb], 0.0)
            k_chunk = k_pos // CHUNK
            if c0 + kc > n_valid:
                k_chunk = jnp.where(k_pos < n_valid, k_chunk, np.int32(2 ** 30))
            adm = k_chunk <= (q_pos // CHUNK)
            sc = jnp.where(adm, sc, neg_inf)
            s_ref[rows, c0:c0 + kc] = sc
            if rb < LANES:
                sc = jnp.concatenate([sc, jnp.full((LANES - rb, kc), neg_inf, F32)], axis=0)
            st_ref[r, c0:c0 + kc, :] = sc.T
            d_ref[rows, c0:c0 + kc] = jnp.abs(q_pos - k_pos).astype(F32)

    def scores_t(r):
        return st_ref[r, :row_keys[r], :]

    def count_ge(r, t):
        return _reduce_keys(jnp.where(scores_t(r) >= t, 1.0, 0.0), jnp.sum)

    brackets, states, n_adms = [], [], []
    for r in range(n_rb):
        s = scores_t(r)
        finite = s > neg_inf
        n_adm = _reduce_keys(jnp.where(finite, 1.0, 0.0), jnp.sum)
        row_max = _reduce_keys(s, jnp.max)
        row_min = _reduce_keys(jnp.where(finite, s, pos_inf), jnp.min)
        brackets.append((row_min, row_max, jnp.full((1, LANES), pos_inf, F32)))
        n_adms.append(n_adm)

    def bisect(_, carry):
        out = []
        for r, (lo, hb, hiv) in enumerate(carry):
            mid = 0.5 * lo + 0.5 * hb
            ge = count_ge(r, mid) >= kf
            out.append((jnp.where(ge, mid, lo), jnp.where(ge, hb, mid), jnp.where(ge, hiv, mid)))
        return tuple(out)

    brackets = lax.fori_loop(0, BISECT_STEPS, bisect, tuple(brackets))

    for (_, _, hiv), n_adm in zip(brackets, n_adms):
        done = jnp.where(n_adm <= kf, 1.0, 0.0)
        states.append((jnp.full((1, LANES), np.finfo(np.float32).min, F32), hiv, done))

    def n_open(states):
        return sum(jnp.sum(1.0 - done) for _, _, done in states)

    def scan_cond(carry):
        _, n_left, it = carry
        return jnp.logical_and(n_left > 0.0, it < max(row_keys))

    def scan_body(carry):
        states, _, it = carry
        out = []
        for r, (thr, hiv, done) in enumerate(states):
            sv = scores_t(r)
            cand = _reduce_keys(jnp.where(sv < hiv, sv, neg_inf), jnp.max)
            found = jnp.where(done > 0.5, 0.0, jnp.where(count_ge(r, cand) >= kf, 1.0, 0.0))
            thr = jnp.where(found > 0.5, cand, thr)
            done = jnp.maximum(done, found)
            out.append((thr, jnp.where(done > 0.5, hiv, cand), done))
        return tuple(out), n_open(out), it + 1

    states, _, _ = lax.while_loop(scan_cond, scan_body, (tuple(states), n_open(states), jnp.int32(0)))

    over = []
    for r, ((thr, _, _), n_adm) in enumerate(zip(states, n_adms)):
        thr_ref[blocks[r][0], :] = jnp.broadcast_to(thr, (LANES, LANES)).T[:rb, 0:1]
        over.append(jnp.max(jnp.where(n_adm > kf, count_ge(r, thr), 0.0)))
    tied = functools.reduce(jnp.maximum, over) > kf

    ones_row = jnp.where(lax.broadcasted_iota(jnp.int32, (LANES - B_HEAD_DIM, ATT_CHUNK), 0) == 0, 1.0, 0.0)
    ones_row = ones_row.astype(BF16)

    for rows, n_keys in blocks:
        thr = thr_ref[rows, :]

        @pl.when(jnp.logical_not(tied))
        def _():
            d_ref[rows, :n_keys] = jnp.where(s_ref[rows, :n_keys] >= thr, d_ref[rows, :n_keys], pos_inf)

        @pl.when(tied)
        def _():
            n_gt = jnp.sum(jnp.where(s_ref[rows, :n_keys] > thr, 1.0, 0.0), axis=1, keepdims=True)
            room = kf - n_gt
            tri = (lax.broadcasted_iota(jnp.int32, (LANES, LANES), 0)
                   <= lax.broadcasted_iota(jnp.int32, (LANES, LANES), 1))
            tri = jnp.where(tri, 1.0, 0.0).astype(BF16)
            run = jnp.zeros((rb, 1), F32)
            for c0 in range(0, n_keys, LANES):
                blk = s_ref[rows, c0:c0 + LANES]
                eq = jnp.where(blk == thr, 1.0, 0.0)
                prefix = _dot(eq.astype(BF16), tri) + run
                keep = jnp.where(blk > thr, 1.0, jnp.where(prefix <= room, eq, 0.0))
                d_ref[rows, c0:c0 + LANES] = jnp.where(keep > 0.5, d_ref[rows, c0:c0 + LANES], pos_inf)
                run = run + jnp.sum(eq, axis=1, keepdims=True)

        qs = [q_ref[g * B_GROUP:(g + 1) * B_GROUP, rows, :].reshape(B_GROUP * rb, B_HEAD_DIM)
              for g in range(B_KV_HEADS)]
        m_run = [jnp.full((rb, 1), neg_inf, F32) for _ in range(B_HEADS)]
        acc = [jnp.zeros((rb, LANES), F32) for _ in range(B_HEADS)]
        for c0, kc, si, off in _key_pieces(seg_lens, n_keys, ATT_CHUNK):
            kt_ref, v_ref, _ = seg_refs[si]
            dist = d_ref[rows, c0:c0 + kc]
            for g in range(B_KV_HEADS):
                logits = _dot(qs[g], kt_ref[0, g, :, off:off + kc].astype(BF16))
                ps, alphas = [], []
                for hh in range(B_GROUP):
                    h = g * B_GROUP + hh
                    lgt = logits[hh * rb:(hh + 1) * rb] - np.float32(slopes[h] * LOG2_E) * dist
                    m_new = jnp.maximum(m_run[h], jnp.max(lgt, axis=1, keepdims=True))
                    m_ref = jnp.where(m_new == neg_inf, 0.0, m_new)
                    alphas.append(jnp.exp2(m_run[h] - m_ref))
                    ps.append(jnp.exp2(lgt - m_ref).astype(BF16))
                    m_run[h] = m_new
                p = jnp.concatenate(ps, axis=0)
                if segments[si][1]:
                    vt = jnp.concatenate([v_ref[0, g, :, off:off + kc].astype(BF16), ones_row[:, :kc]], axis=0)
                    pv = _dot_nt(p, vt)
                else:
                    pv = _dot(p, v_ref[0, g, off:off + kc, :])
                for hh in range(B_GROUP):
                    h = g * B_GROUP + hh
                    acc[h] = alphas[hh] * acc[h] + pv[hh * rb:(hh + 1) * rb]
        for h in range(B_HEADS):
            out = acc[h][:, :B_HEAD_DIM] / acc[h][:, B_HEAD_DIM:B_HEAD_DIM + 1]
            o_ref[rows, h * B_HEAD_DIM:(h + 1) * B_HEAD_DIM] = out.astype(BF16)


def _dsa_attention(q_hm, qi_hm, wi, key_segments, *, n, t_q, tq, rb, qb, n_valid, q_off, topk, slopes):
    nqb = t_q // tq
    n_rb = tq // rb
    last_chunk = (q_off + (qb + 1) * tq - 1) // CHUNK
    need = min((last_chunk + 1) * CHUNK, n_valid)
    n_keys = -(-need // LANES) * LANES
    if rb % CHUNK == 0 and need == q_off + (qb + 1) * tq:
        row_keys = tuple(n_keys - (n_rb - 1 - r) * rb for r in range(n_rb))
    else:
        row_keys = (n_keys,) * n_rb
    qmap = lambda b: (0, b * nqb + qb, 0)
    in_specs = [
        pl.BlockSpec((B_HEADS, tq, B_HEAD_DIM), qmap),
        pl.BlockSpec((IDX_HEADS, tq, IDX_DIM), qmap),
        pl.BlockSpec((tq, IDX_HEADS), lambda b: (b * nqb + qb, 0)),
    ]
    operands, segments, col = [], [], 0
    for kt, v, kit, v_transposed in key_segments:
        length = min(kt.shape[-1], n_keys - col)
        if length <= 0:
            break
        v_block = (1, B_KV_HEADS, B_HEAD_DIM, length) if v_transposed else (1, B_KV_HEADS, length, LANES)
        in_specs += [
            pl.BlockSpec((1, B_KV_HEADS, B_HEAD_DIM, length), lambda b: (b, 0, 0, 0)),
            pl.BlockSpec(v_block, lambda b: (b, 0, 0, 0)),
            pl.BlockSpec((1, IDX_DIM, length), lambda b: (b, 0, 0)),
        ]
        operands += [kt, v, kit]
        segments.append((length, v_transposed))
        col += length
    assert col == n_keys, (col, n_keys)
    return pl.pallas_call(
        functools.partial(_dsa_block, segments=tuple(segments), qb=qb, tq=tq, rb=rb, row_keys=row_keys,
                          n_valid=n_valid, q_off=q_off, topk=topk, slopes=slopes),
        grid=(n,), in_specs=in_specs,
        out_specs=pl.BlockSpec((tq, B_WIDTH), lambda b: (b, 0)),
        out_shape=jax.ShapeDtypeStruct((n * tq, B_WIDTH), BF16),
        scratch_shapes=[pltpu.VMEM((tq, n_keys), F32), pltpu.VMEM((tq, n_keys), F32), pltpu.VMEM((tq, 1), F32),
                        pltpu.VMEM((n_rb, n_keys, LANES), F32)],
        compiler_params=pltpu.CompilerParams(dimension_semantics=("parallel",), vmem_limit_bytes=VMEM_LIMIT),
        name="dsa_attention",
    )(q_hm, qi_hm, wi, *operands)


def _merge_kernel(x_ref, a_ref, b_ref, wg_ref, wb_ref, wo_ref, ln1_ref, w1_ref, b1_ref, w2_ref, b2_ref, ln2_ref,
                  y_ref, *, alpha):
    d = x_ref.shape[1]
    x = x_ref[...]
    gates = jax.nn.sigmoid(_dot(x.astype(BF16), wg_ref[...]))
    m = gates[:, :d] * _dot(a_ref[...], wb_ref[0]) + gates[:, d:] * _dot(b_ref[...], wb_ref[1])
    h = _layer_norm(alpha * x + _dot(m.astype(BF16), wo_ref[...]), ln1_ref[0:1, :], ln1_ref[1:2, :])
    f = jnp.square(jnp.maximum(_dot(h.astype(BF16), w1_ref[...]) + b1_ref[...], 0.0))
    f = _dot(f.astype(BF16), w2_ref[...]) + b2_ref[...]
    y_ref[...] = _layer_norm(alpha * h + f, ln2_ref[0:1, :], ln2_ref[1:2, :])


def _resident_spec(shape):
    nd = len(shape)
    return pl.BlockSpec(shape, lambda *_: (0,) * nd, pipeline_mode=pl.Buffered(1))


def _merge_ffn(x2, a, b, wts, tm, alpha):
    t, d = x2.shape
    row = lambda i: (i, 0)
    in_specs = [pl.BlockSpec((tm, d), row), pl.BlockSpec((tm, A_WIDTH), row), pl.BlockSpec((tm, B_WIDTH), row)]
    in_specs += [_resident_spec(w.shape) for w in wts]
    return pl.pallas_call(
        functools.partial(_merge_kernel, alpha=np.float32(alpha)),
        grid=(t // tm,), in_specs=in_specs,
        out_specs=pl.BlockSpec((tm, d), row),
        out_shape=jax.ShapeDtypeStruct((t, d), F32),
        compiler_params=pltpu.CompilerParams(dimension_semantics=("parallel",), vmem_limit_bytes=VMEM_LIMIT),
        name="merge_ffn",
    )(x2, a, b, *wts)


def _spatial_weights(w_s, b_s, span):
    pos = jnp.arange(A_SPAN)
    mask = (pos[None, :] // CHUNK) <= (pos[:, None] // CHUNK)
    ws = jnp.where(mask[None], w_s, 0.0)[:, :span, :span].astype(BF16)
    bs = jnp.repeat(b_s[:, :span].T, A_GROUP_CH, axis=1)
    return ws, bs


def _pad_axis(x, axis, size):
    pads = [(0, 0)] * x.ndim
    pads[axis] = (0, size - x.shape[axis])
    return jnp.pad(x, pads)


def kernel(x_prompt, x_sample, cache_k, cache_v, cache_kidx, w_in, lnv_g, lnv_b, w_s, b_s, lnk_g, lnk_b,
           w_branch, w_out, ln1_g, ln1_b, w_ff1, b_ff1, w_ff2, b_ff2, ln2_g, ln2_b):
    depth = w_in.shape[0]
    n_p, s_p, d = x_prompt.shape
    n_s, t_s, _ = x_sample.shape
    past = cache_k.shape[2]
    alpha = (2 * depth) ** 0.25
    slopes = tuple(float(2.0 ** (-8.0 * h / B_HEADS)) for h in range(1, B_HEADS + 1))
    kvw = B_KV_HEADS * B_HEAD_DIM
    c_a = 2 * A_WIDTH
    c_q = c_a + B_WIDTH
    c_k = c_q + kvw
    c_v = c_k + kvw
    c_qi = c_v + IDX_HEADS * IDX_DIM
    c_wi = c_qi + IDX_DIM + IDX_HEADS

    xp = x_prompt.reshape(n_p * s_p, d)
    xs = x_sample.reshape(n_s * t_s, d)
    outs = [[] for _ in range(7)]
    for l in range(depth):
        w = w_in[l]
        pad = jnp.zeros((d, LANES - (c_wi - c_qi)), F32)
        proj_w = (
            w[:, :c_a].astype(BF16),
            (w[:, c_a:c_q] * (LOG2_E * B_HEAD_DIM ** -0.5)).astype(BF16),
            w[:, c_q:c_v].astype(BF16),
            jnp.pad(w[:, c_k:c_v].reshape(d, B_KV_HEADS, B_HEAD_DIM),
                    ((0, 0), (0, 0), (0, LANES - B_HEAD_DIM))).reshape(d, B_KV_HEADS * LANES).astype(BF16),
            (w[:, c_v:c_qi] * (IDX_DIM ** -0.5)).astype(BF16),
            jnp.concatenate([w[:, c_qi:c_wi], pad], axis=1).astype(BF16),
            jnp.stack([lnv_g[l], lnv_b[l]]),
            jnp.stack([lnk_g[l], lnk_b[l]]),
        )
        merge_w = (
            w[:, c_wi:].astype(BF16),
            w_branch[l].astype(BF16),
            w_out[l].astype(BF16),
            jnp.stack([ln1_g[l], ln1_b[l]]),
            w_ff1[l].astype(BF16),
            b_ff1[l][None, :],
            w_ff2[l].astype(BF16),
            b_ff2[l][None, :],
            jnp.stack([ln2_g[l], ln2_b[l]]),
        )

        a_p, q_p, qi_p, k_p, v_p, kt_p, vb_p, ki_p, kit_p, wi_p = _in_projection(
            xp, proj_w + _spatial_weights(w_s[l], b_s[l], A_SPAN), A_SPAN, TOKEN_TILE, False, batch_rows=s_p)
        b_p = [_dsa_attention(
            q_p, qi_p, wi_p, [(kt_p, vb_p, kit_p, False)],
            n=n_p, t_q=s_p, tq=DSA_ROWS, rb=Q_BLOCK, qb=j, n_valid=s_p, q_off=0,
            topk=min(TOPK_MAX, s_p // 4), slopes=slopes).reshape(n_p, DSA_ROWS, B_WIDTH)
            for j in range(s_p // DSA_ROWS)]
        b_p = jnp.stack(b_p, axis=1).reshape(n_p * s_p, B_WIDTH)
        xp = _merge_ffn(xp, a_p, b_p, merge_w, TOKEN_TILE, alpha)

        a_s, q_s, qi_s, k_s, v_s, kt_s, vb_s, ki_s, kit_s, wi_s, va_s = _in_projection(
            xs, proj_w + _spatial_weights(w_s[l], b_s[l], t_s), t_s, n_s * t_s, True)
        n_all = past + t_s
        new_len = -(-t_s // LANES) * LANES
        cache_seg = (jnp.transpose(cache_k[l], (0, 2, 3, 1)), jnp.transpose(cache_v[l], (0, 2, 3, 1)),
                     jnp.transpose(cache_kidx[l], (0, 2, 1)), True)
        kt_new = jnp.transpose(kt_s.reshape(B_KV_HEADS, B_HEAD_DIM, n_s, t_s), (2, 0, 1, 3))
        vb_new = jnp.transpose(vb_s.reshape(B_KV_HEADS, n_s, t_s, LANES), (1, 0, 2, 3))
        kit_new = jnp.transpose(kit_s.reshape(IDX_DIM, n_s, t_s), (1, 0, 2))
        new_seg = (_pad_axis(kt_new, 3, new_len), _pad_axis(vb_new, 2, new_len), _pad_axis(kit_new, 2, new_len), False)
        b_s_ = _dsa_attention(
            q_s, qi_s, wi_s, [cache_seg, new_seg],
            n=n_s, t_q=t_s, tq=t_s, rb=t_s, qb=0, n_valid=n_all, q_off=past,
            topk=min(TOPK_MAX, n_all // 4), slopes=slopes)
        xs = _merge_ffn(xs, a_s, b_s_, merge_w, n_s * t_s, alpha)

        outs[0].append(k_p.reshape(n_p, s_p, B_KV_HEADS, B_HEAD_DIM))
        outs[1].append(v_p.reshape(n_p, s_p, B_KV_HEADS, B_HEAD_DIM))
        outs[2].append(ki_p.reshape(n_p, s_p, IDX_DIM))
        outs[3].append(k_s.reshape(n_s, t_s, B_KV_HEADS, B_HEAD_DIM))
        outs[4].append(v_s.reshape(n_s, t_s, B_KV_HEADS, B_HEAD_DIM))
        outs[5].append(ki_s.reshape(n_s, t_s, IDX_DIM))
        outs[6].append(va_s.reshape(n_s, t_s, A_WIDTH))

    return (xp.reshape(n_p, s_p, d), xs.reshape(n_s, t_s, d)) + tuple(jnp.stack(o) for o in outs)
```

```python
import functools

import numpy as np
import jax
import jax.numpy as jnp
from jax import lax
from jax.experimental import pallas as pl
from jax.experimental.pallas import tpu as pltpu

CHUNK = 64
A_WIDTH = 512
A_GROUPS = 4
A_GROUP_CH = A_WIDTH // A_GROUPS
A_SPAN = 128
B_HEADS = 8
B_HEAD_DIM = 64
B_KV_HEADS = 2
B_GROUP = B_HEADS // B_KV_HEADS
B_WIDTH = B_HEADS * B_HEAD_DIM
IDX_HEADS = 8
IDX_DIM = 64
TOPK_MAX = 256
Q_BLOCK = 128
LN_EPS = 1e-5
LOG2_E = 1.4426950408889634

LANES = 128
VMEM_LIMIT = 52 * 1024 * 1024
TOKEN_TILE = 512
KEY_CHUNK = 512
ATT_CHUNK = 256
DSA_ROWS = 4 * Q_BLOCK
BISECT_STEPS = 14
KEY_ACC_ROWS = 32

F32 = jnp.float32
BF16 = jnp.bfloat16
NT_DIMS = (((1,), (1,)), ((), ()))


def _dot(a, b):
    return jnp.dot(a, b, preferred_element_type=F32)


def _dot_nt(a, b):
    return lax.dot_general(a, b, NT_DIMS, preferred_element_type=F32)


def _reduce_keys(x, reduce):
    part = reduce(x.reshape(-1, KEY_ACC_ROWS, x.shape[-1]), axis=0)
    return reduce(part, axis=0, keepdims=True)


def _layer_norm(x, g, b):
    mu = jnp.mean(x, axis=-1, keepdims=True)
    xc = x - mu
    var = jnp.mean(xc * xc, axis=-1, keepdims=True)
    return xc * lax.rsqrt(var + LN_EPS) * g + b


def _inproj_kernel(x_ref, wa_ref, wq_ref, wkv_ref, wva_ref, wqi_ref, wkw_ref, lnv_ref, lnk_ref, ws_ref, bs_ref,
                   a_ref, q_ref, qi_ref, k_ref, v_ref, kt_ref, vb_ref, ki_ref, kit_ref, wi_ref, *va_refs,
                   span):
    tm = x_ref.shape[0]
    xb = x_ref[...].astype(BF16)

    za = _dot(xb, wa_ref[...])
    ga = 0.5 * za * (1.0 + lax.erf(za * np.float32(np.sqrt(0.5))))
    u = ga[:, :A_WIDTH]
    va = _layer_norm(ga[:, A_WIDTH:], lnv_ref[0:1, :], lnv_ref[1:2, :])
    if va_refs:
        va_refs[0][...] = va
    vab = va.astype(BF16)

    zq = _dot(xb, wq_ref[...])
    zqi = _dot(xb, wqi_ref[...])
    for h in range(B_HEADS):
        q_ref[h] = zq[:, h * B_HEAD_DIM:(h + 1) * B_HEAD_DIM].astype(BF16)
    for h in range(IDX_HEADS):
        qi_ref[h] = zqi[:, h * IDX_DIM:(h + 1) * IDX_DIM].astype(BF16)

    zkv = _dot(xb, wkv_ref[...])
    kvw = B_KV_HEADS * B_HEAD_DIM
    if len(kt_ref.shape) == 4:
        kt_ref, vb_ref, kit_ref = kt_ref.at[0], vb_ref.at[0], kit_ref.at[0]
    zva = _dot(xb, wva_ref[...])
    ones_lane = lax.broadcasted_iota(jnp.int32, (tm, LANES), 1) == B_HEAD_DIM
    for g in range(B_KV_HEADS):
        kg = zkv[:, g * B_HEAD_DIM:(g + 1) * B_HEAD_DIM]
        k_ref[pl.ds(g, tm, stride=B_KV_HEADS), :] = kg
        v_ref[pl.ds(g, tm, stride=B_KV_HEADS), :] = zkv[:, kvw + g * B_HEAD_DIM:kvw + (g + 1) * B_HEAD_DIM]
        kt_ref[g] = kg.T.astype(BF16)
        vb_ref[g] = jnp.where(ones_lane, 1.0, zva[:, g * LANES:(g + 1) * LANES]).astype(BF16)

    zkw = _dot(xb, wkw_ref[...])
    ki = _layer_norm(zkw[:, :IDX_DIM], lnk_ref[0:1, :], lnk_ref[1:2, :])
    ki_ref[...] = ki
    kit_ref[...] = ki.T.astype(BF16)
    wi_ref[...] = zkw[:, IDX_DIM:IDX_DIM + IDX_HEADS] * np.float32(IDX_HEADS ** -0.5)

    for s in range(tm // span):
        rows = slice(s * span, (s + 1) * span)
        for g in range(A_GROUPS):
            cols = slice(g * A_GROUP_CH, (g + 1) * A_GROUP_CH)
            mixed = _dot(ws_ref[g], vab[rows, cols]) + bs_ref[:, cols]
            a_ref[rows, cols] = (u[rows, cols] * mixed).astype(BF16)


def _const_spec(shape):
    nd = len(shape)
    return pl.BlockSpec(shape, lambda *_: (0,) * nd)


def _in_projection(x2, wts, span, tm, emit_va, batch_rows=None):
    t, d = x2.shape
    grid = (t // tm,)
    row = lambda i: (i, 0)
    hm = lambda i: (0, i, 0)
    if batch_rows is None:
        kt_shape, kt_block, kt_map = (B_KV_HEADS, B_HEAD_DIM, t), (B_KV_HEADS, B_HEAD_DIM, tm), lambda i: (0, 0, i)
        vb_shape, vb_block, vb_map = (B_KV_HEADS, t, LANES), (B_KV_HEADS, tm, LANES), hm
        ki_shape, ki_block, ki_map = (IDX_DIM, t), (IDX_DIM, tm), lambda i: (0, i)
    else:
        per, n = batch_rows // tm, t // batch_rows
        kt_shape, kt_block = (n, B_KV_HEADS, B_HEAD_DIM, batch_rows), (1, B_KV_HEADS, B_HEAD_DIM, tm)
        kt_map = lambda i: (i // per, 0, 0, i % per)
        vb_shape, vb_block = (n, B_KV_HEADS, batch_rows, LANES), (1, B_KV_HEADS, tm, LANES)
        vb_map = lambda i: (i // per, 0, i % per, 0)
        ki_shape, ki_block, ki_map = (n, IDX_DIM, batch_rows), (1, IDX_DIM, tm), lambda i: (i // per, 0, i % per)
    in_specs = [pl.BlockSpec((tm, d), row)] + [_const_spec(w.shape) for w in wts]
    out_shape = [
        jax.ShapeDtypeStruct((t, A_WIDTH), BF16),
        jax.ShapeDtypeStruct((B_HEADS, t, B_HEAD_DIM), BF16),
        jax.ShapeDtypeStruct((IDX_HEADS, t, IDX_DIM), BF16),
        jax.ShapeDtypeStruct((t * B_KV_HEADS, B_HEAD_DIM), F32),
        jax.ShapeDtypeStruct((t * B_KV_HEADS, B_HEAD_DIM), F32),
        jax.ShapeDtypeStruct(kt_shape, BF16),
        jax.ShapeDtypeStruct(vb_shape, BF16),
        jax.ShapeDtypeStruct((t, IDX_DIM), F32),
        jax.ShapeDtypeStruct(ki_shape, BF16),
        jax.ShapeDtypeStruct((t, IDX_HEADS), F32),
    ]
    out_specs = [
        pl.BlockSpec((tm, A_WIDTH), row),
        pl.BlockSpec((B_HEADS, tm, B_HEAD_DIM), hm),
        pl.BlockSpec((IDX_HEADS, tm, IDX_DIM), hm),
        pl.BlockSpec((tm * B_KV_HEADS, B_HEAD_DIM), row),
        pl.BlockSpec((tm * B_KV_HEADS, B_HEAD_DIM), row),
        pl.BlockSpec(kt_block, kt_map),
        pl.BlockSpec(vb_block, vb_map),
        pl.BlockSpec((tm, IDX_DIM), row),
        pl.BlockSpec(ki_block, ki_map),
        pl.BlockSpec((tm, IDX_HEADS), row),
    ]
    if emit_va:
        out_shape.append(jax.ShapeDtypeStruct((t, A_WIDTH), F32))
        out_specs.append(pl.BlockSpec((tm, A_WIDTH), row))
    return pl.pallas_call(
        functools.partial(_inproj_kernel, span=span),
        grid=grid, in_specs=in_specs, out_specs=out_specs, out_shape=out_shape,
        compiler_params=pltpu.CompilerParams(dimension_semantics=("parallel",), vmem_limit_bytes=VMEM_LIMIT),
        name="in_projection",
    )(x2, *wts)


def _key_pieces(seg_lens, n_keys, chunk):
    pieces, col = [], 0
    for si, seg_len in enumerate(seg_lens):
        off = 0
        while off < seg_len and col < n_keys:
            width = min(chunk, seg_len - off, n_keys - col)
            pieces.append((col, width, si, off))
            off += width
            col += width
    return pieces


def _dsa_block(q_ref, qi_ref, wi_ref, *refs, segments, qb, tq, rb, row_keys, n_valid, q_off, topk, slopes):
    n_seg = len(segments)
    seg_refs = [refs[3 * i:3 * i + 3] for i in range(n_seg)]
    o_ref, s_ref, d_ref, thr_ref, st_ref = refs[3 * n_seg:]
    seg_lens = [length for length, _ in segments]
    neg_inf = np.float32(-np.inf)
    pos_inf = np.float32(np.inf)
    kf = np.float32(topk)
    n_rb = tq // rb
    assert rb <= LANES and rb % 8 == 0
    blocks = [(slice(r * rb, (r + 1) * rb), row_keys[r]) for r in range(n_rb)]

    for r, (rows, n_keys) in enumerate(blocks):
        q_pos = q_off + qb * tq + rows.start + lax.broadcasted_iota(jnp.int32, (rb, 1), 0)
        qi = qi_ref[:, rows, :].reshape(IDX_HEADS * rb, IDX_DIM)
        wi = wi_ref[rows, :]
        for c0, kc, si, off in _key_pieces(seg_lens, n_keys, KEY_CHUNK):
            k_pos = c0 + lax.broadcasted_iota(jnp.int32, (1, kc), 1)
            lg = _dot(qi, seg_refs[si][2][0, :, off:off + kc].astype(BF16))
            sc = wi[:, 0:1] * jnp.maximum(lg[0:rb], 0.0)
            for h in range(1, IDX_HEADS):
                sc = sc + wi[:, h:h + 1] * jnp.maximum(lg[h * rb:(h + 1) * rb], 0.0)
            k_chunk = k_pos // CHUNK
            if c0 + kc > n_valid:
                k_chunk = jnp.where(k_pos < n_valid, k_chunk, np.int32(2 ** 30))
            adm = k_chunk <= (q_pos // CHUNK)
            sc = jnp.where(adm, sc, neg_inf)
            s_ref[rows, c0:c0 + kc] = sc
            if rb < LANES:
                sc = jnp.concatenate([sc, jnp.full((LANES - rb, kc), neg_inf, F32)], axis=0)
            st_ref[r, c0:c0 + kc, :] = sc.T
            d_ref[rows, c0:c0 + kc] = jnp.abs(q_pos - k_pos).astype(F32)

    def scores_t(r):
        return st_ref[r, :row_keys[r], :]

    def count_ge(r, t):
        return _reduce_keys(jnp.where(scores_t(r) >= t, 1.0, 0.0), jnp.sum)

    lane = lax.broadcasted_iota(jnp.int32, (1, LANES), 1)
    brackets, states, n_adms = [], [], []
    for r in range(n_rb):
        s = scores_t(r)
        q_pos = q_off + qb * tq + r * rb + lane
        n_adm = jnp.minimum((q_pos // CHUNK + 1) * CHUNK, n_valid)
        n_adm = jnp.where(lane < rb, n_adm, 0).astype(F32)
        row_max = _reduce_keys(s, jnp.max)
        row_min = _reduce_keys(jnp.where(s > neg_inf, s, pos_inf), jnp.min)
        brackets.append((row_min, row_max, jnp.full((1, LANES), pos_inf, F32)))
        n_adms.append(n_adm)

    def bisect(_, carry):
        out = []
        for r, (lo, hb, hiv) in enumerate(carry):
            mid = 0.5 * lo + 0.5 * hb
            ge = count_ge(r, mid) >= kf
            out.append((jnp.where(ge, mid, lo), jnp.where(ge, hb, mid), jnp.where(ge, hiv, mid)))
        return tuple(out)

    brackets = lax.fori_loop(0, BISECT_STEPS, bisect, tuple(brackets))

    for (_, _, hiv), n_adm in zip(brackets, n_adms):
        done = jnp.where(n_adm <= kf, 1.0, 0.0)
        states.append((jnp.full((1, LANES), np.finfo(np.float32).min, F32), hiv, done, jnp.zeros((1, LANES), F32)))

    def n_open(states):
        return sum(jnp.sum(1.0 - done) for _, _, done, _ in states)

    def scan_cond(carry):
        _, n_left, it = carry
        return jnp.logical_and(n_left > 0.0, it < max(row_keys))

    def scan_body(carry):
        states, _, it = carry
        out = []
        for r, (thr, hiv, done, n_ge) in enumerate(states):
            sv = scores_t(r)
            cand = _reduce_keys(jnp.where(sv < hiv, sv, neg_inf), jnp.max)
            cnt = count_ge(r, cand)
            found = jnp.where(done > 0.5, 0.0, jnp.where(cnt >= kf, 1.0, 0.0))
            thr = jnp.where(found > 0.5, cand, thr)
            n_ge = jnp.where(found > 0.5, cnt, n_ge)
            done = jnp.maximum(done, found)
            out.append((thr, jnp.where(done > 0.5, hiv, cand), done, n_ge))
        return tuple(out), n_open(out), it + 1

    states, _, _ = lax.while_loop(scan_cond, scan_body, (tuple(states), n_open(states), jnp.int32(0)))

    over = []
    for r, (thr, _, _, n_ge) in enumerate(states):
        thr_ref[blocks[r][0], :] = jnp.broadcast_to(thr, (LANES, LANES)).T[:rb, 0:1]
        over.append(jnp.max(n_ge))
    tied = functools.reduce(jnp.maximum, over) > kf

    ones_row = jnp.where(lax.broadcasted_iota(jnp.int32, (LANES - B_HEAD_DIM, ATT_CHUNK), 0) == 0, 1.0, 0.0)
    ones_row = ones_row.astype(BF16)

    for rows, n_keys in blocks:
        thr = thr_ref[rows, :]

        @pl.when(jnp.logical_not(tied))
        def _():
            d_ref[rows, :n_keys] = jnp.where(s_ref[rows, :n_keys] >= thr, d_ref[rows, :n_keys], pos_inf)

        @pl.when(tied)
        def _():
            n_gt = jnp.sum(jnp.where(s_ref[rows, :n_keys] > thr, 1.0, 0.0), axis=1, keepdims=True)
            room = kf - n_gt
            tri = (lax.broadcasted_iota(jnp.int32, (LANES, LANES), 0)
                   <= lax.broadcasted_iota(jnp.int32, (LANES, LANES), 1))
            tri = jnp.where(tri, 1.0, 0.0).astype(BF16)
            run = jnp.zeros((rb, 1), F32)
            for c0 in range(0, n_keys, LANES):
                blk = s_ref[rows, c0:c0 + LANES]
                eq = jnp.where(blk == thr, 1.0, 0.0)
                prefix = _dot(eq.astype(BF16), tri) + run
                keep = jnp.where(blk > thr, 1.0, jnp.where(prefix <= room, eq, 0.0))
                d_ref[rows, c0:c0 + LANES] = jnp.where(keep > 0.5, d_ref[rows, c0:c0 + LANES], pos_inf)
                run = run + jnp.sum(eq, axis=1, keepdims=True)

        qs = [q_ref[g * B_GROUP:(g + 1) * B_GROUP, rows, :].reshape(B_GROUP * rb, B_HEAD_DIM)
              for g in range(B_KV_HEADS)]
        m_run = [jnp.full((rb, 1), neg_inf, F32) for _ in range(B_HEADS)]
        acc = [jnp.zeros((rb, LANES), F32) for _ in range(B_HEADS)]
        for c0, kc, si, off in _key_pieces(seg_lens, n_keys, ATT_CHUNK):
            kt_ref, v_ref, _ = seg_refs[si]
            dist = d_ref[rows, c0:c0 + kc]
            for g in range(B_KV_HEADS):
                logits = _dot(qs[g], kt_ref[0, g, :, off:off + kc].astype(BF16))
                ps, alphas = [], []
                for hh in range(B_GROUP):
                    h = g * B_GROUP + hh
                    lgt = logits[hh * rb:(hh + 1) * rb] - np.float32(slopes[h] * LOG2_E) * dist
                    m_new = jnp.maximum(m_run[h], jnp.max(lgt, axis=1, keepdims=True))
                    m_ref = jnp.where(m_new == neg_inf, 0.0, m_new)
                    alphas.append(jnp.exp2(m_run[h] - m_ref))
                    ps.append(jnp.exp2(lgt - m_ref).astype(BF16))
                    m_run[h] = m_new
                p = jnp.concatenate(ps, axis=0)
                if segments[si][1]:
                    vt = jnp.concatenate([v_ref[0, g, :, off:off + kc].astype(BF16), ones_row[:, :kc]], axis=0)
                    pv = _dot_nt(p, vt)
                else:
                    pv = _dot(p, v_ref[0, g, off:off + kc, :])
                for hh in range(B_GROUP):
                    h = g * B_GROUP + hh
                    acc[h] = alphas[hh] * acc[h] + pv[hh * rb:(hh + 1) * rb]
        for h in range(B_HEADS):
            out = acc[h][:, :B_HEAD_DIM] / acc[h][:, B_HEAD_DIM:B_HEAD_DIM + 1]
            o_ref[rows, h * B_HEAD_DIM:(h + 1) * B_HEAD_DIM] = out.astype(BF16)


def _dsa_attention(q_hm, qi_hm, wi, key_segments, *, n, t_q, tq, rb, qb, n_valid, q_off, topk, slopes):
    nqb = t_q // tq
    n_rb = tq // rb
    last_chunk = (q_off + (qb + 1) * tq - 1) // CHUNK
    need = min((last_chunk + 1) * CHUNK, n_valid)
    n_keys = -(-need // LANES) * LANES
    if rb % CHUNK == 0 and need == q_off + (qb + 1) * tq:
        row_keys = tuple(n_keys - (n_rb - 1 - r) * rb for r in range(n_rb))
    else:
        row_keys = (n_keys,) * n_rb
    qmap = lambda b: (0, b * nqb + qb, 0)
    in_specs = [
        pl.BlockSpec((B_HEADS, tq, B_HEAD_DIM), qmap),
        pl.BlockSpec((IDX_HEADS, tq, IDX_DIM), qmap),
        pl.BlockSpec((tq, IDX_HEADS), lambda b: (b * nqb + qb, 0)),
    ]
    operands, segments, col = [], [], 0
    for kt, v, kit, v_transposed in key_segments:
        length = min(kt.shape[-1], n_keys - col)
        if length <= 0:
            break
        v_block = (1, B_KV_HEADS, B_HEAD_DIM, length) if v_transposed else (1, B_KV_HEADS, length, LANES)
        in_specs += [
            pl.BlockSpec((1, B_KV_HEADS, B_HEAD_DIM, length), lambda b: (b, 0, 0, 0)),
            pl.BlockSpec(v_block, lambda b: (b, 0, 0, 0)),
            pl.BlockSpec((1, IDX_DIM, length), lambda b: (b, 0, 0)),
        ]
        operands += [kt, v, kit]
        segments.append((length, v_transposed))
        col += length
    assert col == n_keys, (col, n_keys)
    return pl.pallas_call(
        functools.partial(_dsa_block, segments=tuple(segments), qb=qb, tq=tq, rb=rb, row_keys=row_keys,
                          n_valid=n_valid, q_off=q_off, topk=topk, slopes=slopes),
        grid=(n,), in_specs=in_specs,
        out_specs=pl.BlockSpec((tq, B_WIDTH), lambda b: (b, 0)),
        out_shape=jax.ShapeDtypeStruct((n * tq, B_WIDTH), BF16),
        scratch_shapes=[pltpu.VMEM((tq, n_keys), F32), pltpu.VMEM((tq, n_keys), F32), pltpu.VMEM((tq, 1), F32),
                        pltpu.VMEM((n_rb, n_keys, LANES), F32)],
        compiler_params=pltpu.CompilerParams(dimension_semantics=("parallel",), vmem_limit_bytes=VMEM_LIMIT),
        name="dsa_attention",
    )(q_hm, qi_hm, wi, *operands)


def _merge_kernel(x_ref, a_ref, b_ref, wg_ref, wb_ref, wo_ref, ln1_ref, w1_ref, b1_ref, w2_ref, b2_ref, ln2_ref,
                  y_ref, *, alpha):
    d = x_ref.shape[1]
    x = x_ref[...]
    gates = jax.nn.sigmoid(_dot(x.astype(BF16), wg_ref[...]))
    m = gates[:, :d] * _dot(a_ref[...], wb_ref[0]) + gates[:, d:] * _dot(b_ref[...], wb_ref[1])
    h = _layer_norm(alpha * x + _dot(m.astype(BF16), wo_ref[...]), ln1_ref[0:1, :], ln1_ref[1:2, :])
    f = jnp.square(jnp.maximum(_dot(h.astype(BF16), w1_ref[...]) + b1_ref[...], 0.0))
    f = _dot(f.astype(BF16), w2_ref[...]) + b2_ref[...]
    y_ref[...] = _layer_norm(alpha * h + f, ln2_ref[0:1, :], ln2_ref[1:2, :])


def _resident_spec(shape):
    nd = len(shape)
    return pl.BlockSpec(shape, lambda *_: (0,) * nd, pipeline_mode=pl.Buffered(1))


def _merge_ffn(x2, a, b, wts, tm, alpha):
    t, d = x2.shape
    row = lambda i: (i, 0)
    in_specs = [pl.BlockSpec((tm, d), row), pl.BlockSpec((tm, A_WIDTH), row), pl.BlockSpec((tm, B_WIDTH), row)]
    in_specs += [_resident_spec(w.shape) for w in wts]
    return pl.pallas_call(
        functools.partial(_merge_kernel, alpha=np.float32(alpha)),
        grid=(t // tm,), in_specs=in_specs,
        out_specs=pl.BlockSpec((tm, d), row),
        out_shape=jax.ShapeDtypeStruct((t, d), F32),
        compiler_params=pltpu.CompilerParams(dimension_semantics=("parallel",), vmem_limit_bytes=VMEM_LIMIT),
        name="merge_ffn",
    )(x2, a, b, *wts)


def _spatial_weights(w_s, b_s, span):
    pos = jnp.arange(A_SPAN)
    mask = (pos[None, :] // CHUNK) <= (pos[:, None] // CHUNK)
    ws = jnp.where(mask[None], w_s, 0.0)[:, :span, :span].astype(BF16)
    bs = jnp.repeat(b_s[:, :span].T, A_GROUP_CH, axis=1)
    return ws, bs


def _pad_axis(x, axis, size):
    pads = [(0, 0)] * x.ndim
    pads[axis] = (0, size - x.shape[axis])
    return jnp.pad(x, pads)


def kernel(x_prompt, x_sample, cache_k, cache_v, cache_kidx, w_in, lnv_g, lnv_b, w_s, b_s, lnk_g, lnk_b,
           w_branch, w_out, ln1_g, ln1_b, w_ff1, b_ff1, w_ff2, b_ff2, ln2_g, ln2_b):
    depth = w_in.shape[0]
    n_p, s_p, d = x_prompt.shape
    n_s, t_s, _ = x_sample.shape
    past = cache_k.shape[2]
    alpha = (2 * depth) ** 0.25
    slopes = tuple(float(2.0 ** (-8.0 * h / B_HEADS)) for h in range(1, B_HEADS + 1))
    kvw = B_KV_HEADS * B_HEAD_DIM
    c_a = 2 * A_WIDTH
    c_q = c_a + B_WIDTH
    c_k = c_q + kvw
    c_v = c_k + kvw
    c_qi = c_v + IDX_HEADS * IDX_DIM
    c_wi = c_qi + IDX_DIM + IDX_HEADS

    xp = x_prompt.reshape(n_p * s_p, d)
    xs = x_sample.reshape(n_s * t_s, d)
    outs = [[] for _ in range(7)]
    for l in range(depth):
        w = w_in[l]
        pad = jnp.zeros((d, LANES - (c_wi - c_qi)), F32)
        proj_w = (
            w[:, :c_a].astype(BF16),
            (w[:, c_a:c_q] * (LOG2_E * B_HEAD_DIM ** -0.5)).astype(BF16),
            w[:, c_q:c_v].astype(BF16),
            jnp.pad(w[:, c_k:c_v].reshape(d, B_KV_HEADS, B_HEAD_DIM),
                    ((0, 0), (0, 0), (0, LANES - B_HEAD_DIM))).reshape(d, B_KV_HEADS * LANES).astype(BF16),
            (w[:, c_v:c_qi] * (IDX_DIM ** -0.5)).astype(BF16),
            jnp.concatenate([w[:, c_qi:c_wi], pad], axis=1).astype(BF16),
            jnp.stack([lnv_g[l], lnv_b[l]]),
            jnp.stack([lnk_g[l], lnk_b[l]]),
        )
        merge_w = (
            w[:, c_wi:].astype(BF16),
            w_branch[l].astype(BF16),
            w_out[l].astype(BF16),
            jnp.stack([ln1_g[l], ln1_b[l]]),
            w_ff1[l].astype(BF16),
            b_ff1[l][None, :],
            w_ff2[l].astype(BF16),
            b_ff2[l][None, :],
            jnp.stack([ln2_g[l], ln2_b[l]]),
        )

        a_p, q_p, qi_p, k_p, v_p, kt_p, vb_p, ki_p, kit_p, wi_p = _in_projection(
            xp, proj_w + _spatial_weights(w_s[l], b_s[l], A_SPAN), A_SPAN, TOKEN_TILE, False, batch_rows=s_p)
        b_p = [_dsa_attention(
            q_p, qi_p, wi_p, [(kt_p, vb_p, kit_p, False)],
            n=n_p, t_q=s_p, tq=DSA_ROWS, rb=Q_BLOCK, qb=j, n_valid=s_p, q_off=0,
            topk=min(TOPK_MAX, s_p // 4), slopes=slopes).reshape(n_p, DSA_ROWS, B_WIDTH)
            for j in range(s_p // DSA_ROWS)]
        b_p = jnp.stack(b_p, axis=1).reshape(n_p * s_p, B_WIDTH)
        xp = _merge_ffn(xp, a_p, b_p, merge_w, TOKEN_TILE, alpha)

        a_s, q_s, qi_s, k_s, v_s, kt_s, vb_s, ki_s, kit_s, wi_s, va_s = _in_projection(
            xs, proj_w + _spatial_weights(w_s[l], b_s[l], t_s), t_s, n_s * t_s, True)
        n_all = past + t_s
        new_len = -(-t_s // LANES) * LANES
        cache_seg = (jnp.transpose(cache_k[l], (0, 2, 3, 1)), jnp.transpose(cache_v[l], (0, 2, 3, 1)),
                     jnp.transpose(cache_kidx[l], (0, 2, 1)), True)
        kt_new = jnp.transpose(kt_s.reshape(B_KV_HEADS, B_HEAD_DIM, n_s, t_s), (2, 0, 1, 3))
        vb_new = jnp.transpose(vb_s.reshape(B_KV_HEADS, n_s, t_s, LANES), (1, 0, 2, 3))
        kit_new = jnp.transpose(kit_s.reshape(IDX_DIM, n_s, t_s), (1, 0, 2))
        new_seg = (_pad_axis(kt_new, 3, new_len), _pad_axis(vb_new, 2, new_len), _pad_axis(kit_new, 2, new_len), False)
        b_s_ = _dsa_attention(
            q_s, qi_s, wi_s, [cache_seg, new_seg],
            n=n_s, t_q=t_s, tq=t_s, rb=t_s, qb=0, n_valid=n_all, q_off=past,
            topk=min(TOPK_MAX, n_all // 4), slopes=slopes)
        xs = _merge_ffn(xs, a_s, b_s_, merge_w, n_s * t_s, alpha)

        outs[0].append(k_p.reshape(n_p, s_p, B_KV_HEADS, B_HEAD_DIM))
        outs[1].append(v_p.reshape(n_p, s_p, B_KV_HEADS, B_HEAD_DIM))
        outs[2].append(ki_p.reshape(n_p, s_p, IDX_DIM))
        outs[3].append(k_s.reshape(n_s, t_s, B_KV_HEADS, B_HEAD_DIM))
        outs[4].append(v_s.reshape(n_s, t_s, B_KV_HEADS, B_HEAD_DIM))
        outs[5].append(ki_s.reshape(n_s, t_s, IDX_DIM))
        outs[6].append(va_s.reshape(n_s, t_s, A_WIDTH))

    return (xp.reshape(n_p, s_p, d), xs.reshape(n_s, t_s, d)) + tuple(jnp.stack(o) for o in outs)
```

```python
import functools

import numpy as np
import jax
import jax.numpy as jnp
from jax import lax
from jax.experimental import pallas as pl
from jax.experimental.pallas import tpu as pltpu

CHUNK = 64
A_WIDTH = 512
A_GROUPS = 4
A_GROUP_CH = A_WIDTH // A_GROUPS
A_SPAN = 128
B_HEADS = 8
B_HEAD_DIM = 64
B_KV_HEADS = 2
B_GROUP = B_HEADS // B_KV_HEADS
B_WIDTH = B_HEADS * B_HEAD_DIM
IDX_HEADS = 8
IDX_DIM = 64
TOPK_MAX = 256
Q_BLOCK = 128
LN_EPS = 1e-5
LOG2_E = 1.4426950408889634

LANES = 128
VMEM_LIMIT = 52 * 1024 * 1024
TOKEN_TILE = 512
KEY_CHUNK = 512
ATT_CHUNK = 256
ATT_CHUNK_MAX = 1024
DSA_ROWS = 4 * Q_BLOCK
BISECT_STEPS = 14
KEY_ACC_ROWS = 32

F32 = jnp.float32
BF16 = jnp.bfloat16
NT_DIMS = (((1,), (1,)), ((), ()))


def _dot(a, b):
    return jnp.dot(a, b, preferred_element_type=F32)


def _dot_nt(a, b):
    return lax.dot_general(a, b, NT_DIMS, preferred_element_type=F32)


def _reduce_keys(x, reduce):
    part = reduce(x.reshape(-1, KEY_ACC_ROWS, x.shape[-1]), axis=0)
    return reduce(part, axis=0, keepdims=True)


def _layer_norm(x, g, b):
    mu = jnp.mean(x, axis=-1, keepdims=True)
    xc = x - mu
    var = jnp.mean(xc * xc, axis=-1, keepdims=True)
    return xc * lax.rsqrt(var + LN_EPS) * g + b


def _inproj_kernel(x_ref, wa_ref, wq_ref, wkv_ref, wva_ref, wqi_ref, wkw_ref, lnv_ref, lnk_ref, ws_ref, bs_ref,
                   a_ref, q_ref, qi_ref, k_ref, v_ref, kt_ref, vb_ref, ki_ref, kit_ref, wi_ref, *va_refs,
                   span):
    tm = x_ref.shape[0]
    xb = x_ref[...].astype(BF16)

    za = _dot(xb, wa_ref[...])
    ga = 0.5 * za * (1.0 + lax.erf(za * np.float32(np.sqrt(0.5))))
    u = ga[:, :A_WIDTH]
    va = _layer_norm(ga[:, A_WIDTH:], lnv_ref[0:1, :], lnv_ref[1:2, :])
    if va_refs:
        va_refs[0][...] = va
    vab = va.astype(BF16)

    zq = _dot(xb, wq_ref[...])
    zqi = _dot(xb, wqi_ref[...])
    for h in range(B_HEADS):
        q_ref[h] = zq[:, h * B_HEAD_DIM:(h + 1) * B_HEAD_DIM].astype(BF16)
    for h in range(IDX_HEADS):
        qi_ref[h] = zqi[:, h * IDX_DIM:(h + 1) * IDX_DIM].astype(BF16)

    zkv = _dot(xb, wkv_ref[...])
    kvw = B_KV_HEADS * B_HEAD_DIM
    if len(kt_ref.shape) == 4:
        kt_ref, vb_ref, kit_ref = kt_ref.at[0], vb_ref.at[0], kit_ref.at[0]
    zva = _dot(xb, wva_ref[...])
    ones_lane = lax.broadcasted_iota(jnp.int32, (tm, LANES), 1) == B_HEAD_DIM
    for g in range(B_KV_HEADS):
        kg = zkv[:, g * B_HEAD_DIM:(g + 1) * B_HEAD_DIM]
        k_ref[pl.ds(g, tm, stride=B_KV_HEADS), :] = kg
        v_ref[pl.ds(g, tm, stride=B_KV_HEADS), :] = zkv[:, kvw + g * B_HEAD_DIM:kvw + (g + 1) * B_HEAD_DIM]
        kt_ref[g] = kg.T.astype(BF16)
        vb_ref[g] = jnp.where(ones_lane, 1.0, zva[:, g * LANES:(g + 1) * LANES]).astype(BF16)

    zkw = _dot(xb, wkw_ref[...])
    ki = _layer_norm(zkw[:, :IDX_DIM], lnk_ref[0:1, :], lnk_ref[1:2, :])
    ki_ref[...] = ki
    kit_ref[...] = ki.T.astype(BF16)
    wi_ref[...] = zkw[:, IDX_DIM:IDX_DIM + IDX_HEADS] * np.float32(IDX_HEADS ** -0.5)

    for s in range(tm // span):
        rows = slice(s * span, (s + 1) * span)
        for g in range(A_GROUPS):
            cols = slice(g * A_GROUP_CH, (g + 1) * A_GROUP_CH)
            mixed = _dot(ws_ref[g], vab[rows, cols]) + bs_ref[:, cols]
            a_ref[rows, cols] = (u[rows, cols] * mixed).astype(BF16)


def _const_spec(shape):
    nd = len(shape)
    return pl.BlockSpec(shape, lambda *_: (0,) * nd)


def _in_projection(x2, wts, span, tm, emit_va, batch_rows=None):
    t, d = x2.shape
    grid = (t // tm,)
    row = lambda i: (i, 0)
    hm = lambda i: (0, i, 0)
    if batch_rows is None:
        kt_shape, kt_block, kt_map = (B_KV_HEADS, B_HEAD_DIM, t), (B_KV_HEADS, B_HEAD_DIM, tm), lambda i: (0, 0, i)
        vb_shape, vb_block, vb_map = (B_KV_HEADS, t, LANES), (B_KV_HEADS, tm, LANES), hm
        ki_shape, ki_block, ki_map = (IDX_DIM, t), (IDX_DIM, tm), lambda i: (0, i)
    else:
        per, n = batch_rows // tm, t // batch_rows
        kt_shape, kt_block = (n, B_KV_HEADS, B_HEAD_DIM, batch_rows), (1, B_KV_HEADS, B_HEAD_DIM, tm)
        kt_map = lambda i: (i // per, 0, 0, i % per)
        vb_shape, vb_block = (n, B_KV_HEADS, batch_rows, LANES), (1, B_KV_HEADS, tm, LANES)
        vb_map = lambda i: (i // per, 0, i % per, 0)
        ki_shape, ki_block, ki_map = (n, IDX_DIM, batch_rows), (1, IDX_DIM, tm), lambda i: (i // per, 0, i % per)
    in_specs = [pl.BlockSpec((tm, d), row)] + [_const_spec(w.shape) for w in wts]
    out_shape = [
        jax.ShapeDtypeStruct((t, A_WIDTH), BF16),
        jax.ShapeDtypeStruct((B_HEADS, t, B_HEAD_DIM), BF16),
        jax.ShapeDtypeStruct((IDX_HEADS, t, IDX_DIM), BF16),
        jax.ShapeDtypeStruct((t * B_KV_HEADS, B_HEAD_DIM), F32),
        jax.ShapeDtypeStruct((t * B_KV_HEADS, B_HEAD_DIM), F32),
        jax.ShapeDtypeStruct(kt_shape, BF16),
        jax.ShapeDtypeStruct(vb_shape, BF16),
        jax.ShapeDtypeStruct((t, IDX_DIM), F32),
        jax.ShapeDtypeStruct(ki_shape, BF16),
        jax.ShapeDtypeStruct((t, IDX_HEADS), F32),
    ]
    out_specs = [
        pl.BlockSpec((tm, A_WIDTH), row),
        pl.BlockSpec((B_HEADS, tm, B_HEAD_DIM), hm),
        pl.BlockSpec((IDX_HEADS, tm, IDX_DIM), hm),
        pl.BlockSpec((tm * B_KV_HEADS, B_HEAD_DIM), row),
        pl.BlockSpec((tm * B_KV_HEADS, B_HEAD_DIM), row),
        pl.BlockSpec(kt_block, kt_map),
        pl.BlockSpec(vb_block, vb_map),
        pl.BlockSpec((tm, IDX_DIM), row),
        pl.BlockSpec(ki_block, ki_map),
        pl.BlockSpec((tm, IDX_HEADS), row),
    ]
    if emit_va:
        out_shape.append(jax.ShapeDtypeStruct((t, A_WIDTH), F32))
        out_specs.append(pl.BlockSpec((tm, A_WIDTH), row))
    return pl.pallas_call(
        functools.partial(_inproj_kernel, span=span),
        grid=grid, in_specs=in_specs, out_specs=out_specs, out_shape=out_shape,
        compiler_params=pltpu.CompilerParams(dimension_semantics=("parallel",), vmem_limit_bytes=VMEM_LIMIT),
        name="in_projection",
    )(x2, *wts)


def _key_pieces(seg_lens, n_keys, chunk):
    pieces, col = [], 0
    for si, seg_len in enumerate(seg_lens):
        off = 0
        while off < seg_len and col < n_keys:
            width = min(chunk, seg_len - off, n_keys - col)
            pieces.append((col, width, si, off))
            off += width
            col += width
    return pieces


def _dsa_block(q_ref, qi_ref, wi_ref, *refs, segments, qb, tq, rb, row_keys, n_valid, q_off, topk, slopes):
    n_seg = len(segments)
    seg_refs = [refs[3 * i:3 * i + 3] for i in range(n_seg)]
    o_ref, s_ref, d_ref, thr_ref, st_ref = refs[3 * n_seg:]
    seg_lens = [length for length, _ in segments]
    neg_inf = np.float32(-np.inf)
    pos_inf = np.float32(np.inf)
    kf = np.float32(topk)
    n_rb = tq // rb
    assert rb <= LANES and rb % 8 == 0
    blocks = [(slice(r * rb, (r + 1) * rb), row_keys[r]) for r in range(n_rb)]

    for r, (rows, n_keys) in enumerate(blocks):
        q_pos = q_off + qb * tq + rows.start + lax.broadcasted_iota(jnp.int32, (rb, 1), 0)
        qi = qi_ref[:, rows, :].reshape(IDX_HEADS * rb, IDX_DIM)
        wi = wi_ref[rows, :]
        for c0, kc, si, off in _key_pieces(seg_lens, n_keys, KEY_CHUNK):
            k_pos = c0 + lax.broadcasted_iota(jnp.int32, (1, kc), 1)
            lg = _dot(qi, seg_refs[si][2][0, :, off:off + kc].astype(BF16))
            sc = wi[:, 0:1] * jnp.maximum(lg[0:rb], 0.0)
            for h in range(1, IDX_HEADS):
                sc = sc + wi[:, h:h + 1] * jnp.maximum(lg[h * rb:(h + 1) * rb], 0.0)
            k_chunk = k_pos // CHUNK
            if c0 + kc > n_valid:
                k_chunk = jnp.where(k_pos < n_valid, k_chunk, np.int32(2 ** 30))
            adm = k_chunk <= (q_pos // CHUNK)
            sc = jnp.where(adm, sc, neg_inf)
            s_ref[rows, c0:c0 + kc] = sc
            if rb < LANES:
                sc = jnp.concatenate([sc, jnp.full((LANES - rb, kc), neg_inf, F32)], axis=0)
            st_ref[r, c0:c0 + kc, :] = sc.T
            d_ref[rows, c0:c0 + kc] = jnp.abs(q_pos - k_pos).astype(F32)

    def scores_t(r):
        return st_ref[r, :row_keys[r], :]

    def count_ge(r, t):
        return _reduce_keys(jnp.where(scores_t(r) >= t, 1.0, 0.0), jnp.sum)

    lane = lax.broadcasted_iota(jnp.int32, (1, LANES), 1)
    brackets, states, n_adms = [], [], []
    for r in range(n_rb):
        s = scores_t(r)
        q_pos = q_off + qb * tq + r * rb + lane
        n_adm = jnp.minimum((q_pos // CHUNK + 1) * CHUNK, n_valid)
        n_adm = jnp.where(lane < rb, n_adm, 0).astype(F32)
        row_max = _reduce_keys(s, jnp.max)
        row_min = _reduce_keys(jnp.where(s > neg_inf, s, pos_inf), jnp.min)
        brackets.append((row_min, row_max, jnp.full((1, LANES), pos_inf, F32)))
        n_adms.append(n_adm)

    def bisect(_, carry):
        out = []
        for r, (lo, hb, hiv) in enumerate(carry):
            mid = 0.5 * lo + 0.5 * hb
            ge = count_ge(r, mid) >= kf
            out.append((jnp.where(ge, mid, lo), jnp.where(ge, hb, mid), jnp.where(ge, hiv, mid)))
        return tuple(out)

    brackets = lax.fori_loop(0, BISECT_STEPS, bisect, tuple(brackets))

    for (_, _, hiv), n_adm in zip(brackets, n_adms):
        done = jnp.where(n_adm <= kf, 1.0, 0.0)
        states.append((jnp.full((1, LANES), np.finfo(np.float32).min, F32), hiv, done, jnp.zeros((1, LANES), F32)))

    def n_open(states):
        return sum(jnp.sum(1.0 - done) for _, _, done, _ in states)

    def scan_cond(carry):
        _, n_left, it = carry
        return jnp.logical_and(n_left > 0.0, it < max(row_keys))

    def scan_body(carry):
        states, _, it = carry
        out = []
        for r, (thr, hiv, done, n_ge) in enumerate(states):
            sv = scores_t(r)
            cand = _reduce_keys(jnp.where(sv < hiv, sv, neg_inf), jnp.max)
            cnt = count_ge(r, cand)
            found = jnp.where(done > 0.5, 0.0, jnp.where(cnt >= kf, 1.0, 0.0))
            thr = jnp.where(found > 0.5, cand, thr)
            n_ge = jnp.where(found > 0.5, cnt, n_ge)
            done = jnp.maximum(done, found)
            out.append((thr, jnp.where(done > 0.5, hiv, cand), done, n_ge))
        return tuple(out), n_open(out), it + 1

    states, _, _ = lax.while_loop(scan_cond, scan_body, (tuple(states), n_open(states), jnp.int32(0)))

    over = []
    for r, (thr, _, _, n_ge) in enumerate(states):
        thr_ref[blocks[r][0], :] = jnp.broadcast_to(thr, (LANES, LANES)).T[:rb, 0:1]
        over.append(jnp.max(n_ge))
    tied = functools.reduce(jnp.maximum, over) > kf

    att_chunk = min(ATT_CHUNK_MAX, ATT_CHUNK * LANES // rb)
    ones_row = jnp.where(lax.broadcasted_iota(jnp.int32, (LANES - B_HEAD_DIM, att_chunk), 0) == 0, 1.0, 0.0)
    ones_row = ones_row.astype(BF16)

    for rows, n_keys in blocks:
        thr = thr_ref[rows, :]

        @pl.when(jnp.logical_not(tied))
        def _():
            d_ref[rows, :n_keys] = jnp.where(s_ref[rows, :n_keys] >= thr, d_ref[rows, :n_keys], pos_inf)

        @pl.when(tied)
        def _():
            n_gt = jnp.sum(jnp.where(s_ref[rows, :n_keys] > thr, 1.0, 0.0), axis=1, keepdims=True)
            room = kf - n_gt
            tri = (lax.broadcasted_iota(jnp.int32, (LANES, LANES), 0)
                   <= lax.broadcasted_iota(jnp.int32, (LANES, LANES), 1))
            tri = jnp.where(tri, 1.0, 0.0).astype(BF16)
            run = jnp.zeros((rb, 1), F32)
            for c0 in range(0, n_keys, LANES):
                blk = s_ref[rows, c0:c0 + LANES]
                eq = jnp.where(blk == thr, 1.0, 0.0)
                prefix = _dot(eq.astype(BF16), tri) + run
                keep = jnp.where(blk > thr, 1.0, jnp.where(prefix <= room, eq, 0.0))
                d_ref[rows, c0:c0 + LANES] = jnp.where(keep > 0.5, d_ref[rows, c0:c0 + LANES], pos_inf)
                run = run + jnp.sum(eq, axis=1, keepdims=True)

        qs = [q_ref[g * B_GROUP:(g + 1) * B_GROUP, rows, :].reshape(B_GROUP * rb, B_HEAD_DIM)
              for g in range(B_KV_HEADS)]
        m_run = [jnp.full((rb, 1), neg_inf, F32) for _ in range(B_HEADS)]
        acc = [jnp.zeros((rb, LANES), F32) for _ in range(B_HEADS)]
        for c0, kc, si, off in _key_pieces(seg_lens, n_keys, att_chunk):
            kt_ref, v_ref, _ = seg_refs[si]
            dist = d_ref[rows, c0:c0 + kc]
            for g in range(B_KV_HEADS):
                logits = _dot(qs[g], kt_ref[0, g, :, off:off + kc].astype(BF16))
                ps, alphas = [], []
                for hh in range(B_GROUP):
                    h = g * B_GROUP + hh
                    lgt = logits[hh * rb:(hh + 1) * rb] - np.float32(slopes[h] * LOG2_E) * dist
                    m_new = jnp.maximum(m_run[h], jnp.max(lgt, axis=1, keepdims=True))
                    m_ref = jnp.where(m_new == neg_inf, 0.0, m_new)
                    alphas.append(jnp.exp2(m_run[h] - m_ref))
                    ps.append(jnp.exp2(lgt - m_ref).astype(BF16))
                    m_run[h] = m_new
                p = jnp.concatenate(ps, axis=0)
                if segments[si][1]:
                    vt = jnp.concatenate([v_ref[0, g, :, off:off + kc].astype(BF16), ones_row[:, :kc]], axis=0)
                    pv = _dot_nt(p, vt)
                else:
                    pv = _dot(p, v_ref[0, g, off:off + kc, :])
                for hh in range(B_GROUP):
                    h = g * B_GROUP + hh
                    acc[h] = alphas[hh] * acc[h] + pv[hh * rb:(hh + 1) * rb]
        for h in range(B_HEADS):
            out = acc[h][:, :B_HEAD_DIM] / acc[h][:, B_HEAD_DIM:B_HEAD_DIM + 1]
            o_ref[rows, h * B_HEAD_DIM:(h + 1) * B_HEAD_DIM] = out.astype(BF16)


def _dsa_attention(q_hm, qi_hm, wi, key_segments, *, n, t_q, tq, rb, qb, n_valid, q_off, topk, slopes):
    nqb = t_q // tq
    n_rb = tq // rb
    last_chunk = (q_off + (qb + 1) * tq - 1) // CHUNK
    need = min((last_chunk + 1) * CHUNK, n_valid)
    n_keys = -(-need // LANES) * LANES
    if rb % CHUNK == 0 and need == q_off + (qb + 1) * tq:
        row_keys = tuple(n_keys - (n_rb - 1 - r) * rb for r in range(n_rb))
    else:
        row_keys = (n_keys,) * n_rb
    qmap = lambda b: (0, b * nqb + qb, 0)
    in_specs = [
        pl.BlockSpec((B_HEADS, tq, B_HEAD_DIM), qmap),
        pl.BlockSpec((IDX_HEADS, tq, IDX_DIM), qmap),
        pl.BlockSpec((tq, IDX_HEADS), lambda b: (b * nqb + qb, 0)),
    ]
    operands, segments, col = [], [], 0
    for kt, v, kit, v_transposed in key_segments:
        length = min(kt.shape[-1], n_keys - col)
        if length <= 0:
            break
        v_block = (1, B_KV_HEADS, B_HEAD_DIM, length) if v_transposed else (1, B_KV_HEADS, length, LANES)
        in_specs += [
            pl.BlockSpec((1, B_KV_HEADS, B_HEAD_DIM, length), lambda b: (b, 0, 0, 0)),
            pl.BlockSpec(v_block, lambda b: (b, 0, 0, 0)),
            pl.BlockSpec((1, IDX_DIM, length), lambda b: (b, 0, 0)),
        ]
        operands += [kt, v, kit]
        segments.append((length, v_transposed))
        col += length
    assert col == n_keys, (col, n_keys)
    return pl.pallas_call(
        functools.partial(_dsa_block, segments=tuple(segments), qb=qb, tq=tq, rb=rb, row_keys=row_keys,
                          n_valid=n_valid, q_off=q_off, topk=topk, slopes=slopes),
        grid=(n,), in_specs=in_specs,
        out_specs=pl.BlockSpec((tq, B_WIDTH), lambda b: (b, 0)),
        out_shape=jax.ShapeDtypeStruct((n * tq, B_WIDTH), BF16),
        scratch_shapes=[pltpu.VMEM((tq, n_keys), F32), pltpu.VMEM((tq, n_keys), F32), pltpu.VMEM((tq, 1), F32),
                        pltpu.VMEM((n_rb, n_keys, LANES), F32)],
        compiler_params=pltpu.CompilerParams(dimension_semantics=("parallel",), vmem_limit_bytes=VMEM_LIMIT),
        name="dsa_attention",
    )(q_hm, qi_hm, wi, *operands)


def _merge_kernel(x_ref, a_ref, b_ref, wg_ref, wb_ref, wo_ref, ln1_ref, w1_ref, b1_ref, w2_ref, b2_ref, ln2_ref,
                  y_ref, *, alpha):
    d = x_ref.shape[1]
    x = x_ref[...]
    gates = jax.nn.sigmoid(_dot(x.astype(BF16), wg_ref[...]))
    m = gates[:, :d] * _dot(a_ref[...], wb_ref[0]) + gates[:, d:] * _dot(b_ref[...], wb_ref[1])
    h = _layer_norm(alpha * x + _dot(m.astype(BF16), wo_ref[...]), ln1_ref[0:1, :], ln1_ref[1:2, :])
    f = jnp.square(jnp.maximum(_dot(h.astype(BF16), w1_ref[...]) + b1_ref[...], 0.0))
    f = _dot(f.astype(BF16), w2_ref[...]) + b2_ref[...]
    y_ref[...] = _layer_norm(alpha * h + f, ln2_ref[0:1, :], ln2_ref[1:2, :])


def _resident_spec(shape):
    nd = len(shape)
    return pl.BlockSpec(shape, lambda *_: (0,) * nd, pipeline_mode=pl.Buffered(1))


def _merge_ffn(x2, a, b, wts, tm, alpha):
    t, d = x2.shape
    row = lambda i: (i, 0)
    in_specs = [pl.BlockSpec((tm, d), row), pl.BlockSpec((tm, A_WIDTH), row), pl.BlockSpec((tm, B_WIDTH), row)]
    in_specs += [_resident_spec(w.shape) for w in wts]
    return pl.pallas_call(
        functools.partial(_merge_kernel, alpha=np.float32(alpha)),
        grid=(t // tm,), in_specs=in_specs,
        out_specs=pl.BlockSpec((tm, d), row),
        out_shape=jax.ShapeDtypeStruct((t, d), F32),
        compiler_params=pltpu.CompilerParams(dimension_semantics=("parallel",), vmem_limit_bytes=VMEM_LIMIT),
        name="merge_ffn",
    )(x2, a, b, *wts)


def _spatial_weights(w_s, b_s, span):
    pos = jnp.arange(A_SPAN)
    mask = (pos[None, :] // CHUNK) <= (pos[:, None] // CHUNK)
    ws = jnp.where(mask[None], w_s, 0.0)[:, :span, :span].astype(BF16)
    bs = jnp.repeat(b_s[:, :span].T, A_GROUP_CH, axis=1)
    return ws, bs


def _pad_axis(x, axis, size):
    pads = [(0, 0)] * x.ndim
    pads[axis] = (0, size - x.shape[axis])
    return jnp.pad(x, pads)


def kernel(x_prompt, x_sample, cache_k, cache_v, cache_kidx, w_in, lnv_g, lnv_b, w_s, b_s, lnk_g, lnk_b,
           w_branch, w_out, ln1_g, ln1_b, w_ff1, b_ff1, w_ff2, b_ff2, ln2_g, ln2_b):
    depth = w_in.shape[0]
    n_p, s_p, d = x_prompt.shape
    n_s, t_s, _ = x_sample.shape
    past = cache_k.shape[2]
    alpha = (2 * depth) ** 0.25
    slopes = tuple(float(2.0 ** (-8.0 * h / B_HEADS)) for h in range(1, B_HEADS + 1))
    kvw = B_KV_HEADS * B_HEAD_DIM
    c_a = 2 * A_WIDTH
    c_q = c_a + B_WIDTH
    c_k = c_q + kvw
    c_v = c_k + kvw
    c_qi = c_v + IDX_HEADS * IDX_DIM
    c_wi = c_qi + IDX_DIM + IDX_HEADS

    xp = x_prompt.reshape(n_p * s_p, d)
    xs = x_sample.reshape(n_s * t_s, d)
    outs = [[] for _ in range(7)]
    for l in range(depth):
        w = w_in[l]
        pad = jnp.zeros((d, LANES - (c_wi - c_qi)), F32)
        proj_w = (
            w[:, :c_a].astype(BF16),
            (w[:, c_a:c_q] * (LOG2_E * B_HEAD_DIM ** -0.5)).astype(BF16),
            w[:, c_q:c_v].astype(BF16),
            jnp.pad(w[:, c_k:c_v].reshape(d, B_KV_HEADS, B_HEAD_DIM),
                    ((0, 0), (0, 0), (0, LANES - B_HEAD_DIM))).reshape(d, B_KV_HEADS * LANES).astype(BF16),
            (w[:, c_v:c_qi] * (IDX_DIM ** -0.5)).astype(BF16),
            jnp.concatenate([w[:, c_qi:c_wi], pad], axis=1).astype(BF16),
            jnp.stack([lnv_g[l], lnv_b[l]]),
            jnp.stack([lnk_g[l], lnk_b[l]]),
        )
        merge_w = (
            w[:, c_wi:].astype(BF16),
            w_branch[l].astype(BF16),
            w_out[l].astype(BF16),
            jnp.stack([ln1_g[l], ln1_b[l]]),
            w_ff1[l].astype(BF16),
            b_ff1[l][None, :],
            w_ff2[l].astype(BF16),
            b_ff2[l][None, :],
            jnp.stack([ln2_g[l], ln2_b[l]]),
        )

        a_p, q_p, qi_p, k_p, v_p, kt_p, vb_p, ki_p, kit_p, wi_p = _in_projection(
            xp, proj_w + _spatial_weights(w_s[l], b_s[l], A_SPAN), A_SPAN, TOKEN_TILE, False, batch_rows=s_p)
        b_p = [_dsa_attention(
            q_p, qi_p, wi_p, [(kt_p, vb_p, kit_p, False)],
            n=n_p, t_q=s_p, tq=DSA_ROWS, rb=Q_BLOCK, qb=j, n_valid=s_p, q_off=0,
            topk=min(TOPK_MAX, s_p // 4), slopes=slopes).reshape(n_p, DSA_ROWS, B_WIDTH)
            for j in range(s_p // DSA_ROWS)]
        b_p = jnp.stack(b_p, axis=1).reshape(n_p * s_p, B_WIDTH)
        xp = _merge_ffn(xp, a_p, b_p, merge_w, TOKEN_TILE, alpha)

        a_s, q_s, qi_s, k_s, v_s, kt_s, vb_s, ki_s, kit_s, wi_s, va_s = _in_projection(
            xs, proj_w + _spatial_weights(w_s[l], b_s[l], t_s), t_s, n_s * t_s, True)
        n_all = past + t_s
        new_len = -(-t_s // LANES) * LANES
        cache_seg = (jnp.transpose(cache_k[l], (0, 2, 3, 1)), jnp.transpose(cache_v[l], (0, 2, 3, 1)),
                     jnp.transpose(cache_kidx[l], (0, 2, 1)), True)
        kt_new = jnp.transpose(kt_s.reshape(B_KV_HEADS, B_HEAD_DIM, n_s, t_s), (2, 0, 1, 3))
        vb_new = jnp.transpose(vb_s.reshape(B_KV_HEADS, n_s, t_s, LANES), (1, 0, 2, 3))
        kit_new = jnp.transpose(kit_s.reshape(IDX_DIM, n_s, t_s), (1, 0, 2))
        new_seg = (_pad_axis(kt_new, 3, new_len), _pad_axis(vb_new, 2, new_len), _pad_axis(kit_new, 2, new_len), False)
        b_s_ = _dsa_attention(
            q_s, qi_s, wi_s, [cache_seg, new_seg],
            n=n_s, t_q=t_s, tq=t_s, rb=t_s, qb=0, n_valid=n_all, q_off=past,
            topk=min(TOPK_MAX, n_all // 4), slopes=slopes)
        xs = _merge_ffn(xs, a_s, b_s_, merge_w, n_s * t_s, alpha)

        outs[0].append(k_p.reshape(n_p, s_p, B_KV_HEADS, B_HEAD_DIM))
        outs[1].append(v_p.reshape(n_p, s_p, B_KV_HEADS, B_HEAD_DIM))
        outs[2].append(ki_p.reshape(n_p, s_p, IDX_DIM))
        outs[3].append(k_s.reshape(n_s, t_s, B_KV_HEADS, B_HEAD_DIM))
        outs[4].append(v_s.reshape(n_s, t_s, B_KV_HEADS, B_HEAD_DIM))
        outs[5].append(ki_s.reshape(n_s, t_s, IDX_DIM))
        outs[6].append(va_s.reshape(n_s, t_s, A_WIDTH))

    return (xp.reshape(n_p, s_p, d), xs.reshape(n_s, t_s, d)) + tuple(jnp.stack(o) for o in outs)
```

```python
import functools

import numpy as np
import jax
import jax.numpy as jnp
from jax import lax
from jax.experimental import pallas as pl
from jax.experimental.pallas import tpu as pltpu

CHUNK = 64
A_WIDTH = 512
A_GROUPS = 4
A_GROUP_CH = A_WIDTH // A_GROUPS
A_SPAN = 128
B_HEADS = 8
B_HEAD_DIM = 64
B_KV_HEADS = 2
B_GROUP = B_HEADS // B_KV_HEADS
B_WIDTH = B_HEADS * B_HEAD_DIM
IDX_HEADS = 8
IDX_DIM = 64
TOPK_MAX = 256
Q_BLOCK = 128
LN_EPS = 1e-5
LOG2_E = 1.4426950408889634

LANES = 128
VMEM_LIMIT = 52 * 1024 * 1024
TOKEN_TILE = 512
KEY_CHUNK = 512
ATT_CHUNK = 256
ATT_CHUNK_MAX = 1024
DSA_ROWS = 4 * Q_BLOCK
BISECT_STEPS = 14
KEY_ACC_ROWS = 32

F32 = jnp.float32
BF16 = jnp.bfloat16
NT_DIMS = (((1,), (1,)), ((), ()))


def _dot(a, b):
    return jnp.dot(a, b, preferred_element_type=F32)


def _dot_nt(a, b):
    return lax.dot_general(a, b, NT_DIMS, preferred_element_type=F32)


def _reduce_keys(x, reduce):
    part = reduce(x.reshape(-1, KEY_ACC_ROWS, x.shape[-1]), axis=0)
    return reduce(part, axis=0, keepdims=True)


def _layer_norm(x, g, b):
    mu = jnp.mean(x, axis=-1, keepdims=True)
    xc = x - mu
    var = jnp.mean(xc * xc, axis=-1, keepdims=True)
    return xc * lax.rsqrt(var + LN_EPS) * g + b


def _inproj_kernel(x_ref, wa_ref, wq_ref, wkv_ref, wva_ref, wqi_ref, wkw_ref, lnv_ref, lnk_ref, ws_ref, bs_ref,
                   a_ref, q_ref, qi_ref, k_ref, v_ref, kt_ref, vb_ref, ki_ref, kit_ref, wi_ref, *va_refs,
                   span):
    tm = x_ref.shape[0]
    xb = x_ref[...].astype(BF16)

    za = _dot(xb, wa_ref[...])
    ga = 0.5 * za * (1.0 + lax.erf(za * np.float32(np.sqrt(0.5))))
    u = ga[:, :A_WIDTH]
    va = _layer_norm(ga[:, A_WIDTH:], lnv_ref[0:1, :], lnv_ref[1:2, :])
    if va_refs:
        va_refs[0][...] = va
    vab = va.astype(BF16)

    zq = _dot(xb, wq_ref[...])
    zqi = _dot(xb, wqi_ref[...])
    for h in range(B_HEADS):
        q_ref[h] = zq[:, h * B_HEAD_DIM:(h + 1) * B_HEAD_DIM].astype(BF16)
    for h in range(IDX_HEADS):
        qi_ref[h] = zqi[:, h * IDX_DIM:(h + 1) * IDX_DIM].astype(BF16)

    zkv = _dot(xb, wkv_ref[...])
    kvw = B_KV_HEADS * B_HEAD_DIM
    if len(kt_ref.shape) == 4:
        kt_ref, vb_ref, kit_ref = kt_ref.at[0], vb_ref.at[0], kit_ref.at[0]
    zva = _dot(xb, wva_ref[...])
    ones_lane = lax.broadcasted_iota(jnp.int32, (tm, LANES), 1) == B_HEAD_DIM
    for g in range(B_KV_HEADS):
        kg = zkv[:, g * B_HEAD_DIM:(g + 1) * B_HEAD_DIM]
        k_ref[pl.ds(g, tm, stride=B_KV_HEADS), :] = kg
        v_ref[pl.ds(g, tm, stride=B_KV_HEADS), :] = zkv[:, kvw + g * B_HEAD_DIM:kvw + (g + 1) * B_HEAD_DIM]
        kt_ref[g] = kg.T.astype(BF16)
        vb_ref[g] = jnp.where(ones_lane, 1.0, zva[:, g * LANES:(g + 1) * LANES]).astype(BF16)

    zkw = _dot(xb, wkw_ref[...])
    ki = _layer_norm(zkw[:, :IDX_DIM], lnk_ref[0:1, :], lnk_ref[1:2, :])
    ki_ref[...] = ki
    kit_ref[...] = ki.T.astype(BF16)
    wi_ref[...] = zkw[:, IDX_DIM:IDX_DIM + IDX_HEADS] * np.float32(IDX_HEADS ** -0.5)

    for s in range(tm // span):
        rows = slice(s * span, (s + 1) * span)
        for g in range(A_GROUPS):
            cols = slice(g * A_GROUP_CH, (g + 1) * A_GROUP_CH)
            mixed = _dot(ws_ref[g], vab[rows, cols]) + bs_ref[:, cols]
            a_ref[rows, cols] = (u[rows, cols] * mixed).astype(BF16)


def _const_spec(shape):
    nd = len(shape)
    return pl.BlockSpec(shape, lambda *_: (0,) * nd)


def _in_projection(x2, wts, span, tm, emit_va, batch_rows=None):
    t, d = x2.shape
    grid = (t // tm,)
    row = lambda i: (i, 0)
    hm = lambda i: (0, i, 0)
    if batch_rows is None:
        kt_shape, kt_block, kt_map = (B_KV_HEADS, B_HEAD_DIM, t), (B_KV_HEADS, B_HEAD_DIM, tm), lambda i: (0, 0, i)
        vb_shape, vb_block, vb_map = (B_KV_HEADS, t, LANES), (B_KV_HEADS, tm, LANES), hm
        ki_shape, ki_block, ki_map = (IDX_DIM, t), (IDX_DIM, tm), lambda i: (0, i)
    else:
        per, n = batch_rows // tm, t // batch_rows
        kt_shape, kt_block = (n, B_KV_HEADS, B_HEAD_DIM, batch_rows), (1, B_KV_HEADS, B_HEAD_DIM, tm)
        kt_map = lambda i: (i // per, 0, 0, i % per)
        vb_shape, vb_block = (n, B_KV_HEADS, batch_rows, LANES), (1, B_KV_HEADS, tm, LANES)
        vb_map = lambda i: (i // per, 0, i % per, 0)
        ki_shape, ki_block, ki_map = (n, IDX_DIM, batch_rows), (1, IDX_DIM, tm), lambda i: (i // per, 0, i % per)
    in_specs = [pl.BlockSpec((tm, d), row)] + [_const_spec(w.shape) for w in wts]
    out_shape = [
        jax.ShapeDtypeStruct((t, A_WIDTH), BF16),
        jax.ShapeDtypeStruct((B_HEADS, t, B_HEAD_DIM), BF16),
        jax.ShapeDtypeStruct((IDX_HEADS, t, IDX_DIM), BF16),
        jax.ShapeDtypeStruct((t * B_KV_HEADS, B_HEAD_DIM), F32),
        jax.ShapeDtypeStruct((t * B_KV_HEADS, B_HEAD_DIM), F32),
        jax.ShapeDtypeStruct(kt_shape, BF16),
        jax.ShapeDtypeStruct(vb_shape, BF16),
        jax.ShapeDtypeStruct((t, IDX_DIM), F32),
        jax.ShapeDtypeStruct(ki_shape, BF16),
        jax.ShapeDtypeStruct((t, IDX_HEADS), F32),
    ]
    out_specs = [
        pl.BlockSpec((tm, A_WIDTH), row),
        pl.BlockSpec((B_HEADS, tm, B_HEAD_DIM), hm),
        pl.BlockSpec((IDX_HEADS, tm, IDX_DIM), hm),
        pl.BlockSpec((tm * B_KV_HEADS, B_HEAD_DIM), row),
        pl.BlockSpec((tm * B_KV_HEADS, B_HEAD_DIM), row),
        pl.BlockSpec(kt_block, kt_map),
        pl.BlockSpec(vb_block, vb_map),
        pl.BlockSpec((tm, IDX_DIM), row),
        pl.BlockSpec(ki_block, ki_map),
        pl.BlockSpec((tm, IDX_HEADS), row),
    ]
    if emit_va:
        out_shape.append(jax.ShapeDtypeStruct((t, A_WIDTH), F32))
        out_specs.append(pl.BlockSpec((tm, A_WIDTH), row))
    return pl.pallas_call(
        functools.partial(_inproj_kernel, span=span),
        grid=grid, in_specs=in_specs, out_specs=out_specs, out_shape=out_shape,
        compiler_params=pltpu.CompilerParams(dimension_semantics=("parallel",), vmem_limit_bytes=VMEM_LIMIT),
        name="in_projection",
    )(x2, *wts)


def _key_pieces(seg_lens, n_keys, chunk):
    pieces, col = [], 0
    for si, seg_len in enumerate(seg_lens):
        off = 0
        while off < seg_len and col < n_keys:
            width = min(chunk, seg_len - off, n_keys - col)
            pieces.append((col, width, si, off))
            off += width
            col += width
    return pieces


def _dsa_block(q_ref, qi_ref, wi_ref, *refs, segments, qb, tq, rb, row_keys, n_valid, q_off, topk, slopes):
    n_seg = len(segments)
    seg_refs = [refs[3 * i:3 * i + 3] for i in range(n_seg)]
    o_ref, s_ref, d_ref, thr_ref, st_ref = refs[3 * n_seg:]
    seg_lens = [length for length, _ in segments]
    neg_inf = np.float32(-np.inf)
    pos_inf = np.float32(np.inf)
    kf = np.float32(topk)
    n_rb = tq // rb
    assert rb <= LANES and rb % 8 == 0
    blocks = [(slice(r * rb, (r + 1) * rb), row_keys[r]) for r in range(n_rb)]

    for r, (rows, n_keys) in enumerate(blocks):
        q_pos = q_off + qb * tq + rows.start + lax.broadcasted_iota(jnp.int32, (rb, 1), 0)
        qi = qi_ref[:, rows, :].reshape(IDX_HEADS * rb, IDX_DIM)
        wi = wi_ref[rows, :]
        for c0, kc, si, off in _key_pieces(seg_lens, n_keys, KEY_CHUNK):
            k_pos = c0 + lax.broadcasted_iota(jnp.int32, (1, kc), 1)
            lg = _dot(qi, seg_refs[si][2][0, :, off:off + kc].astype(BF16))
            sc = wi[:, 0:1] * jnp.maximum(lg[0:rb], 0.0)
            for h in range(1, IDX_HEADS):
                sc = sc + wi[:, h:h + 1] * jnp.maximum(lg[h * rb:(h + 1) * rb], 0.0)
            k_chunk = k_pos // CHUNK
            if c0 + kc > n_valid:
                k_chunk = jnp.where(k_pos < n_valid, k_chunk, np.int32(2 ** 30))
            adm = k_chunk <= (q_pos // CHUNK)
            sc = jnp.where(adm, sc, neg_inf)
            s_ref[rows, c0:c0 + kc] = sc
            if rb < LANES:
                sc = jnp.concatenate([sc, jnp.full((LANES - rb, kc), neg_inf, F32)], axis=0)
            st_ref[r, c0:c0 + kc, :] = sc.T
            d_ref[rows, c0:c0 + kc] = jnp.abs(q_pos - k_pos).astype(F32)

    def scores_t(r):
        return st_ref[r, :row_keys[r], :]

    def count_ge(r, t):
        return _reduce_keys(jnp.where(scores_t(r) >= t, 1.0, 0.0), jnp.sum)

    lane = lax.broadcasted_iota(jnp.int32, (1, LANES), 1)
    brackets, states, n_adms = [], [], []
    for r in range(n_rb):
        s = scores_t(r)
        q_pos = q_off + qb * tq + r * rb + lane
        n_adm = jnp.minimum((q_pos // CHUNK + 1) * CHUNK, n_valid)
        n_adm = jnp.where(lane < rb, n_adm, 0).astype(F32)
        row_max = _reduce_keys(s, jnp.max)
        row_min = _reduce_keys(jnp.where(s > neg_inf, s, pos_inf), jnp.min)
        brackets.append((row_min, row_max, jnp.full((1, LANES), pos_inf, F32)))
        n_adms.append(n_adm)

    def bisect(_, carry):
        out = []
        for r, (lo, hb, hiv) in enumerate(carry):
            mid = 0.5 * lo + 0.5 * hb
            ge = count_ge(r, mid) >= kf
            out.append((jnp.where(ge, mid, lo), jnp.where(ge, hb, mid), jnp.where(ge, hiv, mid)))
        return tuple(out)

    brackets = lax.fori_loop(0, BISECT_STEPS, bisect, tuple(brackets))

    for (_, _, hiv), n_adm in zip(brackets, n_adms):
        done = jnp.where(n_adm <= kf, 1.0, 0.0)
        states.append((jnp.full((1, LANES), np.finfo(np.float32).min, F32), hiv, done, jnp.zeros((1, LANES), F32)))

    def n_open(states):
        return sum(jnp.sum(1.0 - done) for _, _, done, _ in states)

    def scan_cond(carry):
        _, n_left, it = carry
        return jnp.logical_and(n_left > 0.0, it < max(row_keys))

    def scan_body(carry):
        states, _, it = carry
        out = []
        for r, (thr, hiv, done, n_ge) in enumerate(states):
            sv = scores_t(r)
            cand = _reduce_keys(jnp.where(sv < hiv, sv, neg_inf), jnp.max)
            cnt = count_ge(r, cand)
            found = jnp.where(done > 0.5, 0.0, jnp.where(cnt >= kf, 1.0, 0.0))
            thr = jnp.where(found > 0.5, cand, thr)
            n_ge = jnp.where(found > 0.5, cnt, n_ge)
            done = jnp.maximum(done, found)
            out.append((thr, jnp.where(done > 0.5, hiv, cand), done, n_ge))
        return tuple(out), n_open(out), it + 1

    states, _, _ = lax.while_loop(scan_cond, scan_body, (tuple(states), n_open(states), jnp.int32(0)))

    over = []
    for r, (thr, _, _, n_ge) in enumerate(states):
        thr_ref[blocks[r][0], :] = jnp.broadcast_to(thr, (LANES, LANES)).T[:rb, 0:1]
        over.append(jnp.max(n_ge))
    tied = functools.reduce(jnp.maximum, over) > kf

    att_chunk = min(ATT_CHUNK_MAX, ATT_CHUNK * LANES // rb)
    ones_row = jnp.where(lax.broadcasted_iota(jnp.int32, (LANES - B_HEAD_DIM, att_chunk), 0) == 0, 1.0, 0.0)
    ones_row = ones_row.astype(BF16)

    for rows, n_keys in blocks:
        thr = thr_ref[rows, :]

        @pl.when(jnp.logical_not(tied))
        def _():
            d_ref[rows, :n_keys] = jnp.where(s_ref[rows, :n_keys] >= thr, d_ref[rows, :n_keys], pos_inf)

        @pl.when(tied)
        def _():
            n_gt = jnp.sum(jnp.where(s_ref[rows, :n_keys] > thr, 1.0, 0.0), axis=1, keepdims=True)
            room = kf - n_gt
            tri = (lax.broadcasted_iota(jnp.int32, (LANES, LANES), 0)
                   <= lax.broadcasted_iota(jnp.int32, (LANES, LANES), 1))
            tri = jnp.where(tri, 1.0, 0.0).astype(BF16)
            run = jnp.zeros((rb, 1), F32)
            for c0 in range(0, n_keys, LANES):
                blk = s_ref[rows, c0:c0 + LANES]
                eq = jnp.where(blk == thr, 1.0, 0.0)
                prefix = _dot(eq.astype(BF16), tri) + run
                keep = jnp.where(blk > thr, 1.0, jnp.where(prefix <= room, eq, 0.0))
                d_ref[rows, c0:c0 + LANES] = jnp.where(keep > 0.5, d_ref[rows, c0:c0 + LANES], pos_inf)
                run = run + jnp.sum(eq, axis=1, keepdims=True)

        qs = [q_ref[g * B_GROUP:(g + 1) * B_GROUP, rows, :].reshape(B_GROUP * rb, B_HEAD_DIM)
              for g in range(B_KV_HEADS)]
        m_run = [jnp.full((rb, 1), neg_inf, F32) for _ in range(B_HEADS)]
        acc = [jnp.zeros((rb, LANES), F32) for _ in range(B_HEADS)]
        for c0, kc, si, off in _key_pieces(seg_lens, n_keys, att_chunk):
            kt_ref, v_ref, _ = seg_refs[si]
            dist = d_ref[rows, c0:c0 + kc]
            for g in range(B_KV_HEADS):
                logits = _dot(qs[g], kt_ref[0, g, :, off:off + kc].astype(BF16))
                ps, alphas = [], []
                for hh in range(B_GROUP):
                    h = g * B_GROUP + hh
                    lgt = logits[hh * rb:(hh + 1) * rb] - np.float32(slopes[h] * LOG2_E) * dist
                    m_new = jnp.maximum(m_run[h], jnp.max(lgt, axis=1, keepdims=True))
                    m_ref = jnp.where(m_new == neg_inf, 0.0, m_new)
                    alphas.append(jnp.exp2(m_run[h] - m_ref))
                    ps.append(jnp.exp2(lgt - m_ref).astype(BF16))
                    m_run[h] = m_new
                p = jnp.concatenate(ps, axis=0)
                if segments[si][1]:
                    vt = jnp.concatenate([v_ref[0, g, :, off:off + kc].astype(BF16), ones_row[:, :kc]], axis=0)
                    pv = _dot_nt(p, vt)
                else:
                    pv = _dot(p, v_ref[0, g, off:off + kc, :])
                for hh in range(B_GROUP):
                    h = g * B_GROUP + hh
                    acc[h] = alphas[hh] * acc[h] + pv[hh * rb:(hh + 1) * rb]
        for h in range(B_HEADS):
            out = acc[h][:, :B_HEAD_DIM] / acc[h][:, B_HEAD_DIM:B_HEAD_DIM + 1]
            o_ref[rows, h * B_HEAD_DIM:(h + 1) * B_HEAD_DIM] = out.astype(BF16)


def _dsa_attention(q_hm, qi_hm, wi, key_segments, *, n, t_q, tq, rb, qb, n_valid, q_off, topk, slopes):
    nqb = t_q // tq
    n_rb = tq // rb
    last_chunk = (q_off + (qb + 1) * tq - 1) // CHUNK
    need = min((last_chunk + 1) * CHUNK, n_valid)
    n_keys = -(-need // LANES) * LANES
    if rb % CHUNK == 0 and need == q_off + (qb + 1) * tq:
        row_keys = tuple(n_keys - (n_rb - 1 - r) * rb for r in range(n_rb))
    else:
        row_keys = (n_keys,) * n_rb
    qmap = lambda b: (0, b * nqb + qb, 0)
    in_specs = [
        pl.BlockSpec((B_HEADS, tq, B_HEAD_DIM), qmap),
        pl.BlockSpec((IDX_HEADS, tq, IDX_DIM), qmap),
        pl.BlockSpec((tq, IDX_HEADS), lambda b: (b * nqb + qb, 0)),
    ]
    operands, segments, col = [], [], 0
    for kt, v, kit, v_transposed in key_segments:
        length = min(kt.shape[-1], n_keys - col)
        if length <= 0:
            break
        v_block = (1, B_KV_HEADS, B_HEAD_DIM, length) if v_transposed else (1, B_KV_HEADS, length, LANES)
        in_specs += [
            pl.BlockSpec((1, B_KV_HEADS, B_HEAD_DIM, length), lambda b: (b, 0, 0, 0)),
            pl.BlockSpec(v_block, lambda b: (b, 0, 0, 0)),
            pl.BlockSpec((1, IDX_DIM, length), lambda b: (b, 0, 0)),
        ]
        operands += [kt, v, kit]
        segments.append((length, v_transposed))
        col += length
    assert col == n_keys, (col, n_keys)
    return pl.pallas_call(
        functools.partial(_dsa_block, segments=tuple(segments), qb=qb, tq=tq, rb=rb, row_keys=row_keys,
                          n_valid=n_valid, q_off=q_off, topk=topk, slopes=slopes),
        grid=(n,), in_specs=in_specs,
        out_specs=pl.BlockSpec((tq, B_WIDTH), lambda b: (b, 0)),
        out_shape=jax.ShapeDtypeStruct((n * tq, B_WIDTH), BF16),
        scratch_shapes=[pltpu.VMEM((tq, n_keys), F32), pltpu.VMEM((tq, n_keys), F32), pltpu.VMEM((tq, 1), F32),
                        pltpu.VMEM((n_rb, n_keys, LANES), F32)],
        compiler_params=pltpu.CompilerParams(dimension_semantics=("parallel",), vmem_limit_bytes=VMEM_LIMIT),
        name="dsa_attention",
    )(q_hm, qi_hm, wi, *operands)


def _merge_kernel(x_ref, a_ref, *refs, n_parts, alpha):
    b_refs = refs[:n_parts]
    wg_ref, wb_ref, wo_ref, ln1_ref, w1_ref, b1_ref, w2_ref, b2_ref, ln2_ref, y_ref = refs[n_parts:]
    d = x_ref.shape[1]
    x = x_ref[...]
    gates = jax.nn.sigmoid(_dot(x.astype(BF16), wg_ref[...]))
    b = b_refs[0][...]
    part = pl.program_id(0) % n_parts
    for j in range(1, n_parts):
        b = jnp.where(part == j, b_refs[j][...], b)
    m = gates[:, :d] * _dot(a_ref[...], wb_ref[0]) + gates[:, d:] * _dot(b, wb_ref[1])
    h = _layer_norm(alpha * x + _dot(m.astype(BF16), wo_ref[...]), ln1_ref[0:1, :], ln1_ref[1:2, :])
    f = jnp.square(jnp.maximum(_dot(h.astype(BF16), w1_ref[...]) + b1_ref[...], 0.0))
    f = _dot(f.astype(BF16), w2_ref[...]) + b2_ref[...]
    y_ref[...] = _layer_norm(alpha * h + f, ln2_ref[0:1, :], ln2_ref[1:2, :])


def _resident_spec(shape):
    nd = len(shape)
    return pl.BlockSpec(shape, lambda *_: (0,) * nd, pipeline_mode=pl.Buffered(1))


def _merge_ffn(x2, a, b_parts, wts, tm, alpha):
    t, d = x2.shape
    n_parts = len(b_parts)
    assert all(p.shape == (t // n_parts, B_WIDTH) for p in b_parts)
    row = lambda i: (i, 0)
    in_specs = [pl.BlockSpec((tm, d), row), pl.BlockSpec((tm, A_WIDTH), row)]
    in_specs += [pl.BlockSpec((tm, B_WIDTH), lambda i: (i // n_parts, 0)) for _ in b_parts]
    in_specs += [_resident_spec(w.shape) for w in wts]
    return pl.pallas_call(
        functools.partial(_merge_kernel, n_parts=n_parts, alpha=np.float32(alpha)),
        grid=(t // tm,), in_specs=in_specs,
        out_specs=pl.BlockSpec((tm, d), row),
        out_shape=jax.ShapeDtypeStruct((t, d), F32),
        compiler_params=pltpu.CompilerParams(dimension_semantics=("parallel",), vmem_limit_bytes=VMEM_LIMIT),
        name="merge_ffn",
    )(x2, a, *b_parts, *wts)


def _spatial_weights(w_s, b_s, span):
    pos = jnp.arange(A_SPAN)
    mask = (pos[None, :] // CHUNK) <= (pos[:, None] // CHUNK)
    ws = jnp.where(mask[None], w_s, 0.0)[:, :span, :span].astype(BF16)
    bs = jnp.repeat(b_s[:, :span].T, A_GROUP_CH, axis=1)
    return ws, bs


def _pad_axis(x, axis, size):
    pads = [(0, 0)] * x.ndim
    pads[axis] = (0, size - x.shape[axis])
    return jnp.pad(x, pads)


def kernel(x_prompt, x_sample, cache_k, cache_v, cache_kidx, w_in, lnv_g, lnv_b, w_s, b_s, lnk_g, lnk_b,
           w_branch, w_out, ln1_g, ln1_b, w_ff1, b_ff1, w_ff2, b_ff2, ln2_g, ln2_b):
    depth = w_in.shape[0]
    n_p, s_p, d = x_prompt.shape
    n_s, t_s, _ = x_sample.shape
    past = cache_k.shape[2]
    alpha = (2 * depth) ** 0.25
    slopes = tuple(float(2.0 ** (-8.0 * h / B_HEADS)) for h in range(1, B_HEADS + 1))
    kvw = B_KV_HEADS * B_HEAD_DIM
    c_a = 2 * A_WIDTH
    c_q = c_a + B_WIDTH
    c_k = c_q + kvw
    c_v = c_k + kvw
    c_qi = c_v + IDX_HEADS * IDX_DIM
    c_wi = c_qi + IDX_DIM + IDX_HEADS

    xp = x_prompt.reshape(n_p * s_p, d)
    xs = x_sample.reshape(n_s * t_s, d)
    outs = [[] for _ in range(7)]
    for l in range(depth):
        w = w_in[l]
        pad = jnp.zeros((d, LANES - (c_wi - c_qi)), F32)
        proj_w = (
            w[:, :c_a].astype(BF16),
            (w[:, c_a:c_q] * (LOG2_E * B_HEAD_DIM ** -0.5)).astype(BF16),
            w[:, c_q:c_v].astype(BF16),
            jnp.pad(w[:, c_k:c_v].reshape(d, B_KV_HEADS, B_HEAD_DIM),
                    ((0, 0), (0, 0), (0, LANES - B_HEAD_DIM))).reshape(d, B_KV_HEADS * LANES).astype(BF16),
            (w[:, c_v:c_qi] * (IDX_DIM ** -0.5)).astype(BF16),
            jnp.concatenate([w[:, c_qi:c_wi], pad], axis=1).astype(BF16),
            jnp.stack([lnv_g[l], lnv_b[l]]),
            jnp.stack([lnk_g[l], lnk_b[l]]),
        )
        merge_w = (
            w[:, c_wi:].astype(BF16),
            w_branch[l].astype(BF16),
            w_out[l].astype(BF16),
            jnp.stack([ln1_g[l], ln1_b[l]]),
            w_ff1[l].astype(BF16),
            b_ff1[l][None, :],
            w_ff2[l].astype(BF16),
            b_ff2[l][None, :],
            jnp.stack([ln2_g[l], ln2_b[l]]),
        )

        a_p, q_p, qi_p, k_p, v_p, kt_p, vb_p, ki_p, kit_p, wi_p = _in_projection(
            xp, proj_w + _spatial_weights(w_s[l], b_s[l], A_SPAN), A_SPAN, TOKEN_TILE, False, batch_rows=s_p)
        b_p = [_dsa_attention(
            q_p, qi_p, wi_p, [(kt_p, vb_p, kit_p, False)],
            n=n_p, t_q=s_p, tq=DSA_ROWS, rb=Q_BLOCK, qb=j, n_valid=s_p, q_off=0,
            topk=min(TOPK_MAX, s_p // 4), slopes=slopes)
            for j in range(s_p // DSA_ROWS)]
        assert TOKEN_TILE == DSA_ROWS
        xp = _merge_ffn(xp, a_p, b_p, merge_w, TOKEN_TILE, alpha)

        a_s, q_s, qi_s, k_s, v_s, kt_s, vb_s, ki_s, kit_s, wi_s, va_s = _in_projection(
            xs, proj_w + _spatial_weights(w_s[l], b_s[l], t_s), t_s, n_s * t_s, True)
        n_all = past + t_s
        new_len = -(-t_s // LANES) * LANES
        cache_seg = (jnp.transpose(cache_k[l], (0, 2, 3, 1)), jnp.transpose(cache_v[l], (0, 2, 3, 1)),
                     jnp.transpose(cache_kidx[l], (0, 2, 1)), True)
        kt_new = jnp.transpose(kt_s.reshape(B_KV_HEADS, B_HEAD_DIM, n_s, t_s), (2, 0, 1, 3))
        vb_new = jnp.transpose(vb_s.reshape(B_KV_HEADS, n_s, t_s, LANES), (1, 0, 2, 3))
        kit_new = jnp.transpose(kit_s.reshape(IDX_DIM, n_s, t_s), (1, 0, 2))
        new_seg = (_pad_axis(kt_new, 3, new_len), _pad_axis(vb_new, 2, new_len), _pad_axis(kit_new, 2, new_len), False)
        b_s_ = _dsa_attention(
            q_s, qi_s, wi_s, [cache_seg, new_seg],
            n=n_s, t_q=t_s, tq=t_s, rb=t_s, qb=0, n_valid=n_all, q_off=past,
            topk=min(TOPK_MAX, n_all // 4), slopes=slopes)
        xs = _merge_ffn(xs, a_s, [b_s_], merge_w, n_s * t_s, alpha)

        outs[0].append(k_p.reshape(n_p, s_p, B_KV_HEADS, B_HEAD_DIM))
        outs[1].append(v_p.reshape(n_p, s_p, B_KV_HEADS, B_HEAD_DIM))
        outs[2].append(ki_p.reshape(n_p, s_p, IDX_DIM))
        outs[3].append(k_s.reshape(n_s, t_s, B_KV_HEADS, B_HEAD_DIM))
        outs[4].append(v_s.reshape(n_s, t_s, B_KV_HEADS, B_HEAD_DIM))
        outs[5].append(ki_s.reshape(n_s, t_s, IDX_DIM))
        outs[6].append(va_s.reshape(n_s, t_s, A_WIDTH))

    return (xp.reshape(n_p, s_p, d), xs.reshape(n_s, t_s, d)) + tuple(jnp.stack(o) for o in outs)
```

```python
import functools

import numpy as np
import jax
import jax.numpy as jnp
from jax import lax
from jax.experimental import pallas as pl
from jax.experimental.pallas import tpu as pltpu

CHUNK = 64
A_WIDTH = 512
A_GROUPS = 4
A_GROUP_CH = A_WIDTH // A_GROUPS
A_SPAN = 128
B_HEADS = 8
B_HEAD_DIM = 64
B_KV_HEADS = 2
B_GROUP = B_HEADS // B_KV_HEADS
B_WIDTH = B_HEADS * B_HEAD_DIM
IDX_HEADS = 8
IDX_DIM = 64
TOPK_MAX = 256
Q_BLOCK = 128
LN_EPS = 1e-5
LOG2_E = 1.4426950408889634

LANES = 128
VMEM_LIMIT = 52 * 1024 * 1024
TOKEN_TILE = 512
KEY_CHUNK = 512
ATT_CHUNK = 256
ATT_CHUNK_MAX = 1024
DSA_ROWS = 4 * Q_BLOCK
SAMPLE_BATCHES = 4
BISECT_STEPS = 14
KEY_ACC_ROWS = 32

F32 = jnp.float32
BF16 = jnp.bfloat16
NT_DIMS = (((1,), (1,)), ((), ()))


def _dot(a, b):
    return jnp.dot(a, b, preferred_element_type=F32)


def _dot_nt(a, b):
    return lax.dot_general(a, b, NT_DIMS, preferred_element_type=F32)


def _reduce_keys(x, reduce):
    part = reduce(x.reshape(-1, KEY_ACC_ROWS, x.shape[-1]), axis=0)
    return reduce(part, axis=0, keepdims=True)


def _layer_norm(x, g, b):
    mu = jnp.mean(x, axis=-1, keepdims=True)
    xc = x - mu
    var = jnp.mean(xc * xc, axis=-1, keepdims=True)
    return xc * lax.rsqrt(var + LN_EPS) * g + b


def _inproj_kernel(x_ref, wa_ref, wq_ref, wkv_ref, wva_ref, wqi_ref, wkw_ref, lnv_ref, lnk_ref, ws_ref, bs_ref,
                   a_ref, q_ref, qi_ref, k_ref, v_ref, kt_ref, vb_ref, ki_ref, kit_ref, wi_ref, *va_refs,
                   span):
    tm = x_ref.shape[0]
    xb = x_ref[...].astype(BF16)

    za = _dot(xb, wa_ref[...])
    ga = 0.5 * za * (1.0 + lax.erf(za * np.float32(np.sqrt(0.5))))
    u = ga[:, :A_WIDTH]
    va = _layer_norm(ga[:, A_WIDTH:], lnv_ref[0:1, :], lnv_ref[1:2, :])
    if va_refs:
        va_refs[0][...] = va
    vab = va.astype(BF16)

    zq = _dot(xb, wq_ref[...])
    zqi = _dot(xb, wqi_ref[...])
    for h in range(B_HEADS):
        q_ref[h] = zq[:, h * B_HEAD_DIM:(h + 1) * B_HEAD_DIM].astype(BF16)
    for h in range(IDX_HEADS):
        qi_ref[h] = zqi[:, h * IDX_DIM:(h + 1) * IDX_DIM].astype(BF16)

    zkv = _dot(xb, wkv_ref[...])
    kvw = B_KV_HEADS * B_HEAD_DIM
    if len(kt_ref.shape) == 4:
        kt_ref, vb_ref, kit_ref = kt_ref.at[0], vb_ref.at[0], kit_ref.at[0]
    zva = _dot(xb, wva_ref[...])
    ones_lane = lax.broadcasted_iota(jnp.int32, (tm, LANES), 1) == B_HEAD_DIM
    for g in range(B_KV_HEADS):
        kg = zkv[:, g * B_HEAD_DIM:(g + 1) * B_HEAD_DIM]
        k_ref[pl.ds(g, tm, stride=B_KV_HEADS), :] = kg
        v_ref[pl.ds(g, tm, stride=B_KV_HEADS), :] = zkv[:, kvw + g * B_HEAD_DIM:kvw + (g + 1) * B_HEAD_DIM]
        kt_ref[g] = kg.T.astype(BF16)
        vb_ref[g] = jnp.where(ones_lane, 1.0, zva[:, g * LANES:(g + 1) * LANES]).astype(BF16)

    zkw = _dot(xb, wkw_ref[...])
    ki = _layer_norm(zkw[:, :IDX_DIM], lnk_ref[0:1, :], lnk_ref[1:2, :])
    ki_ref[...] = ki
    kit_ref[...] = ki.T.astype(BF16)
    wi_ref[...] = zkw[:, IDX_DIM:IDX_DIM + IDX_HEADS] * np.float32(IDX_HEADS ** -0.5)

    for s in range(tm // span):
        rows = slice(s * span, (s + 1) * span)
        for g in range(A_GROUPS):
            cols = slice(g * A_GROUP_CH, (g + 1) * A_GROUP_CH)
            mixed = _dot(ws_ref[g], vab[rows, cols]) + bs_ref[:, cols]
            a_ref[rows, cols] = (u[rows, cols] * mixed).astype(BF16)


def _const_spec(shape):
    nd = len(shape)
    return pl.BlockSpec(shape, lambda *_: (0,) * nd)


def _in_projection(x2, wts, span, tm, emit_va, batch_rows=None):
    t, d = x2.shape
    grid = (t // tm,)
    row = lambda i: (i, 0)
    hm = lambda i: (0, i, 0)
    if batch_rows is None:
        kt_shape, kt_block, kt_map = (B_KV_HEADS, B_HEAD_DIM, t), (B_KV_HEADS, B_HEAD_DIM, tm), lambda i: (0, 0, i)
        vb_shape, vb_block, vb_map = (B_KV_HEADS, t, LANES), (B_KV_HEADS, tm, LANES), hm
        ki_shape, ki_block, ki_map = (IDX_DIM, t), (IDX_DIM, tm), lambda i: (0, i)
    else:
        per, n = batch_rows // tm, t // batch_rows
        kt_shape, kt_block = (n, B_KV_HEADS, B_HEAD_DIM, batch_rows), (1, B_KV_HEADS, B_HEAD_DIM, tm)
        kt_map = lambda i: (i // per, 0, 0, i % per)
        vb_shape, vb_block = (n, B_KV_HEADS, batch_rows, LANES), (1, B_KV_HEADS, tm, LANES)
        vb_map = lambda i: (i // per, 0, i % per, 0)
        ki_shape, ki_block, ki_map = (n, IDX_DIM, batch_rows), (1, IDX_DIM, tm), lambda i: (i // per, 0, i % per)
    in_specs = [pl.BlockSpec((tm, d), row)] + [_const_spec(w.shape) for w in wts]
    out_shape = [
        jax.ShapeDtypeStruct((t, A_WIDTH), BF16),
        jax.ShapeDtypeStruct((B_HEADS, t, B_HEAD_DIM), BF16),
        jax.ShapeDtypeStruct((IDX_HEADS, t, IDX_DIM), BF16),
        jax.ShapeDtypeStruct((t * B_KV_HEADS, B_HEAD_DIM), F32),
        jax.ShapeDtypeStruct((t * B_KV_HEADS, B_HEAD_DIM), F32),
        jax.ShapeDtypeStruct(kt_shape, BF16),
        jax.ShapeDtypeStruct(vb_shape, BF16),
        jax.ShapeDtypeStruct((t, IDX_DIM), F32),
        jax.ShapeDtypeStruct(ki_shape, BF16),
        jax.ShapeDtypeStruct((t, IDX_HEADS), F32),
    ]
    out_specs = [
        pl.BlockSpec((tm, A_WIDTH), row),
        pl.BlockSpec((B_HEADS, tm, B_HEAD_DIM), hm),
        pl.BlockSpec((IDX_HEADS, tm, IDX_DIM), hm),
        pl.BlockSpec((tm * B_KV_HEADS, B_HEAD_DIM), row),
        pl.BlockSpec((tm * B_KV_HEADS, B_HEAD_DIM), row),
        pl.BlockSpec(kt_block, kt_map),
        pl.BlockSpec(vb_block, vb_map),
        pl.BlockSpec((tm, IDX_DIM), row),
        pl.BlockSpec(ki_block, ki_map),
        pl.BlockSpec((tm, IDX_HEADS), row),
    ]
    if emit_va:
        out_shape.append(jax.ShapeDtypeStruct((t, A_WIDTH), F32))
        out_specs.append(pl.BlockSpec((tm, A_WIDTH), row))
    return pl.pallas_call(
        functools.partial(_inproj_kernel, span=span),
        grid=grid, in_specs=in_specs, out_specs=out_specs, out_shape=out_shape,
        compiler_params=pltpu.CompilerParams(dimension_semantics=("parallel",), vmem_limit_bytes=VMEM_LIMIT),
        name="in_projection",
    )(x2, *wts)


def _key_pieces(seg_lens, n_keys, chunk):
    pieces, col = [], 0
    for si, seg_len in enumerate(seg_lens):
        off = 0
        while off < seg_len and col < n_keys:
            width = min(chunk, seg_len - off, n_keys - col)
            pieces.append((col, width, si, off))
            off += width
            col += width
    return pieces


def _dsa_block(q_ref, qi_ref, wi_ref, *refs, segments, qb, tq, rb, row_keys, block_batches, n_valid, q_off, topk,
               slopes):
    n_seg = len(segments)
    seg_refs = [refs[3 * i:3 * i + 3] for i in range(n_seg)]
    o_ref, s_ref, d_ref, thr_ref, st_ref = refs[3 * n_seg:]
    seg_lens = [length for length, _ in segments]
    neg_inf = np.float32(-np.inf)
    pos_inf = np.float32(np.inf)
    kf = np.float32(topk)
    n_rb = tq // rb
    assert LANES % rb == 0 and rb % 8 == 0
    blocks = [(slice(r * rb, (r + 1) * rb), row_keys[r]) for r in range(n_rb)]
    batch_of = [r if block_batches else 0 for r in range(n_rb)]
    pos_of = [q_off if block_batches else q_off + qb * tq + r * rb for r in range(n_rb)]

    for r, (rows, n_keys) in enumerate(blocks):
        q_pos = pos_of[r] + lax.broadcasted_iota(jnp.int32, (rb, 1), 0)
        qi = qi_ref[:, rows, :].reshape(IDX_HEADS * rb, IDX_DIM)
        wi = wi_ref[rows, :]
        for c0, kc, si, off in _key_pieces(seg_lens, n_keys, KEY_CHUNK):
            k_pos = c0 + lax.broadcasted_iota(jnp.int32, (1, kc), 1)
            lg = _dot(qi, seg_refs[si][2][batch_of[r], :, off:off + kc].astype(BF16))
            sc = wi[:, 0:1] * jnp.maximum(lg[0:rb], 0.0)
            for h in range(1, IDX_HEADS):
                sc = sc + wi[:, h:h + 1] * jnp.maximum(lg[h * rb:(h + 1) * rb], 0.0)
            k_chunk = k_pos // CHUNK
            if c0 + kc > n_valid:
                k_chunk = jnp.where(k_pos < n_valid, k_chunk, np.int32(2 ** 30))
            adm = k_chunk <= (q_pos // CHUNK)
            sc = jnp.where(adm, sc, neg_inf)
            s_ref[rows, c0:c0 + kc] = sc
            d_ref[rows, c0:c0 + kc] = jnp.abs(q_pos - k_pos).astype(F32)

    groups = []
    for g0 in range(0, tq, LANES):
        g_rows = min(LANES, tq - g0)
        g_keys = {row_keys[r] for r in range(g0 // rb, (g0 + g_rows) // rb)}
        assert len(g_keys) == 1
        groups.append((g0, g_rows, g_keys.pop()))
    for g, (g0, g_rows, g_keys) in enumerate(groups):
        for c0 in range(0, g_keys, KEY_CHUNK):
            kc = min(KEY_CHUNK, g_keys - c0)
            sc = s_ref[g0:g0 + g_rows, c0:c0 + kc]
            if g_rows < LANES:
                sc = jnp.concatenate([sc, jnp.full((LANES - g_rows, kc), neg_inf, F32)], axis=0)
            st_ref[g, c0:c0 + kc, :] = sc.T

    def scores_t(g):
        return st_ref[g, :groups[g][2], :]

    def count_ge(r, t):
        return _reduce_keys(jnp.where(scores_t(r) >= t, 1.0, 0.0), jnp.sum)

    lane = lax.broadcasted_iota(jnp.int32, (1, LANES), 1)
    brackets, states, n_adms = [], [], []
    for g, (g0, g_rows, _) in enumerate(groups):
        s = scores_t(g)
        row = g0 + lane
        q_pos = (q_off + row % rb) if block_batches else (q_off + qb * tq + row)
        n_adm = jnp.minimum((q_pos // CHUNK + 1) * CHUNK, n_valid)
        n_adm = jnp.where(lane < g_rows, n_adm, 0).astype(F32)
        row_max = _reduce_keys(s, jnp.max)
        row_min = _reduce_keys(jnp.where(s > neg_inf, s, pos_inf), jnp.min)
        brackets.append((row_min, row_max, jnp.full((1, LANES), pos_inf, F32)))
        n_adms.append(n_adm)

    def bisect(_, carry):
        out = []
        for r, (lo, hb, hiv) in enumerate(carry):
            mid = 0.5 * lo + 0.5 * hb
            ge = count_ge(r, mid) >= kf
            out.append((jnp.where(ge, mid, lo), jnp.where(ge, hb, mid), jnp.where(ge, hiv, mid)))
        return tuple(out)

    brackets = lax.fori_loop(0, BISECT_STEPS, bisect, tuple(brackets))

    for (_, _, hiv), n_adm in zip(brackets, n_adms):
        done = jnp.where(n_adm <= kf, 1.0, 0.0)
        states.append((jnp.full((1, LANES), np.finfo(np.float32).min, F32), hiv, done, jnp.zeros((1, LANES), F32)))

    def n_open(states):
        return sum(jnp.sum(1.0 - done) for _, _, done, _ in states)

    def scan_cond(carry):
        _, n_left, it = carry
        return jnp.logical_and(n_left > 0.0, it < max(row_keys))

    def scan_body(carry):
        states, _, it = carry
        out = []
        for r, (thr, hiv, done, n_ge) in enumerate(states):
            sv = scores_t(r)
            cand = _reduce_keys(jnp.where(sv < hiv, sv, neg_inf), jnp.max)
            cnt = count_ge(r, cand)
            found = jnp.where(done > 0.5, 0.0, jnp.where(cnt >= kf, 1.0, 0.0))
            thr = jnp.where(found > 0.5, cand, thr)
            n_ge = jnp.where(found > 0.5, cnt, n_ge)
            done = jnp.maximum(done, found)
            out.append((thr, jnp.where(done > 0.5, hiv, cand), done, n_ge))
        return tuple(out), n_open(out), it + 1

    states, _, _ = lax.while_loop(scan_cond, scan_body, (tuple(states), n_open(states), jnp.int32(0)))

    over = []
    for r, (thr, _, _, n_ge) in enumerate(states):
        g0, g_rows, _ = groups[r]
        thr_ref[g0:g0 + g_rows, :] = jnp.broadcast_to(thr, (LANES, LANES)).T[:g_rows, 0:1]
        over.append(jnp.max(n_ge))
    tied = functools.reduce(jnp.maximum, over) > kf

    att_chunk = min(ATT_CHUNK_MAX, ATT_CHUNK * LANES // rb)
    ones_row = jnp.where(lax.broadcasted_iota(jnp.int32, (LANES - B_HEAD_DIM, att_chunk), 0) == 0, 1.0, 0.0)
    ones_row = ones_row.astype(BF16)

    for r, (rows, n_keys) in enumerate(blocks):
        bi = batch_of[r]
        thr = thr_ref[rows, :]

        @pl.when(jnp.logical_not(tied))
        def _():
            d_ref[rows, :n_keys] = jnp.where(s_ref[rows, :n_keys] >= thr, d_ref[rows, :n_keys], pos_inf)

        @pl.when(tied)
        def _():
            n_gt = jnp.sum(jnp.where(s_ref[rows, :n_keys] > thr, 1.0, 0.0), axis=1, keepdims=True)
            room = kf - n_gt
            tri = (lax.broadcasted_iota(jnp.int32, (LANES, LANES), 0)
                   <= lax.broadcasted_iota(jnp.int32, (LANES, LANES), 1))
            tri = jnp.where(tri, 1.0, 0.0).astype(BF16)
            run = jnp.zeros((rb, 1), F32)
            for c0 in range(0, n_keys, LANES):
                blk = s_ref[rows, c0:c0 + LANES]
                eq = jnp.where(blk == thr, 1.0, 0.0)
                prefix = _dot(eq.astype(BF16), tri) + run
                keep = jnp.where(blk > thr, 1.0, jnp.where(prefix <= room, eq, 0.0))
                d_ref[rows, c0:c0 + LANES] = jnp.where(keep > 0.5, d_ref[rows, c0:c0 + LANES], pos_inf)
                run = run + jnp.sum(eq, axis=1, keepdims=True)

        qs = [q_ref[g * B_GROUP:(g + 1) * B_GROUP, rows, :].reshape(B_GROUP * rb, B_HEAD_DIM)
              for g in range(B_KV_HEADS)]
        m_run = [jnp.full((rb, 1), neg_inf, F32) for _ in range(B_HEADS)]
        acc = [jnp.zeros((rb, LANES), F32) for _ in range(B_HEADS)]
        for c0, kc, si, off in _key_pieces(seg_lens, n_keys, att_chunk):
            kt_ref, v_ref, _ = seg_refs[si]
            dist = d_ref[rows, c0:c0 + kc]
            for g in range(B_KV_HEADS):
                logits = _dot(qs[g], kt_ref[bi, g, :, off:off + kc].astype(BF16))
                ps, alphas = [], []
                for hh in range(B_GROUP):
                    h = g * B_GROUP + hh
                    lgt = logits[hh * rb:(hh + 1) * rb] - np.float32(slopes[h] * LOG2_E) * dist
                    m_new = jnp.maximum(m_run[h], jnp.max(lgt, axis=1, keepdims=True))
                    m_ref = jnp.where(m_new == neg_inf, 0.0, m_new)
                    alphas.append(jnp.exp2(m_run[h] - m_ref))
                    ps.append(jnp.exp2(lgt - m_ref).astype(BF16))
                    m_run[h] = m_new
                p = jnp.concatenate(ps, axis=0)
                if segments[si][1]:
                    vt = jnp.concatenate([v_ref[bi, g, :, off:off + kc].astype(BF16), ones_row[:, :kc]], axis=0)
                    pv = _dot_nt(p, vt)
                else:
                    pv = _dot(p, v_ref[bi, g, off:off + kc, :])
                for hh in range(B_GROUP):
                    h = g * B_GROUP + hh
                    acc[h] = alphas[hh] * acc[h] + pv[hh * rb:(hh + 1) * rb]
        for h in range(B_HEADS):
            out = acc[h][:, :B_HEAD_DIM] / acc[h][:, B_HEAD_DIM:B_HEAD_DIM + 1]
            o_ref[rows, h * B_HEAD_DIM:(h + 1) * B_HEAD_DIM] = out.astype(BF16)


def _dsa_attention(q_hm, qi_hm, wi, key_segments, *, n, t_q, tq, rb, qb, n_valid, q_off, topk, slopes,
                   batches_per_step=1):
    nqb = t_q // tq
    nb = batches_per_step
    assert nb == 1 or (nqb == 1 and rb == tq and n % nb == 0)
    n_rb = nb * tq // rb
    last_chunk = (q_off + (qb + 1) * tq - 1) // CHUNK
    need = min((last_chunk + 1) * CHUNK, n_valid)
    n_keys = -(-need // LANES) * LANES
    if rb % CHUNK == 0 and need == q_off + (qb + 1) * tq:
        row_keys = tuple(n_keys - (n_rb - 1 - r) * rb for r in range(n_rb))
    else:
        row_keys = (n_keys,) * n_rb
    qmap = lambda b: (0, b * nqb + qb, 0)
    in_specs = [
        pl.BlockSpec((B_HEADS, nb * tq, B_HEAD_DIM), qmap),
        pl.BlockSpec((IDX_HEADS, nb * tq, IDX_DIM), qmap),
        pl.BlockSpec((nb * tq, IDX_HEADS), lambda b: (b * nqb + qb, 0)),
    ]
    operands, segments, col = [], [], 0
    for kt, v, kit, v_transposed in key_segments:
        length = min(kt.shape[-1], n_keys - col)
        if length <= 0:
            break
        v_block = (nb, B_KV_HEADS, B_HEAD_DIM, length) if v_transposed else (nb, B_KV_HEADS, length, LANES)
        in_specs += [
            pl.BlockSpec((nb, B_KV_HEADS, B_HEAD_DIM, length), lambda b: (b, 0, 0, 0)),
            pl.BlockSpec(v_block, lambda b: (b, 0, 0, 0)),
            pl.BlockSpec((nb, IDX_DIM, length), lambda b: (b, 0, 0)),
        ]
        operands += [kt, v, kit]
        segments.append((length, v_transposed))
        col += length
    assert col == n_keys, (col, n_keys)
    return pl.pallas_call(
        functools.partial(_dsa_block, segments=tuple(segments), qb=qb, tq=nb * tq, rb=rb, row_keys=row_keys,
                          block_batches=nb > 1, n_valid=n_valid, q_off=q_off, topk=topk, slopes=slopes),
        grid=(n // nb,), in_specs=in_specs,
        out_specs=pl.BlockSpec((nb * tq, B_WIDTH), lambda b: (b, 0)),
        out_shape=jax.ShapeDtypeStruct((n * tq, B_WIDTH), BF16),
        scratch_shapes=[pltpu.VMEM((nb * tq, n_keys), F32), pltpu.VMEM((nb * tq, n_keys), F32),
                        pltpu.VMEM((nb * tq, 1), F32), pltpu.VMEM((-(-nb * tq // LANES), n_keys, LANES), F32)],
        compiler_params=pltpu.CompilerParams(dimension_semantics=("parallel",), vmem_limit_bytes=VMEM_LIMIT),
        name="dsa_attention",
    )(q_hm, qi_hm, wi, *operands)


def _merge_kernel(x_ref, a_ref, *refs, n_parts, alpha):
    b_refs = refs[:n_parts]
    wg_ref, wb_ref, wo_ref, ln1_ref, w1_ref, b1_ref, w2_ref, b2_ref, ln2_ref, y_ref = refs[n_parts:]
    d = x_ref.shape[1]
    x = x_ref[...]
    gates = jax.nn.sigmoid(_dot(x.astype(BF16), wg_ref[...]))
    b = b_refs[0][...]
    part = pl.program_id(0) % n_parts
    for j in range(1, n_parts):
        b = jnp.where(part == j, b_refs[j][...], b)
    m = gates[:, :d] * _dot(a_ref[...], wb_ref[0]) + gates[:, d:] * _dot(b, wb_ref[1])
    h = _layer_norm(alpha * x + _dot(m.astype(BF16), wo_ref[...]), ln1_ref[0:1, :], ln1_ref[1:2, :])
    f = jnp.square(jnp.maximum(_dot(h.astype(BF16), w1_ref[...]) + b1_ref[...], 0.0))
    f = _dot(f.astype(BF16), w2_ref[...]) + b2_ref[...]
    y_ref[...] = _layer_norm(alpha * h + f, ln2_ref[0:1, :], ln2_ref[1:2, :])


def _resident_spec(shape):
    nd = len(shape)
    return pl.BlockSpec(shape, lambda *_: (0,) * nd, pipeline_mode=pl.Buffered(1))


def _merge_ffn(x2, a, b_parts, wts, tm, alpha):
    t, d = x2.shape
    n_parts = len(b_parts)
    assert all(p.shape == (t // n_parts, B_WIDTH) for p in b_parts)
    row = lambda i: (i, 0)
    in_specs = [pl.BlockSpec((tm, d), row), pl.BlockSpec((tm, A_WIDTH), row)]
    in_specs += [pl.BlockSpec((tm, B_WIDTH), lambda i: (i // n_parts, 0)) for _ in b_parts]
    in_specs += [_resident_spec(w.shape) for w in wts]
    return pl.pallas_call(
        functools.partial(_merge_kernel, n_parts=n_parts, alpha=np.float32(alpha)),
        grid=(t // tm,), in_specs=in_specs,
        out_specs=pl.BlockSpec((tm, d), row),
        out_shape=jax.ShapeDtypeStruct((t, d), F32),
        compiler_params=pltpu.CompilerParams(dimension_semantics=("parallel",), vmem_limit_bytes=VMEM_LIMIT),
        name="merge_ffn",
    )(x2, a, *b_parts, *wts)


def _spatial_weights(w_s, b_s, span):
    pos = jnp.arange(A_SPAN)
    mask = (pos[None, :] // CHUNK) <= (pos[:, None] // CHUNK)
    ws = jnp.where(mask[None], w_s, 0.0)[:, :span, :span].astype(BF16)
    bs = jnp.repeat(b_s[:, :span].T, A_GROUP_CH, axis=1)
    return ws, bs


def _pad_axis(x, axis, size):
    pads = [(0, 0)] * x.ndim
    pads[axis] = (0, size - x.shape[axis])
    return jnp.pad(x, pads)


def kernel(x_prompt, x_sample, cache_k, cache_v, cache_kidx, w_in, lnv_g, lnv_b, w_s, b_s, lnk_g, lnk_b,
           w_branch, w_out, ln1_g, ln1_b, w_ff1, b_ff1, w_ff2, b_ff2, ln2_g, ln2_b):
    depth = w_in.shape[0]
    n_p, s_p, d = x_prompt.shape
    n_s, t_s, _ = x_sample.shape
    past = cache_k.shape[2]
    alpha = (2 * depth) ** 0.25
    slopes = tuple(float(2.0 ** (-8.0 * h / B_HEADS)) for h in range(1, B_HEADS + 1))
    kvw = B_KV_HEADS * B_HEAD_DIM
    c_a = 2 * A_WIDTH
    c_q = c_a + B_WIDTH
    c_k = c_q + kvw
    c_v = c_k + kvw
    c_qi = c_v + IDX_HEADS * IDX_DIM
    c_wi = c_qi + IDX_DIM + IDX_HEADS

    xp = x_prompt.reshape(n_p * s_p, d)
    xs = x_sample.reshape(n_s * t_s, d)
    outs = [[] for _ in range(7)]
    for l in range(depth):
        w = w_in[l]
        pad = jnp.zeros((d, LANES - (c_wi - c_qi)), F32)
        proj_w = (
            w[:, :c_a].astype(BF16),
            (w[:, c_a:c_q] * (LOG2_E * B_HEAD_DIM ** -0.5)).astype(BF16),
            w[:, c_q:c_v].astype(BF16),
            jnp.pad(w[:, c_k:c_v].reshape(d, B_KV_HEADS, B_HEAD_DIM),
                    ((0, 0), (0, 0), (0, LANES - B_HEAD_DIM))).reshape(d, B_KV_HEADS * LANES).astype(BF16),
            (w[:, c_v:c_qi] * (IDX_DIM ** -0.5)).astype(BF16),
            jnp.concatenate([w[:, c_qi:c_wi], pad], axis=1).astype(BF16),
            jnp.stack([lnv_g[l], lnv_b[l]]),
            jnp.stack([lnk_g[l], lnk_b[l]]),
        )
        merge_w = (
            w[:, c_wi:].astype(BF16),
            w_branch[l].astype(BF16),
            w_out[l].astype(BF16),
            jnp.stack([ln1_g[l], ln1_b[l]]),
            w_ff1[l].astype(BF16),
            b_ff1[l][None, :],
            w_ff2[l].astype(BF16),
            b_ff2[l][None, :],
            jnp.stack([ln2_g[l], ln2_b[l]]),
        )

        a_p, q_p, qi_p, k_p, v_p, kt_p, vb_p, ki_p, kit_p, wi_p = _in_projection(
            xp, proj_w + _spatial_weights(w_s[l], b_s[l], A_SPAN), A_SPAN, TOKEN_TILE, False, batch_rows=s_p)
        b_p = [_dsa_attention(
            q_p, qi_p, wi_p, [(kt_p, vb_p, kit_p, False)],
            n=n_p, t_q=s_p, tq=DSA_ROWS, rb=Q_BLOCK, qb=j, n_valid=s_p, q_off=0,
            topk=min(TOPK_MAX, s_p // 4), slopes=slopes)
            for j in range(s_p // DSA_ROWS)]
        assert TOKEN_TILE == DSA_ROWS
        xp = _merge_ffn(xp, a_p, b_p, merge_w, TOKEN_TILE, alpha)

        a_s, q_s, qi_s, k_s, v_s, kt_s, vb_s, ki_s, kit_s, wi_s, va_s = _in_projection(
            xs, proj_w + _spatial_weights(w_s[l], b_s[l], t_s), t_s, n_s * t_s, True)
        n_all = past + t_s
        new_len = -(-t_s // LANES) * LANES
        cache_seg = (jnp.transpose(cache_k[l], (0, 2, 3, 1)), jnp.transpose(cache_v[l], (0, 2, 3, 1)),
                     jnp.transpose(cache_kidx[l], (0, 2, 1)), True)
        kt_new = jnp.transpose(kt_s.reshape(B_KV_HEADS, B_HEAD_DIM, n_s, t_s), (2, 0, 1, 3))
        vb_new = jnp.transpose(vb_s.reshape(B_KV_HEADS, n_s, t_s, LANES), (1, 0, 2, 3))
        kit_new = jnp.transpose(kit_s.reshape(IDX_DIM, n_s, t_s), (1, 0, 2))
        new_seg = (_pad_axis(kt_new, 3, new_len), _pad_axis(vb_new, 2, new_len), _pad_axis(kit_new, 2, new_len), False)
        b_s_ = _dsa_attention(
            q_s, qi_s, wi_s, [cache_seg, new_seg],
            n=n_s, t_q=t_s, tq=t_s, rb=t_s, qb=0, n_valid=n_all, q_off=past,
            topk=min(TOPK_MAX, n_all // 4), slopes=slopes, batches_per_step=SAMPLE_BATCHES)
        xs = _merge_ffn(xs, a_s, [b_s_], merge_w, n_s * t_s, alpha)

        outs[0].append(k_p.reshape(n_p, s_p, B_KV_HEADS, B_HEAD_DIM))
        outs[1].append(v_p.reshape(n_p, s_p, B_KV_HEADS, B_HEAD_DIM))
        outs[2].append(ki_p.reshape(n_p, s_p, IDX_DIM))
        outs[3].append(k_s.reshape(n_s, t_s, B_KV_HEADS, B_HEAD_DIM))
        outs[4].append(v_s.reshape(n_s, t_s, B_KV_HEADS, B_HEAD_DIM))
        outs[5].append(ki_s.reshape(n_s, t_s, IDX_DIM))
        outs[6].append(va_s.reshape(n_s, t_s, A_WIDTH))

    return (xp.reshape(n_p, s_p, d), xs.reshape(n_s, t_s, d)) + tuple(jnp.stack(o) for o in outs)
```

```python
import functools

import numpy as np
import jax
import jax.numpy as jnp
from jax import lax
from jax.experimental import pallas as pl
from jax.experimental.pallas import tpu as pltpu

CHUNK = 64
A_WIDTH = 512
A_GROUPS = 4
A_GROUP_CH = A_WIDTH // A_GROUPS
A_SPAN = 128
B_HEADS = 8
B_HEAD_DIM = 64
B_KV_HEADS = 2
B_GROUP = B_HEADS // B_KV_HEADS
B_WIDTH = B_HEADS * B_HEAD_DIM
IDX_HEADS = 8
IDX_DIM = 64
TOPK_MAX = 256
Q_BLOCK = 128
LN_EPS = 1e-5
LOG2_E = 1.4426950408889634

LANES = 128
VMEM_LIMIT = 52 * 1024 * 1024
TOKEN_TILE = 512
KEY_CHUNK = 512
ATT_CHUNK = 256
ATT_CHUNK_MAX = 1024
DSA_ROWS = 4 * Q_BLOCK
SAMPLE_BATCHES = 4
BISECT_STEPS = 14
KEY_ACC_ROWS = 32

F32 = jnp.float32
BF16 = jnp.bfloat16
NT_DIMS = (((1,), (1,)), ((), ()))


def _dot(a, b):
    return jnp.dot(a, b, preferred_element_type=F32)


def _dot_nt(a, b):
    return lax.dot_general(a, b, NT_DIMS, preferred_element_type=F32)


def _reduce_keys(x, reduce):
    part = reduce(x.reshape(-1, KEY_ACC_ROWS, x.shape[-1]), axis=0)
    return reduce(part, axis=0, keepdims=True)


def _layer_norm(x, g, b):
    mu = jnp.mean(x, axis=-1, keepdims=True)
    xc = x - mu
    var = jnp.mean(xc * xc, axis=-1, keepdims=True)
    return xc * lax.rsqrt(var + LN_EPS) * g + b


def _inproj_kernel(x_ref, wa_ref, wq_ref, wkv_ref, wva_ref, wqi_ref, wkw_ref, lnv_ref, lnk_ref, ws_ref, bs_ref,
                   a_ref, q_ref, qi_ref, k_ref, v_ref, kt_ref, vb_ref, ki_ref, kit_ref, wi_ref, *va_refs,
                   span):
    tm = x_ref.shape[0]
    xb = x_ref[...].astype(BF16)

    za = _dot(xb, wa_ref[...])
    ga = 0.5 * za * (1.0 + lax.erf(za * np.float32(np.sqrt(0.5))))
    u = ga[:, :A_WIDTH]
    va = _layer_norm(ga[:, A_WIDTH:], lnv_ref[0:1, :], lnv_ref[1:2, :])
    if va_refs:
        va_refs[0][...] = va
    vab = va.astype(BF16)

    zq = _dot(xb, wq_ref[...])
    zqi = _dot(xb, wqi_ref[...])
    for h in range(B_HEADS):
        q_ref[h] = zq[:, h * B_HEAD_DIM:(h + 1) * B_HEAD_DIM].astype(BF16)
    for h in range(IDX_HEADS):
        qi_ref[h] = zqi[:, h * IDX_DIM:(h + 1) * IDX_DIM].astype(BF16)

    zkv = _dot(xb, wkv_ref[...])
    kvw = B_KV_HEADS * B_HEAD_DIM
    if len(kt_ref.shape) == 4:
        kt_ref, vb_ref, kit_ref = kt_ref.at[0], vb_ref.at[0], kit_ref.at[0]
    zva = _dot(xb, wva_ref[...])
    ones_lane = lax.broadcasted_iota(jnp.int32, (tm, LANES), 1) == B_HEAD_DIM
    for g in range(B_KV_HEADS):
        kg = zkv[:, g * B_HEAD_DIM:(g + 1) * B_HEAD_DIM]
        k_ref[pl.ds(g, tm, stride=B_KV_HEADS), :] = kg
        v_ref[pl.ds(g, tm, stride=B_KV_HEADS), :] = zkv[:, kvw + g * B_HEAD_DIM:kvw + (g + 1) * B_HEAD_DIM]
        kt_ref[g] = kg.T.astype(BF16)
        vb_ref[g] = jnp.where(ones_lane, 1.0, zva[:, g * LANES:(g + 1) * LANES]).astype(BF16)

    zkw = _dot(xb, wkw_ref[...])
    ki = _layer_norm(zkw[:, :IDX_DIM], lnk_ref[0:1, :], lnk_ref[1:2, :])
    ki_ref[...] = ki
    kit_ref[...] = ki.T.astype(BF16)
    wi_ref[...] = zkw[:, IDX_DIM:IDX_DIM + IDX_HEADS] * np.float32(IDX_HEADS ** -0.5)

    for s in range(tm // span):
        rows = slice(s * span, (s + 1) * span)
        for g in range(A_GROUPS):
            cols = slice(g * A_GROUP_CH, (g + 1) * A_GROUP_CH)
            mixed = _dot(ws_ref[g], vab[rows, cols]) + bs_ref[:, cols]
            a_ref[rows, cols] = (u[rows, cols] * mixed).astype(BF16)


def _const_spec(shape):
    nd = len(shape)
    return pl.BlockSpec(shape, lambda *_: (0,) * nd)


def _in_projection(x2, wts, span, tm, emit_va, batch_rows=None):
    t, d = x2.shape
    grid = (t // tm,)
    row = lambda i: (i, 0)
    hm = lambda i: (0, i, 0)
    if batch_rows is None:
        kt_shape, kt_block, kt_map = (B_KV_HEADS, B_HEAD_DIM, t), (B_KV_HEADS, B_HEAD_DIM, tm), lambda i: (0, 0, i)
        vb_shape, vb_block, vb_map = (B_KV_HEADS, t, LANES), (B_KV_HEADS, tm, LANES), hm
        ki_shape, ki_block, ki_map = (IDX_DIM, t), (IDX_DIM, tm), lambda i: (0, i)
    else:
        per, n = batch_rows // tm, t // batch_rows
        kt_shape, kt_block = (n, B_KV_HEADS, B_HEAD_DIM, batch_rows), (1, B_KV_HEADS, B_HEAD_DIM, tm)
        kt_map = lambda i: (i // per, 0, 0, i % per)
        vb_shape, vb_block = (n, B_KV_HEADS, batch_rows, LANES), (1, B_KV_HEADS, tm, LANES)
        vb_map = lambda i: (i // per, 0, i % per, 0)
        ki_shape, ki_block, ki_map = (n, IDX_DIM, batch_rows), (1, IDX_DIM, tm), lambda i: (i // per, 0, i % per)
    in_specs = [pl.BlockSpec((tm, d), row)] + [_const_spec(w.shape) for w in wts]
    out_shape = [
        jax.ShapeDtypeStruct((t, A_WIDTH), BF16),
        jax.ShapeDtypeStruct((B_HEADS, t, B_HEAD_DIM), BF16),
        jax.ShapeDtypeStruct((IDX_HEADS, t, IDX_DIM), BF16),
        jax.ShapeDtypeStruct((t * B_KV_HEADS, B_HEAD_DIM), F32),
        jax.ShapeDtypeStruct((t * B_KV_HEADS, B_HEAD_DIM), F32),
        jax.ShapeDtypeStruct(kt_shape, BF16),
        jax.ShapeDtypeStruct(vb_shape, BF16),
        jax.ShapeDtypeStruct((t, IDX_DIM), F32),
        jax.ShapeDtypeStruct(ki_shape, BF16),
        jax.ShapeDtypeStruct((t, IDX_HEADS), F32),
    ]
    out_specs = [
        pl.BlockSpec((tm, A_WIDTH), row),
        pl.BlockSpec((B_HEADS, tm, B_HEAD_DIM), hm),
        pl.BlockSpec((IDX_HEADS, tm, IDX_DIM), hm),
        pl.BlockSpec((tm * B_KV_HEADS, B_HEAD_DIM), row),
        pl.BlockSpec((tm * B_KV_HEADS, B_HEAD_DIM), row),
        pl.BlockSpec(kt_block, kt_map),
        pl.BlockSpec(vb_block, vb_map),
        pl.BlockSpec((tm, IDX_DIM), row),
        pl.BlockSpec(ki_block, ki_map),
        pl.BlockSpec((tm, IDX_HEADS), row),
    ]
    if emit_va:
        out_shape.append(jax.ShapeDtypeStruct((t, A_WIDTH), F32))
        out_specs.append(pl.BlockSpec((tm, A_WIDTH), row))
    return pl.pallas_call(
        functools.partial(_inproj_kernel, span=span),
        grid=grid, in_specs=in_specs, out_specs=out_specs, out_shape=out_shape,
        compiler_params=pltpu.CompilerParams(dimension_semantics=("parallel",), vmem_limit_bytes=VMEM_LIMIT),
        name="in_projection",
    )(x2, *wts)


def _key_pieces(seg_lens, n_keys, chunk):
    pieces, col = [], 0
    for si, seg_len in enumerate(seg_lens):
        off = 0
        while off < seg_len and col < n_keys:
            width = min(chunk, seg_len - off, n_keys - col)
            pieces.append((col, width, si, off))
            off += width
            col += width
    return pieces


def _dsa_block(q_ref, qi_ref, wi_ref, *refs, segments, qb, tq, rb, row_keys, block_batches, n_valid, q_off, topk,
               slopes):
    n_seg = len(segments)
    seg_refs = [refs[3 * i:3 * i + 3] for i in range(n_seg)]
    o_ref, s_ref, d_ref, thr_ref, st_ref = refs[3 * n_seg:]
    seg_lens = [length for length, _ in segments]
    neg_inf = np.float32(-np.inf)
    pos_inf = np.float32(np.inf)
    kf = np.float32(topk)
    n_rb = tq // rb
    assert LANES % rb == 0 and rb % 8 == 0
    blocks = [(slice(r * rb, (r + 1) * rb), row_keys[r]) for r in range(n_rb)]
    batch_of = [r if block_batches else 0 for r in range(n_rb)]
    pos_of = [q_off if block_batches else q_off + qb * tq + r * rb for r in range(n_rb)]

    for r, (rows, n_keys) in enumerate(blocks):
        q_pos = pos_of[r] + lax.broadcasted_iota(jnp.int32, (rb, 1), 0)
        qi = qi_ref[:, rows, :].reshape(IDX_HEADS * rb, IDX_DIM)
        wi = wi_ref[rows, :]
        for c0, kc, si, off in _key_pieces(seg_lens, n_keys, KEY_CHUNK):
            k_pos = c0 + lax.broadcasted_iota(jnp.int32, (1, kc), 1)
            lg = _dot(qi, seg_refs[si][2][batch_of[r], :, off:off + kc].astype(BF16))
            sc = wi[:, 0:1] * jnp.maximum(lg[0:rb], 0.0)
            for h in range(1, IDX_HEADS):
                sc = sc + wi[:, h:h + 1] * jnp.maximum(lg[h * rb:(h + 1) * rb], 0.0)
            k_chunk = k_pos // CHUNK
            if c0 + kc > n_valid:
                k_chunk = jnp.where(k_pos < n_valid, k_chunk, np.int32(2 ** 30))
            adm = k_chunk <= (q_pos // CHUNK)
            sc = jnp.where(adm, sc, neg_inf)
            s_ref[rows, c0:c0 + kc] = sc
            d_ref[rows, c0:c0 + kc] = jnp.abs(q_pos - k_pos).astype(F32)

    groups = []
    for g0 in range(0, tq, LANES):
        g_rows = min(LANES, tq - g0)
        g_keys = {row_keys[r] for r in range(g0 // rb, (g0 + g_rows) // rb)}
        assert len(g_keys) == 1
        groups.append((g0, g_rows, g_keys.pop()))
    for g, (g0, g_rows, g_keys) in enumerate(groups):
        for c0 in range(0, g_keys, KEY_CHUNK):
            kc = min(KEY_CHUNK, g_keys - c0)
            sc = s_ref[g0:g0 + g_rows, c0:c0 + kc]
            if g_rows < LANES:
                sc = jnp.concatenate([sc, jnp.full((LANES - g_rows, kc), neg_inf, F32)], axis=0)
            st_ref[g, c0:c0 + kc, :] = sc.T

    def scores_t(g):
        return st_ref[g, :groups[g][2], :]

    def count_ge(r, t):
        return _reduce_keys(jnp.where(scores_t(r) >= t, 1.0, 0.0), jnp.sum)

    lane = lax.broadcasted_iota(jnp.int32, (1, LANES), 1)
    brackets, states, n_adms = [], [], []
    for g, (g0, g_rows, _) in enumerate(groups):
        s = scores_t(g)
        row = g0 + lane
        q_pos = (q_off + row % rb) if block_batches else (q_off + qb * tq + row)
        n_adm = jnp.minimum((q_pos // CHUNK + 1) * CHUNK, n_valid)
        n_adm = jnp.where(lane < g_rows, n_adm, 0).astype(F32)
        row_max = _reduce_keys(s, jnp.max)
        row_min = _reduce_keys(jnp.where(s > neg_inf, s, pos_inf), jnp.min)
        brackets.append((row_min, row_max, jnp.full((1, LANES), pos_inf, F32)))
        n_adms.append(n_adm)

    def bisect(_, carry):
        out = []
        for r, (lo, hb, hiv) in enumerate(carry):
            mid = 0.5 * lo + 0.5 * hb
            ge = count_ge(r, mid) >= kf
            out.append((jnp.where(ge, mid, lo), jnp.where(ge, hb, mid), jnp.where(ge, hiv, mid)))
        return tuple(out)

    brackets = lax.fori_loop(0, BISECT_STEPS, bisect, tuple(brackets))

    for (_, _, hiv), n_adm in zip(brackets, n_adms):
        done = jnp.where(n_adm <= kf, 1.0, 0.0)
        states.append((jnp.full((1, LANES), np.finfo(np.float32).min, F32), hiv, done, jnp.zeros((1, LANES), F32)))

    def n_open(states):
        return sum(jnp.sum(1.0 - done) for _, _, done, _ in states)

    def scan_cond(carry):
        _, n_left, it = carry
        return jnp.logical_and(n_left > 0.0, it < max(row_keys))

    def scan_body(carry):
        states, _, it = carry
        out = []
        for r, (thr, hiv, done, n_ge) in enumerate(states):
            sv = scores_t(r)
            cand = _reduce_keys(jnp.where(sv < hiv, sv, neg_inf), jnp.max)
            cnt = count_ge(r, cand)
            found = jnp.where(done > 0.5, 0.0, jnp.where(cnt >= kf, 1.0, 0.0))
            thr = jnp.where(found > 0.5, cand, thr)
            n_ge = jnp.where(found > 0.5, cnt, n_ge)
            done = jnp.maximum(done, found)
            out.append((thr, jnp.where(done > 0.5, hiv, cand), done, n_ge))
        return tuple(out), n_open(out), it + 1

    states, _, _ = lax.while_loop(scan_cond, scan_body, (tuple(states), n_open(states), jnp.int32(0)))

    over = []
    for r, (thr, _, _, n_ge) in enumerate(states):
        g0, g_rows, _ = groups[r]
        thr_ref[g0:g0 + g_rows, :] = jnp.broadcast_to(thr, (LANES, LANES)).T[:g_rows, 0:1]
        over.append(jnp.max(n_ge))
    tied = functools.reduce(jnp.maximum, over) > kf

    att_chunk = min(ATT_CHUNK_MAX, ATT_CHUNK * LANES // rb)
    ones_row = jnp.where(lax.broadcasted_iota(jnp.int32, (LANES - B_HEAD_DIM, att_chunk), 0) == 0, 1.0, 0.0)
    ones_row = ones_row.astype(BF16)

    for r, (rows, n_keys) in enumerate(blocks):
        bi = batch_of[r]
        thr = thr_ref[rows, :]

        @pl.when(jnp.logical_not(tied))
        def _():
            d_ref[rows, :n_keys] = jnp.where(s_ref[rows, :n_keys] >= thr, d_ref[rows, :n_keys], pos_inf)

        @pl.when(tied)
        def _():
            n_gt = jnp.sum(jnp.where(s_ref[rows, :n_keys] > thr, 1.0, 0.0), axis=1, keepdims=True)
            room = kf - n_gt
            tri = (lax.broadcasted_iota(jnp.int32, (LANES, LANES), 0)
                   <= lax.broadcasted_iota(jnp.int32, (LANES, LANES), 1))
            tri = jnp.where(tri, 1.0, 0.0).astype(BF16)
            run = jnp.zeros((rb, 1), F32)
            for c0 in range(0, n_keys, LANES):
                blk = s_ref[rows, c0:c0 + LANES]
                eq = jnp.where(blk == thr, 1.0, 0.0)
                prefix = _dot(eq.astype(BF16), tri) + run
                keep = jnp.where(blk > thr, 1.0, jnp.where(prefix <= room, eq, 0.0))
                d_ref[rows, c0:c0 + LANES] = jnp.where(keep > 0.5, d_ref[rows, c0:c0 + LANES], pos_inf)
                run = run + jnp.sum(eq, axis=1, keepdims=True)

        qs = [q_ref[g * B_GROUP:(g + 1) * B_GROUP, rows, :].reshape(B_GROUP * rb, B_HEAD_DIM)
              for g in range(B_KV_HEADS)]
        m_run = [jnp.full((rb, 1), np.finfo(np.float32).min, F32) for _ in range(B_HEADS)]
        acc = [jnp.zeros((rb, LANES), F32) for _ in range(B_HEADS)]
        for c0, kc, si, off in _key_pieces(seg_lens, n_keys, att_chunk):
            kt_ref, v_ref, _ = seg_refs[si]
            dist = d_ref[rows, c0:c0 + kc]
            for g in range(B_KV_HEADS):
                logits = _dot(qs[g], kt_ref[bi, g, :, off:off + kc].astype(BF16))
                ps, alphas = [], []
                for hh in range(B_GROUP):
                    h = g * B_GROUP + hh
                    lgt = logits[hh * rb:(hh + 1) * rb] - np.float32(slopes[h] * LOG2_E) * dist
                    m_new = jnp.maximum(m_run[h], jnp.max(lgt, axis=1, keepdims=True))
                    alphas.append(jnp.exp2(m_run[h] - m_new))
                    ps.append(jnp.exp2(lgt - m_new).astype(BF16))
                    m_run[h] = m_new
                p = jnp.concatenate(ps, axis=0)
                if segments[si][1]:
                    vt = jnp.concatenate([v_ref[bi, g, :, off:off + kc].astype(BF16), ones_row[:, :kc]], axis=0)
                    pv = _dot_nt(p, vt)
                else:
                    pv = _dot(p, v_ref[bi, g, off:off + kc, :])
                for hh in range(B_GROUP):
                    h = g * B_GROUP + hh
                    acc[h] = alphas[hh] * acc[h] + pv[hh * rb:(hh + 1) * rb]
        for h in range(B_HEADS):
            out = acc[h][:, :B_HEAD_DIM] / acc[h][:, B_HEAD_DIM:B_HEAD_DIM + 1]
            o_ref[rows, h * B_HEAD_DIM:(h + 1) * B_HEAD_DIM] = out.astype(BF16)


def _dsa_attention(q_hm, qi_hm, wi, key_segments, *, n, t_q, tq, rb, qb, n_valid, q_off, topk, slopes,
                   batches_per_step=1):
    nqb = t_q // tq
    nb = batches_per_step
    assert nb == 1 or (nqb == 1 and rb == tq and n % nb == 0)
    n_rb = nb * tq // rb
    last_chunk = (q_off + (qb + 1) * tq - 1) // CHUNK
    need = min((last_chunk + 1) * CHUNK, n_valid)
    n_keys = -(-need // LANES) * LANES
    if rb % CHUNK == 0 and need == q_off + (qb + 1) * tq:
        row_keys = tuple(n_keys - (n_rb - 1 - r) * rb for r in range(n_rb))
    else:
        row_keys = (n_keys,) * n_rb
    qmap = lambda b: (0, b * nqb + qb, 0)
    in_specs = [
        pl.BlockSpec((B_HEADS, nb * tq, B_HEAD_DIM), qmap),
        pl.BlockSpec((IDX_HEADS, nb * tq, IDX_DIM), qmap),
        pl.BlockSpec((nb * tq, IDX_HEADS), lambda b: (b * nqb + qb, 0)),
    ]
    operands, segments, col = [], [], 0
    for kt, v, kit, v_transposed in key_segments:
        length = min(kt.shape[-1], n_keys - col)
        if length <= 0:
            break
        v_block = (nb, B_KV_HEADS, B_HEAD_DIM, length) if v_transposed else (nb, B_KV_HEADS, length, LANES)
        in_specs += [
            pl.BlockSpec((nb, B_KV_HEADS, B_HEAD_DIM, length), lambda b: (b, 0, 0, 0)),
            pl.BlockSpec(v_block, lambda b: (b, 0, 0, 0)),
            pl.BlockSpec((nb, IDX_DIM, length), lambda b: (b, 0, 0)),
        ]
        operands += [kt, v, kit]
        segments.append((length, v_transposed))
        col += length
    assert col == n_keys, (col, n_keys)
    return pl.pallas_call(
        functools.partial(_dsa_block, segments=tuple(segments), qb=qb, tq=nb * tq, rb=rb, row_keys=row_keys,
                          block_batches=nb > 1, n_valid=n_valid, q_off=q_off, topk=topk, slopes=slopes),
        grid=(n // nb,), in_specs=in_specs,
        out_specs=pl.BlockSpec((nb * tq, B_WIDTH), lambda b: (b, 0)),
        out_shape=jax.ShapeDtypeStruct((n * tq, B_WIDTH), BF16),
        scratch_shapes=[pltpu.VMEM((nb * tq, n_keys), F32), pltpu.VMEM((nb * tq, n_keys), F32),
                        pltpu.VMEM((nb * tq, 1), F32), pltpu.VMEM((-(-nb * tq // LANES), n_keys, LANES), F32)],
        compiler_params=pltpu.CompilerParams(dimension_semantics=("parallel",), vmem_limit_bytes=VMEM_LIMIT),
        name="dsa_attention",
    )(q_hm, qi_hm, wi, *operands)


def _merge_kernel(x_ref, a_ref, *refs, n_parts, alpha):
    b_refs = refs[:n_parts]
    wg_ref, wb_ref, wo_ref, ln1_ref, w1_ref, b1_ref, w2_ref, b2_ref, ln2_ref, y_ref = refs[n_parts:]
    d = x_ref.shape[1]
    x = x_ref[...]
    gates = jax.nn.sigmoid(_dot(x.astype(BF16), wg_ref[...]))
    b = b_refs[0][...]
    part = pl.program_id(0) % n_parts
    for j in range(1, n_parts):
        b = jnp.where(part == j, b_refs[j][...], b)
    m = gates[:, :d] * _dot(a_ref[...], wb_ref[0]) + gates[:, d:] * _dot(b, wb_ref[1])
    h = _layer_norm(alpha * x + _dot(m.astype(BF16), wo_ref[...]), ln1_ref[0:1, :], ln1_ref[1:2, :])
    f = jnp.square(jnp.maximum(_dot(h.astype(BF16), w1_ref[...]) + b1_ref[...], 0.0))
    f = _dot(f.astype(BF16), w2_ref[...]) + b2_ref[...]
    y_ref[...] = _layer_norm(alpha * h + f, ln2_ref[0:1, :], ln2_ref[1:2, :])


def _resident_spec(shape):
    nd = len(shape)
    return pl.BlockSpec(shape, lambda *_: (0,) * nd, pipeline_mode=pl.Buffered(1))


def _merge_ffn(x2, a, b_parts, wts, tm, alpha):
    t, d = x2.shape
    n_parts = len(b_parts)
    assert all(p.shape == (t // n_parts, B_WIDTH) for p in b_parts)
    row = lambda i: (i, 0)
    in_specs = [pl.BlockSpec((tm, d), row), pl.BlockSpec((tm, A_WIDTH), row)]
    in_specs += [pl.BlockSpec((tm, B_WIDTH), lambda i: (i // n_parts, 0)) for _ in b_parts]
    in_specs += [_resident_spec(w.shape) for w in wts]
    return pl.pallas_call(
        functools.partial(_merge_kernel, n_parts=n_parts, alpha=np.float32(alpha)),
        grid=(t // tm,), in_specs=in_specs,
        out_specs=pl.BlockSpec((tm, d), row),
        out_shape=jax.ShapeDtypeStruct((t, d), F32),
        compiler_params=pltpu.CompilerParams(dimension_semantics=("parallel",), vmem_limit_bytes=VMEM_LIMIT),
        name="merge_ffn",
    )(x2, a, *b_parts, *wts)


def _spatial_weights(w_s, b_s, span):
    pos = jnp.arange(A_SPAN)
    mask = (pos[None, :] // CHUNK) <= (pos[:, None] // CHUNK)
    ws = jnp.where(mask[None], w_s, 0.0)[:, :span, :span].astype(BF16)
    bs = jnp.repeat(b_s[:, :span].T, A_GROUP_CH, axis=1)
    return ws, bs


def _pad_axis(x, axis, size):
    pads = [(0, 0)] * x.ndim
    pads[axis] = (0, size - x.shape[axis])
    return jnp.pad(x, pads)


def kernel(x_prompt, x_sample, cache_k, cache_v, cache_kidx, w_in, lnv_g, lnv_b, w_s, b_s, lnk_g, lnk_b,
           w_branch, w_out, ln1_g, ln1_b, w_ff1, b_ff1, w_ff2, b_ff2, ln2_g, ln2_b):
    depth = w_in.shape[0]
    n_p, s_p, d = x_prompt.shape
    n_s, t_s, _ = x_sample.shape
    past = cache_k.shape[2]
    alpha = (2 * depth) ** 0.25
    slopes = tuple(float(2.0 ** (-8.0 * h / B_HEADS)) for h in range(1, B_HEADS + 1))
    kvw = B_KV_HEADS * B_HEAD_DIM
    c_a = 2 * A_WIDTH
    c_q = c_a + B_WIDTH
    c_k = c_q + kvw
    c_v = c_k + kvw
    c_qi = c_v + IDX_HEADS * IDX_DIM
    c_wi = c_qi + IDX_DIM + IDX_HEADS

    xp = x_prompt.reshape(n_p * s_p, d)
    xs = x_sample.reshape(n_s * t_s, d)
    outs = [[] for _ in range(7)]
    for l in range(depth):
        w = w_in[l]
        pad = jnp.zeros((d, LANES - (c_wi - c_qi)), F32)
        proj_w = (
            w[:, :c_a].astype(BF16),
            (w[:, c_a:c_q] * (LOG2_E * B_HEAD_DIM ** -0.5)).astype(BF16),
            w[:, c_q:c_v].astype(BF16),
            jnp.pad(w[:, c_k:c_v].reshape(d, B_KV_HEADS, B_HEAD_DIM),
                    ((0, 0), (0, 0), (0, LANES - B_HEAD_DIM))).reshape(d, B_KV_HEADS * LANES).astype(BF16),
            (w[:, c_v:c_qi] * (IDX_DIM ** -0.5)).astype(BF16),
            jnp.concatenate([w[:, c_qi:c_wi], pad], axis=1).astype(BF16),
            jnp.stack([lnv_g[l], lnv_b[l]]),
            jnp.stack([lnk_g[l], lnk_b[l]]),
        )
        merge_w = (
            w[:, c_wi:].astype(BF16),
            w_branch[l].astype(BF16),
            w_out[l].astype(BF16),
            jnp.stack([ln1_g[l], ln1_b[l]]),
            w_ff1[l].astype(BF16),
            b_ff1[l][None, :],
            w_ff2[l].astype(BF16),
            b_ff2[l][None, :],
            jnp.stack([ln2_g[l], ln2_b[l]]),
        )

        a_p, q_p, qi_p, k_p, v_p, kt_p, vb_p, ki_p, kit_p, wi_p = _in_projection(
            xp, proj_w + _spatial_weights(w_s[l], b_s[l], A_SPAN), A_SPAN, TOKEN_TILE, False, batch_rows=s_p)
        b_p = [_dsa_attention(
            q_p, qi_p, wi_p, [(kt_p, vb_p, kit_p, False)],
            n=n_p, t_q=s_p, tq=DSA_ROWS, rb=Q_BLOCK, qb=j, n_valid=s_p, q_off=0,
            topk=min(TOPK_MAX, s_p // 4), slopes=slopes)
            for j in range(s_p // DSA_ROWS)]
        assert TOKEN_TILE == DSA_ROWS
        xp = _merge_ffn(xp, a_p, b_p, merge_w, TOKEN_TILE, alpha)

        a_s, q_s, qi_s, k_s, v_s, kt_s, vb_s, ki_s, kit_s, wi_s, va_s = _in_projection(
            xs, proj_w + _spatial_weights(w_s[l], b_s[l], t_s), t_s, n_s * t_s, True)
        n_all = past + t_s
        new_len = -(-t_s // LANES) * LANES
        cache_seg = (jnp.transpose(cache_k[l], (0, 2, 3, 1)), jnp.transpose(cache_v[l], (0, 2, 3, 1)),
                     jnp.transpose(cache_kidx[l], (0, 2, 1)), True)
        kt_new = jnp.transpose(kt_s.reshape(B_KV_HEADS, B_HEAD_DIM, n_s, t_s), (2, 0, 1, 3))
        vb_new = jnp.transpose(vb_s.reshape(B_KV_HEADS, n_s, t_s, LANES), (1, 0, 2, 3))
        kit_new = jnp.transpose(kit_s.reshape(IDX_DIM, n_s, t_s), (1, 0, 2))
        new_seg = (_pad_axis(kt_new, 3, new_len), _pad_axis(vb_new, 2, new_len), _pad_axis(kit_new, 2, new_len), False)
        b_s_ = _dsa_attention(
            q_s, qi_s, wi_s, [cache_seg, new_seg],
            n=n_s, t_q=t_s, tq=t_s, rb=t_s, qb=0, n_valid=n_all, q_off=past,
            topk=min(TOPK_MAX, n_all // 4), slopes=slopes, batches_per_step=SAMPLE_BATCHES)
        xs = _merge_ffn(xs, a_s, [b_s_], merge_w, n_s * t_s, alpha)

        outs[0].append(k_p.reshape(n_p, s_p, B_KV_HEADS, B_HEAD_DIM))
        outs[1].append(v_p.reshape(n_p, s_p, B_KV_HEADS, B_HEAD_DIM))
        outs[2].append(ki_p.reshape(n_p, s_p, IDX_DIM))
        outs[3].append(k_s.reshape(n_s, t_s, B_KV_HEADS, B_HEAD_DIM))
        outs[4].append(v_s.reshape(n_s, t_s, B_KV_HEADS, B_HEAD_DIM))
        outs[5].append(ki_s.reshape(n_s, t_s, IDX_DIM))
        outs[6].append(va_s.reshape(n_s, t_s, A_WIDTH))

    return (xp.reshape(n_p, s_p, d), xs.reshape(n_s, t_s, d)) + tuple(jnp.stack(o) for o in outs)
```

```python
import functools

import numpy as np
import jax
import jax.numpy as jnp
from jax import lax
from jax.experimental import pallas as pl
from jax.experimental.pallas import tpu as pltpu

CHUNK = 64
A_WIDTH = 512
A_GROUPS = 4
A_GROUP_CH = A_WIDTH // A_GROUPS
A_SPAN = 128
B_HEADS = 8
B_HEAD_DIM = 64
B_KV_HEADS = 2
B_GROUP = B_HEADS // B_KV_HEADS
B_WIDTH = B_HEADS * B_HEAD_DIM
IDX_HEADS = 8
IDX_DIM = 64
TOPK_MAX = 256
Q_BLOCK = 128
LN_EPS = 1e-5
LOG2_E = 1.4426950408889634

LANES = 128
VMEM_LIMIT = 52 * 1024 * 1024
TOKEN_TILE = 512
PROJ_TILE = 1024
KEY_CHUNK = 512
ATT_CHUNK = 256
ATT_CHUNK_MAX = 1024
DSA_ROWS = 4 * Q_BLOCK
SAMPLE_BATCHES = 4
BISECT_STEPS = 14
KEY_ACC_ROWS = 32

F32 = jnp.float32
BF16 = jnp.bfloat16
NT_DIMS = (((1,), (1,)), ((), ()))


def _dot(a, b):
    return jnp.dot(a, b, preferred_element_type=F32)


def _dot_nt(a, b):
    return lax.dot_general(a, b, NT_DIMS, preferred_element_type=F32)


def _reduce_keys(x, reduce):
    part = reduce(x.reshape(-1, KEY_ACC_ROWS, x.shape[-1]), axis=0)
    return reduce(part, axis=0, keepdims=True)


def _layer_norm(x, g, b):
    mu = jnp.mean(x, axis=-1, keepdims=True)
    xc = x - mu
    var = jnp.mean(xc * xc, axis=-1, keepdims=True)
    return xc * lax.rsqrt(var + LN_EPS) * g + b


def _inproj_kernel(x_ref, wa_ref, wq_ref, wkv_ref, wva_ref, wqi_ref, wkw_ref, lnv_ref, lnk_ref, ws_ref, bs_ref,
                   a_ref, q_ref, qi_ref, k_ref, v_ref, kt_ref, vb_ref, ki_ref, kit_ref, wi_ref, *va_refs,
                   span):
    tm = x_ref.shape[0]
    xb = x_ref[...].astype(BF16)

    za = _dot(xb, wa_ref[...])
    ga = 0.5 * za * (1.0 + lax.erf(za * np.float32(np.sqrt(0.5))))
    u = ga[:, :A_WIDTH]
    va = _layer_norm(ga[:, A_WIDTH:], lnv_ref[0:1, :], lnv_ref[1:2, :])
    if va_refs:
        va_refs[0][...] = va
    vab = va.astype(BF16)

    zq = _dot(xb, wq_ref[...])
    zqi = _dot(xb, wqi_ref[...])
    for h in range(B_HEADS):
        q_ref[h] = zq[:, h * B_HEAD_DIM:(h + 1) * B_HEAD_DIM].astype(BF16)
    for h in range(IDX_HEADS):
        qi_ref[h] = zqi[:, h * IDX_DIM:(h + 1) * IDX_DIM].astype(BF16)

    zkv = _dot(xb, wkv_ref[...])
    kvw = B_KV_HEADS * B_HEAD_DIM
    if len(kt_ref.shape) == 4:
        kt_ref, vb_ref, kit_ref = kt_ref.at[0], vb_ref.at[0], kit_ref.at[0]
    zva = _dot(xb, wva_ref[...])
    ones_lane = lax.broadcasted_iota(jnp.int32, (tm, LANES), 1) == B_HEAD_DIM
    for g in range(B_KV_HEADS):
        kg = zkv[:, g * B_HEAD_DIM:(g + 1) * B_HEAD_DIM]
        k_ref[pl.ds(g, tm, stride=B_KV_HEADS), :] = kg
        v_ref[pl.ds(g, tm, stride=B_KV_HEADS), :] = zkv[:, kvw + g * B_HEAD_DIM:kvw + (g + 1) * B_HEAD_DIM]
        kt_ref[g] = kg.T.astype(BF16)
        vb_ref[g] = jnp.where(ones_lane, 1.0, zva[:, g * LANES:(g + 1) * LANES]).astype(BF16)

    zkw = _dot(xb, wkw_ref[...])
    ki = _layer_norm(zkw[:, :IDX_DIM], lnk_ref[0:1, :], lnk_ref[1:2, :])
    ki_ref[...] = ki
    kit_ref[...] = ki.T.astype(BF16)
    wi_ref[...] = zkw[:, IDX_DIM:IDX_DIM + IDX_HEADS] * np.float32(IDX_HEADS ** -0.5)

    for s in range(tm // span):
        rows = slice(s * span, (s + 1) * span)
        for g in range(A_GROUPS):
            cols = slice(g * A_GROUP_CH, (g + 1) * A_GROUP_CH)
            mixed = _dot(ws_ref[g], vab[rows, cols]) + bs_ref[:, cols]
            a_ref[rows, cols] = (u[rows, cols] * mixed).astype(BF16)


def _const_spec(shape):
    nd = len(shape)
    return pl.BlockSpec(shape, lambda *_: (0,) * nd)


def _in_projection(x2, wts, span, tm, emit_va, batch_rows=None):
    t, d = x2.shape
    grid = (t // tm,)
    row = lambda i: (i, 0)
    hm = lambda i: (0, i, 0)
    if batch_rows is None:
        kt_shape, kt_block, kt_map = (B_KV_HEADS, B_HEAD_DIM, t), (B_KV_HEADS, B_HEAD_DIM, tm), lambda i: (0, 0, i)
        vb_shape, vb_block, vb_map = (B_KV_HEADS, t, LANES), (B_KV_HEADS, tm, LANES), hm
        ki_shape, ki_block, ki_map = (IDX_DIM, t), (IDX_DIM, tm), lambda i: (0, i)
    else:
        per, n = batch_rows // tm, t // batch_rows
        kt_shape, kt_block = (n, B_KV_HEADS, B_HEAD_DIM, batch_rows), (1, B_KV_HEADS, B_HEAD_DIM, tm)
        kt_map = lambda i: (i // per, 0, 0, i % per)
        vb_shape, vb_block = (n, B_KV_HEADS, batch_rows, LANES), (1, B_KV_HEADS, tm, LANES)
        vb_map = lambda i: (i // per, 0, i % per, 0)
        ki_shape, ki_block, ki_map = (n, IDX_DIM, batch_rows), (1, IDX_DIM, tm), lambda i: (i // per, 0, i % per)
    in_specs = [pl.BlockSpec((tm, d), row)] + [_const_spec(w.shape) for w in wts]
    out_shape = [
        jax.ShapeDtypeStruct((t, A_WIDTH), BF16),
        jax.ShapeDtypeStruct((B_HEADS, t, B_HEAD_DIM), BF16),
        jax.ShapeDtypeStruct((IDX_HEADS, t, IDX_DIM), BF16),
        jax.ShapeDtypeStruct((t * B_KV_HEADS, B_HEAD_DIM), F32),
        jax.ShapeDtypeStruct((t * B_KV_HEADS, B_HEAD_DIM), F32),
        jax.ShapeDtypeStruct(kt_shape, BF16),
        jax.ShapeDtypeStruct(vb_shape, BF16),
        jax.ShapeDtypeStruct((t, IDX_DIM), F32),
        jax.ShapeDtypeStruct(ki_shape, BF16),
        jax.ShapeDtypeStruct((t, IDX_HEADS), F32),
    ]
    out_specs = [
        pl.BlockSpec((tm, A_WIDTH), row),
        pl.BlockSpec((B_HEADS, tm, B_HEAD_DIM), hm),
        pl.BlockSpec((IDX_HEADS, tm, IDX_DIM), hm),
        pl.BlockSpec((tm * B_KV_HEADS, B_HEAD_DIM), row),
        pl.BlockSpec((tm * B_KV_HEADS, B_HEAD_DIM), row),
        pl.BlockSpec(kt_block, kt_map),
        pl.BlockSpec(vb_block, vb_map),
        pl.BlockSpec((tm, IDX_DIM), row),
        pl.BlockSpec(ki_block, ki_map),
        pl.BlockSpec((tm, IDX_HEADS), row),
    ]
    if emit_va:
        out_shape.append(jax.ShapeDtypeStruct((t, A_WIDTH), F32))
        out_specs.append(pl.BlockSpec((tm, A_WIDTH), row))
    return pl.pallas_call(
        functools.partial(_inproj_kernel, span=span),
        grid=grid, in_specs=in_specs, out_specs=out_specs, out_shape=out_shape,
        compiler_params=pltpu.CompilerParams(dimension_semantics=("parallel",), vmem_limit_bytes=VMEM_LIMIT),
        name="in_projection",
    )(x2, *wts)


def _key_pieces(seg_lens, n_keys, chunk):
    pieces, col = [], 0
    for si, seg_len in enumerate(seg_lens):
        off = 0
        while off < seg_len and col < n_keys:
            width = min(chunk, seg_len - off, n_keys - col)
            pieces.append((col, width, si, off))
            off += width
            col += width
    return pieces


def _dsa_block(q_ref, qi_ref, wi_ref, *refs, segments, qb, tq, rb, row_keys, block_batches, n_valid, q_off, topk,
               slopes):
    n_seg = len(segments)
    seg_refs = [refs[3 * i:3 * i + 3] for i in range(n_seg)]
    o_ref, s_ref, d_ref, thr_ref, st_ref = refs[3 * n_seg:]
    seg_lens = [length for length, _ in segments]
    neg_inf = np.float32(-np.inf)
    pos_inf = np.float32(np.inf)
    kf = np.float32(topk)
    n_rb = tq // rb
    assert LANES % rb == 0 and rb % 8 == 0
    blocks = [(slice(r * rb, (r + 1) * rb), row_keys[r]) for r in range(n_rb)]
    batch_of = [r if block_batches else 0 for r in range(n_rb)]
    pos_of = [q_off if block_batches else q_off + qb * tq + r * rb for r in range(n_rb)]

    for r, (rows, n_keys) in enumerate(blocks):
        q_pos = pos_of[r] + lax.broadcasted_iota(jnp.int32, (rb, 1), 0)
        qi = qi_ref[:, rows, :].reshape(IDX_HEADS * rb, IDX_DIM)
        wi = wi_ref[rows, :]
        for c0, kc, si, off in _key_pieces(seg_lens, n_keys, KEY_CHUNK):
            k_pos = c0 + lax.broadcasted_iota(jnp.int32, (1, kc), 1)
            lg = _dot(qi, seg_refs[si][2][batch_of[r], :, off:off + kc].astype(BF16))
            sc = wi[:, 0:1] * jnp.maximum(lg[0:rb], 0.0)
            for h in range(1, IDX_HEADS):
                sc = sc + wi[:, h:h + 1] * jnp.maximum(lg[h * rb:(h + 1) * rb], 0.0)
            k_chunk = k_pos // CHUNK
            if c0 + kc > n_valid:
                k_chunk = jnp.where(k_pos < n_valid, k_chunk, np.int32(2 ** 30))
            adm = k_chunk <= (q_pos // CHUNK)
            sc = jnp.where(adm, sc, neg_inf)
            s_ref[rows, c0:c0 + kc] = sc
            d_ref[rows, c0:c0 + kc] = jnp.abs(q_pos - k_pos).astype(F32)

    groups = []
    for g0 in range(0, tq, LANES):
        g_rows = min(LANES, tq - g0)
        g_keys = {row_keys[r] for r in range(g0 // rb, (g0 + g_rows) // rb)}
        assert len(g_keys) == 1
        groups.append((g0, g_rows, g_keys.pop()))
    for g, (g0, g_rows, g_keys) in enumerate(groups):
        for c0 in range(0, g_keys, KEY_CHUNK):
            kc = min(KEY_CHUNK, g_keys - c0)
            sc = s_ref[g0:g0 + g_rows, c0:c0 + kc]
            if g_rows < LANES:
                sc = jnp.concatenate([sc, jnp.full((LANES - g_rows, kc), neg_inf, F32)], axis=0)
            st_ref[g, c0:c0 + kc, :] = sc.T

    def scores_t(g):
        return st_ref[g, :groups[g][2], :]

    def count_ge(r, t):
        return _reduce_keys(jnp.where(scores_t(r) >= t, 1.0, 0.0), jnp.sum)

    lane = lax.broadcasted_iota(jnp.int32, (1, LANES), 1)
    brackets, states, n_adms = [], [], []
    for g, (g0, g_rows, _) in enumerate(groups):
        s = scores_t(g)
        row = g0 + lane
        q_pos = (q_off + row % rb) if block_batches else (q_off + qb * tq + row)
        n_adm = jnp.minimum((q_pos // CHUNK + 1) * CHUNK, n_valid)
        n_adm = jnp.where(lane < g_rows, n_adm, 0).astype(F32)
        row_max = _reduce_keys(s, jnp.max)
        row_min = _reduce_keys(jnp.where(s > neg_inf, s, pos_inf), jnp.min)
        brackets.append((row_min, row_max, jnp.full((1, LANES), pos_inf, F32)))
        n_adms.append(n_adm)

    def bisect(_, carry):
        out = []
        for r, (lo, hb, hiv) in enumerate(carry):
            mid = 0.5 * lo + 0.5 * hb
            ge = count_ge(r, mid) >= kf
            out.append((jnp.where(ge, mid, lo), jnp.where(ge, hb, mid), jnp.where(ge, hiv, mid)))
        return tuple(out)

    brackets = lax.fori_loop(0, BISECT_STEPS, bisect, tuple(brackets))

    for (_, _, hiv), n_adm in zip(brackets, n_adms):
        done = jnp.where(n_adm <= kf, 1.0, 0.0)
        states.append((jnp.full((1, LANES), np.finfo(np.float32).min, F32), hiv, done, jnp.zeros((1, LANES), F32)))

    def n_open(states):
        return sum(jnp.sum(1.0 - done) for _, _, done, _ in states)

    def scan_cond(carry):
        _, n_left, it = carry
        return jnp.logical_and(n_left > 0.0, it < max(row_keys))

    def scan_body(carry):
        states, _, it = carry
        out = []
        for r, (thr, hiv, done, n_ge) in enumerate(states):
            sv = scores_t(r)
            cand = _reduce_keys(jnp.where(sv < hiv, sv, neg_inf), jnp.max)
            cnt = count_ge(r, cand)
            found = jnp.where(done > 0.5, 0.0, jnp.where(cnt >= kf, 1.0, 0.0))
            thr = jnp.where(found > 0.5, cand, thr)
            n_ge = jnp.where(found > 0.5, cnt, n_ge)
            done = jnp.maximum(done, found)
            out.append((thr, jnp.where(done > 0.5, hiv, cand), done, n_ge))
        return tuple(out), n_open(out), it + 1

    states, _, _ = lax.while_loop(scan_cond, scan_body, (tuple(states), n_open(states), jnp.int32(0)))

    over = []
    for r, (thr, _, _, n_ge) in enumerate(states):
        g0, g_rows, _ = groups[r]
        thr_ref[g0:g0 + g_rows, :] = jnp.broadcast_to(thr, (LANES, LANES)).T[:g_rows, 0:1]
        over.append(jnp.max(n_ge))
    tied = functools.reduce(jnp.maximum, over) > kf

    att_chunk = min(ATT_CHUNK_MAX, ATT_CHUNK * LANES // rb)
    ones_row = jnp.where(lax.broadcasted_iota(jnp.int32, (LANES - B_HEAD_DIM, att_chunk), 0) == 0, 1.0, 0.0)
    ones_row = ones_row.astype(BF16)

    for r, (rows, n_keys) in enumerate(blocks):
        bi = batch_of[r]
        thr = thr_ref[rows, :]

        @pl.when(jnp.logical_not(tied))
        def _():
            d_ref[rows, :n_keys] = jnp.where(s_ref[rows, :n_keys] >= thr, d_ref[rows, :n_keys], pos_inf)

        @pl.when(tied)
        def _():
            n_gt = jnp.sum(jnp.where(s_ref[rows, :n_keys] > thr, 1.0, 0.0), axis=1, keepdims=True)
            room = kf - n_gt
            tri = (lax.broadcasted_iota(jnp.int32, (LANES, LANES), 0)
                   <= lax.broadcasted_iota(jnp.int32, (LANES, LANES), 1))
            tri = jnp.where(tri, 1.0, 0.0).astype(BF16)
            run = jnp.zeros((rb, 1), F32)
            for c0 in range(0, n_keys, LANES):
                blk = s_ref[rows, c0:c0 + LANES]
                eq = jnp.where(blk == thr, 1.0, 0.0)
                prefix = _dot(eq.astype(BF16), tri) + run
                keep = jnp.where(blk > thr, 1.0, jnp.where(prefix <= room, eq, 0.0))
                d_ref[rows, c0:c0 + LANES] = jnp.where(keep > 0.5, d_ref[rows, c0:c0 + LANES], pos_inf)
                run = run + jnp.sum(eq, axis=1, keepdims=True)

        qs = [q_ref[g * B_GROUP:(g + 1) * B_GROUP, rows, :].reshape(B_GROUP * rb, B_HEAD_DIM)
              for g in range(B_KV_HEADS)]
        m_run = [jnp.full((rb, 1), np.finfo(np.float32).min, F32) for _ in range(B_HEADS)]
        acc = [jnp.zeros((rb, LANES), F32) for _ in range(B_HEADS)]
        for c0, kc, si, off in _key_pieces(seg_lens, n_keys, att_chunk):
            kt_ref, v_ref, _ = seg_refs[si]
            dist = d_ref[rows, c0:c0 + kc]
            for g in range(B_KV_HEADS):
                logits = _dot(qs[g], kt_ref[bi, g, :, off:off + kc].astype(BF16))
                ps, alphas = [], []
                for hh in range(B_GROUP):
                    h = g * B_GROUP + hh
                    lgt = logits[hh * rb:(hh + 1) * rb] - np.float32(slopes[h] * LOG2_E) * dist
                    m_new = jnp.maximum(m_run[h], jnp.max(lgt, axis=1, keepdims=True))
                    alphas.append(jnp.exp2(m_run[h] - m_new))
                    ps.append(jnp.exp2(lgt - m_new).astype(BF16))
                    m_run[h] = m_new
                p = jnp.concatenate(ps, axis=0)
                if segments[si][1]:
                    vt = jnp.concatenate([v_ref[bi, g, :, off:off + kc].astype(BF16), ones_row[:, :kc]], axis=0)
                    pv = _dot_nt(p, vt)
                else:
                    pv = _dot(p, v_ref[bi, g, off:off + kc, :])
                for hh in range(B_GROUP):
                    h = g * B_GROUP + hh
                    acc[h] = alphas[hh] * acc[h] + pv[hh * rb:(hh + 1) * rb]
        for h in range(B_HEADS):
            out = acc[h][:, :B_HEAD_DIM] / acc[h][:, B_HEAD_DIM:B_HEAD_DIM + 1]
            o_ref[rows, h * B_HEAD_DIM:(h + 1) * B_HEAD_DIM] = out.astype(BF16)


def _dsa_attention(q_hm, qi_hm, wi, key_segments, *, n, t_q, tq, rb, qb, n_valid, q_off, topk, slopes,
                   batches_per_step=1):
    nqb = t_q // tq
    nb = batches_per_step
    assert nb == 1 or (nqb == 1 and rb == tq and n % nb == 0)
    n_rb = nb * tq // rb
    last_chunk = (q_off + (qb + 1) * tq - 1) // CHUNK
    need = min((last_chunk + 1) * CHUNK, n_valid)
    n_keys = -(-need // LANES) * LANES
    if rb % CHUNK == 0 and need == q_off + (qb + 1) * tq:
        row_keys = tuple(n_keys - (n_rb - 1 - r) * rb for r in range(n_rb))
    else:
        row_keys = (n_keys,) * n_rb
    qmap = lambda b: (0, b * nqb + qb, 0)
    in_specs = [
        pl.BlockSpec((B_HEADS, nb * tq, B_HEAD_DIM), qmap),
        pl.BlockSpec((IDX_HEADS, nb * tq, IDX_DIM), qmap),
        pl.BlockSpec((nb * tq, IDX_HEADS), lambda b: (b * nqb + qb, 0)),
    ]
    operands, segments, col = [], [], 0
    for kt, v, kit, v_transposed in key_segments:
        length = min(kt.shape[-1], n_keys - col)
        if length <= 0:
            break
        v_block = (nb, B_KV_HEADS, B_HEAD_DIM, length) if v_transposed else (nb, B_KV_HEADS, length, LANES)
        in_specs += [
            pl.BlockSpec((nb, B_KV_HEADS, B_HEAD_DIM, length), lambda b: (b, 0, 0, 0)),
            pl.BlockSpec(v_block, lambda b: (b, 0, 0, 0)),
            pl.BlockSpec((nb, IDX_DIM, length), lambda b: (b, 0, 0)),
        ]
        operands += [kt, v, kit]
        segments.append((length, v_transposed))
        col += length
    assert col == n_keys, (col, n_keys)
    return pl.pallas_call(
        functools.partial(_dsa_block, segments=tuple(segments), qb=qb, tq=nb * tq, rb=rb, row_keys=row_keys,
                          block_batches=nb > 1, n_valid=n_valid, q_off=q_off, topk=topk, slopes=slopes),
        grid=(n // nb,), in_specs=in_specs,
        out_specs=pl.BlockSpec((nb * tq, B_WIDTH), lambda b: (b, 0)),
        out_shape=jax.ShapeDtypeStruct((n * tq, B_WIDTH), BF16),
        scratch_shapes=[pltpu.VMEM((nb * tq, n_keys), F32), pltpu.VMEM((nb * tq, n_keys), F32),
                        pltpu.VMEM((nb * tq, 1), F32), pltpu.VMEM((-(-nb * tq // LANES), n_keys, LANES), F32)],
        compiler_params=pltpu.CompilerParams(dimension_semantics=("parallel",), vmem_limit_bytes=VMEM_LIMIT),
        name="dsa_attention",
    )(q_hm, qi_hm, wi, *operands)


def _merge_kernel(x_ref, a_ref, *refs, n_parts, alpha):
    b_refs = refs[:n_parts]
    wg_ref, wb_ref, wo_ref, ln1_ref, w1_ref, b1_ref, w2_ref, b2_ref, ln2_ref, y_ref = refs[n_parts:]
    d = x_ref.shape[1]
    x = x_ref[...]
    gates = jax.nn.sigmoid(_dot(x.astype(BF16), wg_ref[...]))
    b = b_refs[0][...]
    part = pl.program_id(0) % n_parts
    for j in range(1, n_parts):
        b = jnp.where(part == j, b_refs[j][...], b)
    m = gates[:, :d] * _dot(a_ref[...], wb_ref[0]) + gates[:, d:] * _dot(b, wb_ref[1])
    h = _layer_norm(alpha * x + _dot(m.astype(BF16), wo_ref[...]), ln1_ref[0:1, :], ln1_ref[1:2, :])
    f = jnp.square(jnp.maximum(_dot(h.astype(BF16), w1_ref[...]) + b1_ref[...], 0.0))
    f = _dot(f.astype(BF16), w2_ref[...]) + b2_ref[...]
    y_ref[...] = _layer_norm(alpha * h + f, ln2_ref[0:1, :], ln2_ref[1:2, :])


def _resident_spec(shape):
    nd = len(shape)
    return pl.BlockSpec(shape, lambda *_: (0,) * nd, pipeline_mode=pl.Buffered(1))


def _merge_ffn(x2, a, b_parts, wts, tm, alpha):
    t, d = x2.shape
    n_parts = len(b_parts)
    assert all(p.shape == (t // n_parts, B_WIDTH) for p in b_parts)
    row = lambda i: (i, 0)
    in_specs = [pl.BlockSpec((tm, d), row), pl.BlockSpec((tm, A_WIDTH), row)]
    in_specs += [pl.BlockSpec((tm, B_WIDTH), lambda i: (i // n_parts, 0)) for _ in b_parts]
    in_specs += [_resident_spec(w.shape) for w in wts]
    return pl.pallas_call(
        functools.partial(_merge_kernel, n_parts=n_parts, alpha=np.float32(alpha)),
        grid=(t // tm,), in_specs=in_specs,
        out_specs=pl.BlockSpec((tm, d), row),
        out_shape=jax.ShapeDtypeStruct((t, d), F32),
        compiler_params=pltpu.CompilerParams(dimension_semantics=("parallel",), vmem_limit_bytes=VMEM_LIMIT),
        name="merge_ffn",
    )(x2, a, *b_parts, *wts)


def _spatial_weights(w_s, b_s, span):
    pos = jnp.arange(A_SPAN)
    mask = (pos[None, :] // CHUNK) <= (pos[:, None] // CHUNK)
    ws = jnp.where(mask[None], w_s, 0.0)[:, :span, :span].astype(BF16)
    bs = jnp.repeat(b_s[:, :span].T, A_GROUP_CH, axis=1)
    return ws, bs


def _pad_axis(x, axis, size):
    pads = [(0, 0)] * x.ndim
    pads[axis] = (0, size - x.shape[axis])
    return jnp.pad(x, pads)


def kernel(x_prompt, x_sample, cache_k, cache_v, cache_kidx, w_in, lnv_g, lnv_b, w_s, b_s, lnk_g, lnk_b,
           w_branch, w_out, ln1_g, ln1_b, w_ff1, b_ff1, w_ff2, b_ff2, ln2_g, ln2_b):
    depth = w_in.shape[0]
    n_p, s_p, d = x_prompt.shape
    n_s, t_s, _ = x_sample.shape
    past = cache_k.shape[2]
    alpha = (2 * depth) ** 0.25
    slopes = tuple(float(2.0 ** (-8.0 * h / B_HEADS)) for h in range(1, B_HEADS + 1))
    kvw = B_KV_HEADS * B_HEAD_DIM
    c_a = 2 * A_WIDTH
    c_q = c_a + B_WIDTH
    c_k = c_q + kvw
    c_v = c_k + kvw
    c_qi = c_v + IDX_HEADS * IDX_DIM
    c_wi = c_qi + IDX_DIM + IDX_HEADS

    xp = x_prompt.reshape(n_p * s_p, d)
    xs = x_sample.reshape(n_s * t_s, d)
    outs = [[] for _ in range(7)]
    for l in range(depth):
        w = w_in[l]
        pad = jnp.zeros((d, LANES - (c_wi - c_qi)), F32)
        proj_w = (
            w[:, :c_a].astype(BF16),
            (w[:, c_a:c_q] * (LOG2_E * B_HEAD_DIM ** -0.5)).astype(BF16),
            w[:, c_q:c_v].astype(BF16),
            jnp.pad(w[:, c_k:c_v].reshape(d, B_KV_HEADS, B_HEAD_DIM),
                    ((0, 0), (0, 0), (0, LANES - B_HEAD_DIM))).reshape(d, B_KV_HEADS * LANES).astype(BF16),
            (w[:, c_v:c_qi] * (IDX_DIM ** -0.5)).astype(BF16),
            jnp.concatenate([w[:, c_qi:c_wi], pad], axis=1).astype(BF16),
            jnp.stack([lnv_g[l], lnv_b[l]]),
            jnp.stack([lnk_g[l], lnk_b[l]]),
        )
        merge_w = (
            w[:, c_wi:].astype(BF16),
            w_branch[l].astype(BF16),
            w_out[l].astype(BF16),
            jnp.stack([ln1_g[l], ln1_b[l]]),
            w_ff1[l].astype(BF16),
            b_ff1[l][None, :],
            w_ff2[l].astype(BF16),
            b_ff2[l][None, :],
            jnp.stack([ln2_g[l], ln2_b[l]]),
        )

        a_p, q_p, qi_p, k_p, v_p, kt_p, vb_p, ki_p, kit_p, wi_p = _in_projection(
            xp, proj_w + _spatial_weights(w_s[l], b_s[l], A_SPAN), A_SPAN, PROJ_TILE, False, batch_rows=s_p)
        b_p = [_dsa_attention(
            q_p, qi_p, wi_p, [(kt_p, vb_p, kit_p, False)],
            n=n_p, t_q=s_p, tq=DSA_ROWS, rb=Q_BLOCK, qb=j, n_valid=s_p, q_off=0,
            topk=min(TOPK_MAX, s_p // 4), slopes=slopes)
            for j in range(s_p // DSA_ROWS)]
        assert TOKEN_TILE == DSA_ROWS
        xp = _merge_ffn(xp, a_p, b_p, merge_w, TOKEN_TILE, alpha)

        a_s, q_s, qi_s, k_s, v_s, kt_s, vb_s, ki_s, kit_s, wi_s, va_s = _in_projection(
            xs, proj_w + _spatial_weights(w_s[l], b_s[l], t_s), t_s, n_s * t_s, True)
        n_all = past + t_s
        new_len = -(-t_s // LANES) * LANES
        cache_seg = (jnp.transpose(cache_k[l], (0, 2, 3, 1)), jnp.transpose(cache_v[l], (0, 2, 3, 1)),
                     jnp.transpose(cache_kidx[l], (0, 2, 1)), True)
        kt_new = jnp.transpose(kt_s.reshape(B_KV_HEADS, B_HEAD_DIM, n_s, t_s), (2, 0, 1, 3))
        vb_new = jnp.transpose(vb_s.reshape(B_KV_HEADS, n_s, t_s, LANES), (1, 0, 2, 3))
        kit_new = jnp.transpose(kit_s.reshape(IDX_DIM, n_s, t_s), (1, 0, 2))
        new_seg = (_pad_axis(kt_new, 3, new_len), _pad_axis(vb_new, 2, new_len), _pad_axis(kit_new, 2, new_len), False)
        b_s_ = _dsa_attention(
            q_s, qi_s, wi_s, [cache_seg, new_seg],
            n=n_s, t_q=t_s, tq=t_s, rb=t_s, qb=0, n_valid=n_all, q_off=past,
            topk=min(TOPK_MAX, n_all // 4), slopes=slopes, batches_per_step=SAMPLE_BATCHES)
        xs = _merge_ffn(xs, a_s, [b_s_], merge_w, n_s * t_s, alpha)

        outs[0].append(k_p.reshape(n_p, s_p, B_KV_HEADS, B_HEAD_DIM))
        outs[1].append(v_p.reshape(n_p, s_p, B_KV_HEADS, B_HEAD_DIM))
        outs[2].append(ki_p.reshape(n_p, s_p, IDX_DIM))
        outs[3].append(k_s.reshape(n_s, t_s, B_KV_HEADS, B_HEAD_DIM))
        outs[4].append(v_s.reshape(n_s, t_s, B_KV_HEADS, B_HEAD_DIM))
        outs[5].append(ki_s.reshape(n_s, t_s, IDX_DIM))
        outs[6].append(va_s.reshape(n_s, t_s, A_WIDTH))

    return (xp.reshape(n_p, s_p, d), xs.reshape(n_s, t_s, d)) + tuple(jnp.stack(o) for o in outs)
```

```python
import functools

import numpy as np
import jax
import jax.numpy as jnp
from jax import lax
from jax.experimental import pallas as pl
from jax.experimental.pallas import tpu as pltpu

CHUNK = 64
A_WIDTH = 512
A_GROUPS = 4
A_GROUP_CH = A_WIDTH // A_GROUPS
A_SPAN = 128
B_HEADS = 8
B_HEAD_DIM = 64
B_KV_HEADS = 2
B_GROUP = B_HEADS // B_KV_HEADS
B_WIDTH = B_HEADS * B_HEAD_DIM
IDX_HEADS = 8
IDX_DIM = 64
TOPK_MAX = 256
Q_BLOCK = 128
LN_EPS = 1e-5
LOG2_E = 1.4426950408889634

LANES = 128
VMEM_LIMIT = 52 * 1024 * 1024
TOKEN_TILE = 512
PROJ_TILE = 1024
KEY_CHUNK = 512
ATT_CHUNK = 256
ATT_CHUNK_MAX = 1024
DSA_ROWS = 4 * Q_BLOCK
SAMPLE_BATCHES = 4
BISECT_STEPS = 14
KEY_ACC_ROWS = 32

F32 = jnp.float32
BF16 = jnp.bfloat16
NT_DIMS = (((1,), (1,)), ((), ()))


def _dot(a, b):
    return jnp.dot(a, b, preferred_element_type=F32)


def _dot_nt(a, b):
    return lax.dot_general(a, b, NT_DIMS, preferred_element_type=F32)


def _reduce_keys(x, reduce):
    part = reduce(x.reshape(-1, KEY_ACC_ROWS, x.shape[-1]), axis=0)
    return reduce(part, axis=0, keepdims=True)


def _layer_norm(x, g, b):
    mu = jnp.mean(x, axis=-1, keepdims=True)
    xc = x - mu
    var = jnp.mean(xc * xc, axis=-1, keepdims=True)
    return xc * lax.rsqrt(var + LN_EPS) * g + b


def _inproj_kernel(x_ref, wa_ref, wq_ref, wkv_ref, wva_ref, wqi_ref, wkw_ref, lnv_ref, lnk_ref, ws_ref, bs_ref,
                   a_ref, q_ref, qi_ref, k_ref, v_ref, kt_ref, vb_ref, ki_ref, kit_ref, wi_ref, *va_refs,
                   span):
    tm = x_ref.shape[0]
    xb = x_ref[...].astype(BF16)

    za = _dot_nt(xb, wa_ref[...])
    ga = 0.5 * za * (1.0 + lax.erf(za * np.float32(np.sqrt(0.5))))
    u = ga[:, :A_WIDTH]
    va = _layer_norm(ga[:, A_WIDTH:], lnv_ref[0:1, :], lnv_ref[1:2, :])
    if va_refs:
        va_refs[0][...] = va
    vab = va.astype(BF16)

    zq = _dot_nt(xb, wq_ref[...])
    zqi = _dot_nt(xb, wqi_ref[...])
    for h in range(B_HEADS):
        q_ref[h] = zq[:, h * B_HEAD_DIM:(h + 1) * B_HEAD_DIM].astype(BF16)
    for h in range(IDX_HEADS):
        qi_ref[h] = zqi[:, h * IDX_DIM:(h + 1) * IDX_DIM].astype(BF16)

    zkv = _dot_nt(xb, wkv_ref[...])
    kvw = B_KV_HEADS * B_HEAD_DIM
    if len(kt_ref.shape) == 4:
        kt_ref, vb_ref, kit_ref = kt_ref.at[0], vb_ref.at[0], kit_ref.at[0]
    zva = _dot_nt(xb, wva_ref[...])
    ones_lane = lax.broadcasted_iota(jnp.int32, (tm, LANES), 1) == B_HEAD_DIM
    for g in range(B_KV_HEADS):
        kg = zkv[:, g * B_HEAD_DIM:(g + 1) * B_HEAD_DIM]
        k_ref[pl.ds(g, tm, stride=B_KV_HEADS), :] = kg
        v_ref[pl.ds(g, tm, stride=B_KV_HEADS), :] = zkv[:, kvw + g * B_HEAD_DIM:kvw + (g + 1) * B_HEAD_DIM]
        kt_ref[g] = kg.T.astype(BF16)
        vb_ref[g] = jnp.where(ones_lane, 1.0, zva[:, g * LANES:(g + 1) * LANES]).astype(BF16)

    zkw = _dot_nt(xb, wkw_ref[...])
    ki = _layer_norm(zkw[:, :IDX_DIM], lnk_ref[0:1, :], lnk_ref[1:2, :])
    ki_ref[...] = ki
    kit_ref[...] = ki.T.astype(BF16)
    wi_ref[...] = zkw[:, IDX_DIM:IDX_DIM + IDX_HEADS] * np.float32(IDX_HEADS ** -0.5)

    for s in range(tm // span):
        rows = slice(s * span, (s + 1) * span)
        for g in range(A_GROUPS):
            cols = slice(g * A_GROUP_CH, (g + 1) * A_GROUP_CH)
            mixed = _dot(ws_ref[g], vab[rows, cols]) + bs_ref[:, cols]
            a_ref[rows, cols] = (u[rows, cols] * mixed).astype(BF16)


def _const_spec(shape):
    nd = len(shape)
    return pl.BlockSpec(shape, lambda *_: (0,) * nd)


def _in_projection(x2, wts, span, tm, emit_va, batch_rows=None):
    t, d = x2.shape
    grid = (t // tm,)
    row = lambda i: (i, 0)
    hm = lambda i: (0, i, 0)
    if batch_rows is None:
        kt_shape, kt_block, kt_map = (B_KV_HEADS, B_HEAD_DIM, t), (B_KV_HEADS, B_HEAD_DIM, tm), lambda i: (0, 0, i)
        vb_shape, vb_block, vb_map = (B_KV_HEADS, t, LANES), (B_KV_HEADS, tm, LANES), hm
        ki_shape, ki_block, ki_map = (IDX_DIM, t), (IDX_DIM, tm), lambda i: (0, i)
    else:
        per, n = batch_rows // tm, t // batch_rows
        kt_shape, kt_block = (n, B_KV_HEADS, B_HEAD_DIM, batch_rows), (1, B_KV_HEADS, B_HEAD_DIM, tm)
        kt_map = lambda i: (i // per, 0, 0, i % per)
        vb_shape, vb_block = (n, B_KV_HEADS, batch_rows, LANES), (1, B_KV_HEADS, tm, LANES)
        vb_map = lambda i: (i // per, 0, i % per, 0)
        ki_shape, ki_block, ki_map = (n, IDX_DIM, batch_rows), (1, IDX_DIM, tm), lambda i: (i // per, 0, i % per)
    in_specs = [pl.BlockSpec((tm, d), row)] + [_const_spec(w.shape) for w in wts]
    out_shape = [
        jax.ShapeDtypeStruct((t, A_WIDTH), BF16),
        jax.ShapeDtypeStruct((B_HEADS, t, B_HEAD_DIM), BF16),
        jax.ShapeDtypeStruct((IDX_HEADS, t, IDX_DIM), BF16),
        jax.ShapeDtypeStruct((t * B_KV_HEADS, B_HEAD_DIM), F32),
        jax.ShapeDtypeStruct((t * B_KV_HEADS, B_HEAD_DIM), F32),
        jax.ShapeDtypeStruct(kt_shape, BF16),
        jax.ShapeDtypeStruct(vb_shape, BF16),
        jax.ShapeDtypeStruct((t, IDX_DIM), F32),
        jax.ShapeDtypeStruct(ki_shape, BF16),
        jax.ShapeDtypeStruct((t, IDX_HEADS), F32),
    ]
    out_specs = [
        pl.BlockSpec((tm, A_WIDTH), row),
        pl.BlockSpec((B_HEADS, tm, B_HEAD_DIM), hm),
        pl.BlockSpec((IDX_HEADS, tm, IDX_DIM), hm),
        pl.BlockSpec((tm * B_KV_HEADS, B_HEAD_DIM), row),
        pl.BlockSpec((tm * B_KV_HEADS, B_HEAD_DIM), row),
        pl.BlockSpec(kt_block, kt_map),
        pl.BlockSpec(vb_block, vb_map),
        pl.BlockSpec((tm, IDX_DIM), row),
        pl.BlockSpec(ki_block, ki_map),
        pl.BlockSpec((tm, IDX_HEADS), row),
    ]
    if emit_va:
        out_shape.append(jax.ShapeDtypeStruct((t, A_WIDTH), F32))
        out_specs.append(pl.BlockSpec((tm, A_WIDTH), row))
    return pl.pallas_call(
        functools.partial(_inproj_kernel, span=span),
        grid=grid, in_specs=in_specs, out_specs=out_specs, out_shape=out_shape,
        compiler_params=pltpu.CompilerParams(dimension_semantics=("parallel",), vmem_limit_bytes=VMEM_LIMIT),
        name="in_projection",
    )(x2, *wts)


def _key_pieces(seg_lens, n_keys, chunk):
    pieces, col = [], 0
    for si, seg_len in enumerate(seg_lens):
        off = 0
        while off < seg_len and col < n_keys:
            width = min(chunk, seg_len - off, n_keys - col)
            pieces.append((col, width, si, off))
            off += width
            col += width
    return pieces


def _dsa_block(q_ref, qi_ref, wi_ref, *refs, segments, qb, tq, rb, row_keys, block_batches, n_valid, q_off, topk,
               slopes):
    n_seg = len(segments)
    seg_refs = [refs[3 * i:3 * i + 3] for i in range(n_seg)]
    o_ref, s_ref, d_ref, thr_ref, st_ref = refs[3 * n_seg:]
    seg_lens = [length for length, _ in segments]
    neg_inf = np.float32(-np.inf)
    pos_inf = np.float32(np.inf)
    kf = np.float32(topk)
    n_rb = tq // rb
    assert LANES % rb == 0 and rb % 8 == 0
    blocks = [(slice(r * rb, (r + 1) * rb), row_keys[r]) for r in range(n_rb)]
    batch_of = [r if block_batches else 0 for r in range(n_rb)]
    pos_of = [q_off if block_batches else q_off + qb * tq + r * rb for r in range(n_rb)]

    for r, (rows, n_keys) in enumerate(blocks):
        q_pos = pos_of[r] + lax.broadcasted_iota(jnp.int32, (rb, 1), 0)
        qi = qi_ref[:, rows, :].reshape(IDX_HEADS * rb, IDX_DIM)
        wi = wi_ref[rows, :]
        for c0, kc, si, off in _key_pieces(seg_lens, n_keys, KEY_CHUNK):
            k_pos = c0 + lax.broadcasted_iota(jnp.int32, (1, kc), 1)
            lg = _dot(qi, seg_refs[si][2][batch_of[r], :, off:off + kc].astype(BF16))
            sc = wi[:, 0:1] * jnp.maximum(lg[0:rb], 0.0)
            for h in range(1, IDX_HEADS):
                sc = sc + wi[:, h:h + 1] * jnp.maximum(lg[h * rb:(h + 1) * rb], 0.0)
            k_chunk = k_pos // CHUNK
            if c0 + kc > n_valid:
                k_chunk = jnp.where(k_pos < n_valid, k_chunk, np.int32(2 ** 30))
            adm = k_chunk <= (q_pos // CHUNK)
            sc = jnp.where(adm, sc, neg_inf)
            s_ref[rows, c0:c0 + kc] = sc
            d_ref[rows, c0:c0 + kc] = jnp.abs(q_pos - k_pos).astype(F32)

    groups = []
    for g0 in range(0, tq, LANES):
        g_rows = min(LANES, tq - g0)
        g_keys = {row_keys[r] for r in range(g0 // rb, (g0 + g_rows) // rb)}
        assert len(g_keys) == 1
        groups.append((g0, g_rows, g_keys.pop()))
    for g, (g0, g_rows, g_keys) in enumerate(groups):
        for c0 in range(0, g_keys, KEY_CHUNK):
            kc = min(KEY_CHUNK, g_keys - c0)
            sc = s_ref[g0:g0 + g_rows, c0:c0 + kc]
            if g_rows < LANES:
                sc = jnp.concatenate([sc, jnp.full((LANES - g_rows, kc), neg_inf, F32)], axis=0)
            st_ref[g, c0:c0 + kc, :] = sc.T

    def scores_t(g):
        return st_ref[g, :groups[g][2], :]

    def count_ge(r, t):
        return _reduce_keys(jnp.where(scores_t(r) >= t, 1.0, 0.0), jnp.sum)

    lane = lax.broadcasted_iota(jnp.int32, (1, LANES), 1)
    brackets, states, n_adms = [], [], []
    for g, (g0, g_rows, _) in enumerate(groups):
        s = scores_t(g)
        row = g0 + lane
        q_pos = (q_off + row % rb) if block_batches else (q_off + qb * tq + row)
        n_adm = jnp.minimum((q_pos // CHUNK + 1) * CHUNK, n_valid)
        n_adm = jnp.where(lane < g_rows, n_adm, 0).astype(F32)
        row_max = _reduce_keys(s, jnp.max)
        row_min = _reduce_keys(jnp.where(s > neg_inf, s, pos_inf), jnp.min)
        brackets.append((row_min, row_max, jnp.full((1, LANES), pos_inf, F32)))
        n_adms.append(n_adm)

    def bisect(_, carry):
        out = []
        for r, (lo, hb, hiv) in enumerate(carry):
            mid = 0.5 * lo + 0.5 * hb
            ge = count_ge(r, mid) >= kf
            out.append((jnp.where(ge, mid, lo), jnp.where(ge, hb, mid), jnp.where(ge, hiv, mid)))
        return tuple(out)

    brackets = lax.fori_loop(0, BISECT_STEPS, bisect, tuple(brackets))

    for (_, _, hiv), n_adm in zip(brackets, n_adms):
        done = jnp.where(n_adm <= kf, 1.0, 0.0)
        states.append((jnp.full((1, LANES), np.finfo(np.float32).min, F32), hiv, done, jnp.zeros((1, LANES), F32)))

    def n_open(states):
        return sum(jnp.sum(1.0 - done) for _, _, done, _ in states)

    def scan_cond(carry):
        _, n_left, it = carry
        return jnp.logical_and(n_left > 0.0, it < max(row_keys))

    def scan_body(carry):
        states, _, it = carry
        out = []
        for r, (thr, hiv, done, n_ge) in enumerate(states):
            sv = scores_t(r)
            cand = _reduce_keys(jnp.where(sv < hiv, sv, neg_inf), jnp.max)
            cnt = count_ge(r, cand)
            found = jnp.where(done > 0.5, 0.0, jnp.where(cnt >= kf, 1.0, 0.0))
            thr = jnp.where(found > 0.5, cand, thr)
            n_ge = jnp.where(found > 0.5, cnt, n_ge)
            done = jnp.maximum(done, found)
            out.append((thr, jnp.where(done > 0.5, hiv, cand), done, n_ge))
        return tuple(out), n_open(out), it + 1

    states, _, _ = lax.while_loop(scan_cond, scan_body, (tuple(states), n_open(states), jnp.int32(0)))

    over = []
    for r, (thr, _, _, n_ge) in enumerate(states):
        g0, g_rows, _ = groups[r]
        thr_ref[g0:g0 + g_rows, :] = jnp.broadcast_to(thr, (LANES, LANES)).T[:g_rows, 0:1]
        over.append(jnp.max(n_ge))
    tied = functools.reduce(jnp.maximum, over) > kf

    att_chunk = min(ATT_CHUNK_MAX, ATT_CHUNK * LANES // rb)
    ones_row = jnp.where(lax.broadcasted_iota(jnp.int32, (LANES - B_HEAD_DIM, att_chunk), 0) == 0, 1.0, 0.0)
    ones_row = ones_row.astype(BF16)

    for r, (rows, n_keys) in enumerate(blocks):
        bi = batch_of[r]
        thr = thr_ref[rows, :]

        @pl.when(jnp.logical_not(tied))
        def _():
            d_ref[rows, :n_keys] = jnp.where(s_ref[rows, :n_keys] >= thr, d_ref[rows, :n_keys], pos_inf)

        @pl.when(tied)
        def _():
            n_gt = jnp.sum(jnp.where(s_ref[rows, :n_keys] > thr, 1.0, 0.0), axis=1, keepdims=True)
            room = kf - n_gt
            tri = (lax.broadcasted_iota(jnp.int32, (LANES, LANES), 0)
                   <= lax.broadcasted_iota(jnp.int32, (LANES, LANES), 1))
            tri = jnp.where(tri, 1.0, 0.0).astype(BF16)
            run = jnp.zeros((rb, 1), F32)
            for c0 in range(0, n_keys, LANES):
                blk = s_ref[rows, c0:c0 + LANES]
                eq = jnp.where(blk == thr, 1.0, 0.0)
                prefix = _dot(eq.astype(BF16), tri) + run
                keep = jnp.where(blk > thr, 1.0, jnp.where(prefix <= room, eq, 0.0))
                d_ref[rows, c0:c0 + LANES] = jnp.where(keep > 0.5, d_ref[rows, c0:c0 + LANES], pos_inf)
                run = run + jnp.sum(eq, axis=1, keepdims=True)

        qs = [q_ref[g * B_GROUP:(g + 1) * B_GROUP, rows, :].reshape(B_GROUP * rb, B_HEAD_DIM)
              for g in range(B_KV_HEADS)]
        m_run = [jnp.full((rb, 1), np.finfo(np.float32).min, F32) for _ in range(B_HEADS)]
        acc = [jnp.zeros((rb, LANES), F32) for _ in range(B_HEADS)]
        for c0, kc, si, off in _key_pieces(seg_lens, n_keys, att_chunk):
            kt_ref, v_ref, _ = seg_refs[si]
            dist = d_ref[rows, c0:c0 + kc]
            for g in range(B_KV_HEADS):
                logits = _dot(qs[g], kt_ref[bi, g, :, off:off + kc].astype(BF16))
                ps, alphas = [], []
                for hh in range(B_GROUP):
                    h = g * B_GROUP + hh
                    lgt = logits[hh * rb:(hh + 1) * rb] - np.float32(slopes[h] * LOG2_E) * dist
                    m_new = jnp.maximum(m_run[h], jnp.max(lgt, axis=1, keepdims=True))
                    alphas.append(jnp.exp2(m_run[h] - m_new))
                    ps.append(jnp.exp2(lgt - m_new).astype(BF16))
                    m_run[h] = m_new
                p = jnp.concatenate(ps, axis=0)
                if segments[si][1]:
                    vt = jnp.concatenate([v_ref[bi, g, :, off:off + kc].astype(BF16), ones_row[:, :kc]], axis=0)
                    pv = _dot_nt(p, vt)
                else:
                    pv = _dot(p, v_ref[bi, g, off:off + kc, :])
                for hh in range(B_GROUP):
                    h = g * B_GROUP + hh
                    acc[h] = alphas[hh] * acc[h] + pv[hh * rb:(hh + 1) * rb]
        for h in range(B_HEADS):
            out = acc[h][:, :B_HEAD_DIM] / acc[h][:, B_HEAD_DIM:B_HEAD_DIM + 1]
            o_ref[rows, h * B_HEAD_DIM:(h + 1) * B_HEAD_DIM] = out.astype(BF16)


def _dsa_attention(q_hm, qi_hm, wi, key_segments, *, n, t_q, tq, rb, qb, n_valid, q_off, topk, slopes,
                   batches_per_step=1):
    nqb = t_q // tq
    nb = batches_per_step
    assert nb == 1 or (nqb == 1 and rb == tq and n % nb == 0)
    n_rb = nb * tq // rb
    last_chunk = (q_off + (qb + 1) * tq - 1) // CHUNK
    need = min((last_chunk + 1) * CHUNK, n_valid)
    n_keys = -(-need // LANES) * LANES
    if rb % CHUNK == 0 and need == q_off + (qb + 1) * tq:
        row_keys = tuple(n_keys - (n_rb - 1 - r) * rb for r in range(n_rb))
    else:
        row_keys = (n_keys,) * n_rb
    qmap = lambda b: (0, b * nqb + qb, 0)
    in_specs = [
        pl.BlockSpec((B_HEADS, nb * tq, B_HEAD_DIM), qmap),
        pl.BlockSpec((IDX_HEADS, nb * tq, IDX_DIM), qmap),
        pl.BlockSpec((nb * tq, IDX_HEADS), lambda b: (b * nqb + qb, 0)),
    ]
    operands, segments, col = [], [], 0
    for kt, v, kit, v_transposed in key_segments:
        length = min(kt.shape[-1], n_keys - col)
        if length <= 0:
            break
        v_block = (nb, B_KV_HEADS, B_HEAD_DIM, length) if v_transposed else (nb, B_KV_HEADS, length, LANES)
        in_specs += [
            pl.BlockSpec((nb, B_KV_HEADS, B_HEAD_DIM, length), lambda b: (b, 0, 0, 0)),
            pl.BlockSpec(v_block, lambda b: (b, 0, 0, 0)),
            pl.BlockSpec((nb, IDX_DIM, length), lambda b: (b, 0, 0)),
        ]
        operands += [kt, v, kit]
        segments.append((length, v_transposed))
        col += length
    assert col == n_keys, (col, n_keys)
    return pl.pallas_call(
        functools.partial(_dsa_block, segments=tuple(segments), qb=qb, tq=nb * tq, rb=rb, row_keys=row_keys,
                          block_batches=nb > 1, n_valid=n_valid, q_off=q_off, topk=topk, slopes=slopes),
        grid=(n // nb,), in_specs=in_specs,
        out_specs=pl.BlockSpec((nb * tq, B_WIDTH), lambda b: (b, 0)),
        out_shape=jax.ShapeDtypeStruct((n * tq, B_WIDTH), BF16),
        scratch_shapes=[pltpu.VMEM((nb * tq, n_keys), F32), pltpu.VMEM((nb * tq, n_keys), F32),
                        pltpu.VMEM((nb * tq, 1), F32), pltpu.VMEM((-(-nb * tq // LANES), n_keys, LANES), F32)],
        compiler_params=pltpu.CompilerParams(dimension_semantics=("parallel",), vmem_limit_bytes=VMEM_LIMIT),
        name="dsa_attention",
    )(q_hm, qi_hm, wi, *operands)


def _merge_kernel(x_ref, a_ref, *refs, n_parts, alpha):
    b_refs = refs[:n_parts]
    wg_ref, wb_ref, wo_ref, ln1_ref, w1_ref, b1_ref, w2_ref, b2_ref, ln2_ref, y_ref = refs[n_parts:]
    d = x_ref.shape[1]
    x = x_ref[...]
    gates = jax.nn.sigmoid(_dot_nt(x.astype(BF16), wg_ref[...]))
    b = b_refs[0][...]
    part = pl.program_id(0) % n_parts
    for j in range(1, n_parts):
        b = jnp.where(part == j, b_refs[j][...], b)
    m = gates[:, :d] * _dot(a_ref[...], wb_ref[0]) + gates[:, d:] * _dot(b, wb_ref[1])
    h = _layer_norm(alpha * x + _dot(m.astype(BF16), wo_ref[...]), ln1_ref[0:1, :], ln1_ref[1:2, :])
    f = jnp.square(jnp.maximum(_dot(h.astype(BF16), w1_ref[...]) + b1_ref[...], 0.0))
    f = _dot(f.astype(BF16), w2_ref[...]) + b2_ref[...]
    y_ref[...] = _layer_norm(alpha * h + f, ln2_ref[0:1, :], ln2_ref[1:2, :])


def _resident_spec(shape):
    nd = len(shape)
    return pl.BlockSpec(shape, lambda *_: (0,) * nd, pipeline_mode=pl.Buffered(1))


def _merge_ffn(x2, a, b_parts, wts, tm, alpha):
    t, d = x2.shape
    n_parts = len(b_parts)
    assert all(p.shape == (t // n_parts, B_WIDTH) for p in b_parts)
    row = lambda i: (i, 0)
    in_specs = [pl.BlockSpec((tm, d), row), pl.BlockSpec((tm, A_WIDTH), row)]
    in_specs += [pl.BlockSpec((tm, B_WIDTH), lambda i: (i // n_parts, 0)) for _ in b_parts]
    in_specs += [_resident_spec(w.shape) for w in wts]
    return pl.pallas_call(
        functools.partial(_merge_kernel, n_parts=n_parts, alpha=np.float32(alpha)),
        grid=(t // tm,), in_specs=in_specs,
        out_specs=pl.BlockSpec((tm, d), row),
        out_shape=jax.ShapeDtypeStruct((t, d), F32),
        compiler_params=pltpu.CompilerParams(dimension_semantics=("parallel",), vmem_limit_bytes=VMEM_LIMIT),
        name="merge_ffn",
    )(x2, a, *b_parts, *wts)


def _spatial_weights(w_s, b_s, span):
    pos = jnp.arange(A_SPAN)
    mask = (pos[None, :] // CHUNK) <= (pos[:, None] // CHUNK)
    ws = jnp.where(mask[None], w_s, 0.0)[:, :span, :span].astype(BF16)
    bs = jnp.repeat(b_s[:, :span].T, A_GROUP_CH, axis=1)
    return ws, bs


def _pad_axis(x, axis, size):
    pads = [(0, 0)] * x.ndim
    pads[axis] = (0, size - x.shape[axis])
    return jnp.pad(x, pads)


def kernel(x_prompt, x_sample, cache_k, cache_v, cache_kidx, w_in, lnv_g, lnv_b, w_s, b_s, lnk_g, lnk_b,
           w_branch, w_out, ln1_g, ln1_b, w_ff1, b_ff1, w_ff2, b_ff2, ln2_g, ln2_b):
    depth = w_in.shape[0]
    n_p, s_p, d = x_prompt.shape
    n_s, t_s, _ = x_sample.shape
    past = cache_k.shape[2]
    alpha = (2 * depth) ** 0.25
    slopes = tuple(float(2.0 ** (-8.0 * h / B_HEADS)) for h in range(1, B_HEADS + 1))
    kvw = B_KV_HEADS * B_HEAD_DIM
    c_a = 2 * A_WIDTH
    c_q = c_a + B_WIDTH
    c_k = c_q + kvw
    c_v = c_k + kvw
    c_qi = c_v + IDX_HEADS * IDX_DIM
    c_wi = c_qi + IDX_DIM + IDX_HEADS

    xp = x_prompt.reshape(n_p * s_p, d)
    xs = x_sample.reshape(n_s * t_s, d)
    outs = [[] for _ in range(7)]
    for l in range(depth):
        w = jnp.transpose(w_in[l])
        pad = jnp.zeros((LANES - (c_wi - c_qi), d), F32)
        proj_w = (
            w[:c_a].astype(BF16),
            (w[c_a:c_q] * (LOG2_E * B_HEAD_DIM ** -0.5)).astype(BF16),
            w[c_q:c_v].astype(BF16),
            jnp.pad(w[c_k:c_v].reshape(B_KV_HEADS, B_HEAD_DIM, d),
                    ((0, 0), (0, LANES - B_HEAD_DIM), (0, 0))).reshape(B_KV_HEADS * LANES, d).astype(BF16),
            (w[c_v:c_qi] * (IDX_DIM ** -0.5)).astype(BF16),
            jnp.concatenate([w[c_qi:c_wi], pad], axis=0).astype(BF16),
            jnp.stack([lnv_g[l], lnv_b[l]]),
            jnp.stack([lnk_g[l], lnk_b[l]]),
        )
        merge_w = (
            w[c_wi:].astype(BF16),
            w_branch[l].astype(BF16),
            w_out[l].astype(BF16),
            jnp.stack([ln1_g[l], ln1_b[l]]),
            w_ff1[l].astype(BF16),
            b_ff1[l][None, :],
            w_ff2[l].astype(BF16),
            b_ff2[l][None, :],
            jnp.stack([ln2_g[l], ln2_b[l]]),
        )

        a_p, q_p, qi_p, k_p, v_p, kt_p, vb_p, ki_p, kit_p, wi_p = _in_projection(
            xp, proj_w + _spatial_weights(w_s[l], b_s[l], A_SPAN), A_SPAN, PROJ_TILE, False, batch_rows=s_p)
        b_p = [_dsa_attention(
            q_p, qi_p, wi_p, [(kt_p, vb_p, kit_p, False)],
            n=n_p, t_q=s_p, tq=DSA_ROWS, rb=Q_BLOCK, qb=j, n_valid=s_p, q_off=0,
            topk=min(TOPK_MAX, s_p // 4), slopes=slopes)
            for j in range(s_p // DSA_ROWS)]
        assert TOKEN_TILE == DSA_ROWS
        xp = _merge_ffn(xp, a_p, b_p, merge_w, TOKEN_TILE, alpha)

        a_s, q_s, qi_s, k_s, v_s, kt_s, vb_s, ki_s, kit_s, wi_s, va_s = _in_projection(
            xs, proj_w + _spatial_weights(w_s[l], b_s[l], t_s), t_s, n_s * t_s, True)
        n_all = past + t_s
        new_len = -(-t_s // LANES) * LANES
        cache_seg = (jnp.transpose(cache_k[l], (0, 2, 3, 1)), jnp.transpose(cache_v[l], (0, 2, 3, 1)),
                     jnp.transpose(cache_kidx[l], (0, 2, 1)), True)
        kt_new = jnp.transpose(kt_s.reshape(B_KV_HEADS, B_HEAD_DIM, n_s, t_s), (2, 0, 1, 3))
        vb_new = jnp.transpose(vb_s.reshape(B_KV_HEADS, n_s, t_s, LANES), (1, 0, 2, 3))
        kit_new = jnp.transpose(kit_s.reshape(IDX_DIM, n_s, t_s), (1, 0, 2))
        new_seg = (_pad_axis(kt_new, 3, new_len), _pad_axis(vb_new, 2, new_len), _pad_axis(kit_new, 2, new_len), False)
        b_s_ = _dsa_attention(
            q_s, qi_s, wi_s, [cache_seg, new_seg],
            n=n_s, t_q=t_s, tq=t_s, rb=t_s, qb=0, n_valid=n_all, q_off=past,
            topk=min(TOPK_MAX, n_all // 4), slopes=slopes, batches_per_step=SAMPLE_BATCHES)
        xs = _merge_ffn(xs, a_s, [b_s_], merge_w, n_s * t_s, alpha)

        outs[0].append(k_p.reshape(n_p, s_p, B_KV_HEADS, B_HEAD_DIM))
        outs[1].append(v_p.reshape(n_p, s_p, B_KV_HEADS, B_HEAD_DIM))
        outs[2].append(ki_p.reshape(n_p, s_p, IDX_DIM))
        outs[3].append(k_s.reshape(n_s, t_s, B_KV_HEADS, B_HEAD_DIM))
        outs[4].append(v_s.reshape(n_s, t_s, B_KV_HEADS, B_HEAD_DIM))
        outs[5].append(ki_s.reshape(n_s, t_s, IDX_DIM))
        outs[6].append(va_s.reshape(n_s, t_s, A_WIDTH))

    return (xp.reshape(n_p, s_p, d), xs.reshape(n_s, t_s, d)) + tuple(jnp.stack(o) for o in outs)
```
